```python
import math
import jax
import jax.numpy as jnp
from jax import lax
import numpy as np

D_MODEL = 2048
BATCH = 8
SEQ = 4096
DEPTH = 2

A_HEADS = 8
A_HEAD_DIM = 128
A_WIDTH = A_HEADS * A_HEAD_DIM
A_Q_LORA = 512
A_KV_LORA = 256
IDX_HEADS = 16
IDX_DIM = 64
TOPK_MAX = 256
Q_BLOCK = 128
REL_BUCKETS = 32
REL_MAX_DIST = 128
B_HEADS = 8
B_HEAD_DIM = 128
B_WIDTH = B_HEADS * B_HEAD_DIM
B_CONV = 4
GDN_CHUNK = 64
C_WIDTH = 1024
C_CONV = 31
D_WIDTH = 1024
D_CONV = 3

EPS = 1e-6
AB_SPLITS = (A_Q_LORA, A_KV_LORA, IDX_DIM, IDX_HEADS, A_WIDTH,
             3 * B_WIDTH, B_HEADS, B_HEADS, B_WIDTH)
AB_IN = sum(AB_SPLITS)
AB_MIX = A_WIDTH + B_WIDTH
CD_SPLITS = (C_WIDTH, C_WIDTH, C_WIDTH, D_WIDTH, D_WIDTH, D_WIDTH, D_WIDTH)
CD_IN = sum(CD_SPLITS)
CD_MIX = C_WIDTH + D_WIDTH
N_AB = (DEPTH + 1) // 2
N_CD = DEPTH // 2

kernel_name = 'hybrid_dsa_gdn_conformer_shortconv'


def _split(y, sizes):
    return jnp.split(y, np.cumsum(sizes)[:-1].tolist(), axis=-1)


def rmsnorm(x, w):
    xf = x.astype(jnp.float32)
    y = xf * lax.rsqrt(jnp.mean(xf * xf, axis=-1, keepdims=True) + EPS)
    return (y * w.astype(jnp.float32)).astype(x.dtype)


def layernorm(x, w, b):
    xf = x.astype(jnp.float32)
    xc = xf - jnp.mean(xf, axis=-1, keepdims=True)
    y = xc * lax.rsqrt(jnp.mean(xc * xc, axis=-1, keepdims=True) + EPS)
    return (y * w.astype(jnp.float32) + b.astype(jnp.float32)).astype(x.dtype)


def l2norm(x):
    return x * lax.rsqrt(jnp.sum(x * x, axis=-1, keepdims=True) + EPS)


def causal_dwconv(x, w):
    width, ch = w.shape
    return lax.conv_general_dilated(
        x, w[:, None, :].astype(x.dtype), window_strides=(1,), padding=[(width - 1, 0)],
        dimension_numbers=('NWC', 'WIO', 'NWC'), feature_group_count=ch)


def t5_bucket(dist):
    max_exact = REL_BUCKETS // 2
    large = max_exact + (jnp.log(jnp.maximum(dist, 1).astype(jnp.float32) / max_exact)
                         / math.log(REL_MAX_DIST / max_exact)
                         * (REL_BUCKETS - max_exact)).astype(jnp.int32)
    large = jnp.minimum(large, REL_BUCKETS - 1)
    return jnp.where(dist < max_exact, dist, large)


def dsa_attention(c_q, c_kv, k_idx, w_idx, q_norm, w_uq, w_iq, kv_norm, w_uk, w_uv,
                  q_gain, k_gain, ik_w, ik_b, rel_bias):
    b, t, _ = c_q.shape
    c_q = rmsnorm(c_q, q_norm)
    q = rmsnorm((c_q @ w_uq).reshape(b, t, A_HEADS, A_HEAD_DIM), q_gain)
    q_idx = (c_q @ w_iq).reshape(b, t, IDX_HEADS, IDX_DIM)
    c_kv = rmsnorm(c_kv, kv_norm)
    k = rmsnorm((c_kv @ w_uk).reshape(b, t, A_HEADS, A_HEAD_DIM), k_gain)
    v = (c_kv @ w_uv).reshape(b, t, A_HEADS, A_HEAD_DIM)
    k_idx = layernorm(k_idx, ik_w, ik_b).astype(jnp.float32)
    w_idx = w_idx.astype(jnp.float32) * (IDX_HEADS ** -0.5)
    q_idx = q_idx.astype(jnp.float32)
    n_sel = min(TOPK_MAX, t // 4)
    key_pos = jnp.arange(t, dtype=jnp.int32)

    def query_block(i):
        start = i * Q_BLOCK
        qi = lax.dynamic_slice_in_dim(q_idx, start, Q_BLOCK, axis=1)
        wi = lax.dynamic_slice_in_dim(w_idx, start, Q_BLOCK, axis=1)
        qb = lax.dynamic_slice_in_dim(q, start, Q_BLOCK, axis=1)
        qpos = start + jnp.arange(Q_BLOCK, dtype=jnp.int32)
        rel = jax.nn.relu(jnp.einsum('bqhd,bsd->bqhs', qi, k_idx))
        score = jnp.einsum('bqhs,bqh->bqs', rel, wi) * (IDX_DIM ** -0.5)
        causal = key_pos[None, :] <= qpos[:, None]
        score = jnp.where(causal[None], score, -jnp.inf)
        _, idx = lax.top_k(score, n_sel)
        k_sel = jax.vmap(lambda a, j: a[j])(k, idx)
        v_sel = jax.vmap(lambda a, j: a[j])(v, idx)
        logits = jnp.einsum('bqhd,bqkhd->bqhk', qb, k_sel).astype(jnp.float32) * (A_HEAD_DIM ** -0.5)
        dist = qpos[None, :, None] - idx
        bias = rel_bias[t5_bucket(jnp.maximum(dist, 0))]
        logits = logits + jnp.transpose(bias, (0, 1, 3, 2)).astype(jnp.float32)
        logits = jnp.where((dist >= 0)[:, :, None, :], logits, -jnp.inf)
        p = jax.nn.softmax(logits, axis=-1).astype(v.dtype)
        return jnp.einsum('bqhk,bqkhd->bqhd', p, v_sel)

    out = lax.map(query_block, jnp.arange(t // Q_BLOCK))
    return jnp.moveaxis(out, 0, 1).reshape(b, t, A_WIDTH)


def chunk_gated_delta_rule(q, k, v, g, beta):
    b, t, h, dk = q.shape
    dv = v.shape[-1]
    c = GDN_CHUNK
    n = t // c

    def chunk(a):
        return a.reshape(b, n, c, h, a.shape[-1]).transpose(1, 0, 3, 2, 4)

    q, k, v = chunk(q), chunk(k), chunk(v)
    g = g.reshape(b, n, c, h).transpose(1, 0, 3, 2)
    beta = beta.reshape(b, n, c, h).transpose(1, 0, 3, 2)
    gc = jnp.cumsum(g, axis=-1)
    lower = jnp.tril(jnp.ones((c, c), dtype=bool))
    diff = gc[..., :, None] - gc[..., None, :]
    decay = jnp.where(lower, jnp.exp(jnp.where(lower, diff, 0.0)), 0.0)
    k_beta = k * beta[..., None]
    v_beta = v * beta[..., None]
    strict = jnp.tril(jnp.ones((c, c), dtype=jnp.float32), -1)
    a_mat = jnp.eye(c, dtype=jnp.float32) + jnp.einsum('nbhid,nbhjd->nbhij', k_beta, k) * decay * strict
    rhs = jnp.concatenate([v_beta, k_beta * jnp.exp(gc)[..., None]], axis=-1)
    sol = lax.linalg.triangular_solve(a_mat, rhs, left_side=True, lower=True, unit_diagonal=True)
    u, w = sol[..., :dv], sol[..., dv:]
    attn = jnp.einsum('nbhid,nbhjd->nbhij', q, k) * decay

    def step(state, xs):
        q_i, k_i, u_i, w_i, attn_i, gc_i = xs
        v_new = u_i - jnp.einsum('bhck,bhkv->bhcv', w_i, state)
        o_i = (jnp.einsum('bhck,bhkv->bhcv', q_i * jnp.exp(gc_i)[..., None], state)
               + jnp.einsum('bhij,bhjv->bhiv', attn_i, v_new))
        g_last = gc_i[..., -1]
        state = (state * jnp.exp(g_last)[..., None, None]
                 + jnp.einsum('bhck,bhcv->bhkv', k_i * jnp.exp(g_last[..., None] - gc_i)[..., None], v_new))
        return state, o_i

    s0 = jnp.zeros((b, h, dk, dv), jnp.float32)
    _, o = lax.scan(step, s0, (q, k, u, w, attn, gc))
    return o.transpose(1, 0, 3, 2, 4).reshape(b, t, h, dv)


def gated_deltanet(qkv, beta_raw, alpha_raw, z, conv_w, a_log, dt_bias, o_norm):
    b, t, _ = qkv.shape
    qkv = jax.nn.silu(causal_dwconv(qkv, conv_w)).astype(jnp.float32)
    q, k, v = [a.reshape(b, t, B_HEADS, B_HEAD_DIM) for a in jnp.split(qkv, 3, axis=-1)]
    q = l2norm(q) * (B_HEAD_DIM ** -0.5)
    k = l2norm(k)
    beta = jax.nn.sigmoid(beta_raw.astype(jnp.float32))
    g = -jnp.exp(a_log.astype(jnp.float32)) * jax.nn.softplus(
        alpha_raw.astype(jnp.float32) + dt_bias.astype(jnp.float32))
    o = chunk_gated_delta_rule(q, k, v, g, beta)
    o = rmsnorm(o, o_norm).reshape(b, t, B_WIDTH).astype(z.dtype)
    return o * jax.nn.silu(z)


def ab_mix(h, w_in, q_norm, w_uq, w_iq, kv_norm, w_uk, w_uv, q_gain, k_gain, ik_w, ik_b,
           conv_w, a_log, dt_bias, o_norm, rel_bias):
    c_q, c_kv, k_idx, w_idx, z_a, qkv_b, beta_b, alpha_b, z_b = _split(h @ w_in, AB_SPLITS)
    y_a = dsa_attention(c_q, c_kv, k_idx, w_idx, q_norm, w_uq, w_iq, kv_norm, w_uk, w_uv,
                        q_gain, k_gain, ik_w, ik_b, rel_bias) * jax.nn.silu(z_a)
    y_b = gated_deltanet(qkv_b, beta_b, alpha_b, z_b, conv_w, a_log, dt_bias, o_norm)
    return jnp.concatenate([y_a, y_b], axis=-1)


def cd_mix(h, w_in, dw_w, dw_b, ln_w, ln_b, d_conv_w):
    glu_a, glu_g, z_c, b_gate, c_gate, u_d, z_d = _split(h @ w_in, CD_SPLITS)
    u = glu_a * jax.nn.sigmoid(glu_g)
    u = causal_dwconv(u, dw_w) + dw_b.astype(u.dtype)
    u = jax.nn.silu(layernorm(u, ln_w, ln_b))
    y_c = u * jax.nn.silu(z_c)
    y_d = b_gate * causal_dwconv(c_gate * u_d, d_conv_w) * jax.nn.silu(z_d)
    return jnp.concatenate([y_c, y_d], axis=-1)


def setup_inputs(seed: int = 0) -> dict:
    key = jax.random.key(seed)
    keys = iter(jax.random.split(key, 40))
    f32 = jnp.float32

    def dense(shape, fan_in):
        return jax.random.normal(next(keys), shape, f32) * (fan_in ** -0.5)

    def gain(shape):
        return 1.0 + 0.02 * jax.random.normal(next(keys), shape, f32)

    def small(shape):
        return 0.02 * jax.random.normal(next(keys), shape, f32)

    x = jax.random.normal(next(keys), (BATCH, SEQ, D_MODEL), f32)
    dt = jnp.exp(jax.random.uniform(next(keys), (N_AB, B_HEADS), f32, math.log(1e-3), math.log(1e-1)))
    a_log = jnp.log(jax.random.uniform(next(keys), (N_AB, B_HEADS), f32, 1.0, 16.0))
    return {
        'x': x,
        'norm_w': gain((DEPTH, D_MODEL)),
        'rel_bias': 0.1 * jax.random.normal(next(keys), (REL_BUCKETS, A_HEADS), f32),
        'ab_w_in': dense((N_AB, D_MODEL, AB_IN), D_MODEL),
        'a_q_norm': gain((N_AB, A_Q_LORA)),
        'a_w_uq': dense((N_AB, A_Q_LORA, A_WIDTH), A_Q_LORA),
        'a_w_iq': dense((N_AB, A_Q_LORA, IDX_HEADS * IDX_DIM), A_Q_LORA),
        'a_kv_norm': gain((N_AB, A_KV_LORA)),
        'a_w_uk': dense((N_AB, A_KV_LORA, A_WIDTH), A_KV_LORA),
        'a_w_uv': dense((N_AB, A_KV_LORA, A_WIDTH), A_KV_LORA),
        'a_q_gain': gain((N_AB, A_HEAD_DIM)),
        'a_k_gain': gain((N_AB, A_HEAD_DIM)),
        'a_ik_norm_w': gain((N_AB, IDX_DIM)),
        'a_ik_norm_b': small((N_AB, IDX_DIM)),
        'b_conv_w': dense((N_AB, B_CONV, 3 * B_WIDTH), B_CONV),
        'b_a_log': a_log,
        'b_dt_bias': dt + jnp.log(-jnp.expm1(-dt)),
        'b_o_norm': gain((N_AB, B_HEAD_DIM)),
        'ab_w_out': dense((N_AB, AB_MIX, D_MODEL), AB_MIX),
        'cd_w_in': dense((N_CD, D_MODEL, CD_IN), D_MODEL),
        'c_dw_w': dense((N_CD, C_CONV, C_WIDTH), C_CONV),
        'c_dw_b': small((N_CD, C_WIDTH)),
        'c_ln_w': gain((N_CD, C_WIDTH)),
        'c_ln_b': small((N_CD, C_WIDTH)),
        'd_conv_w': dense((N_CD, D_CONV, D_WIDTH), D_CONV),
        'cd_w_out': dense((N_CD, CD_MIX, D_MODEL), CD_MIX),
    }


def reference(x, norm_w, rel_bias, ab_w_in, a_q_norm, a_w_uq, a_w_iq, a_kv_norm, a_w_uk,
              a_w_uv, a_q_gain, a_k_gain, a_ik_norm_w, a_ik_norm_b, b_conv_w, b_a_log,
              b_dt_bias, b_o_norm, ab_w_out, cd_w_in, c_dw_w, c_dw_b, c_ln_w, c_ln_b,
              d_conv_w, cd_w_out):
    for i in range(DEPTH):
        h = rmsnorm(x, norm_w[i])
        j = i // 2
        if i % 2 == 0:
            y = ab_mix(h, ab_w_in[j], a_q_norm[j], a_w_uq[j], a_w_iq[j], a_kv_norm[j],
                       a_w_uk[j], a_w_uv[j], a_q_gain[j], a_k_gain[j], a_ik_norm_w[j],
                       a_ik_norm_b[j], b_conv_w[j], b_a_log[j], b_dt_bias[j], b_o_norm[j],
                       rel_bias)
            x = x + y @ ab_w_out[j]
        else:
            y = cd_mix(h, cd_w_in[j], c_dw_w[j], c_dw_b[j], c_ln_w[j], c_ln_b[j], d_conv_w[j])
            x = x + y @ cd_w_out[j]
    return x
```

```python
import functools
import math

import jax
import jax.numpy as jnp
from jax import lax
from jax.experimental import pallas as pl
from jax.experimental.pallas import tpu as pltpu

D_MODEL = 2048
A_HEADS = 8
A_HEAD_DIM = 128
A_WIDTH = A_HEADS * A_HEAD_DIM
A_Q_LORA = 512
A_KV_LORA = 256
IDX_HEADS = 16
IDX_DIM = 64
TOPK_MAX = 256
REL_BUCKETS = 32
REL_MAX_DIST = 128
B_HEADS = 8
B_HEAD_DIM = 128
B_WIDTH = B_HEADS * B_HEAD_DIM
B_CONV = 4
GDN_CHUNK = 64
C_WIDTH = 1024
C_CONV = 31
D_WIDTH = 1024
D_CONV = 3
EPS = 1e-6

AB_SPLITS = (A_Q_LORA, A_KV_LORA, IDX_DIM, IDX_HEADS, A_WIDTH, 3 * B_WIDTH, B_HEADS, B_HEADS, B_WIDTH)
AB_IN_PAD = 6144
SM_KIDX = 0
SM_WIDX = 64
SM_BETA = 80
SM_ALPHA = 88
COL_QKV_B = 0
COL_Z_A = 3072
COL_Z_B = 4096
COL_C_Q = 5120
COL_C_KV = 5632
COL_SMALL = 5888

LANES = 128
MASK_NEG = -1e30
INT_MIN = -(2 ** 31)
VMEM_LIMIT = 56 * 1024 * 1024

F32 = jnp.float32
BF16 = jnp.bfloat16
HIGHEST = lax.Precision.HIGHEST


def _tile(n, pref):
    t = min(n, pref)
    assert n % t == 0, (n, t)
    return t


def _params(*sem):
    return pltpu.CompilerParams(dimension_semantics=sem, vmem_limit_bytes=VMEM_LIMIT)


def _dot(a, b):
    return jnp.dot(a, b, preferred_element_type=F32)


def _dot_nt(a, b):
    return lax.dot_general(a, b, (((1,), (1,)), ((), ())), preferred_element_type=F32)


def _dot_tn(a, b):
    return lax.dot_general(a, b, (((0,), (0,)), ((), ())), preferred_element_type=F32)


def _silu(x):
    return x * (1.0 / (1.0 + jnp.exp(-x)))


def _sigmoid(x):
    return 1.0 / (1.0 + jnp.exp(-x))


def _norm_matmul_body(x_ref, nw_ref, w_ref, o_ref, h_ref):
    @pl.when(pl.program_id(1) == 0)
    def _():
        x = x_ref[...]
        ms = jnp.mean(x * x, axis=-1, keepdims=True)
        h_ref[...] = (x * lax.rsqrt(ms + EPS) * nw_ref[...]).astype(h_ref.dtype)

    o_ref[...] = _dot(h_ref[...], w_ref[...]).astype(o_ref.dtype)


def norm_matmul(x2d, norm_w, w_bf16, out_dtype=F32):
    m, k = x2d.shape
    n = w_bf16.shape[1]
    tm, tn = _tile(m, 512), _tile(n, 1024)
    return pl.pallas_call(
        _norm_matmul_body,
        grid=(m // tm, n // tn),
        in_specs=[
            pl.BlockSpec((tm, k), lambda i, j: (i, 0)),
            pl.BlockSpec((1, k), lambda i, j: (0, 0)),
            pl.BlockSpec((k, tn), lambda i, j: (0, j)),
        ],
        out_specs=pl.BlockSpec((tm, tn), lambda i, j: (i, j)),
        out_shape=jax.ShapeDtypeStruct((m, n), out_dtype),
        scratch_shapes=[pltpu.VMEM((tm, k), BF16)],
        compiler_params=_params("parallel", "arbitrary"),
        name="norm_matmul",
    )(x2d, norm_w.reshape(1, k), w_bf16)


def _out_proj_body(x_ref, ya_ref, yb_ref, wa_ref, wb_ref, o_ref):
    o_ref[...] = x_ref[...] + _dot(ya_ref[...], wa_ref[...]) + _dot(yb_ref[...], wb_ref[...])


def out_proj(x2d, ya, yb, wa_bf16, wb_bf16):
    m, n = x2d.shape
    ka, kb = ya.shape[1], yb.shape[1]
    tm = _tile(m, 512)
    return pl.pallas_call(
        _out_proj_body,
        grid=(m // tm,),
        in_specs=[
            pl.BlockSpec((tm, n), lambda i: (i, 0)),
            pl.BlockSpec((tm, ka), lambda i: (i, 0)),
            pl.BlockSpec((tm, kb), lambda i: (i, 0)),
            pl.BlockSpec((ka, n), lambda i: (0, 0)),
            pl.BlockSpec((kb, n), lambda i: (0, 0)),
        ],
        out_specs=pl.BlockSpec((tm, n), lambda i: (i, 0)),
        out_shape=jax.ShapeDtypeStruct((m, n), F32),
        compiler_params=_params("parallel"),
        name="out_proj",
    )(x2d, ya, yb, wa_bf16, wb_bf16)


def _head_rmsnorm(y, gain_row, scale):
    outs = []
    for h in range(y.shape[1] // LANES):
        seg = y[:, h * LANES:(h + 1) * LANES]
        ms = jnp.mean(seg * seg, axis=-1, keepdims=True)
        outs.append(seg * lax.rsqrt(ms + EPS) * (gain_row * scale))
    return jnp.concatenate(outs, axis=-1)


def _dsa_prep_body(cq_ref, ckv_ref, sm_ref, qn_ref, kvn_ref, wuq_ref, wiq_ref, wuk_ref, wuv_ref,
                   qg_ref, kg_ref, ikw_ref, ikb_ref,
                   q_ref, k_ref, v_ref, qi_ref, ke_ref, ko_ref):
    cq = cq_ref[...]
    cq = cq * lax.rsqrt(jnp.mean(cq * cq, axis=-1, keepdims=True) + EPS) * qn_ref[...]
    cqb = cq.astype(BF16)
    q = _dot(cqb, wuq_ref[...])
    q_ref[...] = _head_rmsnorm(q, qg_ref[...], A_HEAD_DIM ** -0.5).astype(q_ref.dtype)
    qi_ref[...] = _dot(cqb, wiq_ref[...]).astype(qi_ref.dtype)

    ckv = ckv_ref[...]
    ckv = ckv * lax.rsqrt(jnp.mean(ckv * ckv, axis=-1, keepdims=True) + EPS) * kvn_ref[...]
    ckvb = ckv.astype(BF16)
    k = _dot(ckvb, wuk_ref[...])
    k_ref[...] = _head_rmsnorm(k, kg_ref[...], 1.0).astype(k_ref.dtype)
    v_ref[...] = _dot(ckvb, wuv_ref[...]).astype(v_ref.dtype)

    kx = sm_ref[...][:, SM_KIDX:SM_KIDX + IDX_DIM]
    mu = jnp.mean(kx, axis=-1, keepdims=True)
    kc = kx - mu
    kl = kc * lax.rsqrt(jnp.mean(kc * kc, axis=-1, keepdims=True) + EPS) * ikw_ref[...] + ikb_ref[...]
    z = jnp.zeros_like(kl)
    ke_ref[...] = jnp.concatenate([kl, z], axis=-1).astype(ke_ref.dtype)
    ko_ref[...] = jnp.concatenate([z, kl], axis=-1).astype(ko_ref.dtype)


def dsa_prep(proj, q_norm, kv_norm, w_uq, w_iq, w_uk, w_uv, q_gain, k_gain, ik_w, ik_b):
    m = proj.shape[0]
    tm = _tile(m, 512)
    row = lambda a: a.reshape(1, -1)
    full = lambda a: pl.BlockSpec(a.shape, lambda i: (0,) * a.ndim)
    consts = [row(q_norm), row(kv_norm), w_uq.astype(BF16), w_iq.astype(BF16), w_uk.astype(BF16),
              w_uv.astype(BF16), row(q_gain), row(k_gain), row(ik_w), row(ik_b)]
    outs = [jax.ShapeDtypeStruct((m, A_WIDTH), BF16)] * 3 + [
        jax.ShapeDtypeStruct((m, IDX_HEADS * IDX_DIM), BF16),
        jax.ShapeDtypeStruct((m, LANES), BF16), jax.ShapeDtypeStruct((m, LANES), BF16)]
    return pl.pallas_call(
        _dsa_prep_body,
        grid=(m // tm,),
        in_specs=[
            pl.BlockSpec((tm, A_Q_LORA), lambda i: (i, COL_C_Q // A_Q_LORA)),
            pl.BlockSpec((tm, A_KV_LORA), lambda i: (i, COL_C_KV // A_KV_LORA)),
            pl.BlockSpec((tm, LANES), lambda i: (i, COL_SMALL // LANES)),
        ] + [full(c) for c in consts],
        out_specs=[pl.BlockSpec((tm, s.shape[1]), lambda i: (i, 0)) for s in outs],
        out_shape=outs,
        compiler_params=_params("parallel"),
        name="dsa_prep",
    )(proj, proj, proj, *consts)


def _t5_bucket(dist):
    max_exact = REL_BUCKETS // 2
    large = max_exact + (jnp.log(jnp.maximum(dist, 1).astype(F32) / max_exact)
                         / math.log(REL_MAX_DIST / max_exact) * (REL_BUCKETS - max_exact)).astype(jnp.int32)
    large = jnp.minimum(large, REL_BUCKETS - 1)
    return jnp.where(dist < max_exact, dist, large)


def _bias_tiles_body(rb_ref, o_ref, *, t):
    h = pl.program_id(0)
    off = pl.program_id(1) * t
    dist = off + lax.broadcasted_iota(jnp.int32, (t, t), 0) - lax.broadcasted_iota(jnp.int32, (t, t), 1)
    bucket = _t5_bucket(jnp.maximum(dist, 0))
    acc = jnp.zeros((t, t), F32)
    for b in range(REL_BUCKETS):
        acc = jnp.where(bucket == b, rb_ref[b, h], acc)
    o_ref[0, 0] = acc - rb_ref[REL_BUCKETS - 1, h]


def bias_tiles(rel_bias, t):
    return pl.pallas_call(
        functools.partial(_bias_tiles_body, t=t),
        grid=(A_HEADS, 2),
        in_specs=[pl.BlockSpec(memory_space=pltpu.SMEM)],
        out_specs=pl.BlockSpec((1, 1, t, t), lambda h, c: (h, c, 0, 0)),
        out_shape=jax.ShapeDtypeStruct((A_HEADS, 2, t, t), F32),
        compiler_params=_params("parallel", "parallel"),
        name="bias_tiles",
    )(rel_bias)


def _indexer_body(qi_ref, sm_ref, ke_ref, ko_ref, e_ref, mask_ref, key_ref, wb_ref, *, tq, n_sel):
    i = pl.program_id(1)
    nk = i + 1
    wb_ref[...] = lax.dot_general(sm_ref[0], e_ref[...], (((1,), (0,)), ((), ())),
                                  precision=HIGHEST, preferred_element_type=F32)
    qpos = i * tq + lax.broadcasted_iota(jnp.int32, (tq, tq), 0)
    kloc = lax.broadcasted_iota(jnp.int32, (tq, tq), 1)

    def score_chunk(c, carry):
        ks = pl.multiple_of(c * tq, tq)
        ke = ke_ref[0, pl.ds(ks, tq), :]
        ko = ko_ref[0, pl.ds(ks, tq), :]
        acc = jnp.zeros((tq, tq), F32)
        for j in range(IDX_HEADS // 2):
            qp = qi_ref[0, :, j * LANES:(j + 1) * LANES]
            for par, kk in ((0, ke), (1, ko)):
                h = 2 * j + par
                s = jnp.maximum(_dot_nt(qp, kk), 0.0)
                w = wb_ref[:, h * LANES:(h + 1) * LANES]
                acc = acc + s * jnp.concatenate([w] * (tq // LANES), axis=-1)
        bits = pltpu.bitcast(acc, jnp.int32)
        key = jnp.where(bits < 0, bits ^ jnp.int32(0x7FFFFFFF), bits)
        key = jnp.where(bits == jnp.int32(INT_MIN), 0, key)
        key = jnp.where(c * tq + kloc <= qpos, key, jnp.int32(INT_MIN))
        key_ref[:, pl.ds(ks, tq)] = key
        return carry

    lax.fori_loop(0, nk, score_chunk, 0)

    def count_ge(cand):
        cand_b = jnp.broadcast_to(cand, (tq, LANES))

        def body(c, cnt):
            ks = pl.multiple_of(c * tq, tq)
            kk = key_ref[:, pl.ds(ks, tq)]
            for s in range(tq // LANES):
                cnt = cnt + jnp.where(kk[:, s * LANES:(s + 1) * LANES] >= cand_b, 1, 0)
            return cnt

        cnt = lax.fori_loop(0, nk, body, jnp.zeros((tq, LANES), jnp.int32))
        return jnp.sum(cnt, axis=-1, keepdims=True)

    def bit_step(it, t):
        cand = t + lax.shift_left(jnp.int32(1), 31 - it)
        return jnp.where(count_ge(cand) >= n_sel, cand, t)

    t = lax.fori_loop(0, 32, bit_step, jnp.full((tq, 1), INT_MIN, jnp.int32))
    t = jnp.maximum(t, jnp.int32(INT_MIN + 1))
    t_b = jnp.broadcast_to(t, (tq, tq))

    def write_chunk(c, carry):
        ks = pl.multiple_of(c * tq, tq)
        kk = key_ref[:, pl.ds(ks, tq)]
        mask_ref[0, :, pl.ds(ks, tq)] = jnp.where(kk >= t_b, 1, 0).astype(mask_ref.dtype)
        return carry

    lax.fori_loop(0, nk, write_chunk, 0)

    def zero_chunk(c, carry):
        ks = pl.multiple_of(c * tq, tq)
        mask_ref[0, :, pl.ds(ks, tq)] = jnp.zeros((tq, tq), mask_ref.dtype)
        return carry

    lax.fori_loop(nk, pl.num_programs(1), zero_chunk, 0)


def indexer_mask(qi, small, k_even, k_odd, b, t, n_sel):
    tq = _tile(t, 256)
    scale = (IDX_HEADS ** -0.5) * (IDX_DIM ** -0.5)
    rows = jnp.arange(LANES)[:, None]
    cols = jnp.arange(IDX_HEADS * LANES)[None, :]
    expand = jnp.where(rows == SM_WIDX + cols // LANES, scale, 0.0).astype(F32)
    return pl.pallas_call(
        functools.partial(_indexer_body, tq=tq, n_sel=n_sel),
        grid=(b, t // tq),
        in_specs=[
            pl.BlockSpec((1, tq, IDX_HEADS * IDX_DIM), lambda bb, i: (bb, i, 0)),
            pl.BlockSpec((1, tq, LANES), lambda bb, i: (bb, i, COL_SMALL // LANES)),
            pl.BlockSpec((1, t, LANES), lambda bb, i: (bb, 0, 0)),
            pl.BlockSpec((1, t, LANES), lambda bb, i: (bb, 0, 0)),
            pl.BlockSpec(expand.shape, lambda bb, i: (0, 0)),
        ],
        out_specs=pl.BlockSpec((1, tq, t), lambda bb, i: (bb, i, 0)),
        out_shape=jax.ShapeDtypeStruct((b, t, t), jnp.int8),
        scratch_shapes=[pltpu.VMEM((tq, t), jnp.int32), pltpu.VMEM((tq, IDX_HEADS * LANES), F32)],
        compiler_params=_params("parallel", "parallel"),
        name="indexer",
    )(qi.reshape(b, t, -1), small, k_even.reshape(b, t, LANES), k_odd.reshape(b, t, LANES), expand)


def _attn_body(q_ref, k_ref, v_ref, mask_ref, bias_ref, z_ref, o_ref, m_ref, l_ref, acc_ref, *, tq):
    qi, ki = pl.program_id(1), pl.program_id(2)

    @pl.when(ki == 0)
    def _():
        m_ref[...] = jnp.full(m_ref.shape, MASK_NEG, F32)
        l_ref[...] = jnp.zeros(l_ref.shape, F32)
        acc_ref[...] = jnp.zeros(acc_ref.shape, F32)

    def tile(near):
        madd = (1.0 - mask_ref[0].astype(F32)) * MASK_NEG
        for h in range(A_HEADS):
            sl = slice(h * A_HEAD_DIM, (h + 1) * A_HEAD_DIM)
            s = _dot_nt(q_ref[0, :, sl], k_ref[0, :, sl]) + madd
            if near:
                s = s + bias_ref[h, 0]
            m_prev = m_ref[h]
            m_new = jnp.maximum(m_prev, jnp.max(s, axis=-1, keepdims=True))
            alpha = jnp.exp(m_prev - m_new)
            p = jnp.exp(s - m_new)
            l_ref[h] = alpha * l_ref[h] + jnp.sum(p, axis=-1, keepdims=True)
            acc_ref[h] = alpha * acc_ref[h] + _dot(p.astype(BF16), v_ref[0, :, sl])
            m_ref[h] = m_new

    @pl.when(ki + 1 < qi)
    def _():
        tile(False)

    @pl.when(jnp.logical_and(ki + 1 >= qi, ki <= qi))
    def _():
        tile(True)

    @pl.when(ki == qi)
    def _():
        outs = [acc_ref[h] / l_ref[h] for h in range(A_HEADS)]
        o_ref[0] = (jnp.concatenate(outs, axis=-1) * _silu(z_ref[0])).astype(o_ref.dtype)


def attention(q, k, v, mask, bias, proj3d, b, t, tq):
    nq = t // tq
    kidx = lambda bb, i, j: (bb, jnp.minimum(j, i), 0)
    return pl.pallas_call(
        functools.partial(_attn_body, tq=tq),
        grid=(b, nq, nq),
        in_specs=[
            pl.BlockSpec((1, tq, A_WIDTH), lambda bb, i, j: (bb, i, 0)),
            pl.BlockSpec((1, tq, A_WIDTH), kidx),
            pl.BlockSpec((1, tq, A_WIDTH), kidx),
            pl.BlockSpec((1, tq, tq), lambda bb, i, j: (bb, i, jnp.minimum(j, i))),
            pl.BlockSpec((A_HEADS, 1, tq, tq), lambda bb, i, j: (0, jnp.where(j >= i, 0, 1), 0, 0)),
            pl.BlockSpec((1, tq, A_WIDTH), lambda bb, i, j: (bb, i, COL_Z_A // A_WIDTH)),
        ],
        out_specs=pl.BlockSpec((1, tq, A_WIDTH), lambda bb, i, j: (bb, i, 0)),
        out_shape=jax.ShapeDtypeStruct((b, t, A_WIDTH), BF16),
        scratch_shapes=[pltpu.VMEM((A_HEADS, tq, 1), F32), pltpu.VMEM((A_HEADS, tq, 1), F32),
                        pltpu.VMEM((A_HEADS, tq, A_HEAD_DIM), F32)],
        compiler_params=_params("parallel", "parallel", "arbitrary"),
        name="dsa_attention",
    )(q.reshape(b, t, -1), k.reshape(b, t, -1), v.reshape(b, t, -1), mask, bias, proj3d)


def _gdn_prep_body(x_ref, halo_ref, sm_ref, cw_ref, alog_ref, dtb_ref, eg_ref, eb_ref,
                   q_ref, k_ref, v_ref, gcb_ref, bb_ref, grow_ref, xs_ref, *, tm):
    i = pl.program_id(1)
    hal = halo_ref[0]
    xs_ref[0:8, :] = jnp.where(i > 0, hal, jnp.zeros_like(hal))
    xs_ref[8:, :] = x_ref[0]
    y = jnp.zeros((tm, 3 * B_WIDTH), F32)
    for j in range(B_CONV):
        y = y + cw_ref[j:j + 1, :] * xs_ref[pl.ds(8 - (B_CONV - 1) + j, tm), :]
    y = _silu(y)
    for h in range(B_HEADS):
        sl = slice(h * LANES, (h + 1) * LANES)
        qh = y[:, sl]
        q_ref[0, :, sl] = qh * lax.rsqrt(jnp.sum(qh * qh, axis=-1, keepdims=True) + EPS) * (B_HEAD_DIM ** -0.5)
        kh = y[:, B_WIDTH + h * LANES:B_WIDTH + (h + 1) * LANES]
        k_ref[0, :, sl] = kh * lax.rsqrt(jnp.sum(kh * kh, axis=-1, keepdims=True) + EPS)
    v_ref[0] = y[:, 2 * B_WIDTH:]

    sm = sm_ref[0]
    xg = sm + dtb_ref[...]
    softplus = jnp.maximum(xg, 0.0) + jnp.log(1.0 + jnp.exp(-jnp.abs(xg)))
    g = -jnp.exp(alog_ref[...]) * softplus
    r = lax.broadcasted_iota(jnp.int32, (tm, tm), 0)
    c = lax.broadcasted_iota(jnp.int32, (tm, tm), 1)
    sh = int(math.log2(GDN_CHUNK))
    same_chunk = jnp.right_shift(r, sh) == jnp.right_shift(c, sh)
    tri = jnp.where(jnp.logical_and(same_chunk, c <= r), 1.0, 0.0).astype(F32)
    gc = lax.dot_general(tri, g, (((1,), (0,)), ((), ())), precision=HIGHEST, preferred_element_type=F32)
    gcb_ref[0] = lax.dot_general(gc, eg_ref[...], (((1,), (0,)), ((), ())), precision=HIGHEST,
                                 preferred_element_type=F32)
    bb_ref[0] = lax.dot_general(_sigmoid(sm), eb_ref[...], (((1,), (0,)), ((), ())), precision=HIGHEST,
                                preferred_element_type=F32)
    gct = gc.T
    for cc in range(tm // GDN_CHUNK):
        grow_ref[0, cc] = gct[SM_ALPHA:SM_ALPHA + B_HEADS, cc * GDN_CHUNK:(cc + 1) * GDN_CHUNK]


def gdn_prep(proj3d, conv_w, a_log, dt_bias, b, t):
    tm = _tile(t, 256)
    lane = jnp.arange(LANES)
    pad_row = lambda v: jnp.zeros((1, LANES), F32).at[0, SM_ALPHA:SM_ALPHA + B_HEADS].set(v)
    rows = lane[:, None]
    cols = jnp.arange(B_WIDTH)[None, :]
    e_g = (rows == SM_ALPHA + cols // LANES).astype(F32)
    e_b = (rows == SM_BETA + cols // LANES).astype(F32)
    nc = tm // GDN_CHUNK
    act = jax.ShapeDtypeStruct((b, t, B_WIDTH), F32)
    full = lambda a: pl.BlockSpec(a.shape, lambda bb, i: (0,) * a.ndim)
    consts = [conv_w, pad_row(a_log), pad_row(dt_bias), e_g, e_b]
    return pl.pallas_call(
        functools.partial(_gdn_prep_body, tm=tm),
        grid=(b, t // tm),
        in_specs=[
            pl.BlockSpec((1, tm, 3 * B_WIDTH), lambda bb, i: (bb, i, COL_QKV_B // (3 * B_WIDTH))),
            pl.BlockSpec((1, 8, 3 * B_WIDTH), lambda bb, i: (bb, jnp.maximum(i * (tm // 8) - 1, 0), 0)),
            pl.BlockSpec((1, tm, LANES), lambda bb, i: (bb, i, COL_SMALL // LANES)),
        ] + [full(c) for c in consts],
        out_specs=[pl.BlockSpec((1, tm, B_WIDTH), lambda bb, i: (bb, i, 0))] * 5
        + [pl.BlockSpec((1, nc, B_HEADS, GDN_CHUNK), lambda bb, i: (bb, i, 0, 0))],
        out_shape=[act] * 5 + [jax.ShapeDtypeStruct((b, t // GDN_CHUNK, B_HEADS, GDN_CHUNK), F32)],
        scratch_shapes=[pltpu.VMEM((tm + 8, 3 * B_WIDTH), F32)],
        compiler_params=_params("parallel", "parallel"),
        name="gdn_prep",
    )(proj3d, proj3d, proj3d, *consts)


def _gdn_body(q_ref, k_ref, v_ref, gcb_ref, bb_ref, grow_ref, z_ref, on_ref, o_ref, s_ref, *, nc):
    @pl.when(pl.program_id(1) == 0)
    def _():
        s_ref[...] = jnp.zeros(s_ref.shape, F32)

    cs = GDN_CHUNK
    ri = lax.broadcasted_iota(jnp.int32, (cs, cs), 0)
    ci = lax.broadcasted_iota(jnp.int32, (cs, cs), 1)
    lower = ci <= ri
    strict = ci < ri
    eye = jnp.where(ci == ri, 1.0, 0.0).astype(F32)
    bf = lambda a: a.astype(BF16)

    def chunk(c, carry):
        rows = pl.ds(pl.multiple_of(c * cs, cs), cs)
        for h in range(B_HEADS):
            sl = slice(h * LANES, (h + 1) * LANES)
            q = q_ref[0, rows, sl]
            k = k_ref[0, rows, sl]
            v = v_ref[0, rows, sl]
            gcb = gcb_ref[0, rows, sl]
            beta = bb_ref[0, rows, sl]
            grow = grow_ref[0, c, h:h + 1, :]
            diff = gcb[:, :cs] - grow
            decay = jnp.where(lower, jnp.exp(jnp.where(lower, diff, 0.0)), 0.0)
            eg = jnp.exp(gcb)
            glast = gcb[cs - 1:cs, :]
            kb = k * beta
            lmat = jnp.where(strict, _dot_nt(bf(kb), bf(k)) * decay, 0.0)
            n = -lmat
            tinv = eye + n
            for _ in range(int(math.log2(cs)) - 1):
                n = _dot(bf(n), bf(n))
                tinv = tinv + _dot(bf(tinv), bf(n))
            uw = _dot(bf(tinv), bf(jnp.concatenate([v * beta, kb * eg], axis=-1)))
            u, w = uw[:, :LANES], uw[:, LANES:]
            attn = jnp.where(lower, _dot_nt(bf(q), bf(k)) * decay, 0.0)
            s = s_ref[h]
            sb = bf(s)
            v_new = u - _dot(bf(w), sb)
            o = _dot(bf(q * eg), sb) + _dot(bf(attn), bf(v_new))
            s_ref[h] = s * jnp.exp(glast) + _dot_tn(bf(k * jnp.exp(glast - gcb)), bf(v_new))
            o = o * lax.rsqrt(jnp.mean(o * o, axis=-1, keepdims=True) + EPS) * on_ref[...]
            o_ref[0, rows, sl] = (o * _silu(z_ref[0, rows, sl])).astype(o_ref.dtype)
        return carry

    lax.fori_loop(0, nc, chunk, 0)


def gdn_scan(qh, kh, v, gcb, bb, grow, proj3d, o_norm, b, t):
    tt = _tile(t, 256)
    nc = tt // GDN_CHUNK
    blk = pl.BlockSpec((1, tt, B_WIDTH), lambda bb_, i: (bb_, i, 0))
    return pl.pallas_call(
        functools.partial(_gdn_body, nc=nc),
        grid=(b, t // tt),
        in_specs=[blk] * 5 + [
            pl.BlockSpec((1, nc, B_HEADS, GDN_CHUNK), lambda bb_, i: (bb_, i, 0, 0)),
            pl.BlockSpec((1, tt, B_WIDTH), lambda bb_, i: (bb_, i, COL_Z_B // B_WIDTH)),
            pl.BlockSpec((1, LANES), lambda bb_, i: (0, 0)),
        ],
        out_specs=blk,
        out_shape=jax.ShapeDtypeStruct((b, t, B_WIDTH), BF16),
        scratch_shapes=[pltpu.VMEM((B_HEADS, B_HEAD_DIM, B_HEAD_DIM), F32)],
        compiler_params=_params("parallel", "arbitrary"),
        name="gdn_scan",
    )(qh, kh, v, gcb, bb, grow, proj3d, o_norm.reshape(1, LANES))


C_HALO = 32


def _cd_mix_body(a_ref, g_ref, ah_ref, gh_ref, zc_ref, bg_ref, cg_ref, ud_ref, cgh_ref, udh_ref, zd_ref,
                 dww_ref, dwb_ref, lnw_ref, lnb_ref, dcw_ref, yc_ref, yd_ref, us_ref, ds_ref, *, tm):
    i = pl.program_id(1)
    first = i == 0
    uh = ah_ref[0] * _sigmoid(gh_ref[0])
    us_ref[0:C_HALO, :] = jnp.where(first, jnp.zeros_like(uh), uh)
    us_ref[C_HALO:, :] = a_ref[0] * _sigmoid(g_ref[0])
    u = jnp.zeros((tm, C_WIDTH), F32)
    for j in range(C_CONV):
        u = u + dww_ref[j:j + 1, :] * us_ref[pl.ds(C_HALO - (C_CONV - 1) + j, tm), :]
    u = u + dwb_ref[...]
    mu = jnp.mean(u, axis=-1, keepdims=True)
    uc = u - mu
    u = uc * lax.rsqrt(jnp.mean(uc * uc, axis=-1, keepdims=True) + EPS) * lnw_ref[...] + lnb_ref[...]
    yc_ref[0] = (_silu(u) * _silu(zc_ref[0])).astype(yc_ref.dtype)

    dh = cgh_ref[0] * udh_ref[0]
    ds_ref[0:8, :] = jnp.where(first, jnp.zeros_like(dh), dh)
    ds_ref[8:, :] = cg_ref[0] * ud_ref[0]
    d = jnp.zeros((tm, D_WIDTH), F32)
    for j in range(D_CONV):
        d = d + dcw_ref[j:j + 1, :] * ds_ref[pl.ds(8 - (D_CONV - 1) + j, tm), :]
    yd_ref[0] = (bg_ref[0] * d * _silu(zd_ref[0])).astype(yd_ref.dtype)


def cd_mix(proj3d, dw_w, dw_b, ln_w, ln_b, d_conv_w, b, t):
    tm = _tile(t, 256)
    w = C_WIDTH
    col = lambda n: pl.BlockSpec((1, tm, w), lambda bb, i, n=n: (bb, i, n))
    halo = lambda n, rows: pl.BlockSpec(
        (1, rows, w), lambda bb, i, n=n, rows=rows: (bb, jnp.maximum(i * (tm // rows) - 1, 0), n))
    row = lambda a: a.reshape(1, -1)
    full = lambda a: pl.BlockSpec(a.shape, lambda bb, i: (0,) * a.ndim)
    consts = [dw_w, row(dw_b), row(ln_w), row(ln_b), d_conv_w]
    out = jax.ShapeDtypeStruct((b, t, w), BF16)
    return pl.pallas_call(
        functools.partial(_cd_mix_body, tm=tm),
        grid=(b, t // tm),
        in_specs=[col(0), col(1), halo(0, C_HALO), halo(1, C_HALO), col(2), col(3), col(4), col(5),
                  halo(4, 8), halo(5, 8), col(6)] + [full(c) for c in consts],
        out_specs=[pl.BlockSpec((1, tm, w), lambda bb, i: (bb, i, 0))] * 2,
        out_shape=[out, out],
        scratch_shapes=[pltpu.VMEM((tm + C_HALO, w), F32), pltpu.VMEM((tm + 8, w), F32)],
        compiler_params=_params("parallel", "parallel"),
        name="cd_mix",
    )(*([proj3d] * 11), *consts)


def _reorder_ab_w_in(w):
    offs = [0]
    for s in AB_SPLITS:
        offs.append(offs[-1] + s)
    part = lambda n: w[:, offs[n]:offs[n + 1]]
    c_q, c_kv, k_idx, w_idx, z_a, qkv_b, beta_b, alpha_b, z_b = (part(n) for n in range(9))
    small = jnp.concatenate([k_idx, w_idx, beta_b, alpha_b], axis=1)
    small = jnp.pad(small, ((0, 0), (0, LANES - small.shape[1])))
    out = jnp.concatenate([qkv_b, z_a, z_b, c_q, c_kv, small], axis=1)
    return jnp.pad(out, ((0, 0), (0, AB_IN_PAD - out.shape[1]))).astype(BF16)


def _ab_layer(x2d, b, t, norm_w, rel_bias, w_in, q_norm, w_uq, w_iq, kv_norm, w_uk, w_uv, q_gain, k_gain,
              ik_w, ik_b, conv_w, a_log, dt_bias, o_norm, w_out):
    proj = norm_matmul(x2d, norm_w, _reorder_ab_w_in(w_in))
    proj3d = proj.reshape(b, t, AB_IN_PAD)
    q, k, v, qi, k_even, k_odd = dsa_prep(proj, q_norm, kv_norm, w_uq, w_iq, w_uk, w_uv, q_gain, k_gain, ik_w, ik_b)
    n_sel = min(TOPK_MAX, t // 4)
    mask = indexer_mask(qi, proj3d, k_even, k_odd, b, t, n_sel)
    tq = _tile(t, 512)
    y_a = attention(q, k, v, mask, bias_tiles(rel_bias, tq), proj3d, b, t, tq)
    qh, kh, vv, gcb, bb, grow = gdn_prep(proj3d, conv_w, a_log, dt_bias, b, t)
    y_b = gdn_scan(qh, kh, vv, gcb, bb, grow, proj3d, o_norm, b, t)
    wo = w_out.astype(BF16)
    return out_proj(x2d, y_a.reshape(b * t, -1), y_b.reshape(b * t, -1), wo[:A_WIDTH], wo[A_WIDTH:])


def _cd_layer(x2d, b, t, norm_w, w_in, dw_w, dw_b, ln_w, ln_b, d_conv_w, w_out):
    proj = norm_matmul(x2d, norm_w, w_in.astype(BF16))
    y_c, y_d = cd_mix(proj.reshape(b, t, -1), dw_w, dw_b, ln_w, ln_b, d_conv_w, b, t)
    wo = w_out.astype(BF16)
    return out_proj(x2d, y_c.reshape(b * t, -1), y_d.reshape(b * t, -1), wo[:C_WIDTH], wo[C_WIDTH:])


def kernel(x, norm_w, rel_bias, ab_w_in, a_q_norm, a_w_uq, a_w_iq, a_kv_norm, a_w_uk, a_w_uv, a_q_gain,
           a_k_gain, a_ik_norm_w, a_ik_norm_b, b_conv_w, b_a_log, b_dt_bias, b_o_norm, ab_w_out, cd_w_in,
           c_dw_w, c_dw_b, c_ln_w, c_ln_b, d_conv_w, cd_w_out):
    b, t, d = x.shape
    depth = norm_w.shape[0]
    x2d = x.reshape(b * t, d)
    for i in range(depth):
        j = i // 2
        if i % 2 == 0:
            x2d = _ab_layer(x2d, b, t, norm_w[i], rel_bias, ab_w_in[j], a_q_norm[j], a_w_uq[j], a_w_iq[j],
                            a_kv_norm[j], a_w_uk[j], a_w_uv[j], a_q_gain[j], a_k_gain[j], a_ik_norm_w[j],
                            a_ik_norm_b[j], b_conv_w[j], b_a_log[j], b_dt_bias[j], b_o_norm[j], ab_w_out[j])
        else:
            x2d = _cd_layer(x2d, b, t, norm_w[i], cd_w_in[j], c_dw_w[j], c_dw_b[j], c_ln_w[j], c_ln_b[j],
                            d_conv_w[j], cd_w_out[j])
    return x2d.reshape(b, t, d)
```

```python
import functools
import math

import jax
import jax.numpy as jnp
from jax import lax
from jax.experimental import pallas as pl
from jax.experimental.pallas import tpu as pltpu

D_MODEL = 2048
A_HEADS = 8
A_HEAD_DIM = 128
A_WIDTH = A_HEADS * A_HEAD_DIM
A_Q_LORA = 512
A_KV_LORA = 256
IDX_HEADS = 16
IDX_DIM = 64
TOPK_MAX = 256
REL_BUCKETS = 32
REL_MAX_DIST = 128
B_HEADS = 8
B_HEAD_DIM = 128
B_WIDTH = B_HEADS * B_HEAD_DIM
B_CONV = 4
GDN_CHUNK = 64
C_WIDTH = 1024
C_CONV = 31
D_WIDTH = 1024
D_CONV = 3
EPS = 1e-6

AB_SPLITS = (A_Q_LORA, A_KV_LORA, IDX_DIM, IDX_HEADS, A_WIDTH, 3 * B_WIDTH, B_HEADS, B_HEADS, B_WIDTH)
AB_IN_PAD = 6144
SM_KIDX = 0
SM_WIDX = 64
SM_BETA = 80
SM_ALPHA = 88
COL_QKV_B = 0
COL_Z_A = 3072
COL_Z_B = 4096
COL_C_Q = 5120
COL_C_KV = 5632
COL_SMALL = 5888

LANES = 128
MASK_NEG = -1e30
INT_MIN = -(2 ** 31)
VMEM_LIMIT = 56 * 1024 * 1024

F32 = jnp.float32
BF16 = jnp.bfloat16
HIGHEST = lax.Precision.HIGHEST


def _tile(n, pref):
    t = min(n, pref)
    assert n % t == 0, (n, t)
    return t


def _params(*sem):
    return pltpu.CompilerParams(dimension_semantics=sem, vmem_limit_bytes=VMEM_LIMIT)


def _dot(a, b):
    return jnp.dot(a, b, preferred_element_type=F32)


def _dot_nt(a, b):
    return lax.dot_general(a, b, (((1,), (1,)), ((), ())), preferred_element_type=F32)


def _dot_tn(a, b):
    return lax.dot_general(a, b, (((0,), (0,)), ((), ())), preferred_element_type=F32)


def _silu(x):
    return x * (1.0 / (1.0 + jnp.exp(-x)))


def _sigmoid(x):
    return 1.0 / (1.0 + jnp.exp(-x))


def _norm_matmul_body(x_ref, nw_ref, w_ref, o_ref, h_ref):
    @pl.when(pl.program_id(1) == 0)
    def _():
        x = x_ref[...]
        ms = jnp.mean(x * x, axis=-1, keepdims=True)
        h_ref[...] = (x * lax.rsqrt(ms + EPS) * nw_ref[...]).astype(h_ref.dtype)

    o_ref[...] = _dot(h_ref[...], w_ref[...]).astype(o_ref.dtype)


def norm_matmul(x2d, norm_w, w_bf16, out_dtype=F32):
    m, k = x2d.shape
    n = w_bf16.shape[1]
    tm, tn = _tile(m, 512), _tile(n, 1024)
    return pl.pallas_call(
        _norm_matmul_body,
        grid=(m // tm, n // tn),
        in_specs=[
            pl.BlockSpec((tm, k), lambda i, j: (i, 0)),
            pl.BlockSpec((1, k), lambda i, j: (0, 0)),
            pl.BlockSpec((k, tn), lambda i, j: (0, j)),
        ],
        out_specs=pl.BlockSpec((tm, tn), lambda i, j: (i, j)),
        out_shape=jax.ShapeDtypeStruct((m, n), out_dtype),
        scratch_shapes=[pltpu.VMEM((tm, k), BF16)],
        compiler_params=_params("parallel", "arbitrary"),
        name="norm_matmul",
    )(x2d, norm_w.reshape(1, k), w_bf16)


def _out_proj_body(x_ref, ya_ref, yb_ref, wa_ref, wb_ref, o_ref):
    o_ref[...] = x_ref[...] + _dot(ya_ref[...], wa_ref[...]) + _dot(yb_ref[...], wb_ref[...])


def out_proj(x2d, ya, yb, wa_bf16, wb_bf16):
    m, n = x2d.shape
    ka, kb = ya.shape[1], yb.shape[1]
    tm = _tile(m, 512)
    return pl.pallas_call(
        _out_proj_body,
        grid=(m // tm,),
        in_specs=[
            pl.BlockSpec((tm, n), lambda i: (i, 0)),
            pl.BlockSpec((tm, ka), lambda i: (i, 0)),
            pl.BlockSpec((tm, kb), lambda i: (i, 0)),
            pl.BlockSpec((ka, n), lambda i: (0, 0)),
            pl.BlockSpec((kb, n), lambda i: (0, 0)),
        ],
        out_specs=pl.BlockSpec((tm, n), lambda i: (i, 0)),
        out_shape=jax.ShapeDtypeStruct((m, n), F32),
        compiler_params=_params("parallel"),
        name="out_proj",
    )(x2d, ya, yb, wa_bf16, wb_bf16)


def _head_rmsnorm(y, gain_row, scale):
    outs = []
    for h in range(y.shape[1] // LANES):
        seg = y[:, h * LANES:(h + 1) * LANES]
        ms = jnp.mean(seg * seg, axis=-1, keepdims=True)
        outs.append(seg * lax.rsqrt(ms + EPS) * (gain_row * scale))
    return jnp.concatenate(outs, axis=-1)


def _dsa_prep_body(cq_ref, ckv_ref, sm_ref, qn_ref, kvn_ref, wuq_ref, wiq_ref, wuk_ref, wuv_ref,
                   qg_ref, kg_ref, ikw_ref, ikb_ref,
                   q_ref, k_ref, v_ref, qi_ref, ke_ref, ko_ref):
    cq = cq_ref[...]
    cq = cq * lax.rsqrt(jnp.mean(cq * cq, axis=-1, keepdims=True) + EPS) * qn_ref[...]
    cqb = cq.astype(BF16)
    q = _dot(cqb, wuq_ref[...])
    q_ref[...] = _head_rmsnorm(q, qg_ref[...], A_HEAD_DIM ** -0.5).astype(q_ref.dtype)
    qi_ref[...] = _dot(cqb, wiq_ref[...]).astype(qi_ref.dtype)

    ckv = ckv_ref[...]
    ckv = ckv * lax.rsqrt(jnp.mean(ckv * ckv, axis=-1, keepdims=True) + EPS) * kvn_ref[...]
    ckvb = ckv.astype(BF16)
    k = _dot(ckvb, wuk_ref[...])
    k_ref[...] = _head_rmsnorm(k, kg_ref[...], 1.0).astype(k_ref.dtype)
    v_ref[...] = _dot(ckvb, wuv_ref[...]).astype(v_ref.dtype)

    kx = sm_ref[...][:, SM_KIDX:SM_KIDX + IDX_DIM]
    mu = jnp.mean(kx, axis=-1, keepdims=True)
    kc = kx - mu
    kl = kc * lax.rsqrt(jnp.mean(kc * kc, axis=-1, keepdims=True) + EPS) * ikw_ref[...] + ikb_ref[...]
    z = jnp.zeros_like(kl)
    ke_ref[...] = jnp.concatenate([kl, z], axis=-1).astype(ke_ref.dtype)
    ko_ref[...] = jnp.concatenate([z, kl], axis=-1).astype(ko_ref.dtype)


def dsa_prep(proj, q_norm, kv_norm, w_uq, w_iq, w_uk, w_uv, q_gain, k_gain, ik_w, ik_b):
    m = proj.shape[0]
    tm = _tile(m, 512)
    row = lambda a: a.reshape(1, -1)
    full = lambda a: pl.BlockSpec(a.shape, lambda i: (0,) * a.ndim)
    consts = [row(q_norm), row(kv_norm), w_uq.astype(BF16), w_iq.astype(BF16), w_uk.astype(BF16),
              w_uv.astype(BF16), row(q_gain), row(k_gain), row(ik_w), row(ik_b)]
    outs = [jax.ShapeDtypeStruct((m, A_WIDTH), BF16)] * 3 + [
        jax.ShapeDtypeStruct((m, IDX_HEADS * IDX_DIM), BF16),
        jax.ShapeDtypeStruct((m, LANES), BF16), jax.ShapeDtypeStruct((m, LANES), BF16)]
    return pl.pallas_call(
        _dsa_prep_body,
        grid=(m // tm,),
        in_specs=[
            pl.BlockSpec((tm, A_Q_LORA), lambda i: (i, COL_C_Q // A_Q_LORA)),
            pl.BlockSpec((tm, A_KV_LORA), lambda i: (i, COL_C_KV // A_KV_LORA)),
            pl.BlockSpec((tm, LANES), lambda i: (i, COL_SMALL // LANES)),
        ] + [full(c) for c in consts],
        out_specs=[pl.BlockSpec((tm, s.shape[1]), lambda i: (i, 0)) for s in outs],
        out_shape=outs,
        compiler_params=_params("parallel"),
        name="dsa_prep",
    )(proj, proj, proj, *consts)


def _t5_bucket(dist):
    max_exact = REL_BUCKETS // 2
    large = max_exact + (jnp.log(jnp.maximum(dist, 1).astype(F32) / max_exact)
                         / math.log(REL_MAX_DIST / max_exact) * (REL_BUCKETS - max_exact)).astype(jnp.int32)
    large = jnp.minimum(large, REL_BUCKETS - 1)
    return jnp.where(dist < max_exact, dist, large)


def _bias_tiles_body(rb_ref, o_ref, *, t):
    h = pl.program_id(0)
    off = pl.program_id(1) * t
    dist = off + lax.broadcasted_iota(jnp.int32, (t, t), 0) - lax.broadcasted_iota(jnp.int32, (t, t), 1)
    bucket = _t5_bucket(jnp.maximum(dist, 0))
    acc = jnp.zeros((t, t), F32)
    for b in range(REL_BUCKETS):
        acc = jnp.where(bucket == b, rb_ref[b, h], acc)
    o_ref[0, 0] = acc - rb_ref[REL_BUCKETS - 1, h]


def bias_tiles(rel_bias, t):
    return pl.pallas_call(
        functools.partial(_bias_tiles_body, t=t),
        grid=(A_HEADS, 2),
        in_specs=[pl.BlockSpec(memory_space=pltpu.SMEM)],
        out_specs=pl.BlockSpec((1, 1, t, t), lambda h, c: (h, c, 0, 0)),
        out_shape=jax.ShapeDtypeStruct((A_HEADS, 2, t, t), F32),
        compiler_params=_params("parallel", "parallel"),
        name="bias_tiles",
    )(rel_bias)


SUBLANES = 8
MAX_SELECT_STEPS = 40
SELECT_STEPS_PER_CHECK = 4


def _key_of(x):
    bits = pltpu.bitcast(x, jnp.int32)
    key = jnp.where(bits < 0, bits ^ jnp.int32(0x7FFFFFFF), bits)
    return jnp.where(bits == jnp.int32(INT_MIN), 0, key)


def _indexer_body(qi_ref, sm_ref, ke_ref, ko_ref, mask_ref, key_ref, w_ref, *, tq, n_sel):
    i = pl.program_id(1)
    nk = i + 1
    groups = tq // SUBLANES
    scale = (IDX_HEADS ** -0.5) * (IDX_DIM ** -0.5)
    smt = sm_ref[0].T
    for h in range(IDX_HEADS):
        w_ref[h] = jnp.broadcast_to(smt[SM_WIDX + h:SM_WIDX + h + 1, :] * scale, (SUBLANES, tq))
    kloc = lax.broadcasted_iota(jnp.int32, (tq, tq), 0)
    qpos = i * tq + lax.broadcasted_iota(jnp.int32, (tq, tq), 1)
    int_max = jnp.int32(2 ** 31 - 1)

    def score_chunk(c, carry):
        kmin, kmax = carry
        ks = pl.multiple_of(c * tq, tq)
        ke = ke_ref[0, pl.ds(ks, tq), :]
        ko = ko_ref[0, pl.ds(ks, tq), :]
        acc = jnp.zeros((tq, tq), F32)
        for j in range(IDX_HEADS // 2):
            qp = qi_ref[0, :, j * LANES:(j + 1) * LANES]
            for par, kk in ((0, ke), (1, ko)):
                s = jnp.maximum(_dot_nt(kk, qp), 0.0)
                acc = acc + s * jnp.tile(w_ref[2 * j + par], (groups, 1))
        key = _key_of(acc)
        valid = c * tq + kloc <= qpos
        key_ref[pl.ds(ks, tq), :] = jnp.where(valid, key, jnp.int32(INT_MIN))
        kmin = jnp.minimum(kmin, jnp.min(jnp.where(valid, key, int_max).reshape(groups, SUBLANES, tq), axis=0))
        kmax = jnp.maximum(kmax, jnp.max(jnp.where(valid, key, jnp.int32(INT_MIN)).reshape(groups, SUBLANES, tq), axis=0))
        return kmin, kmax

    kmin, kmax = lax.fori_loop(0, nk, score_chunk, (jnp.full((SUBLANES, tq), int_max, jnp.int32),
                                                    jnp.full((SUBLANES, tq), INT_MIN, jnp.int32)))

    def count_ge(p):
        p8 = jnp.broadcast_to(p, (SUBLANES, tq))

        def body(c, cnt):
            ks = pl.multiple_of(c * tq, tq)
            kk = key_ref[pl.ds(ks, tq), :].reshape(groups, SUBLANES, tq)
            return cnt + jnp.sum(jnp.where(kk >= p8[None], 1, 0), axis=0)

        cnt = lax.fori_loop(0, nk, body, jnp.zeros((SUBLANES, tq), jnp.int32))
        return jnp.sum(cnt, axis=0, keepdims=True)

    n_valid = i * tq + lax.broadcasted_iota(jnp.int32, (1, tq), 1) + 1
    few = n_valid <= n_sel
    lo0 = jnp.min(kmin, axis=0, keepdims=True)
    hi0 = jnp.max(kmax, axis=0, keepdims=True) + 1

    def open_rows(lo, hi, clo):
        return jnp.logical_not(few | (clo == n_sel) | (hi == lo + 1))

    def cond(st):
        it, lo, hi, clo, chi = st
        n_open = jnp.max(jnp.where(open_rows(lo, hi, clo), 1, 0))
        return jnp.logical_and(it < MAX_SELECT_STEPS, n_open > 0)

    def step(st):
        it, lo, hi, clo, chi = st
        for _ in range(SELECT_STEPS_PER_CHECK):
            upd = open_rows(lo, hi, clo)
            p = lo + jnp.maximum(lax.shift_right_logical(hi - lo, 1), 1)
            cnt = count_ge(p)
            ge = cnt >= n_sel
            up, dn = upd & ge, upd & jnp.logical_not(ge)
            lo, hi = jnp.where(up, p, lo), jnp.where(dn, p, hi)
            clo, chi = jnp.where(up, cnt, clo), jnp.where(dn, cnt, chi)
        return it + SELECT_STEPS_PER_CHECK, lo, hi, clo, chi

    _, lo, _, _, _ = lax.while_loop(cond, step, (jnp.int32(0), lo0, hi0, n_valid, jnp.zeros((1, tq), jnp.int32)))
    thr = jnp.where(few, jnp.int32(INT_MIN + 1), lo)
    thr8 = jnp.broadcast_to(thr, (SUBLANES, tq))

    def write_chunk(c, carry):
        ks = pl.multiple_of(c * tq, tq)
        kk = key_ref[pl.ds(ks, tq), :].reshape(groups, SUBLANES, tq)
        mask_ref[0, pl.ds(ks, tq), :] = jnp.where(kk >= thr8[None], 1, 0).reshape(tq, tq).astype(mask_ref.dtype)
        return carry

    lax.fori_loop(0, nk, write_chunk, 0)

    def zero_chunk(c, carry):
        ks = pl.multiple_of(c * tq, tq)
        mask_ref[0, pl.ds(ks, tq), :] = jnp.zeros((tq, tq), mask_ref.dtype)
        return carry

    lax.fori_loop(nk, pl.num_programs(1), zero_chunk, 0)


def indexer_mask(qi, small, k_even, k_odd, b, t, n_sel):
    tq = _tile(t, 256)
    return pl.pallas_call(
        functools.partial(_indexer_body, tq=tq, n_sel=n_sel),
        grid=(b, t // tq),
        in_specs=[
            pl.BlockSpec((1, tq, IDX_HEADS * IDX_DIM), lambda bb, i: (bb, i, 0)),
            pl.BlockSpec((1, tq, LANES), lambda bb, i: (bb, i, COL_SMALL // LANES)),
            pl.BlockSpec((1, t, LANES), lambda bb, i: (bb, 0, 0)),
            pl.BlockSpec((1, t, LANES), lambda bb, i: (bb, 0, 0)),
        ],
        out_specs=pl.BlockSpec((1, t, tq), lambda bb, i: (bb, 0, i)),
        out_shape=jax.ShapeDtypeStruct((b, t, t), jnp.int8),
        scratch_shapes=[pltpu.VMEM((t, tq), jnp.int32), pltpu.VMEM((IDX_HEADS, SUBLANES, tq), F32)],
        compiler_params=_params("parallel", "parallel"),
        name="indexer",
    )(qi.reshape(b, t, -1), small, k_even.reshape(b, t, LANES), k_odd.reshape(b, t, LANES))


def _attn_body(q_ref, k_ref, v_ref, mask_ref, bias_ref, z_ref, o_ref, m_ref, l_ref, acc_ref, *, tq):
    qi, ki = pl.program_id(1), pl.program_id(2)

    @pl.when(ki == 0)
    def _():
        m_ref[...] = jnp.full(m_ref.shape, MASK_NEG, F32)
        l_ref[...] = jnp.zeros(l_ref.shape, F32)
        acc_ref[...] = jnp.zeros(acc_ref.shape, F32)

    def tile(near):
        madd = ((1.0 - mask_ref[0].astype(F32)) * MASK_NEG).T
        for h in range(A_HEADS):
            sl = slice(h * A_HEAD_DIM, (h + 1) * A_HEAD_DIM)
            s = _dot_nt(q_ref[0, :, sl], k_ref[0, :, sl]) + madd
            if near:
                s = s + bias_ref[h, 0]
            m_prev = m_ref[h]
            m_new = jnp.maximum(m_prev, jnp.max(s, axis=-1, keepdims=True))
            alpha = jnp.exp(m_prev - m_new)
            p = jnp.exp(s - m_new)
            l_ref[h] = alpha * l_ref[h] + jnp.sum(p, axis=-1, keepdims=True)
            acc_ref[h] = alpha * acc_ref[h] + _dot(p.astype(BF16), v_ref[0, :, sl])
            m_ref[h] = m_new

    @pl.when(ki + 1 < qi)
    def _():
        tile(False)

    @pl.when(jnp.logical_and(ki + 1 >= qi, ki <= qi))
    def _():
        tile(True)

    @pl.when(ki == qi)
    def _():
        outs = [acc_ref[h] / l_ref[h] for h in range(A_HEADS)]
        o_ref[0] = (jnp.concatenate(outs, axis=-1) * _silu(z_ref[0])).astype(o_ref.dtype)


def attention(q, k, v, mask, bias, proj3d, b, t, tq):
    nq = t // tq
    kidx = lambda bb, i, j: (bb, jnp.minimum(j, i), 0)
    return pl.pallas_call(
        functools.partial(_attn_body, tq=tq),
        grid=(b, nq, nq),
        in_specs=[
            pl.BlockSpec((1, tq, A_WIDTH), lambda bb, i, j: (bb, i, 0)),
            pl.BlockSpec((1, tq, A_WIDTH), kidx),
            pl.BlockSpec((1, tq, A_WIDTH), kidx),
            pl.BlockSpec((1, tq, tq), lambda bb, i, j: (bb, jnp.minimum(j, i), i)),
            pl.BlockSpec((A_HEADS, 1, tq, tq), lambda bb, i, j: (0, jnp.where(j >= i, 0, 1), 0, 0)),
            pl.BlockSpec((1, tq, A_WIDTH), lambda bb, i, j: (bb, i, COL_Z_A // A_WIDTH)),
        ],
        out_specs=pl.BlockSpec((1, tq, A_WIDTH), lambda bb, i, j: (bb, i, 0)),
        out_shape=jax.ShapeDtypeStruct((b, t, A_WIDTH), BF16),
        scratch_shapes=[pltpu.VMEM((A_HEADS, tq, 1), F32), pltpu.VMEM((A_HEADS, tq, 1), F32),
                        pltpu.VMEM((A_HEADS, tq, A_HEAD_DIM), F32)],
        compiler_params=_params("parallel", "parallel", "arbitrary"),
        name="dsa_attention",
    )(q.reshape(b, t, -1), k.reshape(b, t, -1), v.reshape(b, t, -1), mask, bias, proj3d)


def _gdn_prep_body(x_ref, halo_ref, sm_ref, cw_ref, alog_ref, dtb_ref, eg_ref, eb_ref,
                   q_ref, k_ref, v_ref, gcb_ref, bb_ref, grow_ref, xs_ref, *, tm):
    i = pl.program_id(1)
    hal = halo_ref[0]
    xs_ref[0:8, :] = jnp.where(i > 0, hal, jnp.zeros_like(hal))
    xs_ref[8:, :] = x_ref[0]
    y = jnp.zeros((tm, 3 * B_WIDTH), F32)
    for j in range(B_CONV):
        y = y + cw_ref[j:j + 1, :] * xs_ref[pl.ds(8 - (B_CONV - 1) + j, tm), :]
    y = _silu(y)
    for h in range(B_HEADS):
        sl = slice(h * LANES, (h + 1) * LANES)
        qh = y[:, sl]
        q_ref[0, :, sl] = qh * lax.rsqrt(jnp.sum(qh * qh, axis=-1, keepdims=True) + EPS) * (B_HEAD_DIM ** -0.5)
        kh = y[:, B_WIDTH + h * LANES:B_WIDTH + (h + 1) * LANES]
        k_ref[0, :, sl] = kh * lax.rsqrt(jnp.sum(kh * kh, axis=-1, keepdims=True) + EPS)
    v_ref[0] = y[:, 2 * B_WIDTH:]

    sm = sm_ref[0]
    xg = sm + dtb_ref[...]
    softplus = jnp.maximum(xg, 0.0) + jnp.log(1.0 + jnp.exp(-jnp.abs(xg)))
    g = -jnp.exp(alog_ref[...]) * softplus
    r = lax.broadcasted_iota(jnp.int32, (tm, tm), 0)
    c = lax.broadcasted_iota(jnp.int32, (tm, tm), 1)
    sh = int(math.log2(GDN_CHUNK))
    same_chunk = jnp.right_shift(r, sh) == jnp.right_shift(c, sh)
    tri = jnp.where(jnp.logical_and(same_chunk, c <= r), 1.0, 0.0).astype(F32)
    gc = lax.dot_general(tri, g, (((1,), (0,)), ((), ())), precision=HIGHEST, preferred_element_type=F32)
    gcb_ref[0] = lax.dot_general(gc, eg_ref[...], (((1,), (0,)), ((), ())), precision=HIGHEST,
                                 preferred_element_type=F32)
    bb_ref[0] = lax.dot_general(_sigmoid(sm), eb_ref[...], (((1,), (0,)), ((), ())), precision=HIGHEST,
                                preferred_element_type=F32)
    gct = gc.T
    for cc in range(tm // GDN_CHUNK):
        grow_ref[0, cc] = gct[SM_ALPHA:SM_ALPHA + B_HEADS, cc * GDN_CHUNK:(cc + 1) * GDN_CHUNK]


def gdn_prep(proj3d, conv_w, a_log, dt_bias, b, t):
    tm = _tile(t, 256)
    lane = jnp.arange(LANES)
    pad_row = lambda v: jnp.zeros((1, LANES), F32).at[0, SM_ALPHA:SM_ALPHA + B_HEADS].set(v)
    rows = lane[:, None]
    cols = jnp.arange(B_WIDTH)[None, :]
    e_g = (rows == SM_ALPHA + cols // LANES).astype(F32)
    e_b = (rows == SM_BETA + cols // LANES).astype(F32)
    nc = tm // GDN_CHUNK
    act = jax.ShapeDtypeStruct((b, t, B_WIDTH), F32)
    full = lambda a: pl.BlockSpec(a.shape, lambda bb, i: (0,) * a.ndim)
    consts = [conv_w, pad_row(a_log), pad_row(dt_bias), e_g, e_b]
    return pl.pallas_call(
        functools.partial(_gdn_prep_body, tm=tm),
        grid=(b, t // tm),
        in_specs=[
            pl.BlockSpec((1, tm, 3 * B_WIDTH), lambda bb, i: (bb, i, COL_QKV_B // (3 * B_WIDTH))),
            pl.BlockSpec((1, 8, 3 * B_WIDTH), lambda bb, i: (bb, jnp.maximum(i * (tm // 8) - 1, 0), 0)),
            pl.BlockSpec((1, tm, LANES), lambda bb, i: (bb, i, COL_SMALL // LANES)),
        ] + [full(c) for c in consts],
        out_specs=[pl.BlockSpec((1, tm, B_WIDTH), lambda bb, i: (bb, i, 0))] * 5
        + [pl.BlockSpec((1, nc, B_HEADS, GDN_CHUNK), lambda bb, i: (bb, i, 0, 0))],
        out_shape=[act] * 5 + [jax.ShapeDtypeStruct((b, t // GDN_CHUNK, B_HEADS, GDN_CHUNK), F32)],
        scratch_shapes=[pltpu.VMEM((tm + 8, 3 * B_WIDTH), F32)],
        compiler_params=_params("parallel", "parallel"),
        name="gdn_prep",
    )(proj3d, proj3d, proj3d, *consts)


def _gdn_body(q_ref, k_ref, v_ref, gcb_ref, bb_ref, grow_ref, z_ref, on_ref, o_ref, s_ref, *, nc):
    @pl.when(pl.program_id(1) == 0)
    def _():
        s_ref[...] = jnp.zeros(s_ref.shape, F32)

    cs, nh = GDN_CHUNK, B_HEADS
    nb = nc * nh
    ri = lax.broadcasted_iota(jnp.int32, (nb, cs, cs), 1)
    ci = lax.broadcasted_iota(jnp.int32, (nb, cs, cs), 2)
    lower = ci <= ri
    strict = ci < ri
    eye = jnp.where(ci == ri, 1.0, 0.0).astype(F32)
    bf = lambda a: a.astype(BF16)
    bmm = lambda a, b: lax.dot_general(a, b, (((2,), (1,)), ((0,), (0,))), preferred_element_type=F32)
    bmm_nt = lambda a, b: lax.dot_general(a, b, (((2,), (2,)), ((0,), (0,))), preferred_element_type=F32)
    bmm_tn = lambda a, b: lax.dot_general(a, b, (((1,), (1,)), ((0,), (0,))), preferred_element_type=F32)

    def stack(ref):
        return jnp.stack([ref[0, c * cs:(c + 1) * cs, h * LANES:(h + 1) * LANES]
                          for c in range(nc) for h in range(nh)])

    q, k, v = stack(q_ref), stack(k_ref), stack(v_ref)
    gcb = stack(gcb_ref)
    beta = stack(bb_ref)
    grow = jnp.stack([grow_ref[0, c, h:h + 1, :] for c in range(nc) for h in range(nh)])
    diff = gcb[:, :, :cs] - grow
    decay = jnp.where(lower, jnp.exp(jnp.where(lower, diff, 0.0)), 0.0)
    eg = jnp.exp(gcb)
    glast = gcb[:, cs - 1:cs, :]
    kb = k * beta
    lmat = jnp.where(strict, bmm_nt(bf(kb), bf(k)) * decay, 0.0)
    n = -lmat
    tinv = eye + n
    for _ in range(int(math.log2(cs)) - 1):
        n = bmm(bf(n), bf(n))
        tinv = tinv + bmm(bf(tinv), bf(n))
    uw = bmm(bf(tinv), bf(jnp.concatenate([v * beta, kb * eg], axis=-1)))
    attn = bf(jnp.where(lower, bmm_nt(bf(q), bf(k)) * decay, 0.0))
    qg = bf(q * eg)
    kdec = bf(k * jnp.exp(glast - gcb))
    egl = jnp.exp(glast)

    s = s_ref[...]
    for c in range(nc):
        sl = slice(c * nh, (c + 1) * nh)
        sb = bf(s)
        v_new = uw[sl, :, :LANES] - bmm(bf(uw[sl, :, LANES:]), sb)
        o = bmm(qg[sl], sb) + bmm(attn[sl], bf(v_new))
        s = s * egl[sl] + bmm_tn(kdec[sl], bf(v_new))
        o = o * lax.rsqrt(jnp.mean(o * o, axis=-1, keepdims=True) + EPS) * on_ref[...]
        for h in range(nh):
            rows, cols = slice(c * cs, (c + 1) * cs), slice(h * LANES, (h + 1) * LANES)
            o_ref[0, rows, cols] = (o[h] * _silu(z_ref[0, rows, cols])).astype(o_ref.dtype)
    s_ref[...] = s


def gdn_scan(qh, kh, v, gcb, bb, grow, proj3d, o_norm, b, t):
    tt = _tile(t, 256)
    nc = tt // GDN_CHUNK
    blk = pl.BlockSpec((1, tt, B_WIDTH), lambda bb_, i: (bb_, i, 0))
    return pl.pallas_call(
        functools.partial(_gdn_body, nc=nc),
        grid=(b, t // tt),
        in_specs=[blk] * 5 + [
            pl.BlockSpec((1, nc, B_HEADS, GDN_CHUNK), lambda bb_, i: (bb_, i, 0, 0)),
            pl.BlockSpec((1, tt, B_WIDTH), lambda bb_, i: (bb_, i, COL_Z_B // B_WIDTH)),
            pl.BlockSpec((1, LANES), lambda bb_, i: (0, 0)),
        ],
        out_specs=blk,
        out_shape=jax.ShapeDtypeStruct((b, t, B_WIDTH), BF16),
        scratch_shapes=[pltpu.VMEM((B_HEADS, B_HEAD_DIM, B_HEAD_DIM), F32)],
        compiler_params=_params("parallel", "arbitrary"),
        name="gdn_scan",
    )(qh, kh, v, gcb, bb, grow, proj3d, o_norm.reshape(1, LANES))


C_HALO = 32


def _cd_mix_body(a_ref, g_ref, ah_ref, gh_ref, zc_ref, bg_ref, cg_ref, ud_ref, cgh_ref, udh_ref, zd_ref,
                 dww_ref, dwb_ref, lnw_ref, lnb_ref, dcw_ref, yc_ref, yd_ref, us_ref, ds_ref, *, tm):
    i = pl.program_id(1)
    first = i == 0
    uh = ah_ref[0] * _sigmoid(gh_ref[0])
    us_ref[0:C_HALO, :] = jnp.where(first, jnp.zeros_like(uh), uh)
    us_ref[C_HALO:, :] = a_ref[0] * _sigmoid(g_ref[0])
    u = jnp.zeros((tm, C_WIDTH), F32)
    for j in range(C_CONV):
        u = u + dww_ref[j:j + 1, :] * us_ref[pl.ds(C_HALO - (C_CONV - 1) + j, tm), :]
    u = u + dwb_ref[...]
    mu = jnp.mean(u, axis=-1, keepdims=True)
    uc = u - mu
    u = uc * lax.rsqrt(jnp.mean(uc * uc, axis=-1, keepdims=True) + EPS) * lnw_ref[...] + lnb_ref[...]
    yc_ref[0] = (_silu(u) * _silu(zc_ref[0])).astype(yc_ref.dtype)

    dh = cgh_ref[0] * udh_ref[0]
    ds_ref[0:8, :] = jnp.where(first, jnp.zeros_like(dh), dh)
    ds_ref[8:, :] = cg_ref[0] * ud_ref[0]
    d = jnp.zeros((tm, D_WIDTH), F32)
    for j in range(D_CONV):
        d = d + dcw_ref[j:j + 1, :] * ds_ref[pl.ds(8 - (D_CONV - 1) + j, tm), :]
    yd_ref[0] = (bg_ref[0] * d * _silu(zd_ref[0])).astype(yd_ref.dtype)


def cd_mix(proj3d, dw_w, dw_b, ln_w, ln_b, d_conv_w, b, t):
    tm = _tile(t, 256)
    w = C_WIDTH
    col = lambda n: pl.BlockSpec((1, tm, w), lambda bb, i, n=n: (bb, i, n))
    halo = lambda n, rows: pl.BlockSpec(
        (1, rows, w), lambda bb, i, n=n, rows=rows: (bb, jnp.maximum(i * (tm // rows) - 1, 0), n))
    row = lambda a: a.reshape(1, -1)
    full = lambda a: pl.BlockSpec(a.shape, lambda bb, i: (0,) * a.ndim)
    consts = [dw_w, row(dw_b), row(ln_w), row(ln_b), d_conv_w]
    out = jax.ShapeDtypeStruct((b, t, w), BF16)
    return pl.pallas_call(
        functools.partial(_cd_mix_body, tm=tm),
        grid=(b, t // tm),
        in_specs=[col(0), col(1), halo(0, C_HALO), halo(1, C_HALO), col(2), col(3), col(4), col(5),
                  halo(4, 8), halo(5, 8), col(6)] + [full(c) for c in consts],
        out_specs=[pl.BlockSpec((1, tm, w), lambda bb, i: (bb, i, 0))] * 2,
        out_shape=[out, out],
        scratch_shapes=[pltpu.VMEM((tm + C_HALO, w), F32), pltpu.VMEM((tm + 8, w), F32)],
        compiler_params=_params("parallel", "parallel"),
        name="cd_mix",
    )(*([proj3d] * 11), *consts)


def _reorder_ab_w_in(w):
    offs = [0]
    for s in AB_SPLITS:
        offs.append(offs[-1] + s)
    part = lambda n: w[:, offs[n]:offs[n + 1]]
    c_q, c_kv, k_idx, w_idx, z_a, qkv_b, beta_b, alpha_b, z_b = (part(n) for n in range(9))
    small = jnp.concatenate([k_idx, w_idx, beta_b, alpha_b], axis=1)
    small = jnp.pad(small, ((0, 0), (0, LANES - small.shape[1])))
    out = jnp.concatenate([qkv_b, z_a, z_b, c_q, c_kv, small], axis=1)
    return jnp.pad(out, ((0, 0), (0, AB_IN_PAD - out.shape[1]))).astype(BF16)


def _ab_layer(x2d, b, t, norm_w, rel_bias, w_in, q_norm, w_uq, w_iq, kv_norm, w_uk, w_uv, q_gain, k_gain,
              ik_w, ik_b, conv_w, a_log, dt_bias, o_norm, w_out):
    proj = norm_matmul(x2d, norm_w, _reorder_ab_w_in(w_in))
    proj3d = proj.reshape(b, t, AB_IN_PAD)
    q, k, v, qi, k_even, k_odd = dsa_prep(proj, q_norm, kv_norm, w_uq, w_iq, w_uk, w_uv, q_gain, k_gain, ik_w, ik_b)
    n_sel = min(TOPK_MAX, t // 4)
    mask = indexer_mask(qi, proj3d, k_even, k_odd, b, t, n_sel)
    tq = _tile(t, 512)
    y_a = attention(q, k, v, mask, bias_tiles(rel_bias, tq), proj3d, b, t, tq)
    qh, kh, vv, gcb, bb, grow = gdn_prep(proj3d, conv_w, a_log, dt_bias, b, t)
    y_b = gdn_scan(qh, kh, vv, gcb, bb, grow, proj3d, o_norm, b, t)
    wo = w_out.astype(BF16)
    return out_proj(x2d, y_a.reshape(b * t, -1), y_b.reshape(b * t, -1), wo[:A_WIDTH], wo[A_WIDTH:])


def _cd_layer(x2d, b, t, norm_w, w_in, dw_w, dw_b, ln_w, ln_b, d_conv_w, w_out):
    proj = norm_matmul(x2d, norm_w, w_in.astype(BF16))
    y_c, y_d = cd_mix(proj.reshape(b, t, -1), dw_w, dw_b, ln_w, ln_b, d_conv_w, b, t)
    wo = w_out.astype(BF16)
    return out_proj(x2d, y_c.reshape(b * t, -1), y_d.reshape(b * t, -1), wo[:C_WIDTH], wo[C_WIDTH:])


def kernel(x, norm_w, rel_bias, ab_w_in, a_q_norm, a_w_uq, a_w_iq, a_kv_norm, a_w_uk, a_w_uv, a_q_gain,
           a_k_gain, a_ik_norm_w, a_ik_norm_b, b_conv_w, b_a_log, b_dt_bias, b_o_norm, ab_w_out, cd_w_in,
           c_dw_w, c_dw_b, c_ln_w, c_ln_b, d_conv_w, cd_w_out):
    b, t, d = x.shape
    depth = norm_w.shape[0]
    x2d = x.reshape(b * t, d)
    for i in range(depth):
        j = i // 2
        if i % 2 == 0:
            x2d = _ab_layer(x2d, b, t, norm_w[i], rel_bias, ab_w_in[j], a_q_norm[j], a_w_uq[j], a_w_iq[j],
                            a_kv_norm[j], a_w_uk[j], a_w_uv[j], a_q_gain[j], a_k_gain[j], a_ik_norm_w[j],
                            a_ik_norm_b[j], b_conv_w[j], b_a_log[j], b_dt_bias[j], b_o_norm[j], ab_w_out[j])
        else:
            x2d = _cd_layer(x2d, b, t, norm_w[i], cd_w_in[j], c_dw_w[j], c_dw_b[j], c_ln_w[j], c_ln_b[j],
                            d_conv_w[j], cd_w_out[j])
    return x2d.reshape(b, t, d)
```

```python
import functools
import math

import jax
import jax.numpy as jnp
from jax import lax
from jax.experimental import pallas as pl
from jax.experimental.pallas import tpu as pltpu

D_MODEL = 2048
A_HEADS = 8
A_HEAD_DIM = 128
A_WIDTH = A_HEADS * A_HEAD_DIM
A_Q_LORA = 512
A_KV_LORA = 256
IDX_HEADS = 16
IDX_DIM = 64
TOPK_MAX = 256
REL_BUCKETS = 32
REL_MAX_DIST = 128
B_HEADS = 8
B_HEAD_DIM = 128
B_WIDTH = B_HEADS * B_HEAD_DIM
B_CONV = 4
GDN_CHUNK = 64
C_WIDTH = 1024
C_CONV = 31
D_WIDTH = 1024
D_CONV = 3
EPS = 1e-6

AB_SPLITS = (A_Q_LORA, A_KV_LORA, IDX_DIM, IDX_HEADS, A_WIDTH, 3 * B_WIDTH, B_HEADS, B_HEADS, B_WIDTH)
AB_IN_PAD = 6144
SM_KIDX = 0
SM_WIDX = 64
SM_BETA = 80
SM_ALPHA = 88
COL_QKV_B = 0
COL_Z_A = 3072
COL_Z_B = 4096
COL_C_Q = 5120
COL_C_KV = 5632
COL_SMALL = 5888

LANES = 128
MASK_NEG = -1e30
INT_MIN = -(2 ** 31)
LOG2E = math.log2(math.e)
BF16_ROWS = 16
V_ROWS = A_HEAD_DIM + BF16_ROWS
VMEM_LIMIT = 56 * 1024 * 1024

F32 = jnp.float32
BF16 = jnp.bfloat16
HIGHEST = lax.Precision.HIGHEST


def _tile(n, pref):
    t = min(n, pref)
    assert n % t == 0, (n, t)
    return t


def _params(*sem):
    return pltpu.CompilerParams(dimension_semantics=sem, vmem_limit_bytes=VMEM_LIMIT)


def _dot(a, b):
    return jnp.dot(a, b, preferred_element_type=F32)


def _dot_nt(a, b):
    return lax.dot_general(a, b, (((1,), (1,)), ((), ())), preferred_element_type=F32)


def _dot_tn(a, b):
    return lax.dot_general(a, b, (((0,), (0,)), ((), ())), preferred_element_type=F32)


def _silu(x):
    return x * (1.0 / (1.0 + jnp.exp(-x)))


def _sigmoid(x):
    return 1.0 / (1.0 + jnp.exp(-x))


def _norm_matmul_body(x_ref, nw_ref, w_ref, o_ref, h_ref):
    @pl.when(pl.program_id(1) == 0)
    def _():
        x = x_ref[...]
        ms = jnp.mean(x * x, axis=-1, keepdims=True)
        h_ref[...] = (x * lax.rsqrt(ms + EPS) * nw_ref[...]).astype(h_ref.dtype)

    o_ref[...] = _dot(h_ref[...], w_ref[...]).astype(o_ref.dtype)


def norm_matmul(x2d, norm_w, w_bf16, out_dtype=F32):
    m, k = x2d.shape
    n = w_bf16.shape[1]
    tm, tn = _tile(m, 512), _tile(n, 1024)
    return pl.pallas_call(
        _norm_matmul_body,
        grid=(m // tm, n // tn),
        in_specs=[
            pl.BlockSpec((tm, k), lambda i, j: (i, 0)),
            pl.BlockSpec((1, k), lambda i, j: (0, 0)),
            pl.BlockSpec((k, tn), lambda i, j: (0, j)),
        ],
        out_specs=pl.BlockSpec((tm, tn), lambda i, j: (i, j)),
        out_shape=jax.ShapeDtypeStruct((m, n), out_dtype),
        scratch_shapes=[pltpu.VMEM((tm, k), BF16)],
        compiler_params=_params("parallel", "arbitrary"),
        name="norm_matmul",
    )(x2d, norm_w.reshape(1, k), w_bf16)


def _out_proj_body(x_ref, ya_ref, yb_ref, wa_ref, wb_ref, o_ref):
    o_ref[...] = x_ref[...] + _dot(ya_ref[...], wa_ref[...]) + _dot(yb_ref[...], wb_ref[...])


def out_proj(x2d, ya, yb, wa_bf16, wb_bf16):
    m, n = x2d.shape
    ka, kb = ya.shape[1], yb.shape[1]
    tm = _tile(m, 512)
    return pl.pallas_call(
        _out_proj_body,
        grid=(m // tm,),
        in_specs=[
            pl.BlockSpec((tm, n), lambda i: (i, 0)),
            pl.BlockSpec((tm, ka), lambda i: (i, 0)),
            pl.BlockSpec((tm, kb), lambda i: (i, 0)),
            pl.BlockSpec((ka, n), lambda i: (0, 0)),
            pl.BlockSpec((kb, n), lambda i: (0, 0)),
        ],
        out_specs=pl.BlockSpec((tm, n), lambda i: (i, 0)),
        out_shape=jax.ShapeDtypeStruct((m, n), F32),
        compiler_params=_params("parallel"),
        name="out_proj",
    )(x2d, ya, yb, wa_bf16, wb_bf16)


def _head_rmsnorm(y, gain_row, scale):
    outs = []
    for h in range(y.shape[1] // LANES):
        seg = y[:, h * LANES:(h + 1) * LANES]
        ms = jnp.mean(seg * seg, axis=-1, keepdims=True)
        outs.append(seg * lax.rsqrt(ms + EPS) * (gain_row * scale))
    return jnp.concatenate(outs, axis=-1)


def _dsa_prep_body(cq_ref, ckv_ref, sm_ref, qn_ref, kvn_ref, wuq_ref, wiq_ref, wuk_ref, wuv_ref,
                   qg_ref, kg_ref, ikw_ref, ikb_ref,
                   q_ref, k_ref, v_ref, qi_ref, ke_ref, ko_ref):
    cq = cq_ref[...]
    cq = cq * lax.rsqrt(jnp.mean(cq * cq, axis=-1, keepdims=True) + EPS) * qn_ref[...]
    cqb = cq.astype(BF16)
    q = _dot(cqb, wuq_ref[...])
    q_ref[...] = _head_rmsnorm(q, qg_ref[...], A_HEAD_DIM ** -0.5 * LOG2E).astype(q_ref.dtype)
    qi_ref[...] = _dot(cqb, wiq_ref[...]).astype(qi_ref.dtype)

    ckv = ckv_ref[...]
    ckv = ckv * lax.rsqrt(jnp.mean(ckv * ckv, axis=-1, keepdims=True) + EPS) * kvn_ref[...]
    ckvb = ckv.astype(BF16)
    k = _dot(ckvb, wuk_ref[...])
    k_ref[...] = _head_rmsnorm(k, kg_ref[...], 1.0).astype(k_ref.dtype)
    vt = _dot_nt(wuv_ref[...], ckvb).astype(v_ref.dtype)
    ones = jnp.ones((BF16_ROWS, vt.shape[1]), v_ref.dtype)
    for h in range(A_HEADS):
        v_ref[h * V_ROWS:h * V_ROWS + A_HEAD_DIM, :] = vt[h * A_HEAD_DIM:(h + 1) * A_HEAD_DIM, :]
        v_ref[h * V_ROWS + A_HEAD_DIM:(h + 1) * V_ROWS, :] = ones

    kx = sm_ref[...][:, SM_KIDX:SM_KIDX + IDX_DIM]
    mu = jnp.mean(kx, axis=-1, keepdims=True)
    kc = kx - mu
    kl = kc * lax.rsqrt(jnp.mean(kc * kc, axis=-1, keepdims=True) + EPS) * ikw_ref[...] + ikb_ref[...]
    z = jnp.zeros_like(kl)
    ke_ref[...] = jnp.concatenate([kl, z], axis=-1).astype(ke_ref.dtype)
    ko_ref[...] = jnp.concatenate([z, kl], axis=-1).astype(ko_ref.dtype)


def dsa_prep(proj, q_norm, kv_norm, w_uq, w_iq, w_uk, w_uv, q_gain, k_gain, ik_w, ik_b):
    m = proj.shape[0]
    tm = _tile(m, 512)
    row = lambda a: a.reshape(1, -1)
    full = lambda a: pl.BlockSpec(a.shape, lambda i: (0,) * a.ndim)
    consts = [row(q_norm), row(kv_norm), w_uq.astype(BF16), w_iq.astype(BF16), w_uk.astype(BF16),
              w_uv.T.astype(BF16), row(q_gain), row(k_gain), row(ik_w), row(ik_b)]
    outs = [jax.ShapeDtypeStruct((m, A_WIDTH), BF16)] * 2 + [
        jax.ShapeDtypeStruct((A_HEADS * V_ROWS, m), BF16),
        jax.ShapeDtypeStruct((m, IDX_HEADS * IDX_DIM), BF16),
        jax.ShapeDtypeStruct((m, LANES), BF16), jax.ShapeDtypeStruct((m, LANES), BF16)]
    out_specs = [pl.BlockSpec((tm, s.shape[1]), lambda i: (i, 0)) for s in outs]
    out_specs[2] = pl.BlockSpec((A_HEADS * V_ROWS, tm), lambda i: (0, i))
    return pl.pallas_call(
        _dsa_prep_body,
        grid=(m // tm,),
        in_specs=[
            pl.BlockSpec((tm, A_Q_LORA), lambda i: (i, COL_C_Q // A_Q_LORA)),
            pl.BlockSpec((tm, A_KV_LORA), lambda i: (i, COL_C_KV // A_KV_LORA)),
            pl.BlockSpec((tm, LANES), lambda i: (i, COL_SMALL // LANES)),
        ] + [full(c) for c in consts],
        out_specs=out_specs,
        out_shape=outs,
        compiler_params=_params("parallel"),
        name="dsa_prep",
    )(proj, proj, proj, *consts)


def _t5_bucket(dist):
    max_exact = REL_BUCKETS // 2
    large = max_exact + (jnp.log(jnp.maximum(dist, 1).astype(F32) / max_exact)
                         / math.log(REL_MAX_DIST / max_exact) * (REL_BUCKETS - max_exact)).astype(jnp.int32)
    large = jnp.minimum(large, REL_BUCKETS - 1)
    return jnp.where(dist < max_exact, dist, large)


def _bias_tiles_body(rb_ref, o_ref, *, t):
    h = pl.program_id(0)
    off = pl.program_id(1) * t
    dist = off + lax.broadcasted_iota(jnp.int32, (t, t), 1) - lax.broadcasted_iota(jnp.int32, (t, t), 0)
    bucket = _t5_bucket(jnp.maximum(dist, 0))
    acc = jnp.zeros((t, t), F32)
    for b in range(REL_BUCKETS):
        acc = jnp.where(bucket == b, rb_ref[b, h], acc)
    o_ref[0, 0] = (acc - rb_ref[REL_BUCKETS - 1, h]) * LOG2E


def bias_tiles(rel_bias, t):
    return pl.pallas_call(
        functools.partial(_bias_tiles_body, t=t),
        grid=(A_HEADS, 2),
        in_specs=[pl.BlockSpec(memory_space=pltpu.SMEM)],
        out_specs=pl.BlockSpec((1, 1, t, t), lambda h, c: (h, c, 0, 0)),
        out_shape=jax.ShapeDtypeStruct((A_HEADS, 2, t, t), F32),
        compiler_params=_params("parallel", "parallel"),
        name="bias_tiles",
    )(rel_bias)


SUBLANES = 8
MAX_SELECT_STEPS = 40
SELECT_STEPS_PER_CHECK = 4


def _key_of(x):
    bits = pltpu.bitcast(x, jnp.int32)
    key = jnp.where(bits < 0, bits ^ jnp.int32(0x7FFFFFFF), bits)
    return jnp.where(bits == jnp.int32(INT_MIN), 0, key)


def _indexer_body(qi_ref, sm_ref, ke_ref, ko_ref, mask_ref, key_ref, w_ref, *, tq, n_sel):
    i = pl.program_id(1)
    nk = i + 1
    groups = tq // SUBLANES
    scale = (IDX_HEADS ** -0.5) * (IDX_DIM ** -0.5)
    smt = sm_ref[0].T
    for h in range(IDX_HEADS):
        w_ref[h] = jnp.broadcast_to(smt[SM_WIDX + h:SM_WIDX + h + 1, :] * scale, (SUBLANES, tq))
    kloc = lax.broadcasted_iota(jnp.int32, (tq, tq), 0)
    qpos = i * tq + lax.broadcasted_iota(jnp.int32, (tq, tq), 1)
    int_max = jnp.int32(2 ** 31 - 1)

    def score_chunk(c, carry):
        kmin, kmax = carry
        ks = pl.multiple_of(c * tq, tq)
        ke = ke_ref[0, pl.ds(ks, tq), :]
        ko = ko_ref[0, pl.ds(ks, tq), :]
        acc = jnp.zeros((tq, tq), F32)
        for j in range(IDX_HEADS // 2):
            qp = qi_ref[0, :, j * LANES:(j + 1) * LANES]
            for par, kk in ((0, ke), (1, ko)):
                s = jnp.maximum(_dot_nt(kk, qp), 0.0)
                acc = acc + s * jnp.tile(w_ref[2 * j + par], (groups, 1))
        key = _key_of(acc)
        valid = c * tq + kloc <= qpos
        key_ref[pl.ds(ks, tq), :] = jnp.where(valid, key, jnp.int32(INT_MIN))
        kmin = jnp.minimum(kmin, jnp.min(jnp.where(valid, key, int_max).reshape(groups, SUBLANES, tq), axis=0))
        kmax = jnp.maximum(kmax, jnp.max(jnp.where(valid, key, jnp.int32(INT_MIN)).reshape(groups, SUBLANES, tq), axis=0))
        return kmin, kmax

    kmin, kmax = lax.fori_loop(0, nk, score_chunk, (jnp.full((SUBLANES, tq), int_max, jnp.int32),
                                                    jnp.full((SUBLANES, tq), INT_MIN, jnp.int32)))

    def count_ge(p):
        p8 = jnp.broadcast_to(p, (SUBLANES, tq))

        def body(c, cnt):
            ks = pl.multiple_of(c * tq, tq)
            kk = key_ref[pl.ds(ks, tq), :].reshape(groups, SUBLANES, tq)
            return cnt + jnp.sum(jnp.where(kk >= p8[None], 1, 0), axis=0)

        cnt = lax.fori_loop(0, nk, body, jnp.zeros((SUBLANES, tq), jnp.int32))
        return jnp.sum(cnt, axis=0, keepdims=True)

    n_valid = i * tq + lax.broadcasted_iota(jnp.int32, (1, tq), 1) + 1
    few = n_valid <= n_sel
    lo0 = jnp.min(kmin, axis=0, keepdims=True)
    hi0 = jnp.max(kmax, axis=0, keepdims=True) + 1

    def open_rows(lo, hi, clo):
        return jnp.logical_not(few | (clo == n_sel) | (hi == lo + 1))

    def cond(st):
        it, lo, hi, clo, chi = st
        n_open = jnp.max(jnp.where(open_rows(lo, hi, clo), 1, 0))
        return jnp.logical_and(it < MAX_SELECT_STEPS, n_open > 0)

    def step(st):
        it, lo, hi, clo, chi = st
        for _ in range(SELECT_STEPS_PER_CHECK):
            upd = open_rows(lo, hi, clo)
            p = lo + jnp.maximum(lax.shift_right_logical(hi - lo, 1), 1)
            cnt = count_ge(p)
            ge = cnt >= n_sel
            up, dn = upd & ge, upd & jnp.logical_not(ge)
            lo, hi = jnp.where(up, p, lo), jnp.where(dn, p, hi)
            clo, chi = jnp.where(up, cnt, clo), jnp.where(dn, cnt, chi)
        return it + SELECT_STEPS_PER_CHECK, lo, hi, clo, chi

    _, lo, _, _, _ = lax.while_loop(cond, step, (jnp.int32(0), lo0, hi0, n_valid, jnp.zeros((1, tq), jnp.int32)))
    thr = jnp.where(few, jnp.int32(INT_MIN + 1), lo)
    thr8 = jnp.broadcast_to(thr, (SUBLANES, tq))

    def write_chunk(c, carry):
        ks = pl.multiple_of(c * tq, tq)
        kk = key_ref[pl.ds(ks, tq), :].reshape(groups, SUBLANES, tq)
        mask_ref[0, pl.ds(ks, tq), :] = jnp.where(kk >= thr8[None], 1, 0).reshape(tq, tq).astype(mask_ref.dtype)
        return carry

    lax.fori_loop(0, nk, write_chunk, 0)

    def zero_chunk(c, carry):
        ks = pl.multiple_of(c * tq, tq)
        mask_ref[0, pl.ds(ks, tq), :] = jnp.zeros((tq, tq), mask_ref.dtype)
        return carry

    lax.fori_loop(nk, pl.num_programs(1), zero_chunk, 0)


def indexer_mask(qi, small, k_even, k_odd, b, t, n_sel):
    tq = _tile(t, 256)
    return pl.pallas_call(
        functools.partial(_indexer_body, tq=tq, n_sel=n_sel),
        grid=(b, t // tq),
        in_specs=[
            pl.BlockSpec((1, tq, IDX_HEADS * IDX_DIM), lambda bb, i: (bb, i, 0)),
            pl.BlockSpec((1, tq, LANES), lambda bb, i: (bb, i, COL_SMALL // LANES)),
            pl.BlockSpec((1, t, LANES), lambda bb, i: (bb, 0, 0)),
            pl.BlockSpec((1, t, LANES), lambda bb, i: (bb, 0, 0)),
        ],
        out_specs=pl.BlockSpec((1, t, tq), lambda bb, i: (bb, 0, i)),
        out_shape=jax.ShapeDtypeStruct((b, t, t), jnp.int8),
        scratch_shapes=[pltpu.VMEM((t, tq), jnp.int32), pltpu.VMEM((IDX_HEADS, SUBLANES, tq), F32)],
        compiler_params=_params("parallel", "parallel"),
        name="indexer",
    )(qi.reshape(b, t, -1), small, k_even.reshape(b, t, LANES), k_odd.reshape(b, t, LANES))


def _attn_body(q_ref, k_ref, vt_ref, mask_ref, bias_ref, z_ref, o_ref, m_ref, acc_ref, *, tq):
    qi, ki = pl.program_id(1), pl.program_id(2)

    @pl.when(ki == 0)
    def _():
        m_ref[...] = jnp.full(m_ref.shape, MASK_NEG, F32)
        acc_ref[...] = jnp.zeros(acc_ref.shape, F32)

    def tile(near):
        madd = (1.0 - mask_ref[0].astype(F32)) * MASK_NEG
        def scores(h):
            sl = slice(h * A_HEAD_DIM, (h + 1) * A_HEAD_DIM)
            return _dot_nt(k_ref[0, :, sl], q_ref[0, :, sl])

        s_next = scores(0)
        for h in range(A_HEADS):
            s = s_next + madd
            if h + 1 < A_HEADS:
                s_next = scores(h + 1)
            if near:
                s = s + bias_ref[h, 0]
            m_prev = m_ref[h]
            m_new = jnp.maximum(m_prev, jnp.max(s, axis=0, keepdims=True))
            p = jnp.exp2(s - m_new).astype(BF16)
            acc_ref[h] = jnp.exp2(m_prev - m_new) * acc_ref[h] + _dot(vt_ref[h * V_ROWS:(h + 1) * V_ROWS, :], p)
            m_ref[h] = m_new

    @pl.when(ki + 1 < qi)
    def _():
        tile(False)

    @pl.when(jnp.logical_and(ki + 1 >= qi, ki <= qi))
    def _():
        tile(True)

    @pl.when(ki == qi)
    def _():
        outs = []
        for h in range(A_HEADS):
            a = acc_ref[h]
            outs.append((a[:A_HEAD_DIM] / a[A_HEAD_DIM:A_HEAD_DIM + 1]).T)
        o_ref[0] = (jnp.concatenate(outs, axis=-1) * _silu(z_ref[0])).astype(o_ref.dtype)


def attention(q, k, vt, mask, bias, proj3d, b, t, tq):
    nq = t // tq
    return pl.pallas_call(
        functools.partial(_attn_body, tq=tq),
        grid=(b, nq, nq),
        in_specs=[
            pl.BlockSpec((1, tq, A_WIDTH), lambda bb, i, j: (bb, i, 0)),
            pl.BlockSpec((1, tq, A_WIDTH), lambda bb, i, j: (bb, jnp.minimum(j, i), 0)),
            pl.BlockSpec((A_HEADS * V_ROWS, tq), lambda bb, i, j: (0, bb * nq + jnp.minimum(j, i))),
            pl.BlockSpec((1, tq, tq), lambda bb, i, j: (bb, jnp.minimum(j, i), i)),
            pl.BlockSpec((A_HEADS, 1, tq, tq), lambda bb, i, j: (0, jnp.where(j >= i, 0, 1), 0, 0)),
            pl.BlockSpec((1, tq, A_WIDTH), lambda bb, i, j: (bb, i, COL_Z_A // A_WIDTH)),
        ],
        out_specs=pl.BlockSpec((1, tq, A_WIDTH), lambda bb, i, j: (bb, i, 0)),
        out_shape=jax.ShapeDtypeStruct((b, t, A_WIDTH), BF16),
        scratch_shapes=[pltpu.VMEM((A_HEADS, 1, tq), F32), pltpu.VMEM((A_HEADS, V_ROWS, tq), F32)],
        compiler_params=_params("parallel", "parallel", "arbitrary"),
        name="dsa_attention",
    )(q.reshape(b, t, -1), k.reshape(b, t, -1), vt, mask, bias, proj3d)


def _gdn_prep_body(x_ref, halo_ref, sm_ref, cw_ref, alog_ref, dtb_ref, eg_ref, eb_ref,
                   q_ref, k_ref, v_ref, gcb_ref, bb_ref, grow_ref, xs_ref, *, tm):
    i = pl.program_id(1)
    hal = halo_ref[0]
    xs_ref[0:8, :] = jnp.where(i > 0, hal, jnp.zeros_like(hal))
    xs_ref[8:, :] = x_ref[0]
    y = jnp.zeros((tm, 3 * B_WIDTH), F32)
    for j in range(B_CONV):
        y = y + cw_ref[j:j + 1, :] * xs_ref[pl.ds(8 - (B_CONV - 1) + j, tm), :]
    y = _silu(y)
    for h in range(B_HEADS):
        sl = slice(h * LANES, (h + 1) * LANES)
        qh = y[:, sl]
        q_ref[0, :, sl] = qh * lax.rsqrt(jnp.sum(qh * qh, axis=-1, keepdims=True) + EPS) * (B_HEAD_DIM ** -0.5)
        kh = y[:, B_WIDTH + h * LANES:B_WIDTH + (h + 1) * LANES]
        k_ref[0, :, sl] = kh * lax.rsqrt(jnp.sum(kh * kh, axis=-1, keepdims=True) + EPS)
    v_ref[0] = y[:, 2 * B_WIDTH:]

    sm = sm_ref[0]
    xg = sm + dtb_ref[...]
    softplus = jnp.maximum(xg, 0.0) + jnp.log(1.0 + jnp.exp(-jnp.abs(xg)))
    g = -jnp.exp(alog_ref[...]) * softplus
    r = lax.broadcasted_iota(jnp.int32, (tm, tm), 0)
    c = lax.broadcasted_iota(jnp.int32, (tm, tm), 1)
    sh = int(math.log2(GDN_CHUNK))
    same_chunk = jnp.right_shift(r, sh) == jnp.right_shift(c, sh)
    tri = jnp.where(jnp.logical_and(same_chunk, c <= r), 1.0, 0.0).astype(F32)
    gc = lax.dot_general(tri, g, (((1,), (0,)), ((), ())), precision=HIGHEST, preferred_element_type=F32)
    gcb_ref[0] = lax.dot_general(gc, eg_ref[...], (((1,), (0,)), ((), ())), precision=HIGHEST,
                                 preferred_element_type=F32)
    bb_ref[0] = lax.dot_general(_sigmoid(sm), eb_ref[...], (((1,), (0,)), ((), ())), precision=HIGHEST,
                                preferred_element_type=F32)
    gct = gc.T
    for cc in range(tm // GDN_CHUNK):
        grow_ref[0, cc] = gct[SM_ALPHA:SM_ALPHA + B_HEADS, cc * GDN_CHUNK:(cc + 1) * GDN_CHUNK]


def gdn_prep(proj3d, conv_w, a_log, dt_bias, b, t):
    tm = _tile(t, 256)
    lane = jnp.arange(LANES)
    pad_row = lambda v: jnp.zeros((1, LANES), F32).at[0, SM_ALPHA:SM_ALPHA + B_HEADS].set(v)
    rows = lane[:, None]
    cols = jnp.arange(B_WIDTH)[None, :]
    e_g = (rows == SM_ALPHA + cols // LANES).astype(F32)
    e_b = (rows == SM_BETA + cols // LANES).astype(F32)
    nc = tm // GDN_CHUNK
    act = jax.ShapeDtypeStruct((b, t, B_WIDTH), F32)
    full = lambda a: pl.BlockSpec(a.shape, lambda bb, i: (0,) * a.ndim)
    consts = [conv_w, pad_row(a_log), pad_row(dt_bias), e_g, e_b]
    return pl.pallas_call(
        functools.partial(_gdn_prep_body, tm=tm),
        grid=(b, t // tm),
        in_specs=[
            pl.BlockSpec((1, tm, 3 * B_WIDTH), lambda bb, i: (bb, i, COL_QKV_B // (3 * B_WIDTH))),
            pl.BlockSpec((1, 8, 3 * B_WIDTH), lambda bb, i: (bb, jnp.maximum(i * (tm // 8) - 1, 0), 0)),
            pl.BlockSpec((1, tm, LANES), lambda bb, i: (bb, i, COL_SMALL // LANES)),
        ] + [full(c) for c in consts],
        out_specs=[pl.BlockSpec((1, tm, B_WIDTH), lambda bb, i: (bb, i, 0))] * 5
        + [pl.BlockSpec((1, nc, B_HEADS, GDN_CHUNK), lambda bb, i: (bb, i, 0, 0))],
        out_shape=[act] * 5 + [jax.ShapeDtypeStruct((b, t // GDN_CHUNK, B_HEADS, GDN_CHUNK), F32)],
        scratch_shapes=[pltpu.VMEM((tm + 8, 3 * B_WIDTH), F32)],
        compiler_params=_params("parallel", "parallel"),
        name="gdn_prep",
    )(proj3d, proj3d, proj3d, *consts)


def _gdn_body(q_ref, k_ref, v_ref, gcb_ref, bb_ref, grow_ref, z_ref, on_ref, o_ref, s_ref, *, nc):
    @pl.when(pl.program_id(1) == 0)
    def _():
        s_ref[...] = jnp.zeros(s_ref.shape, F32)

    cs, nh = GDN_CHUNK, B_HEADS
    nb = nc * nh
    ri = lax.broadcasted_iota(jnp.int32, (nb, cs, cs), 1)
    ci = lax.broadcasted_iota(jnp.int32, (nb, cs, cs), 2)
    lower = ci <= ri
    strict = ci < ri
    eye = jnp.where(ci == ri, 1.0, 0.0).astype(F32)
    bf = lambda a: a.astype(BF16)
    bmm = lambda a, b: lax.dot_general(a, b, (((2,), (1,)), ((0,), (0,))), preferred_element_type=F32)
    bmm_nt = lambda a, b: lax.dot_general(a, b, (((2,), (2,)), ((0,), (0,))), preferred_element_type=F32)
    bmm_tn = lambda a, b: lax.dot_general(a, b, (((1,), (1,)), ((0,), (0,))), preferred_element_type=F32)

    def stack(ref):
        return jnp.stack([ref[0, c * cs:(c + 1) * cs, h * LANES:(h + 1) * LANES]
                          for c in range(nc) for h in range(nh)])

    q, k, v = stack(q_ref), stack(k_ref), stack(v_ref)
    gcb = stack(gcb_ref)
    beta = stack(bb_ref)
    grow = jnp.stack([grow_ref[0, c, h:h + 1, :] for c in range(nc) for h in range(nh)])
    diff = gcb[:, :, :cs] - grow
    decay = jnp.where(lower, jnp.exp(jnp.where(lower, diff, 0.0)), 0.0)
    eg = jnp.exp(gcb)
    glast = gcb[:, cs - 1:cs, :]
    kb = k * beta
    lmat = jnp.where(strict, bmm_nt(bf(kb), bf(k)) * decay, 0.0)
    n = -lmat
    tinv = eye + n
    for _ in range(int(math.log2(cs)) - 1):
        n = bmm(bf(n), bf(n))
        tinv = tinv + bmm(bf(tinv), bf(n))
    uw = bmm(bf(tinv), bf(jnp.concatenate([v * beta, kb * eg], axis=-1)))
    attn = bf(jnp.where(lower, bmm_nt(bf(q), bf(k)) * decay, 0.0))
    qg = bf(q * eg)
    kdec = bf(k * jnp.exp(glast - gcb))
    egl = jnp.exp(glast)

    s = s_ref[...]
    for c in range(nc):
        sl = slice(c * nh, (c + 1) * nh)
        sb = bf(s)
        v_new = uw[sl, :, :LANES] - bmm(bf(uw[sl, :, LANES:]), sb)
        o = bmm(qg[sl], sb) + bmm(attn[sl], bf(v_new))
        s = s * egl[sl] + bmm_tn(kdec[sl], bf(v_new))
        o = o * lax.rsqrt(jnp.mean(o * o, axis=-1, keepdims=True) + EPS) * on_ref[...]
        for h in range(nh):
            rows, cols = slice(c * cs, (c + 1) * cs), slice(h * LANES, (h + 1) * LANES)
            o_ref[0, rows, cols] = (o[h] * _silu(z_ref[0, rows, cols])).astype(o_ref.dtype)
    s_ref[...] = s


def gdn_scan(qh, kh, v, gcb, bb, grow, proj3d, o_norm, b, t):
    tt = _tile(t, 256)
    nc = tt // GDN_CHUNK
    blk = pl.BlockSpec((1, tt, B_WIDTH), lambda bb_, i: (bb_, i, 0))
    return pl.pallas_call(
        functools.partial(_gdn_body, nc=nc),
        grid=(b, t // tt),
        in_specs=[blk] * 5 + [
            pl.BlockSpec((1, nc, B_HEADS, GDN_CHUNK), lambda bb_, i: (bb_, i, 0, 0)),
            pl.BlockSpec((1, tt, B_WIDTH), lambda bb_, i: (bb_, i, COL_Z_B // B_WIDTH)),
            pl.BlockSpec((1, LANES), lambda bb_, i: (0, 0)),
        ],
        out_specs=blk,
        out_shape=jax.ShapeDtypeStruct((b, t, B_WIDTH), BF16),
        scratch_shapes=[pltpu.VMEM((B_HEADS, B_HEAD_DIM, B_HEAD_DIM), F32)],
        compiler_params=_params("parallel", "arbitrary"),
        name="gdn_scan",
    )(qh, kh, v, gcb, bb, grow, proj3d, o_norm.reshape(1, LANES))


C_HALO = 32


def _cd_mix_body(a_ref, g_ref, ah_ref, gh_ref, zc_ref, bg_ref, cg_ref, ud_ref, cgh_ref, udh_ref, zd_ref,
                 dww_ref, dwb_ref, lnw_ref, lnb_ref, dcw_ref, yc_ref, yd_ref, us_ref, ds_ref, *, tm):
    i = pl.program_id(1)
    first = i == 0
    uh = ah_ref[0] * _sigmoid(gh_ref[0])
    us_ref[0:C_HALO, :] = jnp.where(first, jnp.zeros_like(uh), uh)
    us_ref[C_HALO:, :] = a_ref[0] * _sigmoid(g_ref[0])
    u = jnp.zeros((tm, C_WIDTH), F32)
    for j in range(C_CONV):
        u = u + dww_ref[j:j + 1, :] * us_ref[pl.ds(C_HALO - (C_CONV - 1) + j, tm), :]
    u = u + dwb_ref[...]
    mu = jnp.mean(u, axis=-1, keepdims=True)
    uc = u - mu
    u = uc * lax.rsqrt(jnp.mean(uc * uc, axis=-1, keepdims=True) + EPS) * lnw_ref[...] + lnb_ref[...]
    yc_ref[0] = (_silu(u) * _silu(zc_ref[0])).astype(yc_ref.dtype)

    dh = cgh_ref[0] * udh_ref[0]
    ds_ref[0:8, :] = jnp.where(first, jnp.zeros_like(dh), dh)
    ds_ref[8:, :] = cg_ref[0] * ud_ref[0]
    d = jnp.zeros((tm, D_WIDTH), F32)
    for j in range(D_CONV):
        d = d + dcw_ref[j:j + 1, :] * ds_ref[pl.ds(8 - (D_CONV - 1) + j, tm), :]
    yd_ref[0] = (bg_ref[0] * d * _silu(zd_ref[0])).astype(yd_ref.dtype)


def cd_mix(proj3d, dw_w, dw_b, ln_w, ln_b, d_conv_w, b, t):
    tm = _tile(t, 256)
    w = C_WIDTH
    col = lambda n: pl.BlockSpec((1, tm, w), lambda bb, i, n=n: (bb, i, n))
    halo = lambda n, rows: pl.BlockSpec(
        (1, rows, w), lambda bb, i, n=n, rows=rows: (bb, jnp.maximum(i * (tm // rows) - 1, 0), n))
    row = lambda a: a.reshape(1, -1)
    full = lambda a: pl.BlockSpec(a.shape, lambda bb, i: (0,) * a.ndim)
    consts = [dw_w, row(dw_b), row(ln_w), row(ln_b), d_conv_w]
    out = jax.ShapeDtypeStruct((b, t, w), BF16)
    return pl.pallas_call(
        functools.partial(_cd_mix_body, tm=tm),
        grid=(b, t // tm),
        in_specs=[col(0), col(1), halo(0, C_HALO), halo(1, C_HALO), col(2), col(3), col(4), col(5),
                  halo(4, 8), halo(5, 8), col(6)] + [full(c) for c in consts],
        out_specs=[pl.BlockSpec((1, tm, w), lambda bb, i: (bb, i, 0))] * 2,
        out_shape=[out, out],
        scratch_shapes=[pltpu.VMEM((tm + C_HALO, w), F32), pltpu.VMEM((tm + 8, w), F32)],
        compiler_params=_params("parallel", "parallel"),
        name="cd_mix",
    )(*([proj3d] * 11), *consts)


def _reorder_ab_w_in(w):
    offs = [0]
    for s in AB_SPLITS:
        offs.append(offs[-1] + s)
    part = lambda n: w[:, offs[n]:offs[n + 1]]
    c_q, c_kv, k_idx, w_idx, z_a, qkv_b, beta_b, alpha_b, z_b = (part(n) for n in range(9))
    small = jnp.concatenate([k_idx, w_idx, beta_b, alpha_b], axis=1)
    small = jnp.pad(small, ((0, 0), (0, LANES - small.shape[1])))
    out = jnp.concatenate([qkv_b, z_a, z_b, c_q, c_kv, small], axis=1)
    return jnp.pad(out, ((0, 0), (0, AB_IN_PAD - out.shape[1]))).astype(BF16)


def _ab_layer(x2d, b, t, norm_w, rel_bias, w_in, q_norm, w_uq, w_iq, kv_norm, w_uk, w_uv, q_gain, k_gain,
              ik_w, ik_b, conv_w, a_log, dt_bias, o_norm, w_out):
    proj = norm_matmul(x2d, norm_w, _reorder_ab_w_in(w_in))
    proj3d = proj.reshape(b, t, AB_IN_PAD)
    q, k, v, qi, k_even, k_odd = dsa_prep(proj, q_norm, kv_norm, w_uq, w_iq, w_uk, w_uv, q_gain, k_gain, ik_w, ik_b)
    n_sel = min(TOPK_MAX, t // 4)
    mask = indexer_mask(qi, proj3d, k_even, k_odd, b, t, n_sel)
    tq = _tile(t, 512)
    y_a = attention(q, k, v, mask, bias_tiles(rel_bias, tq), proj3d, b, t, tq)
    qh, kh, vv, gcb, bb, grow = gdn_prep(proj3d, conv_w, a_log, dt_bias, b, t)
    y_b = gdn_scan(qh, kh, vv, gcb, bb, grow, proj3d, o_norm, b, t)
    wo = w_out.astype(BF16)
    return out_proj(x2d, y_a.reshape(b * t, -1), y_b.reshape(b * t, -1), wo[:A_WIDTH], wo[A_WIDTH:])


def _cd_layer(x2d, b, t, norm_w, w_in, dw_w, dw_b, ln_w, ln_b, d_conv_w, w_out):
    proj = norm_matmul(x2d, norm_w, w_in.astype(BF16))
    y_c, y_d = cd_mix(proj.reshape(b, t, -1), dw_w, dw_b, ln_w, ln_b, d_conv_w, b, t)
    wo = w_out.astype(BF16)
    return out_proj(x2d, y_c.reshape(b * t, -1), y_d.reshape(b * t, -1), wo[:C_WIDTH], wo[C_WIDTH:])


def kernel(x, norm_w, rel_bias, ab_w_in, a_q_norm, a_w_uq, a_w_iq, a_kv_norm, a_w_uk, a_w_uv, a_q_gain,
           a_k_gain, a_ik_norm_w, a_ik_norm_b, b_conv_w, b_a_log, b_dt_bias, b_o_norm, ab_w_out, cd_w_in,
           c_dw_w, c_dw_b, c_ln_w, c_ln_b, d_conv_w, cd_w_out):
    b, t, d = x.shape
    depth = norm_w.shape[0]
    x2d = x.reshape(b * t, d)
    for i in range(depth):
        j = i // 2
        if i % 2 == 0:
            x2d = _ab_layer(x2d, b, t, norm_w[i], rel_bias, ab_w_in[j], a_q_norm[j], a_w_uq[j], a_w_iq[j],
                            a_kv_norm[j], a_w_uk[j], a_w_uv[j], a_q_gain[j], a_k_gain[j], a_ik_norm_w[j],
                            a_ik_norm_b[j], b_conv_w[j], b_a_log[j], b_dt_bias[j], b_o_norm[j], ab_w_out[j])
        else:
            x2d = _cd_layer(x2d, b, t, norm_w[i], cd_w_in[j], c_dw_w[j], c_dw_b[j], c_ln_w[j], c_ln_b[j],
                            d_conv_w[j], cd_w_out[j])
    return x2d.reshape(b, t, d)
```

```python
import functools
import math

import jax
import jax.numpy as jnp
from jax import lax
from jax.experimental import pallas as pl
from jax.experimental.pallas import tpu as pltpu

D_MODEL = 2048
A_HEADS = 8
A_HEAD_DIM = 128
A_WIDTH = A_HEADS * A_HEAD_DIM
A_Q_LORA = 512
A_KV_LORA = 256
IDX_HEADS = 16
IDX_DIM = 64
TOPK_MAX = 256
REL_BUCKETS = 32
REL_MAX_DIST = 128
B_HEADS = 8
B_HEAD_DIM = 128
B_WIDTH = B_HEADS * B_HEAD_DIM
B_CONV = 4
GDN_CHUNK = 64
C_WIDTH = 1024
C_CONV = 31
D_WIDTH = 1024
D_CONV = 3
EPS = 1e-6

AB_SPLITS = (A_Q_LORA, A_KV_LORA, IDX_DIM, IDX_HEADS, A_WIDTH, 3 * B_WIDTH, B_HEADS, B_HEADS, B_WIDTH)
AB_IN_PAD = 6144
SM_KIDX = 0
SM_WIDX = 64
SM_BETA = 80
SM_ALPHA = 88
COL_QKV_B = 0
COL_Z_A = 3072
COL_Z_B = 4096
COL_C_Q = 5120
COL_C_KV = 5632

LANES = 128
MASK_NEG = -1e30
INT_MIN = -(2 ** 31)
LOG2E = math.log2(math.e)
BF16_ROWS = 16
V_ROWS = A_HEAD_DIM + BF16_ROWS
VMEM_LIMIT = 56 * 1024 * 1024

F32 = jnp.float32
BF16 = jnp.bfloat16
HIGHEST = lax.Precision.HIGHEST


def _tile(n, pref):
    t = min(n, pref)
    assert n % t == 0, (n, t)
    return t


def _params(*sem):
    return pltpu.CompilerParams(dimension_semantics=sem, vmem_limit_bytes=VMEM_LIMIT)


def _dot(a, b):
    return jnp.dot(a, b, preferred_element_type=F32)


def _dot_nt(a, b):
    return lax.dot_general(a, b, (((1,), (1,)), ((), ())), preferred_element_type=F32)


def _dot_tn(a, b):
    return lax.dot_general(a, b, (((0,), (0,)), ((), ())), preferred_element_type=F32)


def _silu(x):
    return x * (1.0 / (1.0 + jnp.exp(-x)))


def _sigmoid(x):
    return 1.0 / (1.0 + jnp.exp(-x))


def _norm_matmul_body(*refs, with_f32_cols):
    if with_f32_cols:
        x_ref, nw_ref, w_ref, ws_ref, o_ref, os_ref, h_ref = refs
    else:
        x_ref, nw_ref, w_ref, o_ref, h_ref = refs

    @pl.when(pl.program_id(1) == 0)
    def _():
        x = x_ref[...]
        ms = jnp.mean(x * x, axis=-1, keepdims=True)
        h_ref[...] = (x * lax.rsqrt(ms + EPS) * nw_ref[...]).astype(h_ref.dtype)
        if with_f32_cols:
            os_ref[...] = _dot(h_ref[...], ws_ref[...])

    o_ref[...] = _dot(h_ref[...], w_ref[...]).astype(o_ref.dtype)


def norm_matmul(x2d, norm_w, w_bf16, w_f32_cols=None):
    m, k = x2d.shape
    n = w_bf16.shape[1]
    tm, tn = _tile(m, 1024), _tile(n, 1024)
    extra = w_f32_cols is not None
    in_specs = [
        pl.BlockSpec((tm, k), lambda i, j: (i, 0)),
        pl.BlockSpec((1, k), lambda i, j: (0, 0)),
        pl.BlockSpec((k, tn), lambda i, j: (0, j)),
    ]
    out_specs = [pl.BlockSpec((tm, tn), lambda i, j: (i, j))]
    out_shape = [jax.ShapeDtypeStruct((m, n), BF16)]
    args = [x2d, norm_w.reshape(1, k), w_bf16]
    if extra:
        in_specs.append(pl.BlockSpec((k, LANES), lambda i, j: (0, 0)))
        out_specs.append(pl.BlockSpec((tm, LANES), lambda i, j: (i, 0)))
        out_shape.append(jax.ShapeDtypeStruct((m, LANES), F32))
        args.append(w_f32_cols)
    res = pl.pallas_call(
        functools.partial(_norm_matmul_body, with_f32_cols=extra),
        grid=(m // tm, n // tn),
        in_specs=in_specs,
        out_specs=out_specs,
        out_shape=out_shape,
        scratch_shapes=[pltpu.VMEM((tm, k), BF16)],
        compiler_params=_params("parallel", "arbitrary"),
        name="norm_matmul",
    )(*args)
    return res if extra else res[0]


def _out_proj_body(x_ref, ya_ref, yb_ref, wa_ref, wb_ref, o_ref):
    o_ref[...] = x_ref[...] + _dot(ya_ref[...], wa_ref[...]) + _dot(yb_ref[...], wb_ref[...])


def out_proj(x2d, ya, yb, wa_bf16, wb_bf16):
    m, n = x2d.shape
    ka, kb = ya.shape[1], yb.shape[1]
    tm = _tile(m, 512)
    return pl.pallas_call(
        _out_proj_body,
        grid=(m // tm,),
        in_specs=[
            pl.BlockSpec((tm, n), lambda i: (i, 0)),
            pl.BlockSpec((tm, ka), lambda i: (i, 0)),
            pl.BlockSpec((tm, kb), lambda i: (i, 0)),
            pl.BlockSpec((ka, n), lambda i: (0, 0)),
            pl.BlockSpec((kb, n), lambda i: (0, 0)),
        ],
        out_specs=pl.BlockSpec((tm, n), lambda i: (i, 0)),
        out_shape=jax.ShapeDtypeStruct((m, n), F32),
        compiler_params=_params("parallel"),
        name="out_proj",
    )(x2d, ya, yb, wa_bf16, wb_bf16)


def _head_rmsnorm(y, gain_row, scale):
    outs = []
    for h in range(y.shape[1] // LANES):
        seg = y[:, h * LANES:(h + 1) * LANES]
        ms = jnp.mean(seg * seg, axis=-1, keepdims=True)
        outs.append(seg * lax.rsqrt(ms + EPS) * (gain_row * scale))
    return jnp.concatenate(outs, axis=-1)


def _dsa_prep_body(cq_ref, ckv_ref, sm_ref, qn_ref, kvn_ref, wuq_ref, wiq_ref, wuk_ref, wuv_ref,
                   qg_ref, kg_ref, ikw_ref, ikb_ref,
                   q_ref, k_ref, v_ref, qi_ref, ke_ref, ko_ref):
    cq = cq_ref[...].astype(F32)
    cq = cq * lax.rsqrt(jnp.mean(cq * cq, axis=-1, keepdims=True) + EPS) * qn_ref[...]
    cqb = cq.astype(BF16)
    q = _dot(cqb, wuq_ref[...])
    q_ref[...] = _head_rmsnorm(q, qg_ref[...], A_HEAD_DIM ** -0.5 * LOG2E).astype(q_ref.dtype)
    qi_ref[...] = _dot(cqb, wiq_ref[...]).astype(qi_ref.dtype)

    ckv = ckv_ref[...].astype(F32)
    ckv = ckv * lax.rsqrt(jnp.mean(ckv * ckv, axis=-1, keepdims=True) + EPS) * kvn_ref[...]
    ckvb = ckv.astype(BF16)
    k = _dot(ckvb, wuk_ref[...])
    k_ref[...] = _head_rmsnorm(k, kg_ref[...], 1.0).astype(k_ref.dtype)
    vt = _dot_nt(wuv_ref[...], ckvb).astype(v_ref.dtype)
    ones = jnp.ones((BF16_ROWS, vt.shape[1]), v_ref.dtype)
    for h in range(A_HEADS):
        v_ref[h * V_ROWS:h * V_ROWS + A_HEAD_DIM, :] = vt[h * A_HEAD_DIM:(h + 1) * A_HEAD_DIM, :]
        v_ref[h * V_ROWS + A_HEAD_DIM:(h + 1) * V_ROWS, :] = ones

    kx = sm_ref[...][:, SM_KIDX:SM_KIDX + IDX_DIM]
    mu = jnp.mean(kx, axis=-1, keepdims=True)
    kc = kx - mu
    kl = kc * lax.rsqrt(jnp.mean(kc * kc, axis=-1, keepdims=True) + EPS) * ikw_ref[...] + ikb_ref[...]
    z = jnp.zeros_like(kl)
    ke_ref[...] = jnp.concatenate([kl, z], axis=-1).astype(ke_ref.dtype)
    ko_ref[...] = jnp.concatenate([z, kl], axis=-1).astype(ko_ref.dtype)


def dsa_prep(proj, small, q_norm, kv_norm, w_uq, w_iq, w_uk, w_uv, q_gain, k_gain, ik_w, ik_b):
    m = proj.shape[0]
    tm = _tile(m, 512)
    row = lambda a: a.reshape(1, -1)
    full = lambda a: pl.BlockSpec(a.shape, lambda i: (0,) * a.ndim)
    consts = [row(q_norm), row(kv_norm), w_uq.astype(BF16), w_iq.astype(BF16), w_uk.astype(BF16),
              w_uv.T.astype(BF16), row(q_gain), row(k_gain), row(ik_w), row(ik_b)]
    outs = [jax.ShapeDtypeStruct((m, A_WIDTH), BF16)] * 2 + [
        jax.ShapeDtypeStruct((A_HEADS * V_ROWS, m), BF16),
        jax.ShapeDtypeStruct((m, IDX_HEADS * IDX_DIM), BF16),
        jax.ShapeDtypeStruct((m, LANES), BF16), jax.ShapeDtypeStruct((m, LANES), BF16)]
    out_specs = [pl.BlockSpec((tm, s.shape[1]), lambda i: (i, 0)) for s in outs]
    out_specs[2] = pl.BlockSpec((A_HEADS * V_ROWS, tm), lambda i: (0, i))
    return pl.pallas_call(
        _dsa_prep_body,
        grid=(m // tm,),
        in_specs=[
            pl.BlockSpec((tm, A_Q_LORA), lambda i: (i, COL_C_Q // A_Q_LORA)),
            pl.BlockSpec((tm, A_KV_LORA), lambda i: (i, COL_C_KV // A_KV_LORA)),
            pl.BlockSpec((tm, LANES), lambda i: (i, 0)),
        ] + [full(c) for c in consts],
        out_specs=out_specs,
        out_shape=outs,
        compiler_params=_params("parallel"),
        name="dsa_prep",
    )(proj, proj, small, *consts)


def _t5_bucket(dist):
    max_exact = REL_BUCKETS // 2
    large = max_exact + (jnp.log(jnp.maximum(dist, 1).astype(F32) / max_exact)
                         / math.log(REL_MAX_DIST / max_exact) * (REL_BUCKETS - max_exact)).astype(jnp.int32)
    large = jnp.minimum(large, REL_BUCKETS - 1)
    return jnp.where(dist < max_exact, dist, large)


def _bias_tiles_body(rb_ref, o_ref, *, t):
    h = pl.program_id(0)
    off = pl.program_id(1) * t
    dist = off + lax.broadcasted_iota(jnp.int32, (t, t), 1) - lax.broadcasted_iota(jnp.int32, (t, t), 0)
    bucket = _t5_bucket(jnp.maximum(dist, 0))
    acc = jnp.zeros((t, t), F32)
    for b in range(REL_BUCKETS):
        acc = jnp.where(bucket == b, rb_ref[b, h], acc)
    o_ref[0, 0] = (acc - rb_ref[REL_BUCKETS - 1, h]) * LOG2E


def bias_tiles(rel_bias, t):
    return pl.pallas_call(
        functools.partial(_bias_tiles_body, t=t),
        grid=(A_HEADS, 2),
        in_specs=[pl.BlockSpec(memory_space=pltpu.SMEM)],
        out_specs=pl.BlockSpec((1, 1, t, t), lambda h, c: (h, c, 0, 0)),
        out_shape=jax.ShapeDtypeStruct((A_HEADS, 2, t, t), F32),
        compiler_params=_params("parallel", "parallel"),
        name="bias_tiles",
    )(rel_bias)


SUBLANES = 8
MAX_SELECT_STEPS = 40
SELECT_STEPS_PER_CHECK = 4


def _key_of(x):
    bits = pltpu.bitcast(x, jnp.int32)
    key = jnp.where(bits < 0, bits ^ jnp.int32(0x7FFFFFFF), bits)
    return jnp.where(bits == jnp.int32(INT_MIN), 0, key)


def _indexer_body(qi_ref, sm_ref, ke_ref, ko_ref, mask_ref, key_ref, w_ref, *, tq, n_sel):
    i = pl.program_id(1)
    nk = i + 1
    groups = tq // SUBLANES
    scale = (IDX_HEADS ** -0.5) * (IDX_DIM ** -0.5)
    smt = sm_ref[0].T
    for h in range(IDX_HEADS):
        w_ref[h] = jnp.broadcast_to(smt[SM_WIDX + h:SM_WIDX + h + 1, :] * scale, (SUBLANES, tq))
    kloc = lax.broadcasted_iota(jnp.int32, (tq, tq), 0)
    qpos = i * tq + lax.broadcasted_iota(jnp.int32, (tq, tq), 1)
    int_max = jnp.int32(2 ** 31 - 1)

    def score_chunk(c, carry):
        kmin, kmax = carry
        ks = pl.multiple_of(c * tq, tq)
        ke = ke_ref[0, pl.ds(ks, tq), :]
        ko = ko_ref[0, pl.ds(ks, tq), :]
        acc = jnp.zeros((tq, tq), F32)
        for j in range(IDX_HEADS // 2):
            qp = qi_ref[0, :, j * LANES:(j + 1) * LANES]
            for par, kk in ((0, ke), (1, ko)):
                s = jnp.maximum(_dot_nt(kk, qp), 0.0)
                acc = acc + s * jnp.tile(w_ref[2 * j + par], (groups, 1))
        key = _key_of(acc)
        valid = c * tq + kloc <= qpos
        key_ref[pl.ds(ks, tq), :] = jnp.where(valid, key, jnp.int32(INT_MIN))
        kmin = jnp.minimum(kmin, jnp.min(jnp.where(valid, key, int_max).reshape(groups, SUBLANES, tq), axis=0))
        kmax = jnp.maximum(kmax, jnp.max(jnp.where(valid, key, jnp.int32(INT_MIN)).reshape(groups, SUBLANES, tq), axis=0))
        return kmin, kmax

    kmin, kmax = lax.fori_loop(0, nk, score_chunk, (jnp.full((SUBLANES, tq), int_max, jnp.int32),
                                                    jnp.full((SUBLANES, tq), INT_MIN, jnp.int32)))

    def count_ge(p):
        p8 = jnp.broadcast_to(p, (SUBLANES, tq))

        def body(c, cnt):
            ks = pl.multiple_of(c * tq, tq)
            kk = key_ref[pl.ds(ks, tq), :].reshape(groups, SUBLANES, tq)
            return cnt + jnp.sum(jnp.where(kk >= p8[None], 1, 0), axis=0)

        cnt = lax.fori_loop(0, nk, body, jnp.zeros((SUBLANES, tq), jnp.int32))
        return jnp.sum(cnt, axis=0, keepdims=True)

    n_valid = i * tq + lax.broadcasted_iota(jnp.int32, (1, tq), 1) + 1
    few = n_valid <= n_sel
    lo0 = jnp.min(kmin, axis=0, keepdims=True)
    hi0 = jnp.max(kmax, axis=0, keepdims=True) + 1

    def open_rows(lo, hi, clo):
        return jnp.logical_not(few | (clo == n_sel) | (hi == lo + 1))

    def cond(st):
        it, lo, hi, clo, chi = st
        n_open = jnp.max(jnp.where(open_rows(lo, hi, clo), 1, 0))
        return jnp.logical_and(it < MAX_SELECT_STEPS, n_open > 0)

    def step(st):
        it, lo, hi, clo, chi = st
        for _ in range(SELECT_STEPS_PER_CHECK):
            upd = open_rows(lo, hi, clo)
            p = lo + jnp.maximum(lax.shift_right_logical(hi - lo, 1), 1)
            cnt = count_ge(p)
            ge = cnt >= n_sel
            up, dn = upd & ge, upd & jnp.logical_not(ge)
            lo, hi = jnp.where(up, p, lo), jnp.where(dn, p, hi)
            clo, chi = jnp.where(up, cnt, clo), jnp.where(dn, cnt, chi)
        return it + SELECT_STEPS_PER_CHECK, lo, hi, clo, chi

    _, lo, _, _, _ = lax.while_loop(cond, step, (jnp.int32(0), lo0, hi0, n_valid, jnp.zeros((1, tq), jnp.int32)))
    thr = jnp.where(few, jnp.int32(INT_MIN + 1), lo)
    thr8 = jnp.broadcast_to(thr, (SUBLANES, tq))

    def write_chunk(c, carry):
        ks = pl.multiple_of(c * tq, tq)
        kk = key_ref[pl.ds(ks, tq), :].reshape(groups, SUBLANES, tq)
        mask_ref[0, pl.ds(ks, tq), :] = jnp.where(kk >= thr8[None], 1, 0).reshape(tq, tq).astype(mask_ref.dtype)
        return carry

    lax.fori_loop(0, nk, write_chunk, 0)

    def zero_chunk(c, carry):
        ks = pl.multiple_of(c * tq, tq)
        mask_ref[0, pl.ds(ks, tq), :] = jnp.zeros((tq, tq), mask_ref.dtype)
        return carry

    lax.fori_loop(nk, pl.num_programs(1), zero_chunk, 0)


def indexer_mask(qi, small, k_even, k_odd, b, t, n_sel):
    tq = _tile(t, 256)
    return pl.pallas_call(
        functools.partial(_indexer_body, tq=tq, n_sel=n_sel),
        grid=(b, t // tq),
        in_specs=[
            pl.BlockSpec((1, tq, IDX_HEADS * IDX_DIM), lambda bb, i: (bb, i, 0)),
            pl.BlockSpec((1, tq, LANES), lambda bb, i: (bb, i, 0)),
            pl.BlockSpec((1, t, LANES), lambda bb, i: (bb, 0, 0)),
            pl.BlockSpec((1, t, LANES), lambda bb, i: (bb, 0, 0)),
        ],
        out_specs=pl.BlockSpec((1, t, tq), lambda bb, i: (bb, 0, i)),
        out_shape=jax.ShapeDtypeStruct((b, t, t), jnp.int8),
        scratch_shapes=[pltpu.VMEM((t, tq), jnp.int32), pltpu.VMEM((IDX_HEADS, SUBLANES, tq), F32)],
        compiler_params=_params("parallel", "parallel"),
        name="indexer",
    )(qi.reshape(b, t, -1), small, k_even.reshape(b, t, LANES), k_odd.reshape(b, t, LANES))


def _attn_body(q_ref, k_ref, vt_ref, mask_ref, bias_ref, z_ref, o_ref, m_ref, acc_ref, *, tq):
    qi, ki = pl.program_id(1), pl.program_id(2)

    @pl.when(ki == 0)
    def _():
        m_ref[...] = jnp.full(m_ref.shape, MASK_NEG, F32)
        acc_ref[...] = jnp.zeros(acc_ref.shape, F32)

    def tile(near):
        madd = (1.0 - mask_ref[0].astype(F32)) * MASK_NEG
        for h in range(A_HEADS):
            sl = slice(h * A_HEAD_DIM, (h + 1) * A_HEAD_DIM)
            s = _dot_nt(k_ref[0, :, sl], q_ref[0, :, sl]) + madd
            if near:
                s = s + bias_ref[h, 0]
            m_prev = m_ref[h]
            m_new = jnp.maximum(m_prev, jnp.max(s, axis=0, keepdims=True))
            p = jnp.exp2(s - m_new).astype(BF16)
            acc_ref[h] = jnp.exp2(m_prev - m_new) * acc_ref[h] + _dot(vt_ref[h * V_ROWS:(h + 1) * V_ROWS, :], p)
            m_ref[h] = m_new

    @pl.when(ki + 1 < qi)
    def _():
        tile(False)

    @pl.when(jnp.logical_and(ki + 1 >= qi, ki <= qi))
    def _():
        tile(True)

    @pl.when(ki == qi)
    def _():
        outs = []
        for h in range(A_HEADS):
            a = acc_ref[h]
            outs.append((a[:A_HEAD_DIM] / a[A_HEAD_DIM:A_HEAD_DIM + 1]).T)
        o_ref[0] = (jnp.concatenate(outs, axis=-1) * _silu(z_ref[0].astype(F32))).astype(o_ref.dtype)


def attention(q, k, vt, mask, bias, proj3d, b, t, tq):
    nq = t // tq
    return pl.pallas_call(
        functools.partial(_attn_body, tq=tq),
        grid=(b, nq, nq),
        in_specs=[
            pl.BlockSpec((1, tq, A_WIDTH), lambda bb, i, j: (bb, i, 0)),
            pl.BlockSpec((1, tq, A_WIDTH), lambda bb, i, j: (bb, jnp.minimum(j, i), 0)),
            pl.BlockSpec((A_HEADS * V_ROWS, tq), lambda bb, i, j: (0, bb * nq + jnp.minimum(j, i))),
            pl.BlockSpec((1, tq, tq), lambda bb, i, j: (bb, jnp.minimum(j, i), i)),
            pl.BlockSpec((A_HEADS, 1, tq, tq), lambda bb, i, j: (0, jnp.where(j >= i, 0, 1), 0, 0)),
            pl.BlockSpec((1, tq, A_WIDTH), lambda bb, i, j: (bb, i, COL_Z_A // A_WIDTH)),
        ],
        out_specs=pl.BlockSpec((1, tq, A_WIDTH), lambda bb, i, j: (bb, i, 0)),
        out_shape=jax.ShapeDtypeStruct((b, t, A_WIDTH), BF16),
        scratch_shapes=[pltpu.VMEM((A_HEADS, 1, tq), F32), pltpu.VMEM((A_HEADS, V_ROWS, tq), F32)],
        compiler_params=_params("parallel", "parallel", "arbitrary"),
        name="dsa_attention",
    )(q.reshape(b, t, -1), k.reshape(b, t, -1), vt, mask, bias, proj3d)


def _gdn_prep_body(x_ref, halo_ref, sm_ref, cw_ref, alog_ref, dtb_ref, eg_ref, eb_ref,
                   q_ref, k_ref, v_ref, gcb_ref, bb_ref, grow_ref, xs_ref, *, tm):
    i = pl.program_id(1)
    hal = halo_ref[0].astype(F32)
    xs_ref[0:BF16_ROWS, :] = jnp.where(i > 0, hal, jnp.zeros_like(hal))
    xs_ref[BF16_ROWS:, :] = x_ref[0].astype(F32)
    y = jnp.zeros((tm, 3 * B_WIDTH), F32)
    for j in range(B_CONV):
        y = y + cw_ref[j:j + 1, :] * xs_ref[pl.ds(BF16_ROWS - (B_CONV - 1) + j, tm), :]
    y = _silu(y)
    for h in range(B_HEADS):
        sl = slice(h * LANES, (h + 1) * LANES)
        qh = y[:, sl]
        q_ref[0, :, sl] = qh * lax.rsqrt(jnp.sum(qh * qh, axis=-1, keepdims=True) + EPS) * (B_HEAD_DIM ** -0.5)
        kh = y[:, B_WIDTH + h * LANES:B_WIDTH + (h + 1) * LANES]
        k_ref[0, :, sl] = kh * lax.rsqrt(jnp.sum(kh * kh, axis=-1, keepdims=True) + EPS)
    v_ref[0] = y[:, 2 * B_WIDTH:]

    sm = sm_ref[0]
    xg = sm + dtb_ref[...]
    softplus = jnp.maximum(xg, 0.0) + jnp.log(1.0 + jnp.exp(-jnp.abs(xg)))
    g = -jnp.exp(alog_ref[...]) * softplus
    r = lax.broadcasted_iota(jnp.int32, (tm, tm), 0)
    c = lax.broadcasted_iota(jnp.int32, (tm, tm), 1)
    sh = int(math.log2(GDN_CHUNK))
    same_chunk = jnp.right_shift(r, sh) == jnp.right_shift(c, sh)
    tri = jnp.where(jnp.logical_and(same_chunk, c <= r), 1.0, 0.0).astype(F32)
    gc = lax.dot_general(tri, g, (((1,), (0,)), ((), ())), precision=HIGHEST, preferred_element_type=F32)
    gcb_ref[0] = lax.dot_general(gc, eg_ref[...], (((1,), (0,)), ((), ())), precision=HIGHEST,
                                 preferred_element_type=F32)
    bb_ref[0] = lax.dot_general(_sigmoid(sm), eb_ref[...], (((1,), (0,)), ((), ())), precision=HIGHEST,
                                preferred_element_type=F32)
    gct = gc.T
    for cc in range(tm // GDN_CHUNK):
        grow_ref[0, cc] = gct[SM_ALPHA:SM_ALPHA + B_HEADS, cc * GDN_CHUNK:(cc + 1) * GDN_CHUNK]


def gdn_prep(proj3d, small3d, conv_w, a_log, dt_bias, b, t):
    tm = _tile(t, 256)
    lane = jnp.arange(LANES)
    pad_row = lambda v: jnp.zeros((1, LANES), F32).at[0, SM_ALPHA:SM_ALPHA + B_HEADS].set(v)
    rows = lane[:, None]
    cols = jnp.arange(B_WIDTH)[None, :]
    e_g = (rows == SM_ALPHA + cols // LANES).astype(F32)
    e_b = (rows == SM_BETA + cols // LANES).astype(F32)
    nc = tm // GDN_CHUNK
    act = jax.ShapeDtypeStruct((b, t, B_WIDTH), F32)
    full = lambda a: pl.BlockSpec(a.shape, lambda bb, i: (0,) * a.ndim)
    consts = [conv_w, pad_row(a_log), pad_row(dt_bias), e_g, e_b]
    return pl.pallas_call(
        functools.partial(_gdn_prep_body, tm=tm),
        grid=(b, t // tm),
        in_specs=[
            pl.BlockSpec((1, tm, 3 * B_WIDTH), lambda bb, i: (bb, i, COL_QKV_B // (3 * B_WIDTH))),
            pl.BlockSpec((1, BF16_ROWS, 3 * B_WIDTH),
                         lambda bb, i: (bb, jnp.maximum(i * (tm // BF16_ROWS) - 1, 0), 0)),
            pl.BlockSpec((1, tm, LANES), lambda bb, i: (bb, i, 0)),
        ] + [full(c) for c in consts],
        out_specs=[pl.BlockSpec((1, tm, B_WIDTH), lambda bb, i: (bb, i, 0))] * 5
        + [pl.BlockSpec((1, nc, B_HEADS, GDN_CHUNK), lambda bb, i: (bb, i, 0, 0))],
        out_shape=[act] * 5 + [jax.ShapeDtypeStruct((b, t // GDN_CHUNK, B_HEADS, GDN_CHUNK), F32)],
        scratch_shapes=[pltpu.VMEM((tm + BF16_ROWS, 3 * B_WIDTH), F32)],
        compiler_params=_params("parallel", "parallel"),
        name="gdn_prep",
    )(proj3d, proj3d, small3d, *consts)


def _gdn_body(q_ref, k_ref, v_ref, gcb_ref, bb_ref, grow_ref, z_ref, on_ref, o_ref, s_ref, *, nc):
    @pl.when(pl.program_id(1) == 0)
    def _():
        s_ref[...] = jnp.zeros(s_ref.shape, F32)

    cs, nh = GDN_CHUNK, B_HEADS
    nb = nc * nh
    ri = lax.broadcasted_iota(jnp.int32, (nb, cs, cs), 1)
    ci = lax.broadcasted_iota(jnp.int32, (nb, cs, cs), 2)
    lower = ci <= ri
    strict = ci < ri
    eye = jnp.where(ci == ri, 1.0, 0.0).astype(F32)
    bf = lambda a: a.astype(BF16)
    bmm = lambda a, b: lax.dot_general(a, b, (((2,), (1,)), ((0,), (0,))), preferred_element_type=F32)
    bmm_nt = lambda a, b: lax.dot_general(a, b, (((2,), (2,)), ((0,), (0,))), preferred_element_type=F32)
    bmm_tn = lambda a, b: lax.dot_general(a, b, (((1,), (1,)), ((0,), (0,))), preferred_element_type=F32)

    def stack(ref):
        return jnp.stack([ref[0, c * cs:(c + 1) * cs, h * LANES:(h + 1) * LANES]
                          for c in range(nc) for h in range(nh)])

    q, k, v = stack(q_ref), stack(k_ref), stack(v_ref)
    gcb = stack(gcb_ref)
    beta = stack(bb_ref)
    grow = jnp.stack([grow_ref[0, c, h:h + 1, :] for c in range(nc) for h in range(nh)])
    diff = gcb[:, :, :cs] - grow
    decay = jnp.where(lower, jnp.exp(jnp.where(lower, diff, 0.0)), 0.0)
    eg = jnp.exp(gcb)
    glast = gcb[:, cs - 1:cs, :]
    kb = k * beta
    lmat = jnp.where(strict, bmm_nt(bf(kb), bf(k)) * decay, 0.0)
    n = -lmat
    tinv = eye + n
    for _ in range(int(math.log2(cs)) - 1):
        n = bmm(bf(n), bf(n))
        tinv = tinv + bmm(bf(tinv), bf(n))
    uw = bmm(bf(tinv), bf(jnp.concatenate([v * beta, kb * eg], axis=-1)))
    attn = bf(jnp.where(lower, bmm_nt(bf(q), bf(k)) * decay, 0.0))
    qg = bf(q * eg)
    kdec = bf(k * jnp.exp(glast - gcb))
    egl = jnp.exp(glast)

    s = s_ref[...]
    for c in range(nc):
        sl = slice(c * nh, (c + 1) * nh)
        sb = bf(s)
        v_new = uw[sl, :, :LANES] - bmm(bf(uw[sl, :, LANES:]), sb)
        o = bmm(qg[sl], sb) + bmm(attn[sl], bf(v_new))
        s = s * egl[sl] + bmm_tn(kdec[sl], bf(v_new))
        o = o * lax.rsqrt(jnp.mean(o * o, axis=-1, keepdims=True) + EPS) * on_ref[...]
        for h in range(nh):
            rows, cols = slice(c * cs, (c + 1) * cs), slice(h * LANES, (h + 1) * LANES)
            o_ref[0, rows, cols] = (o[h] * _silu(z_ref[0, rows, cols].astype(F32))).astype(o_ref.dtype)
    s_ref[...] = s


def gdn_scan(qh, kh, v, gcb, bb, grow, proj3d, o_norm, b, t):
    tt = _tile(t, 256)
    nc = tt // GDN_CHUNK
    blk = pl.BlockSpec((1, tt, B_WIDTH), lambda bb_, i: (bb_, i, 0))
    return pl.pallas_call(
        functools.partial(_gdn_body, nc=nc),
        grid=(b, t // tt),
        in_specs=[blk] * 5 + [
            pl.BlockSpec((1, nc, B_HEADS, GDN_CHUNK), lambda bb_, i: (bb_, i, 0, 0)),
            pl.BlockSpec((1, tt, B_WIDTH), lambda bb_, i: (bb_, i, COL_Z_B // B_WIDTH)),
            pl.BlockSpec((1, LANES), lambda bb_, i: (0, 0)),
        ],
        out_specs=blk,
        out_shape=jax.ShapeDtypeStruct((b, t, B_WIDTH), BF16),
        scratch_shapes=[pltpu.VMEM((B_HEADS, B_HEAD_DIM, B_HEAD_DIM), F32)],
        compiler_params=_params("parallel", "arbitrary"),
        name="gdn_scan",
    )(qh, kh, v, gcb, bb, grow, proj3d, o_norm.reshape(1, LANES))


C_HALO = 32
D_HALO = BF16_ROWS


def _cd_mix_body(a_ref, g_ref, ah_ref, gh_ref, zc_ref, bg_ref, cg_ref, ud_ref, cgh_ref, udh_ref, zd_ref,
                 dww_ref, dwb_ref, lnw_ref, lnb_ref, dcw_ref, yc_ref, yd_ref, us_ref, ds_ref, *, tm):
    i = pl.program_id(1)
    first = i == 0
    f32 = lambda r: r[0].astype(F32)
    uh = f32(ah_ref) * _sigmoid(f32(gh_ref))
    us_ref[0, 0:C_HALO, :] = jnp.where(first, jnp.zeros_like(uh), uh)
    us_ref[0, C_HALO:, :] = f32(a_ref) * _sigmoid(f32(g_ref))
    span = tm + C_HALO - SUBLANES
    for r in range(1, SUBLANES):
        us_ref[r, 0:span, :] = us_ref[0, pl.ds(r, span), :]
    u = jnp.zeros((tm, C_WIDTH), F32)
    for j in range(C_CONV):
        off = C_HALO - (C_CONV - 1) + j
        r, base = off % SUBLANES, off - off % SUBLANES
        u = u + dww_ref[j:j + 1, :] * us_ref[r, base:base + tm, :]
    u = u + dwb_ref[...]
    mu = jnp.mean(u, axis=-1, keepdims=True)
    uc = u - mu
    u = uc * lax.rsqrt(jnp.mean(uc * uc, axis=-1, keepdims=True) + EPS) * lnw_ref[...] + lnb_ref[...]
    yc_ref[0] = (_silu(u) * _silu(f32(zc_ref))).astype(yc_ref.dtype)

    dh = f32(cgh_ref) * f32(udh_ref)
    ds_ref[0:D_HALO, :] = jnp.where(first, jnp.zeros_like(dh), dh)
    ds_ref[D_HALO:, :] = f32(cg_ref) * f32(ud_ref)
    d = jnp.zeros((tm, D_WIDTH), F32)
    for j in range(D_CONV):
        d = d + dcw_ref[j:j + 1, :] * ds_ref[pl.ds(D_HALO - (D_CONV - 1) + j, tm), :]
    yd_ref[0] = (f32(bg_ref) * d * _silu(f32(zd_ref))).astype(yd_ref.dtype)


def cd_mix(proj3d, dw_w, dw_b, ln_w, ln_b, d_conv_w, b, t):
    tm = _tile(t, 256)
    w = C_WIDTH
    col = lambda n: pl.BlockSpec((1, tm, w), lambda bb, i, n=n: (bb, i, n))
    halo = lambda n, rows: pl.BlockSpec(
        (1, rows, w), lambda bb, i, n=n, rows=rows: (bb, jnp.maximum(i * (tm // rows) - 1, 0), n))
    row = lambda a: a.reshape(1, -1)
    full = lambda a: pl.BlockSpec(a.shape, lambda bb, i: (0,) * a.ndim)
    consts = [dw_w, row(dw_b), row(ln_w), row(ln_b), d_conv_w]
    out = jax.ShapeDtypeStruct((b, t, w), BF16)
    return pl.pallas_call(
        functools.partial(_cd_mix_body, tm=tm),
        grid=(b, t // tm),
        in_specs=[col(0), col(1), halo(0, C_HALO), halo(1, C_HALO), col(2), col(3), col(4), col(5),
                  halo(4, D_HALO), halo(5, D_HALO), col(6)] + [full(c) for c in consts],
        out_specs=[pl.BlockSpec((1, tm, w), lambda bb, i: (bb, i, 0))] * 2,
        out_shape=[out, out],
        scratch_shapes=[pltpu.VMEM((SUBLANES, tm + C_HALO, w), F32), pltpu.VMEM((tm + D_HALO, w), F32)],
        compiler_params=_params("parallel", "parallel"),
        name="cd_mix",
    )(*([proj3d] * 11), *consts)


def _reorder_ab_w_in(w):
    offs = [0]
    for s in AB_SPLITS:
        offs.append(offs[-1] + s)
    part = lambda n: w[:, offs[n]:offs[n + 1]]
    c_q, c_kv, k_idx, w_idx, z_a, qkv_b, beta_b, alpha_b, z_b = (part(n) for n in range(9))
    small = jnp.concatenate([k_idx, w_idx, beta_b, alpha_b], axis=1)
    small = jnp.pad(small, ((0, 0), (0, LANES - small.shape[1])))
    out = jnp.concatenate([qkv_b, z_a, z_b, c_q, c_kv], axis=1)
    return jnp.pad(out, ((0, 0), (0, AB_IN_PAD - out.shape[1]))).astype(BF16), small.astype(BF16)


def _ab_layer(x2d, b, t, norm_w, rel_bias, w_in, q_norm, w_uq, w_iq, kv_norm, w_uk, w_uv, q_gain, k_gain,
              ik_w, ik_b, conv_w, a_log, dt_bias, o_norm, w_out):
    proj, small = norm_matmul(x2d, norm_w, *_reorder_ab_w_in(w_in))
    proj3d, small3d = proj.reshape(b, t, AB_IN_PAD), small.reshape(b, t, LANES)
    q, k, v, qi, k_even, k_odd = dsa_prep(proj, small, q_norm, kv_norm, w_uq, w_iq, w_uk, w_uv, q_gain, k_gain,
                                          ik_w, ik_b)
    n_sel = min(TOPK_MAX, t // 4)
    mask = indexer_mask(qi, small3d, k_even, k_odd, b, t, n_sel)
    tq = _tile(t, 512)
    y_a = attention(q, k, v, mask, bias_tiles(rel_bias, tq), proj3d, b, t, tq)
    qh, kh, vv, gcb, bb, grow = gdn_prep(proj3d, small3d, conv_w, a_log, dt_bias, b, t)
    y_b = gdn_scan(qh, kh, vv, gcb, bb, grow, proj3d, o_norm, b, t)
    wo = w_out.astype(BF16)
    return out_proj(x2d, y_a.reshape(b * t, -1), y_b.reshape(b * t, -1), wo[:A_WIDTH], wo[A_WIDTH:])


def _cd_layer(x2d, b, t, norm_w, w_in, dw_w, dw_b, ln_w, ln_b, d_conv_w, w_out):
    proj = norm_matmul(x2d, norm_w, w_in.astype(BF16))
    y_c, y_d = cd_mix(proj.reshape(b, t, -1), dw_w, dw_b, ln_w, ln_b, d_conv_w, b, t)
    wo = w_out.astype(BF16)
    return out_proj(x2d, y_c.reshape(b * t, -1), y_d.reshape(b * t, -1), wo[:C_WIDTH], wo[C_WIDTH:])


def kernel(x, norm_w, rel_bias, ab_w_in, a_q_norm, a_w_uq, a_w_iq, a_kv_norm, a_w_uk, a_w_uv, a_q_gain,
           a_k_gain, a_ik_norm_w, a_ik_norm_b, b_conv_w, b_a_log, b_dt_bias, b_o_norm, ab_w_out, cd_w_in,
           c_dw_w, c_dw_b, c_ln_w, c_ln_b, d_conv_w, cd_w_out):
    b, t, d = x.shape
    depth = norm_w.shape[0]
    x2d = x.reshape(b * t, d)
    for i in range(depth):
        j = i // 2
        if i % 2 == 0:
            x2d = _ab_layer(x2d, b, t, norm_w[i], rel_bias, ab_w_in[j], a_q_norm[j], a_w_uq[j], a_w_iq[j],
                            a_kv_norm[j], a_w_uk[j], a_w_uv[j], a_q_gain[j], a_k_gain[j], a_ik_norm_w[j],
                            a_ik_norm_b[j], b_conv_w[j], b_a_log[j], b_dt_bias[j], b_o_norm[j], ab_w_out[j])
        else:
            x2d = _cd_layer(x2d, b, t, norm_w[i], cd_w_in[j], c_dw_w[j], c_dw_b[j], c_ln_w[j], c_ln_b[j],
                            d_conv_w[j], cd_w_out[j])
    return x2d.reshape(b, t, d)
```

```python
import functools
import math

import jax
import jax.numpy as jnp
from jax import lax
from jax.experimental import pallas as pl
from jax.experimental.pallas import tpu as pltpu

D_MODEL = 2048
A_HEADS = 8
A_HEAD_DIM = 128
A_WIDTH = A_HEADS * A_HEAD_DIM
A_Q_LORA = 512
A_KV_LORA = 256
IDX_HEADS = 16
IDX_DIM = 64
TOPK_MAX = 256
REL_BUCKETS = 32
REL_MAX_DIST = 128
B_HEADS = 8
B_HEAD_DIM = 128
B_WIDTH = B_HEADS * B_HEAD_DIM
B_CONV = 4
GDN_CHUNK = 64
C_WIDTH = 1024
C_CONV = 31
D_WIDTH = 1024
D_CONV = 3
EPS = 1e-6

AB_SPLITS = (A_Q_LORA, A_KV_LORA, IDX_DIM, IDX_HEADS, A_WIDTH, 3 * B_WIDTH, B_HEADS, B_HEADS, B_WIDTH)
AB_IN_PAD = 6144
SM_KIDX = 0
SM_WIDX = 64
SM_BETA = 80
SM_ALPHA = 88
COL_QKV_B = 0
COL_Z_A = 3072
COL_Z_B = 4096
COL_C_Q = 5120
COL_C_KV = 5632

LANES = 128
MASK_NEG = -1e30
INT_MIN = -(2 ** 31)
LOG2E = math.log2(math.e)
BF16_ROWS = 16
V_ROWS = A_HEAD_DIM + BF16_ROWS
VMEM_LIMIT = 56 * 1024 * 1024

F32 = jnp.float32
BF16 = jnp.bfloat16
HIGHEST = lax.Precision.HIGHEST


def _tile(n, pref):
    t = min(n, pref)
    assert n % t == 0, (n, t)
    return t


def _params(*sem):
    return pltpu.CompilerParams(dimension_semantics=sem, vmem_limit_bytes=VMEM_LIMIT)


def _dot(a, b):
    return jnp.dot(a, b, preferred_element_type=F32)


def _dot_nt(a, b):
    return lax.dot_general(a, b, (((1,), (1,)), ((), ())), preferred_element_type=F32)


def _dot_tn(a, b):
    return lax.dot_general(a, b, (((0,), (0,)), ((), ())), preferred_element_type=F32)


def _silu(x):
    return x * (1.0 / (1.0 + jnp.exp(-x)))


def _sigmoid(x):
    return 1.0 / (1.0 + jnp.exp(-x))


def _norm_matmul_body(*refs, with_f32_cols):
    if with_f32_cols:
        x_ref, nw_ref, w_ref, ws_ref, o_ref, os_ref, h_ref = refs
    else:
        x_ref, nw_ref, w_ref, o_ref, h_ref = refs

    @pl.when(pl.program_id(1) == 0)
    def _():
        x = x_ref[...]
        ms = jnp.mean(x * x, axis=-1, keepdims=True)
        h_ref[...] = (x * lax.rsqrt(ms + EPS) * nw_ref[...]).astype(h_ref.dtype)
        if with_f32_cols:
            os_ref[...] = _dot(h_ref[...], ws_ref[...])

    o_ref[...] = _dot(h_ref[...], w_ref[...]).astype(o_ref.dtype)


def norm_matmul(x2d, norm_w, w_bf16, w_f32_cols=None):
    m, k = x2d.shape
    n = w_bf16.shape[1]
    tm, tn = _tile(m, 1024), _tile(n, 1024)
    extra = w_f32_cols is not None
    in_specs = [
        pl.BlockSpec((tm, k), lambda i, j: (i, 0)),
        pl.BlockSpec((1, k), lambda i, j: (0, 0)),
        pl.BlockSpec((k, tn), lambda i, j: (0, j)),
    ]
    out_specs = [pl.BlockSpec((tm, tn), lambda i, j: (i, j))]
    out_shape = [jax.ShapeDtypeStruct((m, n), BF16)]
    args = [x2d, norm_w.reshape(1, k), w_bf16]
    if extra:
        in_specs.append(pl.BlockSpec((k, LANES), lambda i, j: (0, 0)))
        out_specs.append(pl.BlockSpec((tm, LANES), lambda i, j: (i, 0)))
        out_shape.append(jax.ShapeDtypeStruct((m, LANES), F32))
        args.append(w_f32_cols)
    res = pl.pallas_call(
        functools.partial(_norm_matmul_body, with_f32_cols=extra),
        grid=(m // tm, n // tn),
        in_specs=in_specs,
        out_specs=out_specs,
        out_shape=out_shape,
        scratch_shapes=[pltpu.VMEM((tm, k), BF16)],
        compiler_params=_params("parallel", "arbitrary"),
        name="norm_matmul",
    )(*args)
    return res if extra else res[0]


def _out_proj_body(x_ref, ya_ref, yb_ref, wa_ref, wb_ref, o_ref):
    o_ref[...] = x_ref[...] + _dot(ya_ref[...], wa_ref[...]) + _dot(yb_ref[...], wb_ref[...])


def out_proj(x2d, ya, yb, wa_bf16, wb_bf16):
    m, n = x2d.shape
    ka, kb = ya.shape[1], yb.shape[1]
    tm = _tile(m, 512)
    return pl.pallas_call(
        _out_proj_body,
        grid=(m // tm,),
        in_specs=[
            pl.BlockSpec((tm, n), lambda i: (i, 0)),
            pl.BlockSpec((tm, ka), lambda i: (i, 0)),
            pl.BlockSpec((tm, kb), lambda i: (i, 0)),
            pl.BlockSpec((ka, n), lambda i: (0, 0)),
            pl.BlockSpec((kb, n), lambda i: (0, 0)),
        ],
        out_specs=pl.BlockSpec((tm, n), lambda i: (i, 0)),
        out_shape=jax.ShapeDtypeStruct((m, n), F32),
        compiler_params=_params("parallel"),
        name="out_proj",
    )(x2d, ya, yb, wa_bf16, wb_bf16)


def _head_rmsnorm(y, gain_row, scale):
    outs = []
    for h in range(y.shape[1] // LANES):
        seg = y[:, h * LANES:(h + 1) * LANES]
        ms = jnp.mean(seg * seg, axis=-1, keepdims=True)
        outs.append(seg * lax.rsqrt(ms + EPS) * (gain_row * scale))
    return jnp.concatenate(outs, axis=-1)


def _dsa_prep_body(cq_ref, ckv_ref, sm_ref, qn_ref, kvn_ref, wuq_ref, wiq_ref, wuk_ref, wuv_ref,
                   qg_ref, kg_ref, ikw_ref, ikb_ref,
                   q_ref, k_ref, v_ref, qi_ref, ke_ref, ko_ref):
    cq = cq_ref[...].astype(F32)
    cq = cq * lax.rsqrt(jnp.mean(cq * cq, axis=-1, keepdims=True) + EPS) * qn_ref[...]
    cqb = cq.astype(BF16)
    q = _dot(cqb, wuq_ref[...])
    q_ref[...] = _head_rmsnorm(q, qg_ref[...], A_HEAD_DIM ** -0.5 * LOG2E).astype(q_ref.dtype)
    qi_ref[...] = _dot(cqb, wiq_ref[...]).astype(qi_ref.dtype)

    ckv = ckv_ref[...].astype(F32)
    ckv = ckv * lax.rsqrt(jnp.mean(ckv * ckv, axis=-1, keepdims=True) + EPS) * kvn_ref[...]
    ckvb = ckv.astype(BF16)
    k = _dot(ckvb, wuk_ref[...])
    k_ref[...] = _head_rmsnorm(k, kg_ref[...], 1.0).astype(k_ref.dtype)
    vt = _dot_nt(wuv_ref[...], ckvb).astype(v_ref.dtype)
    ones = jnp.ones((BF16_ROWS, vt.shape[1]), v_ref.dtype)
    for h in range(A_HEADS):
        v_ref[h * V_ROWS:h * V_ROWS + A_HEAD_DIM, :] = vt[h * A_HEAD_DIM:(h + 1) * A_HEAD_DIM, :]
        v_ref[h * V_ROWS + A_HEAD_DIM:(h + 1) * V_ROWS, :] = ones

    kx = sm_ref[...][:, SM_KIDX:SM_KIDX + IDX_DIM]
    mu = jnp.mean(kx, axis=-1, keepdims=True)
    kc = kx - mu
    kl = kc * lax.rsqrt(jnp.mean(kc * kc, axis=-1, keepdims=True) + EPS) * ikw_ref[...] + ikb_ref[...]
    z = jnp.zeros_like(kl)
    ke_ref[...] = jnp.concatenate([kl, z], axis=-1).astype(ke_ref.dtype)
    ko_ref[...] = jnp.concatenate([z, kl], axis=-1).astype(ko_ref.dtype)


def dsa_prep(proj, small, q_norm, kv_norm, w_uq, w_iq, w_uk, w_uv, q_gain, k_gain, ik_w, ik_b):
    m = proj.shape[0]
    tm = _tile(m, 512)
    row = lambda a: a.reshape(1, -1)
    full = lambda a: pl.BlockSpec(a.shape, lambda i: (0,) * a.ndim)
    consts = [row(q_norm), row(kv_norm), w_uq.astype(BF16), w_iq.astype(BF16), w_uk.astype(BF16),
              w_uv.T.astype(BF16), row(q_gain), row(k_gain), row(ik_w), row(ik_b)]
    outs = [jax.ShapeDtypeStruct((m, A_WIDTH), BF16)] * 2 + [
        jax.ShapeDtypeStruct((A_HEADS * V_ROWS, m), BF16),
        jax.ShapeDtypeStruct((m, IDX_HEADS * IDX_DIM), BF16),
        jax.ShapeDtypeStruct((m, LANES), BF16), jax.ShapeDtypeStruct((m, LANES), BF16)]
    out_specs = [pl.BlockSpec((tm, s.shape[1]), lambda i: (i, 0)) for s in outs]
    out_specs[2] = pl.BlockSpec((A_HEADS * V_ROWS, tm), lambda i: (0, i))
    return pl.pallas_call(
        _dsa_prep_body,
        grid=(m // tm,),
        in_specs=[
            pl.BlockSpec((tm, A_Q_LORA), lambda i: (i, COL_C_Q // A_Q_LORA)),
            pl.BlockSpec((tm, A_KV_LORA), lambda i: (i, COL_C_KV // A_KV_LORA)),
            pl.BlockSpec((tm, LANES), lambda i: (i, 0)),
        ] + [full(c) for c in consts],
        out_specs=out_specs,
        out_shape=outs,
        compiler_params=_params("parallel"),
        name="dsa_prep",
    )(proj, proj, small, *consts)


def _t5_bucket(dist):
    max_exact = REL_BUCKETS // 2
    large = max_exact + (jnp.log(jnp.maximum(dist, 1).astype(F32) / max_exact)
                         / math.log(REL_MAX_DIST / max_exact) * (REL_BUCKETS - max_exact)).astype(jnp.int32)
    large = jnp.minimum(large, REL_BUCKETS - 1)
    return jnp.where(dist < max_exact, dist, large)


def _bias_tiles_body(rb_ref, o_ref, *, t):
    h = pl.program_id(0)
    off = pl.program_id(1) * t
    dist = off + lax.broadcasted_iota(jnp.int32, (t, t), 1) - lax.broadcasted_iota(jnp.int32, (t, t), 0)
    bucket = _t5_bucket(jnp.maximum(dist, 0))
    acc = jnp.zeros((t, t), F32)
    for b in range(REL_BUCKETS):
        acc = jnp.where(bucket == b, rb_ref[b, h], acc)
    o_ref[0, 0] = ((acc - rb_ref[REL_BUCKETS - 1, h]) * LOG2E).astype(o_ref.dtype)


def bias_tiles(rel_bias, t):
    return pl.pallas_call(
        functools.partial(_bias_tiles_body, t=t),
        grid=(A_HEADS, 2),
        in_specs=[pl.BlockSpec(memory_space=pltpu.SMEM)],
        out_specs=pl.BlockSpec((1, 1, t, t), lambda h, c: (c, h, 0, 0)),
        out_shape=jax.ShapeDtypeStruct((2, A_HEADS, t, t), BF16),
        compiler_params=_params("parallel", "parallel"),
        name="bias_tiles",
    )(rel_bias)


SUBLANES = 8
MAX_SELECT_STEPS = 40
SELECT_STEPS_PER_CHECK = 4


def _key_of(x):
    bits = pltpu.bitcast(x, jnp.int32)
    key = jnp.where(bits < 0, bits ^ jnp.int32(0x7FFFFFFF), bits)
    return jnp.where(bits == jnp.int32(INT_MIN), 0, key)


def _indexer_body(qi_ref, sm_ref, ke_ref, ko_ref, mask_ref, key_ref, w_ref, *, tq, ck, n_sel):
    i = pl.program_id(1)
    nk = ((i + 1) * tq + ck - 1) // ck
    groups = ck // SUBLANES
    scale = (IDX_HEADS ** -0.5) * (IDX_DIM ** -0.5)
    smt = sm_ref[0].T
    for h in range(IDX_HEADS):
        w_ref[h] = jnp.broadcast_to(smt[SM_WIDX + h:SM_WIDX + h + 1, :] * scale, (SUBLANES, tq))
    kloc = lax.broadcasted_iota(jnp.int32, (ck, tq), 0)
    qpos = i * tq + lax.broadcasted_iota(jnp.int32, (ck, tq), 1)
    int_max = jnp.int32(2 ** 31 - 1)

    def score_chunk(c, carry):
        kmin, kmax = carry
        ks = pl.multiple_of(c * ck, ck)
        ke = ke_ref[0, pl.ds(ks, ck), :]
        ko = ko_ref[0, pl.ds(ks, ck), :]
        acc = jnp.zeros((ck, tq), F32)
        for j in range(IDX_HEADS // 2):
            qp = qi_ref[0, :, j * LANES:(j + 1) * LANES]
            for par, kk in ((0, ke), (1, ko)):
                s = jnp.maximum(_dot_nt(kk, qp), 0.0)
                acc = acc + s * jnp.tile(w_ref[2 * j + par], (groups, 1))
        key = _key_of(acc)
        valid = c * ck + kloc <= qpos
        key_ref[pl.ds(ks, ck), :] = jnp.where(valid, key, jnp.int32(INT_MIN))
        kmin = jnp.minimum(kmin, jnp.min(jnp.where(valid, key, int_max).reshape(groups, SUBLANES, tq), axis=0))
        kmax = jnp.maximum(kmax, jnp.max(jnp.where(valid, key, jnp.int32(INT_MIN)).reshape(groups, SUBLANES, tq), axis=0))
        return kmin, kmax

    kmin, kmax = lax.fori_loop(0, nk, score_chunk, (jnp.full((SUBLANES, tq), int_max, jnp.int32),
                                                    jnp.full((SUBLANES, tq), INT_MIN, jnp.int32)))

    def count_ge(p):
        p8 = jnp.broadcast_to(p, (SUBLANES, tq))

        def body(c, cnt):
            ks = pl.multiple_of(c * ck, ck)
            kk = key_ref[pl.ds(ks, ck), :].reshape(groups, SUBLANES, tq)
            return cnt + jnp.sum(jnp.where(kk >= p8[None], 1, 0), axis=0)

        cnt = lax.fori_loop(0, nk, body, jnp.zeros((SUBLANES, tq), jnp.int32))
        return jnp.sum(cnt, axis=0, keepdims=True)

    n_valid = i * tq + lax.broadcasted_iota(jnp.int32, (1, tq), 1) + 1
    few = n_valid <= n_sel
    lo0 = jnp.min(kmin, axis=0, keepdims=True)
    hi0 = jnp.max(kmax, axis=0, keepdims=True) + 1

    def open_rows(lo, hi, clo):
        return jnp.logical_not(few | (clo == n_sel) | (hi == lo + 1))

    def cond(st):
        it, lo, hi, clo, chi = st
        n_open = jnp.max(jnp.where(open_rows(lo, hi, clo), 1, 0))
        return jnp.logical_and(it < MAX_SELECT_STEPS, n_open > 0)

    def step(st):
        it, lo, hi, clo, chi = st
        for _ in range(SELECT_STEPS_PER_CHECK):
            upd = open_rows(lo, hi, clo)
            p = lo + jnp.maximum(lax.shift_right_logical(hi - lo, 1), 1)
            cnt = count_ge(p)
            ge = cnt >= n_sel
            up, dn = upd & ge, upd & jnp.logical_not(ge)
            lo, hi = jnp.where(up, p, lo), jnp.where(dn, p, hi)
            clo, chi = jnp.where(up, cnt, clo), jnp.where(dn, cnt, chi)
        return it + SELECT_STEPS_PER_CHECK, lo, hi, clo, chi

    _, lo, _, _, _ = lax.while_loop(cond, step, (jnp.int32(0), lo0, hi0, n_valid, jnp.zeros((1, tq), jnp.int32)))
    thr = jnp.where(few, jnp.int32(INT_MIN + 1), lo)
    thr8 = jnp.broadcast_to(thr, (SUBLANES, tq))

    def write_chunk(c, carry):
        ks = pl.multiple_of(c * ck, ck)
        kk = key_ref[pl.ds(ks, ck), :].reshape(groups, SUBLANES, tq)
        mask_ref[0, pl.ds(ks, ck), :] = jnp.where(kk >= thr8[None], 1, 0).reshape(ck, tq).astype(mask_ref.dtype)
        return carry

    lax.fori_loop(0, nk, write_chunk, 0)

    def zero_chunk(c, carry):
        ks = pl.multiple_of(c * ck, ck)
        mask_ref[0, pl.ds(ks, ck), :] = jnp.zeros((ck, tq), mask_ref.dtype)
        return carry

    lax.fori_loop(nk, mask_ref.shape[1] // ck, zero_chunk, 0)


def indexer_mask(qi, small, k_even, k_odd, b, t, n_sel):
    tq = _tile(t, 256)
    ck = _tile(t, 512)
    return pl.pallas_call(
        functools.partial(_indexer_body, tq=tq, ck=ck, n_sel=n_sel),
        grid=(b, t // tq),
        in_specs=[
            pl.BlockSpec((1, tq, IDX_HEADS * IDX_DIM), lambda bb, i: (bb, i, 0)),
            pl.BlockSpec((1, tq, LANES), lambda bb, i: (bb, i, 0)),
            pl.BlockSpec((1, t, LANES), lambda bb, i: (bb, 0, 0)),
            pl.BlockSpec((1, t, LANES), lambda bb, i: (bb, 0, 0)),
        ],
        out_specs=pl.BlockSpec((1, t, tq), lambda bb, i: (bb, 0, i)),
        out_shape=jax.ShapeDtypeStruct((b, t, t), jnp.int8),
        scratch_shapes=[pltpu.VMEM((t, tq), jnp.int32), pltpu.VMEM((IDX_HEADS, SUBLANES, tq), F32)],
        compiler_params=_params("parallel", "parallel"),
        name="indexer",
    )(qi.reshape(b, t, -1), small, k_even.reshape(b, t, LANES), k_odd.reshape(b, t, LANES))


def _attn_body(qi_ref, ki_ref, q_ref, k_ref, vt_ref, mask_ref, bias_ref, z_ref, o_ref, m_ref, acc_ref, *, tq):
    qi, ki = qi_ref[pl.program_id(1)], ki_ref[pl.program_id(1)]

    @pl.when(ki == 0)
    def _():
        m_ref[...] = jnp.full(m_ref.shape, MASK_NEG, F32)
        acc_ref[...] = jnp.zeros(acc_ref.shape, F32)

    def tile(near):
        madd = (1.0 - mask_ref[0].astype(F32)) * MASK_NEG
        off = jnp.where(ki == qi, 0, 1)
        for h in range(A_HEADS):
            sl = slice(h * A_HEAD_DIM, (h + 1) * A_HEAD_DIM)
            s = _dot_nt(k_ref[0, :, sl], q_ref[0, :, sl]) + madd
            if near:
                s = s + bias_ref[off, h].astype(F32)
            m_prev = m_ref[h]
            m_new = jnp.maximum(m_prev, jnp.max(s, axis=0, keepdims=True))
            p = jnp.exp2(s - m_new).astype(BF16)
            acc_ref[h] = jnp.exp2(m_prev - m_new) * acc_ref[h] + _dot(vt_ref[h * V_ROWS:(h + 1) * V_ROWS, :], p)
            m_ref[h] = m_new

    @pl.when(ki + 1 < qi)
    def _():
        tile(False)

    @pl.when(ki + 1 >= qi)
    def _():
        tile(True)

    @pl.when(ki == qi)
    def _():
        outs = []
        for h in range(A_HEADS):
            a = acc_ref[h]
            outs.append((a[:A_HEAD_DIM] / a[A_HEAD_DIM:A_HEAD_DIM + 1]).T)
        o_ref[0] = (jnp.concatenate(outs, axis=-1) * _silu(z_ref[0].astype(F32))).astype(o_ref.dtype)


def attention(q, k, vt, mask, bias, proj3d, b, t, tq):
    nq = t // tq
    pairs = [(i, j) for i in range(nq) for j in range(i + 1)]
    qi_tab = jnp.asarray([p[0] for p in pairs], jnp.int32)
    ki_tab = jnp.asarray([p[1] for p in pairs], jnp.int32)
    grid_spec = pltpu.PrefetchScalarGridSpec(
        num_scalar_prefetch=2,
        grid=(b, len(pairs)),
        in_specs=[
            pl.BlockSpec((1, tq, A_WIDTH), lambda bb, p, qi, ki: (bb, qi[p], 0)),
            pl.BlockSpec((1, tq, A_WIDTH), lambda bb, p, qi, ki: (bb, ki[p], 0)),
            pl.BlockSpec((A_HEADS * V_ROWS, tq), lambda bb, p, qi, ki: (0, bb * nq + ki[p])),
            pl.BlockSpec((1, tq, tq), lambda bb, p, qi, ki: (bb, ki[p], qi[p])),
            pl.BlockSpec(bias.shape, lambda bb, p, qi, ki: (0, 0, 0, 0)),
            pl.BlockSpec((1, tq, A_WIDTH), lambda bb, p, qi, ki: (bb, qi[p], COL_Z_A // A_WIDTH)),
        ],
        out_specs=pl.BlockSpec((1, tq, A_WIDTH), lambda bb, p, qi, ki: (bb, qi[p], 0)),
        scratch_shapes=[pltpu.VMEM((A_HEADS, 1, tq), F32), pltpu.VMEM((A_HEADS, V_ROWS, tq), F32)],
    )
    return pl.pallas_call(
        functools.partial(_attn_body, tq=tq),
        grid_spec=grid_spec,
        out_shape=jax.ShapeDtypeStruct((b, t, A_WIDTH), BF16),
        compiler_params=_params("parallel", "arbitrary"),
        name="dsa_attention",
    )(qi_tab, ki_tab, q.reshape(b, t, -1), k.reshape(b, t, -1), vt, mask, bias, proj3d)


def _gdn_prep_body(x_ref, halo_ref, sm_ref, cw_ref, alog_ref, dtb_ref, eg_ref, eb_ref,
                   q_ref, k_ref, v_ref, gcb_ref, bb_ref, grow_ref, xs_ref, *, tm):
    i = pl.program_id(1)
    hal = halo_ref[0].astype(F32)
    xs_ref[0:BF16_ROWS, :] = jnp.where(i > 0, hal, jnp.zeros_like(hal))
    xs_ref[BF16_ROWS:, :] = x_ref[0].astype(F32)
    y = jnp.zeros((tm, 3 * B_WIDTH), F32)
    for j in range(B_CONV):
        y = y + cw_ref[j:j + 1, :] * xs_ref[pl.ds(BF16_ROWS - (B_CONV - 1) + j, tm), :]
    y = _silu(y)
    for h in range(B_HEADS):
        sl = slice(h * LANES, (h + 1) * LANES)
        qh = y[:, sl]
        q_ref[0, :, sl] = qh * lax.rsqrt(jnp.sum(qh * qh, axis=-1, keepdims=True) + EPS) * (B_HEAD_DIM ** -0.5)
        kh = y[:, B_WIDTH + h * LANES:B_WIDTH + (h + 1) * LANES]
        k_ref[0, :, sl] = kh * lax.rsqrt(jnp.sum(kh * kh, axis=-1, keepdims=True) + EPS)
    v_ref[0] = y[:, 2 * B_WIDTH:]

    sm = sm_ref[0]
    xg = sm + dtb_ref[...]
    softplus = jnp.maximum(xg, 0.0) + jnp.log(1.0 + jnp.exp(-jnp.abs(xg)))
    g = -jnp.exp(alog_ref[...]) * softplus
    r = lax.broadcasted_iota(jnp.int32, (tm, tm), 0)
    c = lax.broadcasted_iota(jnp.int32, (tm, tm), 1)
    sh = int(math.log2(GDN_CHUNK))
    same_chunk = jnp.right_shift(r, sh) == jnp.right_shift(c, sh)
    tri = jnp.where(jnp.logical_and(same_chunk, c <= r), 1.0, 0.0).astype(F32)
    gc = lax.dot_general(tri, g, (((1,), (0,)), ((), ())), precision=HIGHEST, preferred_element_type=F32)
    gcb_ref[0] = lax.dot_general(gc, eg_ref[...], (((1,), (0,)), ((), ())), precision=HIGHEST,
                                 preferred_element_type=F32)
    bb_ref[0] = lax.dot_general(_sigmoid(sm), eb_ref[...], (((1,), (0,)), ((), ())), precision=HIGHEST,
                                preferred_element_type=F32)
    gct = gc.T
    for cc in range(tm // GDN_CHUNK):
        grow_ref[0, cc] = gct[SM_ALPHA:SM_ALPHA + B_HEADS, cc * GDN_CHUNK:(cc + 1) * GDN_CHUNK]


def gdn_prep(proj3d, small3d, conv_w, a_log, dt_bias, b, t):
    tm = _tile(t, 256)
    lane = jnp.arange(LANES)
    pad_row = lambda v: jnp.zeros((1, LANES), F32).at[0, SM_ALPHA:SM_ALPHA + B_HEADS].set(v)
    rows = lane[:, None]
    cols = jnp.arange(B_WIDTH)[None, :]
    e_g = (rows == SM_ALPHA + cols // LANES).astype(F32)
    e_b = (rows == SM_BETA + cols // LANES).astype(F32)
    nc = tm // GDN_CHUNK
    act = jax.ShapeDtypeStruct((b, t, B_WIDTH), F32)
    full = lambda a: pl.BlockSpec(a.shape, lambda bb, i: (0,) * a.ndim)
    consts = [conv_w, pad_row(a_log), pad_row(dt_bias), e_g, e_b]
    return pl.pallas_call(
        functools.partial(_gdn_prep_body, tm=tm),
        grid=(b, t // tm),
        in_specs=[
            pl.BlockSpec((1, tm, 3 * B_WIDTH), lambda bb, i: (bb, i, COL_QKV_B // (3 * B_WIDTH))),
            pl.BlockSpec((1, BF16_ROWS, 3 * B_WIDTH),
                         lambda bb, i: (bb, jnp.maximum(i * (tm // BF16_ROWS) - 1, 0), 0)),
            pl.BlockSpec((1, tm, LANES), lambda bb, i: (bb, i, 0)),
        ] + [full(c) for c in consts],
        out_specs=[pl.BlockSpec((1, tm, B_WIDTH), lambda bb, i: (bb, i, 0))] * 5
        + [pl.BlockSpec((1, nc, B_HEADS, GDN_CHUNK), lambda bb, i: (bb, i, 0, 0))],
        out_shape=[act] * 5 + [jax.ShapeDtypeStruct((b, t // GDN_CHUNK, B_HEADS, GDN_CHUNK), F32)],
        scratch_shapes=[pltpu.VMEM((tm + BF16_ROWS, 3 * B_WIDTH), F32)],
        compiler_params=_params("parallel", "parallel"),
        name="gdn_prep",
    )(proj3d, proj3d, small3d, *consts)


def _gdn_body(q_ref, k_ref, v_ref, gcb_ref, bb_ref, grow_ref, z_ref, on_ref, o_ref, s_ref, *, nc):
    @pl.when(pl.program_id(1) == 0)
    def _():
        s_ref[...] = jnp.zeros(s_ref.shape, F32)

    cs, nh = GDN_CHUNK, B_HEADS
    nb = nc * nh
    ri = lax.broadcasted_iota(jnp.int32, (nb, cs, cs), 1)
    ci = lax.broadcasted_iota(jnp.int32, (nb, cs, cs), 2)
    lower = ci <= ri
    strict = ci < ri
    eye = jnp.where(ci == ri, 1.0, 0.0).astype(F32)
    bf = lambda a: a.astype(BF16)
    bmm = lambda a, b: lax.dot_general(a, b, (((2,), (1,)), ((0,), (0,))), preferred_element_type=F32)
    bmm_nt = lambda a, b: lax.dot_general(a, b, (((2,), (2,)), ((0,), (0,))), preferred_element_type=F32)
    bmm_tn = lambda a, b: lax.dot_general(a, b, (((1,), (1,)), ((0,), (0,))), preferred_element_type=F32)

    def stack(ref):
        return jnp.stack([ref[0, c * cs:(c + 1) * cs, h * LANES:(h + 1) * LANES]
                          for c in range(nc) for h in range(nh)])

    q, k, v = stack(q_ref), stack(k_ref), stack(v_ref)
    gcb = stack(gcb_ref)
    beta = stack(bb_ref)
    grow = jnp.stack([grow_ref[0, c, h:h + 1, :] for c in range(nc) for h in range(nh)])
    diff = gcb[:, :, :cs] - grow
    decay = jnp.where(lower, jnp.exp(jnp.where(lower, diff, 0.0)), 0.0)
    eg = jnp.exp(gcb)
    glast = gcb[:, cs - 1:cs, :]
    kb = k * beta
    lmat = jnp.where(strict, bmm_nt(bf(kb), bf(k)) * decay, 0.0)
    n = -lmat
    tinv = eye + n
    for _ in range(int(math.log2(cs)) - 1):
        n = bmm(bf(n), bf(n))
        tinv = tinv + bmm(bf(tinv), bf(n))
    uw = bmm(bf(tinv), bf(jnp.concatenate([v * beta, kb * eg], axis=-1)))
    attn = bf(jnp.where(lower, bmm_nt(bf(q), bf(k)) * decay, 0.0))
    qg = bf(q * eg)
    kdec = bf(k * jnp.exp(glast - gcb))
    egl = jnp.exp(glast)

    s = s_ref[...]
    for c in range(nc):
        sl = slice(c * nh, (c + 1) * nh)
        sb = bf(s)
        v_new = uw[sl, :, :LANES] - bmm(bf(uw[sl, :, LANES:]), sb)
        o = bmm(qg[sl], sb) + bmm(attn[sl], bf(v_new))
        s = s * egl[sl] + bmm_tn(kdec[sl], bf(v_new))
        o = o * lax.rsqrt(jnp.mean(o * o, axis=-1, keepdims=True) + EPS) * on_ref[...]
        for h in range(nh):
            rows, cols = slice(c * cs, (c + 1) * cs), slice(h * LANES, (h + 1) * LANES)
            o_ref[0, rows, cols] = (o[h] * _silu(z_ref[0, rows, cols].astype(F32))).astype(o_ref.dtype)
    s_ref[...] = s


def gdn_scan(qh, kh, v, gcb, bb, grow, proj3d, o_norm, b, t):
    tt = _tile(t, 256)
    nc = tt // GDN_CHUNK
    blk = pl.BlockSpec((1, tt, B_WIDTH), lambda bb_, i: (bb_, i, 0))
    return pl.pallas_call(
        functools.partial(_gdn_body, nc=nc),
        grid=(b, t // tt),
        in_specs=[blk] * 5 + [
            pl.BlockSpec((1, nc, B_HEADS, GDN_CHUNK), lambda bb_, i: (bb_, i, 0, 0)),
            pl.BlockSpec((1, tt, B_WIDTH), lambda bb_, i: (bb_, i, COL_Z_B // B_WIDTH)),
            pl.BlockSpec((1, LANES), lambda bb_, i: (0, 0)),
        ],
        out_specs=blk,
        out_shape=jax.ShapeDtypeStruct((b, t, B_WIDTH), BF16),
        scratch_shapes=[pltpu.VMEM((B_HEADS, B_HEAD_DIM, B_HEAD_DIM), F32)],
        compiler_params=_params("parallel", "arbitrary"),
        name="gdn_scan",
    )(qh, kh, v, gcb, bb, grow, proj3d, o_norm.reshape(1, LANES))


C_HALO = 32
D_HALO = BF16_ROWS


def _cd_mix_body(a_ref, g_ref, ah_ref, gh_ref, zc_ref, bg_ref, cg_ref, ud_ref, cgh_ref, udh_ref, zd_ref,
                 dww_ref, dwb_ref, lnw_ref, lnb_ref, dcw_ref, yc_ref, yd_ref, us_ref, ds_ref, *, tm):
    i = pl.program_id(1)
    first = i == 0
    f32 = lambda r: r[0].astype(F32)
    uh = f32(ah_ref) * _sigmoid(f32(gh_ref))
    us_ref[0, 0:C_HALO, :] = jnp.where(first, jnp.zeros_like(uh), uh)
    us_ref[0, C_HALO:, :] = f32(a_ref) * _sigmoid(f32(g_ref))
    span = tm + C_HALO - SUBLANES
    for r in range(1, SUBLANES):
        us_ref[r, 0:span, :] = us_ref[0, pl.ds(r, span), :]
    u = jnp.zeros((tm, C_WIDTH), F32)
    for j in range(C_CONV):
        off = C_HALO - (C_CONV - 1) + j
        r, base = off % SUBLANES, off - off % SUBLANES
        u = u + dww_ref[j:j + 1, :] * us_ref[r, base:base + tm, :]
    u = u + dwb_ref[...]
    mu = jnp.mean(u, axis=-1, keepdims=True)
    uc = u - mu
    u = uc * lax.rsqrt(jnp.mean(uc * uc, axis=-1, keepdims=True) + EPS) * lnw_ref[...] + lnb_ref[...]
    yc_ref[0] = (_silu(u) * _silu(f32(zc_ref))).astype(yc_ref.dtype)

    dh = f32(cgh_ref) * f32(udh_ref)
    ds_ref[0:D_HALO, :] = jnp.where(first, jnp.zeros_like(dh), dh)
    ds_ref[D_HALO:, :] = f32(cg_ref) * f32(ud_ref)
    d = jnp.zeros((tm, D_WIDTH), F32)
    for j in range(D_CONV):
        d = d + dcw_ref[j:j + 1, :] * ds_ref[pl.ds(D_HALO - (D_CONV - 1) + j, tm), :]
    yd_ref[0] = (f32(bg_ref) * d * _silu(f32(zd_ref))).astype(yd_ref.dtype)


def cd_mix(proj3d, dw_w, dw_b, ln_w, ln_b, d_conv_w, b, t):
    tm = _tile(t, 256)
    w = C_WIDTH
    col = lambda n: pl.BlockSpec((1, tm, w), lambda bb, i, n=n: (bb, i, n))
    halo = lambda n, rows: pl.BlockSpec(
        (1, rows, w), lambda bb, i, n=n, rows=rows: (bb, jnp.maximum(i * (tm // rows) - 1, 0), n))
    row = lambda a: a.reshape(1, -1)
    full = lambda a: pl.BlockSpec(a.shape, lambda bb, i: (0,) * a.ndim)
    consts = [dw_w, row(dw_b), row(ln_w), row(ln_b), d_conv_w]
    out = jax.ShapeDtypeStruct((b, t, w), BF16)
    return pl.pallas_call(
        functools.partial(_cd_mix_body, tm=tm),
        grid=(b, t // tm),
        in_specs=[col(0), col(1), halo(0, C_HALO), halo(1, C_HALO), col(2), col(3), col(4), col(5),
                  halo(4, D_HALO), halo(5, D_HALO), col(6)] + [full(c) for c in consts],
        out_specs=[pl.BlockSpec((1, tm, w), lambda bb, i: (bb, i, 0))] * 2,
        out_shape=[out, out],
        scratch_shapes=[pltpu.VMEM((SUBLANES, tm + C_HALO, w), F32), pltpu.VMEM((tm + D_HALO, w), F32)],
        compiler_params=_params("parallel", "parallel"),
        name="cd_mix",
    )(*([proj3d] * 11), *consts)


def _reorder_ab_w_in(w):
    offs = [0]
    for s in AB_SPLITS:
        offs.append(offs[-1] + s)
    part = lambda n: w[:, offs[n]:offs[n + 1]]
    c_q, c_kv, k_idx, w_idx, z_a, qkv_b, beta_b, alpha_b, z_b = (part(n) for n in range(9))
    small = jnp.concatenate([k_idx, w_idx, beta_b, alpha_b], axis=1)
    small = jnp.pad(small, ((0, 0), (0, LANES - small.shape[1])))
    out = jnp.concatenate([qkv_b, z_a, z_b, c_q, c_kv], axis=1)
    return jnp.pad(out, ((0, 0), (0, AB_IN_PAD - out.shape[1]))).astype(BF16), small.astype(BF16)


def _ab_layer(x2d, b, t, norm_w, rel_bias, w_in, q_norm, w_uq, w_iq, kv_norm, w_uk, w_uv, q_gain, k_gain,
              ik_w, ik_b, conv_w, a_log, dt_bias, o_norm, w_out):
    proj, small = norm_matmul(x2d, norm_w, *_reorder_ab_w_in(w_in))
    proj3d, small3d = proj.reshape(b, t, AB_IN_PAD), small.reshape(b, t, LANES)
    q, k, v, qi, k_even, k_odd = dsa_prep(proj, small, q_norm, kv_norm, w_uq, w_iq, w_uk, w_uv, q_gain, k_gain,
                                          ik_w, ik_b)
    n_sel = min(TOPK_MAX, t // 4)
    mask = indexer_mask(qi, small3d, k_even, k_odd, b, t, n_sel)
    tq = _tile(t, 512)
    y_a = attention(q, k, v, mask, bias_tiles(rel_bias, tq), proj3d, b, t, tq)
    qh, kh, vv, gcb, bb, grow = gdn_prep(proj3d, small3d, conv_w, a_log, dt_bias, b, t)
    y_b = gdn_scan(qh, kh, vv, gcb, bb, grow, proj3d, o_norm, b, t)
    wo = w_out.astype(BF16)
    return out_proj(x2d, y_a.reshape(b * t, -1), y_b.reshape(b * t, -1), wo[:A_WIDTH], wo[A_WIDTH:])


def _cd_layer(x2d, b, t, norm_w, w_in, dw_w, dw_b, ln_w, ln_b, d_conv_w, w_out):
    proj = norm_matmul(x2d, norm_w, w_in.astype(BF16))
    y_c, y_d = cd_mix(proj.reshape(b, t, -1), dw_w, dw_b, ln_w, ln_b, d_conv_w, b, t)
    wo = w_out.astype(BF16)
    return out_proj(x2d, y_c.reshape(b * t, -1), y_d.reshape(b * t, -1), wo[:C_WIDTH], wo[C_WIDTH:])


def kernel(x, norm_w, rel_bias, ab_w_in, a_q_norm, a_w_uq, a_w_iq, a_kv_norm, a_w_uk, a_w_uv, a_q_gain,
           a_k_gain, a_ik_norm_w, a_ik_norm_b, b_conv_w, b_a_log, b_dt_bias, b_o_norm, ab_w_out, cd_w_in,
           c_dw_w, c_dw_b, c_ln_w, c_ln_b, d_conv_w, cd_w_out):
    b, t, d = x.shape
    depth = norm_w.shape[0]
    x2d = x.reshape(b * t, d)
    for i in range(depth):
        j = i // 2
        if i % 2 == 0:
            x2d = _ab_layer(x2d, b, t, norm_w[i], rel_bias, ab_w_in[j], a_q_norm[j], a_w_uq[j], a_w_iq[j],
                            a_kv_norm[j], a_w_uk[j], a_w_uv[j], a_q_gain[j], a_k_gain[j], a_ik_norm_w[j],
                            a_ik_norm_b[j], b_conv_w[j], b_a_log[j], b_dt_bias[j], b_o_norm[j], ab_w_out[j])
        else:
            x2d = _cd_layer(x2d, b, t, norm_w[i], cd_w_in[j], c_dw_w[j], c_dw_b[j], c_ln_w[j], c_ln_b[j],
                            d_conv_w[j], cd_w_out[j])
    return x2d.reshape(b, t, d)
```

```python
import functools
import math

import jax
import jax.numpy as jnp
from jax import lax
from jax.experimental import pallas as pl
from jax.experimental.pallas import tpu as pltpu

D_MODEL = 2048
A_HEADS = 8
A_HEAD_DIM = 128
A_WIDTH = A_HEADS * A_HEAD_DIM
A_Q_LORA = 512
A_KV_LORA = 256
IDX_HEADS = 16
IDX_DIM = 64
TOPK_MAX = 256
REL_BUCKETS = 32
REL_MAX_DIST = 128
B_HEADS = 8
B_HEAD_DIM = 128
B_WIDTH = B_HEADS * B_HEAD_DIM
B_CONV = 4
GDN_CHUNK = 64
C_WIDTH = 1024
C_CONV = 31
D_WIDTH = 1024
D_CONV = 3
EPS = 1e-6

AB_SPLITS = (A_Q_LORA, A_KV_LORA, IDX_DIM, IDX_HEADS, A_WIDTH, 3 * B_WIDTH, B_HEADS, B_HEADS, B_WIDTH)
AB_IN_PAD = 6144
SM_KIDX = 0
SM_WIDX = 64
SM_BETA = 80
SM_ALPHA = 88
COL_QKV_B = 0
COL_Z_A = 3072
COL_Z_B = 4096
COL_C_Q = 5120
COL_C_KV = 5632

LANES = 128
MASK_NEG = -1e30
INT_MIN = -(2 ** 31)
LOG2E = math.log2(math.e)
BF16_ROWS = 16
V_ROWS = A_HEAD_DIM + BF16_ROWS
VMEM_LIMIT = 56 * 1024 * 1024

F32 = jnp.float32
BF16 = jnp.bfloat16
HIGHEST = lax.Precision.HIGHEST


def _tile(n, pref):
    t = min(n, pref)
    assert n % t == 0, (n, t)
    return t


def _params(*sem):
    return pltpu.CompilerParams(dimension_semantics=sem, vmem_limit_bytes=VMEM_LIMIT)


def _dot(a, b):
    return jnp.dot(a, b, preferred_element_type=F32)


def _dot_nt(a, b):
    return lax.dot_general(a, b, (((1,), (1,)), ((), ())), preferred_element_type=F32)


def _dot_tn(a, b):
    return lax.dot_general(a, b, (((0,), (0,)), ((), ())), preferred_element_type=F32)


def _silu(x):
    return x * (1.0 / (1.0 + jnp.exp(-x)))


def _sigmoid(x):
    return 1.0 / (1.0 + jnp.exp(-x))


def _norm_matmul_body(*refs, with_f32_cols):
    if with_f32_cols:
        x_ref, nw_ref, w_ref, ws_ref, o_ref, os_ref, h_ref = refs
    else:
        x_ref, nw_ref, w_ref, o_ref, h_ref = refs

    @pl.when(pl.program_id(1) == 0)
    def _():
        x = x_ref[...]
        ms = jnp.mean(x * x, axis=-1, keepdims=True)
        h_ref[...] = (x * lax.rsqrt(ms + EPS) * nw_ref[...]).astype(h_ref.dtype)
        if with_f32_cols:
            os_ref[...] = _dot(h_ref[...], ws_ref[...])

    o_ref[...] = _dot(h_ref[...], w_ref[...]).astype(o_ref.dtype)


def norm_matmul(x2d, norm_w, w_bf16, w_f32_cols=None):
    m, k = x2d.shape
    n = w_bf16.shape[1]
    tm, tn = _tile(m, 1024), _tile(n, 1024)
    extra = w_f32_cols is not None
    in_specs = [
        pl.BlockSpec((tm, k), lambda i, j: (i, 0)),
        pl.BlockSpec((1, k), lambda i, j: (0, 0)),
        pl.BlockSpec((k, tn), lambda i, j: (0, j)),
    ]
    out_specs = [pl.BlockSpec((tm, tn), lambda i, j: (i, j))]
    out_shape = [jax.ShapeDtypeStruct((m, n), BF16)]
    args = [x2d, norm_w.reshape(1, k), w_bf16]
    if extra:
        in_specs.append(pl.BlockSpec((k, LANES), lambda i, j: (0, 0)))
        out_specs.append(pl.BlockSpec((tm, LANES), lambda i, j: (i, 0)))
        out_shape.append(jax.ShapeDtypeStruct((m, LANES), F32))
        args.append(w_f32_cols)
    res = pl.pallas_call(
        functools.partial(_norm_matmul_body, with_f32_cols=extra),
        grid=(m // tm, n // tn),
        in_specs=in_specs,
        out_specs=out_specs,
        out_shape=out_shape,
        scratch_shapes=[pltpu.VMEM((tm, k), BF16)],
        compiler_params=_params("parallel", "arbitrary"),
        name="norm_matmul",
    )(*args)
    return res if extra else res[0]


def _out_proj_body(x_ref, ya_ref, yb_ref, wa_ref, wb_ref, o_ref):
    o_ref[...] = x_ref[...] + _dot(ya_ref[...], wa_ref[...]) + _dot(yb_ref[...], wb_ref[...])


def out_proj(x2d, ya, yb, wa_bf16, wb_bf16):
    m, n = x2d.shape
    ka, kb = ya.shape[1], yb.shape[1]
    tm = _tile(m, 512)
    return pl.pallas_call(
        _out_proj_body,
        grid=(m // tm,),
        in_specs=[
            pl.BlockSpec((tm, n), lambda i: (i, 0)),
            pl.BlockSpec((tm, ka), lambda i: (i, 0)),
            pl.BlockSpec((tm, kb), lambda i: (i, 0)),
            pl.BlockSpec((ka, n), lambda i: (0, 0)),
            pl.BlockSpec((kb, n), lambda i: (0, 0)),
        ],
        out_specs=pl.BlockSpec((tm, n), lambda i: (i, 0)),
        out_shape=jax.ShapeDtypeStruct((m, n), F32),
        compiler_params=_params("parallel"),
        name="out_proj",
    )(x2d, ya, yb, wa_bf16, wb_bf16)


def _head_rmsnorm(y, gain_row, scale):
    outs = []
    for h in range(y.shape[1] // LANES):
        seg = y[:, h * LANES:(h + 1) * LANES]
        ms = jnp.mean(seg * seg, axis=-1, keepdims=True)
        outs.append(seg * lax.rsqrt(ms + EPS) * (gain_row * scale))
    return jnp.concatenate(outs, axis=-1)


def _dsa_prep_body(cq_ref, ckv_ref, sm_ref, qn_ref, kvn_ref, wuq_ref, wiq_ref, wuk_ref, wuv_ref,
                   qg_ref, kg_ref, ikw_ref, ikb_ref,
                   q_ref, k_ref, v_ref, qi_ref, ke_ref, ko_ref):
    cq = cq_ref[...].astype(F32)
    cq = cq * lax.rsqrt(jnp.mean(cq * cq, axis=-1, keepdims=True) + EPS) * qn_ref[...]
    cqb = cq.astype(BF16)
    q = _dot(cqb, wuq_ref[...])
    q_ref[...] = _head_rmsnorm(q, qg_ref[...], A_HEAD_DIM ** -0.5 * LOG2E).astype(q_ref.dtype)
    qi_ref[...] = _dot(cqb, wiq_ref[...]).astype(qi_ref.dtype)

    ckv = ckv_ref[...].astype(F32)
    ckv = ckv * lax.rsqrt(jnp.mean(ckv * ckv, axis=-1, keepdims=True) + EPS) * kvn_ref[...]
    ckvb = ckv.astype(BF16)
    k = _dot(ckvb, wuk_ref[...])
    k_ref[...] = _head_rmsnorm(k, kg_ref[...], 1.0).astype(k_ref.dtype)
    vt = _dot_nt(wuv_ref[...], ckvb).astype(v_ref.dtype)
    ones = jnp.ones((BF16_ROWS, vt.shape[1]), v_ref.dtype)
    for h in range(A_HEADS):
        v_ref[h * V_ROWS:h * V_ROWS + A_HEAD_DIM, :] = vt[h * A_HEAD_DIM:(h + 1) * A_HEAD_DIM, :]
        v_ref[h * V_ROWS + A_HEAD_DIM:(h + 1) * V_ROWS, :] = ones

    kx = sm_ref[...][:, SM_KIDX:SM_KIDX + IDX_DIM]
    mu = jnp.mean(kx, axis=-1, keepdims=True)
    kc = kx - mu
    kl = kc * lax.rsqrt(jnp.mean(kc * kc, axis=-1, keepdims=True) + EPS) * ikw_ref[...] + ikb_ref[...]
    z = jnp.zeros_like(kl)
    ke_ref[...] = jnp.concatenate([kl, z], axis=-1).astype(ke_ref.dtype)
    ko_ref[...] = jnp.concatenate([z, kl], axis=-1).astype(ko_ref.dtype)


def dsa_prep(proj, small, q_norm, kv_norm, w_uq, w_iq, w_uk, w_uv, q_gain, k_gain, ik_w, ik_b):
    m = proj.shape[0]
    tm = _tile(m, 512)
    row = lambda a: a.reshape(1, -1)
    full = lambda a: pl.BlockSpec(a.shape, lambda i: (0,) * a.ndim)
    consts = [row(q_norm), row(kv_norm), w_uq.astype(BF16), w_iq.astype(BF16), w_uk.astype(BF16),
              w_uv.T.astype(BF16), row(q_gain), row(k_gain), row(ik_w), row(ik_b)]
    outs = [jax.ShapeDtypeStruct((m, A_WIDTH), BF16)] * 2 + [
        jax.ShapeDtypeStruct((A_HEADS * V_ROWS, m), BF16),
        jax.ShapeDtypeStruct((m, IDX_HEADS * IDX_DIM), BF16),
        jax.ShapeDtypeStruct((m, LANES), BF16), jax.ShapeDtypeStruct((m, LANES), BF16)]
    out_specs = [pl.BlockSpec((tm, s.shape[1]), lambda i: (i, 0)) for s in outs]
    out_specs[2] = pl.BlockSpec((A_HEADS * V_ROWS, tm), lambda i: (0, i))
    return pl.pallas_call(
        _dsa_prep_body,
        grid=(m // tm,),
        in_specs=[
            pl.BlockSpec((tm, A_Q_LORA), lambda i: (i, COL_C_Q // A_Q_LORA)),
            pl.BlockSpec((tm, A_KV_LORA), lambda i: (i, COL_C_KV // A_KV_LORA)),
            pl.BlockSpec((tm, LANES), lambda i: (i, 0)),
        ] + [full(c) for c in consts],
        out_specs=out_specs,
        out_shape=outs,
        compiler_params=_params("parallel"),
        name="dsa_prep",
    )(proj, proj, small, *consts)


def _t5_bucket(dist):
    max_exact = REL_BUCKETS // 2
    large = max_exact + (jnp.log(jnp.maximum(dist, 1).astype(F32) / max_exact)
                         / math.log(REL_MAX_DIST / max_exact) * (REL_BUCKETS - max_exact)).astype(jnp.int32)
    large = jnp.minimum(large, REL_BUCKETS - 1)
    return jnp.where(dist < max_exact, dist, large)


def _bias_tiles_body(rb_ref, o_ref, *, t):
    h = pl.program_id(0)
    off = pl.program_id(1) * t
    dist = off + lax.broadcasted_iota(jnp.int32, (t, t), 1) - lax.broadcasted_iota(jnp.int32, (t, t), 0)
    bucket = _t5_bucket(jnp.maximum(dist, 0))
    acc = jnp.zeros((t, t), F32)
    for b in range(REL_BUCKETS):
        acc = jnp.where(bucket == b, rb_ref[b, h], acc)
    o_ref[0, 0] = ((acc - rb_ref[REL_BUCKETS - 1, h]) * LOG2E).astype(o_ref.dtype)


def bias_tiles(rel_bias, t):
    return pl.pallas_call(
        functools.partial(_bias_tiles_body, t=t),
        grid=(A_HEADS, 2),
        in_specs=[pl.BlockSpec(memory_space=pltpu.SMEM)],
        out_specs=pl.BlockSpec((1, 1, t, t), lambda h, c: (c, h, 0, 0)),
        out_shape=jax.ShapeDtypeStruct((2, A_HEADS, t, t), BF16),
        compiler_params=_params("parallel", "parallel"),
        name="bias_tiles",
    )(rel_bias)


SUBLANES = 8
MAX_SELECT_STEPS = 40
SELECT_STEPS_PER_CHECK = 4


def _key_of(x):
    bits = pltpu.bitcast(x, jnp.int32)
    key = jnp.where(bits < 0, bits ^ jnp.int32(0x7FFFFFFF), bits)
    return jnp.where(bits == jnp.int32(INT_MIN), 0, key)


def _indexer_body(qi_ref, sm_ref, ke_ref, ko_ref, mask_ref, key_ref, w_ref, *, tq, ck, n_sel):
    i = pl.program_id(1)
    nk = ((i + 1) * tq + ck - 1) // ck
    groups = ck // SUBLANES
    scale = (IDX_HEADS ** -0.5) * (IDX_DIM ** -0.5)
    smt = sm_ref[0].T
    for h in range(IDX_HEADS):
        w_ref[h] = jnp.broadcast_to(smt[SM_WIDX + h:SM_WIDX + h + 1, :] * scale, (SUBLANES, tq))
    kloc = lax.broadcasted_iota(jnp.int32, (ck, tq), 0)
    qpos = i * tq + lax.broadcasted_iota(jnp.int32, (ck, tq), 1)
    int_max = jnp.int32(2 ** 31 - 1)

    def score_chunk(c, carry):
        kmin, kmax = carry
        ks = pl.multiple_of(c * ck, ck)
        ke = ke_ref[0, pl.ds(ks, ck), :]
        ko = ko_ref[0, pl.ds(ks, ck), :]
        acc = jnp.zeros((ck, tq), F32)
        for j in range(IDX_HEADS // 2):
            qp = qi_ref[0, :, j * LANES:(j + 1) * LANES]
            for par, kk in ((0, ke), (1, ko)):
                s = jnp.maximum(_dot_nt(kk, qp), 0.0)
                acc = acc + s * jnp.tile(w_ref[2 * j + par], (groups, 1))
        key = _key_of(acc)
        valid = c * ck + kloc <= qpos
        key_ref[pl.ds(ks, ck), :] = jnp.where(valid, key, jnp.int32(INT_MIN))
        kmin = jnp.minimum(kmin, jnp.min(jnp.where(valid, key, int_max).reshape(groups, SUBLANES, tq), axis=0))
        kmax = jnp.maximum(kmax, jnp.max(jnp.where(valid, key, jnp.int32(INT_MIN)).reshape(groups, SUBLANES, tq), axis=0))
        return kmin, kmax

    kmin, kmax = lax.fori_loop(0, nk, score_chunk, (jnp.full((SUBLANES, tq), int_max, jnp.int32),
                                                    jnp.full((SUBLANES, tq), INT_MIN, jnp.int32)))

    def count_ge(p):
        p8 = jnp.broadcast_to(p, (SUBLANES, tq))

        def body(c, cnt):
            ks = pl.multiple_of(c * ck, ck)
            kk = key_ref[pl.ds(ks, ck), :].reshape(groups, SUBLANES, tq)
            return cnt + jnp.sum(jnp.where(kk >= p8[None], 1, 0), axis=0)

        cnt = lax.fori_loop(0, nk, body, jnp.zeros((SUBLANES, tq), jnp.int32))
        return jnp.sum(cnt, axis=0, keepdims=True)

    n_valid = i * tq + lax.broadcasted_iota(jnp.int32, (1, tq), 1) + 1
    few = n_valid <= n_sel
    lo0 = jnp.min(kmin, axis=0, keepdims=True)
    hi0 = jnp.max(kmax, axis=0, keepdims=True) + 1

    def open_rows(lo, hi, clo):
        return jnp.logical_not(few | (clo == n_sel) | (hi == lo + 1))

    def cond(st):
        it, lo, hi, clo, chi = st
        n_open = jnp.max(jnp.where(open_rows(lo, hi, clo), 1, 0))
        return jnp.logical_and(it < MAX_SELECT_STEPS, n_open > 0)

    def step(st):
        it, lo, hi, clo, chi = st
        for _ in range(SELECT_STEPS_PER_CHECK):
            upd = open_rows(lo, hi, clo)
            p = lo + jnp.maximum(lax.shift_right_logical(hi - lo, 1), 1)
            cnt = count_ge(p)
            ge = cnt >= n_sel
            up, dn = upd & ge, upd & jnp.logical_not(ge)
            lo, hi = jnp.where(up, p, lo), jnp.where(dn, p, hi)
            clo, chi = jnp.where(up, cnt, clo), jnp.where(dn, cnt, chi)
        return it + SELECT_STEPS_PER_CHECK, lo, hi, clo, chi

    _, lo, _, clo, chi = lax.while_loop(cond, step,
                                        (jnp.int32(0), lo0, hi0, n_valid, jnp.zeros((1, tq), jnp.int32)))
    thr = jnp.where(few, jnp.int32(INT_MIN + 1), lo)
    thr8 = jnp.broadcast_to(thr, (SUBLANES, tq))
    tied = jnp.logical_not(few) & (clo > n_sel)
    need = n_sel - chi
    n_keys = mask_ref.shape[1]

    def tie_cut(_):
        def count_le(j):
            j8 = jnp.broadcast_to(j, (SUBLANES, tq))

            def body(c, cnt):
                ks = pl.multiple_of(c * ck, ck)
                kk = key_ref[pl.ds(ks, ck), :].reshape(groups, SUBLANES, tq)
                idx = (c * ck + kloc).reshape(groups, SUBLANES, tq)
                hit = jnp.where(kk == thr8[None], jnp.where(idx <= j8[None], 1, 0), 0)
                return cnt + jnp.sum(hit, axis=0)

            cnt = lax.fori_loop(0, nk, body, jnp.zeros((SUBLANES, tq), jnp.int32))
            return jnp.sum(cnt, axis=0, keepdims=True)

        def bisect(_, st):
            jlo, jhi = st
            mid = jlo + jnp.right_shift(jhi - jlo, 1)
            ok = count_le(mid) >= need
            return jnp.where(ok, jlo, mid), jnp.where(ok, mid, jhi)

        _, jhi = lax.fori_loop(0, n_keys.bit_length(), bisect,
                               (jnp.full((1, tq), -1, jnp.int32), jnp.full((1, tq), n_keys - 1, jnp.int32)))
        return jnp.where(tied, jhi, int_max)

    jcut = lax.cond(jnp.max(jnp.where(tied, 1, 0)) > 0, tie_cut, lambda _: jnp.full((1, tq), int_max, jnp.int32), 0)
    jcut8 = jnp.broadcast_to(jcut, (SUBLANES, tq))

    def write_chunk(c, carry):
        ks = pl.multiple_of(c * ck, ck)
        kk = key_ref[pl.ds(ks, ck), :].reshape(groups, SUBLANES, tq)
        idx = (c * ck + kloc).reshape(groups, SUBLANES, tq)
        at_thr = jnp.where(kk == thr8[None], jnp.where(idx <= jcut8[None], 1, 0), 0)
        sel = jnp.where(kk > thr8[None], 1, at_thr)
        mask_ref[0, pl.ds(ks, ck), :] = sel.reshape(ck, tq).astype(mask_ref.dtype)
        return carry

    lax.fori_loop(0, nk, write_chunk, 0)

    def zero_chunk(c, carry):
        ks = pl.multiple_of(c * ck, ck)
        mask_ref[0, pl.ds(ks, ck), :] = jnp.zeros((ck, tq), mask_ref.dtype)
        return carry

    lax.fori_loop(nk, mask_ref.shape[1] // ck, zero_chunk, 0)


def indexer_mask(qi, small, k_even, k_odd, b, t, n_sel):
    tq = _tile(t, 256)
    ck = _tile(t, 512)
    return pl.pallas_call(
        functools.partial(_indexer_body, tq=tq, ck=ck, n_sel=n_sel),
        grid=(b, t // tq),
        in_specs=[
            pl.BlockSpec((1, tq, IDX_HEADS * IDX_DIM), lambda bb, i: (bb, i, 0)),
            pl.BlockSpec((1, tq, LANES), lambda bb, i: (bb, i, 0)),
            pl.BlockSpec((1, t, LANES), lambda bb, i: (bb, 0, 0)),
            pl.BlockSpec((1, t, LANES), lambda bb, i: (bb, 0, 0)),
        ],
        out_specs=pl.BlockSpec((1, t, tq), lambda bb, i: (bb, 0, i)),
        out_shape=jax.ShapeDtypeStruct((b, t, t), jnp.int8),
        scratch_shapes=[pltpu.VMEM((t, tq), jnp.int32), pltpu.VMEM((IDX_HEADS, SUBLANES, tq), F32)],
        compiler_params=_params("parallel", "parallel"),
        name="indexer",
    )(qi.reshape(b, t, -1), small, k_even.reshape(b, t, LANES), k_odd.reshape(b, t, LANES))


ATTN_HEAD_GROUP = 4


def _attn_body(qi_ref, ki_ref, q_ref, k_ref, vt_ref, mask_ref, bias_ref, z_ref, o_ref, m_ref, acc_ref, *, tq):
    qi, ki = qi_ref[pl.program_id(1)], ki_ref[pl.program_id(1)]

    @pl.when(ki == 0)
    def _():
        m_ref[...] = jnp.full(m_ref.shape, MASK_NEG, F32)
        acc_ref[...] = jnp.zeros(acc_ref.shape, F32)

    def tile(near):
        madd = (1.0 - mask_ref[0].astype(F32)) * MASK_NEG
        off = jnp.where(ki == qi, 0, 1)
        for g in range(0, A_HEADS, ATTN_HEAD_GROUP):
            hs = range(g, g + ATTN_HEAD_GROUP)
            grp = slice(g, g + ATTN_HEAD_GROUP)
            head = lambda ref, h: ref[0, :, h * A_HEAD_DIM:(h + 1) * A_HEAD_DIM]
            k2 = jnp.stack([head(k_ref, h) for h in hs])
            q2 = jnp.stack([head(q_ref, h) for h in hs])
            s = lax.dot_general(k2, q2, (((2,), (2,)), ((0,), (0,))), preferred_element_type=F32) + madd[None]
            if near:
                s = s + bias_ref[off, grp].astype(F32)
            m_prev = m_ref[grp]
            m_new = jnp.maximum(m_prev, jnp.max(s, axis=1, keepdims=True))
            p = jnp.exp2(s - m_new).astype(BF16)
            vt2 = jnp.stack([vt_ref[h * V_ROWS:(h + 1) * V_ROWS, :] for h in hs])
            pv = lax.dot_general(vt2, p, (((2,), (1,)), ((0,), (0,))), preferred_element_type=F32)
            acc_ref[grp] = jnp.exp2(m_prev - m_new) * acc_ref[grp] + pv
            m_ref[grp] = m_new

    @pl.when(ki + 1 < qi)
    def _():
        tile(False)

    @pl.when(ki + 1 >= qi)
    def _():
        tile(True)

    @pl.when(ki == qi)
    def _():
        outs = []
        for h in range(A_HEADS):
            a = acc_ref[h]
            outs.append((a[:A_HEAD_DIM] / a[A_HEAD_DIM:A_HEAD_DIM + 1]).T)
        o_ref[0] = (jnp.concatenate(outs, axis=-1) * _silu(z_ref[0].astype(F32))).astype(o_ref.dtype)


def attention(q, k, vt, mask, bias, proj3d, b, t, tq):
    nq = t // tq
    pairs = [(i, j) for i in range(nq) for j in range(i + 1)]
    qi_tab = jnp.asarray([p[0] for p in pairs], jnp.int32)
    ki_tab = jnp.asarray([p[1] for p in pairs], jnp.int32)
    grid_spec = pltpu.PrefetchScalarGridSpec(
        num_scalar_prefetch=2,
        grid=(b, len(pairs)),
        in_specs=[
            pl.BlockSpec((1, tq, A_WIDTH), lambda bb, p, qi, ki: (bb, qi[p], 0)),
            pl.BlockSpec((1, tq, A_WIDTH), lambda bb, p, qi, ki: (bb, ki[p], 0)),
            pl.BlockSpec((A_HEADS * V_ROWS, tq), lambda bb, p, qi, ki: (0, bb * nq + ki[p])),
            pl.BlockSpec((1, tq, tq), lambda bb, p, qi, ki: (bb, ki[p], qi[p])),
            pl.BlockSpec(bias.shape, lambda bb, p, qi, ki: (0, 0, 0, 0)),
            pl.BlockSpec((1, tq, A_WIDTH), lambda bb, p, qi, ki: (bb, qi[p], COL_Z_A // A_WIDTH)),
        ],
        out_specs=pl.BlockSpec((1, tq, A_WIDTH), lambda bb, p, qi, ki: (bb, qi[p], 0)),
        scratch_shapes=[pltpu.VMEM((A_HEADS, 1, tq), F32), pltpu.VMEM((A_HEADS, V_ROWS, tq), F32)],
    )
    return pl.pallas_call(
        functools.partial(_attn_body, tq=tq),
        grid_spec=grid_spec,
        out_shape=jax.ShapeDtypeStruct((b, t, A_WIDTH), BF16),
        compiler_params=_params("parallel", "arbitrary"),
        name="dsa_attention",
    )(qi_tab, ki_tab, q.reshape(b, t, -1), k.reshape(b, t, -1), vt, mask, bias, proj3d)


def _gdn_prep_body(x_ref, halo_ref, sm_ref, cw_ref, alog_ref, dtb_ref, eg_ref, eb_ref,
                   q_ref, k_ref, v_ref, gcb_ref, bb_ref, grow_ref, xs_ref, *, tm):
    i = pl.program_id(1)
    hal = halo_ref[0].astype(F32)
    xs_ref[0:BF16_ROWS, :] = jnp.where(i > 0, hal, jnp.zeros_like(hal))
    xs_ref[BF16_ROWS:, :] = x_ref[0].astype(F32)
    y = jnp.zeros((tm, 3 * B_WIDTH), F32)
    for j in range(B_CONV):
        y = y + cw_ref[j:j + 1, :] * xs_ref[pl.ds(BF16_ROWS - (B_CONV - 1) + j, tm), :]
    y = _silu(y)
    for h in range(B_HEADS):
        sl = slice(h * LANES, (h + 1) * LANES)
        qh = y[:, sl]
        q_ref[0, :, sl] = qh * lax.rsqrt(jnp.sum(qh * qh, axis=-1, keepdims=True) + EPS) * (B_HEAD_DIM ** -0.5)
        kh = y[:, B_WIDTH + h * LANES:B_WIDTH + (h + 1) * LANES]
        k_ref[0, :, sl] = kh * lax.rsqrt(jnp.sum(kh * kh, axis=-1, keepdims=True) + EPS)
    v_ref[0] = y[:, 2 * B_WIDTH:]

    sm = sm_ref[0]
    xg = sm + dtb_ref[...]
    softplus = jnp.maximum(xg, 0.0) + jnp.log(1.0 + jnp.exp(-jnp.abs(xg)))
    g = -jnp.exp(alog_ref[...]) * softplus
    r = lax.broadcasted_iota(jnp.int32, (tm, tm), 0)
    c = lax.broadcasted_iota(jnp.int32, (tm, tm), 1)
    sh = int(math.log2(GDN_CHUNK))
    same_chunk = jnp.right_shift(r, sh) == jnp.right_shift(c, sh)
    tri = jnp.where(jnp.logical_and(same_chunk, c <= r), 1.0, 0.0).astype(F32)
    gc = lax.dot_general(tri, g, (((1,), (0,)), ((), ())), precision=HIGHEST, preferred_element_type=F32)
    gcb_ref[0] = lax.dot_general(gc, eg_ref[...], (((1,), (0,)), ((), ())), precision=HIGHEST,
                                 preferred_element_type=F32)
    bb_ref[0] = lax.dot_general(_sigmoid(sm), eb_ref[...], (((1,), (0,)), ((), ())), precision=HIGHEST,
                                preferred_element_type=F32)
    gct = gc.T
    for cc in range(tm // GDN_CHUNK):
        grow_ref[0, cc] = gct[SM_ALPHA:SM_ALPHA + B_HEADS, cc * GDN_CHUNK:(cc + 1) * GDN_CHUNK]


def gdn_prep(proj3d, small3d, conv_w, a_log, dt_bias, b, t):
    tm = _tile(t, 256)
    lane = jnp.arange(LANES)
    pad_row = lambda v: jnp.zeros((1, LANES), F32).at[0, SM_ALPHA:SM_ALPHA + B_HEADS].set(v)
    rows = lane[:, None]
    cols = jnp.arange(B_WIDTH)[None, :]
    e_g = (rows == SM_ALPHA + cols // LANES).astype(F32)
    e_b = (rows == SM_BETA + cols // LANES).astype(F32)
    nc = tm // GDN_CHUNK
    act = jax.ShapeDtypeStruct((b, t, B_WIDTH), F32)
    full = lambda a: pl.BlockSpec(a.shape, lambda bb, i: (0,) * a.ndim)
    consts = [conv_w, pad_row(a_log), pad_row(dt_bias), e_g, e_b]
    return pl.pallas_call(
        functools.partial(_gdn_prep_body, tm=tm),
        grid=(b, t // tm),
        in_specs=[
            pl.BlockSpec((1, tm, 3 * B_WIDTH), lambda bb, i: (bb, i, COL_QKV_B // (3 * B_WIDTH))),
            pl.BlockSpec((1, BF16_ROWS, 3 * B_WIDTH),
                         lambda bb, i: (bb, jnp.maximum(i * (tm // BF16_ROWS) - 1, 0), 0)),
            pl.BlockSpec((1, tm, LANES), lambda bb, i: (bb, i, 0)),
        ] + [full(c) for c in consts],
        out_specs=[pl.BlockSpec((1, tm, B_WIDTH), lambda bb, i: (bb, i, 0))] * 5
        + [pl.BlockSpec((1, nc, B_HEADS, GDN_CHUNK), lambda bb, i: (bb, i, 0, 0))],
        out_shape=[act] * 5 + [jax.ShapeDtypeStruct((b, t // GDN_CHUNK, B_HEADS, GDN_CHUNK), F32)],
        scratch_shapes=[pltpu.VMEM((tm + BF16_ROWS, 3 * B_WIDTH), F32)],
        compiler_params=_params("parallel", "parallel"),
        name="gdn_prep",
    )(proj3d, proj3d, small3d, *consts)


def _gdn_body(q_ref, k_ref, v_ref, gcb_ref, bb_ref, grow_ref, z_ref, on_ref, o_ref, s_ref, *, nc):
    @pl.when(pl.program_id(1) == 0)
    def _():
        s_ref[...] = jnp.zeros(s_ref.shape, F32)

    cs, nh = GDN_CHUNK, B_HEADS
    nb = nc * nh
    ri = lax.broadcasted_iota(jnp.int32, (nb, cs, cs), 1)
    ci = lax.broadcasted_iota(jnp.int32, (nb, cs, cs), 2)
    lower = ci <= ri
    strict = ci < ri
    eye = jnp.where(ci == ri, 1.0, 0.0).astype(F32)
    bf = lambda a: a.astype(BF16)
    bmm = lambda a, b: lax.dot_general(a, b, (((2,), (1,)), ((0,), (0,))), preferred_element_type=F32)
    bmm_nt = lambda a, b: lax.dot_general(a, b, (((2,), (2,)), ((0,), (0,))), preferred_element_type=F32)
    bmm_tn = lambda a, b: lax.dot_general(a, b, (((1,), (1,)), ((0,), (0,))), preferred_element_type=F32)

    def stack(ref):
        return jnp.stack([ref[0, c * cs:(c + 1) * cs, h * LANES:(h + 1) * LANES]
                          for c in range(nc) for h in range(nh)])

    q, k, v = stack(q_ref), stack(k_ref), stack(v_ref)
    gcb = stack(gcb_ref)
    beta = stack(bb_ref)
    grow = jnp.stack([grow_ref[0, c, h:h + 1, :] for c in range(nc) for h in range(nh)])
    diff = gcb[:, :, :cs] - grow
    decay = jnp.where(lower, jnp.exp(jnp.where(lower, diff, 0.0)), 0.0)
    eg = jnp.exp(gcb)
    glast = gcb[:, cs - 1:cs, :]
    kb = k * beta
    lmat = jnp.where(strict, bmm_nt(bf(kb), bf(k)) * decay, 0.0)
    n = -lmat
    tinv = eye + n
    for _ in range(int(math.log2(cs)) - 1):
        n = bmm(bf(n), bf(n))
        tinv = tinv + bmm(bf(tinv), bf(n))
    uw = bmm(bf(tinv), bf(jnp.concatenate([v * beta, kb * eg], axis=-1)))
    attn = bf(jnp.where(lower, bmm_nt(bf(q), bf(k)) * decay, 0.0))
    qg = bf(q * eg)
    kdec = bf(k * jnp.exp(glast - gcb))
    egl = jnp.exp(glast)

    s = s_ref[...]
    for c in range(nc):
        sl = slice(c * nh, (c + 1) * nh)
        sb = bf(s)
        v_new = uw[sl, :, :LANES] - bmm(bf(uw[sl, :, LANES:]), sb)
        o = bmm(qg[sl], sb) + bmm(attn[sl], bf(v_new))
        s = s * egl[sl] + bmm_tn(kdec[sl], bf(v_new))
        o = o * lax.rsqrt(jnp.mean(o * o, axis=-1, keepdims=True) + EPS) * on_ref[...]
        for h in range(nh):
            rows, cols = slice(c * cs, (c + 1) * cs), slice(h * LANES, (h + 1) * LANES)
            o_ref[0, rows, cols] = (o[h] * _silu(z_ref[0, rows, cols].astype(F32))).astype(o_ref.dtype)
    s_ref[...] = s


def gdn_scan(qh, kh, v, gcb, bb, grow, proj3d, o_norm, b, t):
    tt = _tile(t, 256)
    nc = tt // GDN_CHUNK
    blk = pl.BlockSpec((1, tt, B_WIDTH), lambda bb_, i: (bb_, i, 0))
    return pl.pallas_call(
        functools.partial(_gdn_body, nc=nc),
        grid=(b, t // tt),
        in_specs=[blk] * 5 + [
            pl.BlockSpec((1, nc, B_HEADS, GDN_CHUNK), lambda bb_, i: (bb_, i, 0, 0)),
            pl.BlockSpec((1, tt, B_WIDTH), lambda bb_, i: (bb_, i, COL_Z_B // B_WIDTH)),
            pl.BlockSpec((1, LANES), lambda bb_, i: (0, 0)),
        ],
        out_specs=blk,
        out_shape=jax.ShapeDtypeStruct((b, t, B_WIDTH), BF16),
        scratch_shapes=[pltpu.VMEM((B_HEADS, B_HEAD_DIM, B_HEAD_DIM), F32)],
        compiler_params=_params("parallel", "arbitrary"),
        name="gdn_scan",
    )(qh, kh, v, gcb, bb, grow, proj3d, o_norm.reshape(1, LANES))


C_HALO = 32
D_HALO = BF16_ROWS


def _cd_mix_body(a_ref, g_ref, ah_ref, gh_ref, zc_ref, bg_ref, cg_ref, ud_ref, cgh_ref, udh_ref, zd_ref,
                 dww_ref, dwb_ref, lnw_ref, lnb_ref, dcw_ref, yc_ref, yd_ref, us_ref, ds_ref, *, tm):
    i = pl.program_id(1)
    first = i == 0
    f32 = lambda r: r[0].astype(F32)
    uh = f32(ah_ref) * _sigmoid(f32(gh_ref))
    us_ref[0, 0:C_HALO, :] = jnp.where(first, jnp.zeros_like(uh), uh)
    us_ref[0, C_HALO:, :] = f32(a_ref) * _sigmoid(f32(g_ref))
    span = tm + C_HALO - SUBLANES
    for r in range(1, SUBLANES):
        us_ref[r, 0:span, :] = us_ref[0, pl.ds(r, span), :]
    u = jnp.zeros((tm, C_WIDTH), F32)
    for j in range(C_CONV):
        off = C_HALO - (C_CONV - 1) + j
        r, base = off % SUBLANES, off - off % SUBLANES
        u = u + dww_ref[j:j + 1, :] * us_ref[r, base:base + tm, :]
    u = u + dwb_ref[...]
    mu = jnp.mean(u, axis=-1, keepdims=True)
    uc = u - mu
    u = uc * lax.rsqrt(jnp.mean(uc * uc, axis=-1, keepdims=True) + EPS) * lnw_ref[...] + lnb_ref[...]
    yc_ref[0] = (_silu(u) * _silu(f32(zc_ref))).astype(yc_ref.dtype)

    dh = f32(cgh_ref) * f32(udh_ref)
    ds_ref[0:D_HALO, :] = jnp.where(first, jnp.zeros_like(dh), dh)
    ds_ref[D_HALO:, :] = f32(cg_ref) * f32(ud_ref)
    d = jnp.zeros((tm, D_WIDTH), F32)
    for j in range(D_CONV):
        d = d + dcw_ref[j:j + 1, :] * ds_ref[pl.ds(D_HALO - (D_CONV - 1) + j, tm), :]
    yd_ref[0] = (f32(bg_ref) * d * _silu(f32(zd_ref))).astype(yd_ref.dtype)


def cd_mix(proj3d, dw_w, dw_b, ln_w, ln_b, d_conv_w, b, t):
    tm = _tile(t, 256)
    w = C_WIDTH
    col = lambda n: pl.BlockSpec((1, tm, w), lambda bb, i, n=n: (bb, i, n))
    halo = lambda n, rows: pl.BlockSpec(
        (1, rows, w), lambda bb, i, n=n, rows=rows: (bb, jnp.maximum(i * (tm // rows) - 1, 0), n))
    row = lambda a: a.reshape(1, -1)
    full = lambda a: pl.BlockSpec(a.shape, lambda bb, i: (0,) * a.ndim)
    consts = [dw_w, row(dw_b), row(ln_w), row(ln_b), d_conv_w]
    out = jax.ShapeDtypeStruct((b, t, w), BF16)
    return pl.pallas_call(
        functools.partial(_cd_mix_body, tm=tm),
        grid=(b, t // tm),
        in_specs=[col(0), col(1), halo(0, C_HALO), halo(1, C_HALO), col(2), col(3), col(4), col(5),
                  halo(4, D_HALO), halo(5, D_HALO), col(6)] + [full(c) for c in consts],
        out_specs=[pl.BlockSpec((1, tm, w), lambda bb, i: (bb, i, 0))] * 2,
        out_shape=[out, out],
        scratch_shapes=[pltpu.VMEM((SUBLANES, tm + C_HALO, w), F32), pltpu.VMEM((tm + D_HALO, w), F32)],
        compiler_params=_params("parallel", "parallel"),
        name="cd_mix",
    )(*([proj3d] * 11), *consts)


def _reorder_ab_w_in(w):
    offs = [0]
    for s in AB_SPLITS:
        offs.append(offs[-1] + s)
    part = lambda n: w[:, offs[n]:offs[n + 1]]
    c_q, c_kv, k_idx, w_idx, z_a, qkv_b, beta_b, alpha_b, z_b = (part(n) for n in range(9))
    small = jnp.concatenate([k_idx, w_idx, beta_b, alpha_b], axis=1)
    small = jnp.pad(small, ((0, 0), (0, LANES - small.shape[1])))
    out = jnp.concatenate([qkv_b, z_a, z_b, c_q, c_kv], axis=1)
    return jnp.pad(out, ((0, 0), (0, AB_IN_PAD - out.shape[1]))).astype(BF16), small.astype(BF16)


def _ab_layer(x2d, b, t, norm_w, rel_bias, w_in, q_norm, w_uq, w_iq, kv_norm, w_uk, w_uv, q_gain, k_gain,
              ik_w, ik_b, conv_w, a_log, dt_bias, o_norm, w_out):
    proj, small = norm_matmul(x2d, norm_w, *_reorder_ab_w_in(w_in))
    proj3d, small3d = proj.reshape(b, t, AB_IN_PAD), small.reshape(b, t, LANES)
    q, k, v, qi, k_even, k_odd = dsa_prep(proj, small, q_norm, kv_norm, w_uq, w_iq, w_uk, w_uv, q_gain, k_gain,
                                          ik_w, ik_b)
    n_sel = min(TOPK_MAX, t // 4)
    mask = indexer_mask(qi, small3d, k_even, k_odd, b, t, n_sel)
    tq = _tile(t, 512)
    y_a = attention(q, k, v, mask, bias_tiles(rel_bias, tq), proj3d, b, t, tq)
    qh, kh, vv, gcb, bb, grow = gdn_prep(proj3d, small3d, conv_w, a_log, dt_bias, b, t)
    y_b = gdn_scan(qh, kh, vv, gcb, bb, grow, proj3d, o_norm, b, t)
    wo = w_out.astype(BF16)
    return out_proj(x2d, y_a.reshape(b * t, -1), y_b.reshape(b * t, -1), wo[:A_WIDTH], wo[A_WIDTH:])


def _cd_layer(x2d, b, t, norm_w, w_in, dw_w, dw_b, ln_w, ln_b, d_conv_w, w_out):
    proj = norm_matmul(x2d, norm_w, w_in.astype(BF16))
    y_c, y_d = cd_mix(proj.reshape(b, t, -1), dw_w, dw_b, ln_w, ln_b, d_conv_w, b, t)
    wo = w_out.astype(BF16)
    return out_proj(x2d, y_c.reshape(b * t, -1), y_d.reshape(b * t, -1), wo[:C_WIDTH], wo[C_WIDTH:])


def kernel(x, norm_w, rel_bias, ab_w_in, a_q_norm, a_w_uq, a_w_iq, a_kv_norm, a_w_uk, a_w_uv, a_q_gain,
           a_k_gain, a_ik_norm_w, a_ik_norm_b, b_conv_w, b_a_log, b_dt_bias, b_o_norm, ab_w_out, cd_w_in,
           c_dw_w, c_dw_b, c_ln_w, c_ln_b, d_conv_w, cd_w_out):
    b, t, d = x.shape
    depth = norm_w.shape[0]
    x2d = x.reshape(b * t, d)
    for i in range(depth):
        j = i // 2
        if i % 2 == 0:
            x2d = _ab_layer(x2d, b, t, norm_w[i], rel_bias, ab_w_in[j], a_q_norm[j], a_w_uq[j], a_w_iq[j],
                            a_kv_norm[j], a_w_uk[j], a_w_uv[j], a_q_gain[j], a_k_gain[j], a_ik_norm_w[j],
                            a_ik_norm_b[j], b_conv_w[j], b_a_log[j], b_dt_bias[j], b_o_norm[j], ab_w_out[j])
        else:
            x2d = _cd_layer(x2d, b, t, norm_w[i], cd_w_in[j], c_dw_w[j], c_dw_b[j], c_ln_w[j], c_ln_b[j],
                            d_conv_w[j], cd_w_out[j])
    return x2d.reshape(b, t, d)
```

```python
import functools
import math

import jax
import jax.numpy as jnp
from jax import lax
from jax.experimental import pallas as pl
from jax.experimental.pallas import tpu as pltpu

D_MODEL = 2048
A_HEADS = 8
A_HEAD_DIM = 128
A_WIDTH = A_HEADS * A_HEAD_DIM
A_Q_LORA = 512
A_KV_LORA = 256
IDX_HEADS = 16
IDX_DIM = 64
TOPK_MAX = 256
REL_BUCKETS = 32
REL_MAX_DIST = 128
B_HEADS = 8
B_HEAD_DIM = 128
B_WIDTH = B_HEADS * B_HEAD_DIM
B_CONV = 4
GDN_CHUNK = 64
C_WIDTH = 1024
C_CONV = 31
D_WIDTH = 1024
D_CONV = 3
EPS = 1e-6

AB_SPLITS = (A_Q_LORA, A_KV_LORA, IDX_DIM, IDX_HEADS, A_WIDTH, 3 * B_WIDTH, B_HEADS, B_HEADS, B_WIDTH)
AB_IN_PAD = 6144
SM_KIDX = 0
SM_WIDX = 64
SM_BETA = 80
SM_ALPHA = 88
COL_QKV_B = 0
COL_Z_A = 3072
COL_Z_B = 4096
COL_C_Q = 5120
COL_C_KV = 5632

LANES = 128
MASK_NEG = -1e30
INT_MIN = -(2 ** 31)
LOG2E = math.log2(math.e)
BF16_ROWS = 16
V_ROWS = A_HEAD_DIM + BF16_ROWS
VMEM_LIMIT = 56 * 1024 * 1024

F32 = jnp.float32
BF16 = jnp.bfloat16
HIGHEST = lax.Precision.HIGHEST


def _tile(n, pref):
    t = min(n, pref)
    assert n % t == 0, (n, t)
    return t


def _params(*sem):
    return pltpu.CompilerParams(dimension_semantics=sem, vmem_limit_bytes=VMEM_LIMIT)


def _dot(a, b):
    return jnp.dot(a, b, preferred_element_type=F32)


def _dot_nt(a, b):
    return lax.dot_general(a, b, (((1,), (1,)), ((), ())), preferred_element_type=F32)


def _dot_tn(a, b):
    return lax.dot_general(a, b, (((0,), (0,)), ((), ())), preferred_element_type=F32)


def _silu(x):
    return x * (1.0 / (1.0 + jnp.exp(-x)))


def _sigmoid(x):
    return 1.0 / (1.0 + jnp.exp(-x))


def _norm_matmul_body(*refs, with_f32_cols):
    if with_f32_cols:
        x_ref, nw_ref, w_ref, ws_ref, o_ref, os_ref, h_ref = refs
    else:
        x_ref, nw_ref, w_ref, o_ref, h_ref = refs

    @pl.when(pl.program_id(1) == 0)
    def _():
        x = x_ref[...]
        ms = jnp.mean(x * x, axis=-1, keepdims=True)
        h_ref[...] = (x * lax.rsqrt(ms + EPS) * nw_ref[...]).astype(h_ref.dtype)
        if with_f32_cols:
            os_ref[...] = _dot(h_ref[...], ws_ref[...])

    o_ref[...] = _dot(h_ref[...], w_ref[...]).astype(o_ref.dtype)


def norm_matmul(x2d, norm_w, w_bf16, w_f32_cols=None):
    m, k = x2d.shape
    n = w_bf16.shape[1]
    tm, tn = _tile(m, 1024), _tile(n, 1024)
    extra = w_f32_cols is not None
    in_specs = [
        pl.BlockSpec((tm, k), lambda i, j: (i, 0)),
        pl.BlockSpec((1, k), lambda i, j: (0, 0)),
        pl.BlockSpec((k, tn), lambda i, j: (0, j)),
    ]
    out_specs = [pl.BlockSpec((tm, tn), lambda i, j: (i, j))]
    out_shape = [jax.ShapeDtypeStruct((m, n), BF16)]
    args = [x2d, norm_w.reshape(1, k), w_bf16]
    if extra:
        in_specs.append(pl.BlockSpec((k, LANES), lambda i, j: (0, 0)))
        out_specs.append(pl.BlockSpec((tm, LANES), lambda i, j: (i, 0)))
        out_shape.append(jax.ShapeDtypeStruct((m, LANES), F32))
        args.append(w_f32_cols)
    res = pl.pallas_call(
        functools.partial(_norm_matmul_body, with_f32_cols=extra),
        grid=(m // tm, n // tn),
        in_specs=in_specs,
        out_specs=out_specs,
        out_shape=out_shape,
        scratch_shapes=[pltpu.VMEM((tm, k), BF16)],
        compiler_params=_params("parallel", "arbitrary"),
        name="norm_matmul",
    )(*args)
    return res if extra else res[0]


def _out_proj_body(x_ref, ya_ref, yb_ref, wa_ref, wb_ref, o_ref):
    o_ref[...] = x_ref[...] + _dot(ya_ref[...], wa_ref[...]) + _dot(yb_ref[...], wb_ref[...])


def out_proj(x2d, ya, yb, wa_bf16, wb_bf16):
    m, n = x2d.shape
    ka, kb = ya.shape[1], yb.shape[1]
    tm = _tile(m, 512)
    return pl.pallas_call(
        _out_proj_body,
        grid=(m // tm,),
        in_specs=[
            pl.BlockSpec((tm, n), lambda i: (i, 0)),
            pl.BlockSpec((tm, ka), lambda i: (i, 0)),
            pl.BlockSpec((tm, kb), lambda i: (i, 0)),
            pl.BlockSpec((ka, n), lambda i: (0, 0)),
            pl.BlockSpec((kb, n), lambda i: (0, 0)),
        ],
        out_specs=pl.BlockSpec((tm, n), lambda i: (i, 0)),
        out_shape=jax.ShapeDtypeStruct((m, n), F32),
        compiler_params=_params("parallel"),
        name="out_proj",
    )(x2d, ya, yb, wa_bf16, wb_bf16)


def _head_rmsnorm(y, gain_row, scale):
    outs = []
    for h in range(y.shape[1] // LANES):
        seg = y[:, h * LANES:(h + 1) * LANES]
        ms = jnp.mean(seg * seg, axis=-1, keepdims=True)
        outs.append(seg * lax.rsqrt(ms + EPS) * (gain_row * scale))
    return jnp.concatenate(outs, axis=-1)


def _dsa_prep_body(cq_ref, ckv_ref, sm_ref, qn_ref, kvn_ref, wuq_ref, wiq_ref, wuk_ref, wuv_ref,
                   qg_ref, kg_ref, ikw_ref, ikb_ref,
                   q_ref, k_ref, v_ref, qi_ref, ke_ref, ko_ref):
    cq = cq_ref[...].astype(F32)
    cq = cq * lax.rsqrt(jnp.mean(cq * cq, axis=-1, keepdims=True) + EPS) * qn_ref[...]
    cqb = cq.astype(BF16)
    q = _dot(cqb, wuq_ref[...])
    q_ref[...] = _head_rmsnorm(q, qg_ref[...], A_HEAD_DIM ** -0.5 * LOG2E).astype(q_ref.dtype)
    qi_ref[...] = _dot(cqb, wiq_ref[...]).astype(qi_ref.dtype)

    ckv = ckv_ref[...].astype(F32)
    ckv = ckv * lax.rsqrt(jnp.mean(ckv * ckv, axis=-1, keepdims=True) + EPS) * kvn_ref[...]
    ckvb = ckv.astype(BF16)
    k = _dot(ckvb, wuk_ref[...])
    k_ref[...] = _head_rmsnorm(k, kg_ref[...], 1.0).astype(k_ref.dtype)
    vt = _dot_nt(wuv_ref[...], ckvb).astype(v_ref.dtype)
    ones = jnp.ones((BF16_ROWS, vt.shape[1]), v_ref.dtype)
    for h in range(A_HEADS):
        v_ref[h * V_ROWS:h * V_ROWS + A_HEAD_DIM, :] = vt[h * A_HEAD_DIM:(h + 1) * A_HEAD_DIM, :]
        v_ref[h * V_ROWS + A_HEAD_DIM:(h + 1) * V_ROWS, :] = ones

    kx = sm_ref[...][:, SM_KIDX:SM_KIDX + IDX_DIM]
    mu = jnp.mean(kx, axis=-1, keepdims=True)
    kc = kx - mu
    kl = kc * lax.rsqrt(jnp.mean(kc * kc, axis=-1, keepdims=True) + EPS) * ikw_ref[...] + ikb_ref[...]
    z = jnp.zeros_like(kl)
    ke_ref[...] = jnp.concatenate([kl, z], axis=-1).astype(ke_ref.dtype)
    ko_ref[...] = jnp.concatenate([z, kl], axis=-1).astype(ko_ref.dtype)


def dsa_prep(proj, small, q_norm, kv_norm, w_uq, w_iq, w_uk, w_uv, q_gain, k_gain, ik_w, ik_b):
    m = proj.shape[0]
    tm = _tile(m, 512)
    row = lambda a: a.reshape(1, -1)
    full = lambda a: pl.BlockSpec(a.shape, lambda i: (0,) * a.ndim)
    consts = [row(q_norm), row(kv_norm), w_uq.astype(BF16), w_iq.astype(BF16), w_uk.astype(BF16),
              w_uv.T.astype(BF16), row(q_gain), row(k_gain), row(ik_w), row(ik_b)]
    outs = [jax.ShapeDtypeStruct((m, A_WIDTH), BF16)] * 2 + [
        jax.ShapeDtypeStruct((A_HEADS * V_ROWS, m), BF16),
        jax.ShapeDtypeStruct((m, IDX_HEADS * IDX_DIM), BF16),
        jax.ShapeDtypeStruct((m, LANES), BF16), jax.ShapeDtypeStruct((m, LANES), BF16)]
    out_specs = [pl.BlockSpec((tm, s.shape[1]), lambda i: (i, 0)) for s in outs]
    out_specs[2] = pl.BlockSpec((A_HEADS * V_ROWS, tm), lambda i: (0, i))
    return pl.pallas_call(
        _dsa_prep_body,
        grid=(m // tm,),
        in_specs=[
            pl.BlockSpec((tm, A_Q_LORA), lambda i: (i, COL_C_Q // A_Q_LORA)),
            pl.BlockSpec((tm, A_KV_LORA), lambda i: (i, COL_C_KV // A_KV_LORA)),
            pl.BlockSpec((tm, LANES), lambda i: (i, 0)),
        ] + [full(c) for c in consts],
        out_specs=out_specs,
        out_shape=outs,
        compiler_params=_params("parallel"),
        name="dsa_prep",
    )(proj, proj, small, *consts)


def _t5_bucket(dist):
    max_exact = REL_BUCKETS // 2
    large = max_exact + (jnp.log(jnp.maximum(dist, 1).astype(F32) / max_exact)
                         / math.log(REL_MAX_DIST / max_exact) * (REL_BUCKETS - max_exact)).astype(jnp.int32)
    large = jnp.minimum(large, REL_BUCKETS - 1)
    return jnp.where(dist < max_exact, dist, large)


def _bias_tiles_body(rb_ref, o_ref, *, t):
    h = pl.program_id(0)
    off = pl.program_id(1) * t
    dist = off + lax.broadcasted_iota(jnp.int32, (t, t), 1) - lax.broadcasted_iota(jnp.int32, (t, t), 0)
    bucket = _t5_bucket(jnp.maximum(dist, 0))
    acc = jnp.zeros((t, t), F32)
    for b in range(REL_BUCKETS):
        acc = jnp.where(bucket == b, rb_ref[b, h], acc)
    o_ref[0, 0] = ((acc - rb_ref[REL_BUCKETS - 1, h]) * LOG2E).astype(o_ref.dtype)


def bias_tiles(rel_bias, t):
    return pl.pallas_call(
        functools.partial(_bias_tiles_body, t=t),
        grid=(A_HEADS, 2),
        in_specs=[pl.BlockSpec(memory_space=pltpu.SMEM)],
        out_specs=pl.BlockSpec((1, 1, t, t), lambda h, c: (c, h, 0, 0)),
        out_shape=jax.ShapeDtypeStruct((2, A_HEADS, t, t), BF16),
        compiler_params=_params("parallel", "parallel"),
        name="bias_tiles",
    )(rel_bias)


SUBLANES = 8
MAX_SELECT_STEPS = 40
SELECT_STEPS_PER_CHECK = 4


def _key_of(x):
    bits = pltpu.bitcast(x, jnp.int32)
    key = jnp.where(bits < 0, bits ^ jnp.int32(0x7FFFFFFF), bits)
    return jnp.where(bits == jnp.int32(INT_MIN), 0, key)


def _indexer_body(qi_ref, sm_ref, ke_ref, ko_ref, mask_ref, key_ref, w_ref, *, tq, ck, n_sel):
    i = pl.program_id(1)
    nk = ((i + 1) * tq + ck - 1) // ck
    groups = ck // SUBLANES
    scale = (IDX_HEADS ** -0.5) * (IDX_DIM ** -0.5)
    smt = sm_ref[0].T
    for h in range(IDX_HEADS):
        w_ref[h] = jnp.broadcast_to(smt[SM_WIDX + h:SM_WIDX + h + 1, :] * scale, (SUBLANES, tq))
    kloc = lax.broadcasted_iota(jnp.int32, (ck, tq), 0)
    qpos = i * tq + lax.broadcasted_iota(jnp.int32, (ck, tq), 1)
    int_max = jnp.int32(2 ** 31 - 1)

    def score_chunk(c, carry):
        kmin, kmax = carry
        ks = pl.multiple_of(c * ck, ck)
        ke = ke_ref[0, pl.ds(ks, ck), :]
        ko = ko_ref[0, pl.ds(ks, ck), :]
        acc = jnp.zeros((ck, tq), F32)
        for j in range(IDX_HEADS // 2):
            qp = qi_ref[0, :, j * LANES:(j + 1) * LANES]
            for par, kk in ((0, ke), (1, ko)):
                s = jnp.maximum(_dot_nt(kk, qp), 0.0)
                acc = acc + s * jnp.tile(w_ref[2 * j + par], (groups, 1))
        key = _key_of(acc)
        valid = c * ck + kloc <= qpos
        key_ref[pl.ds(ks, ck), :] = jnp.where(valid, key, jnp.int32(INT_MIN))
        kmin = jnp.minimum(kmin, jnp.min(jnp.where(valid, key, int_max).reshape(groups, SUBLANES, tq), axis=0))
        kmax = jnp.maximum(kmax, jnp.max(jnp.where(valid, key, jnp.int32(INT_MIN)).reshape(groups, SUBLANES, tq), axis=0))
        return kmin, kmax

    kmin, kmax = lax.fori_loop(0, nk, score_chunk, (jnp.full((SUBLANES, tq), int_max, jnp.int32),
                                                    jnp.full((SUBLANES, tq), INT_MIN, jnp.int32)))

    def count_ge(p):
        p8 = jnp.broadcast_to(p, (SUBLANES, tq))

        def body(c, cnt):
            ks = pl.multiple_of(c * ck, ck)
            kk = key_ref[pl.ds(ks, ck), :].reshape(groups, SUBLANES, tq)
            return cnt + jnp.sum(jnp.where(kk >= p8[None], 1, 0), axis=0)

        cnt = lax.fori_loop(0, nk, body, jnp.zeros((SUBLANES, tq), jnp.int32))
        return jnp.sum(cnt, axis=0, keepdims=True)

    n_valid = i * tq + lax.broadcasted_iota(jnp.int32, (1, tq), 1) + 1
    few = n_valid <= n_sel
    lo0 = jnp.min(kmin, axis=0, keepdims=True)
    hi0 = jnp.max(kmax, axis=0, keepdims=True) + 1

    def closed(lo, hi, clo):
        return few | (clo == n_sel) | (hi == lo + 1)

    def open_rows(lo, hi, clo, chi):
        return jnp.logical_not(closed(lo, hi, clo) | (chi == n_sel - 1))

    def cond(st):
        it, lo, hi, clo, chi = st
        n_open = jnp.max(jnp.where(open_rows(lo, hi, clo, chi), 1, 0))
        return jnp.logical_and(it < MAX_SELECT_STEPS, n_open > 0)

    def step(st):
        it, lo, hi, clo, chi = st
        for _ in range(SELECT_STEPS_PER_CHECK):
            upd = open_rows(lo, hi, clo, chi)
            p = lo + jnp.maximum(lax.shift_right_logical(hi - lo, 1), 1)
            cnt = count_ge(p)
            ge = cnt >= n_sel
            up, dn = upd & ge, upd & jnp.logical_not(ge)
            lo, hi = jnp.where(up, p, lo), jnp.where(dn, p, hi)
            clo, chi = jnp.where(up, cnt, clo), jnp.where(dn, cnt, chi)
        return it + SELECT_STEPS_PER_CHECK, lo, hi, clo, chi

    _, lo, hi, clo, chi = lax.while_loop(cond, step,
                                         (jnp.int32(0), lo0, hi0, n_valid, jnp.zeros((1, tq), jnp.int32)))

    def max_below(h):
        h8 = jnp.broadcast_to(h, (SUBLANES, tq))

        def body(c, acc):
            ks = pl.multiple_of(c * ck, ck)
            kk = key_ref[pl.ds(ks, ck), :].reshape(groups, SUBLANES, tq)
            return jnp.maximum(acc, jnp.max(jnp.where(kk < h8[None], kk, jnp.int32(INT_MIN)), axis=0))

        acc = lax.fori_loop(0, nk, body, jnp.full((SUBLANES, tq), INT_MIN, jnp.int32))
        return jnp.max(acc, axis=0, keepdims=True)

    last = jnp.logical_not(closed(lo, hi, clo))
    lo = jnp.where(last, max_below(hi), lo)
    clo = jnp.where(last, count_ge(lo), clo)
    thr = jnp.where(few, jnp.int32(INT_MIN + 1), lo)
    thr8 = jnp.broadcast_to(thr, (SUBLANES, tq))
    tied = jnp.logical_not(few) & (clo > n_sel)
    need = n_sel - chi
    n_keys = mask_ref.shape[1]

    def tie_cut(_):
        def count_le(j):
            j8 = jnp.broadcast_to(j, (SUBLANES, tq))

            def body(c, cnt):
                ks = pl.multiple_of(c * ck, ck)
                kk = key_ref[pl.ds(ks, ck), :].reshape(groups, SUBLANES, tq)
                idx = (c * ck + kloc).reshape(groups, SUBLANES, tq)
                hit = jnp.where(kk == thr8[None], jnp.where(idx <= j8[None], 1, 0), 0)
                return cnt + jnp.sum(hit, axis=0)

            cnt = lax.fori_loop(0, nk, body, jnp.zeros((SUBLANES, tq), jnp.int32))
            return jnp.sum(cnt, axis=0, keepdims=True)

        def bisect(_, st):
            jlo, jhi = st
            mid = jlo + jnp.right_shift(jhi - jlo, 1)
            ok = count_le(mid) >= need
            return jnp.where(ok, jlo, mid), jnp.where(ok, mid, jhi)

        _, jhi = lax.fori_loop(0, n_keys.bit_length(), bisect,
                               (jnp.full((1, tq), -1, jnp.int32), jnp.full((1, tq), n_keys - 1, jnp.int32)))
        return jnp.where(tied, jhi, int_max)

    jcut = lax.cond(jnp.max(jnp.where(tied, 1, 0)) > 0, tie_cut, lambda _: jnp.full((1, tq), int_max, jnp.int32), 0)
    jcut8 = jnp.broadcast_to(jcut, (SUBLANES, tq))

    def write_chunk(c, carry):
        ks = pl.multiple_of(c * ck, ck)
        kk = key_ref[pl.ds(ks, ck), :].reshape(groups, SUBLANES, tq)
        idx = (c * ck + kloc).reshape(groups, SUBLANES, tq)
        at_thr = jnp.where(kk == thr8[None], jnp.where(idx <= jcut8[None], 1, 0), 0)
        sel = jnp.where(kk > thr8[None], 1, at_thr)
        mask_ref[0, pl.ds(ks, ck), :] = sel.reshape(ck, tq).astype(mask_ref.dtype)
        return carry

    lax.fori_loop(0, nk, write_chunk, 0)

    def zero_chunk(c, carry):
        ks = pl.multiple_of(c * ck, ck)
        mask_ref[0, pl.ds(ks, ck), :] = jnp.zeros((ck, tq), mask_ref.dtype)
        return carry

    lax.fori_loop(nk, mask_ref.shape[1] // ck, zero_chunk, 0)


def indexer_mask(qi, small, k_even, k_odd, b, t, n_sel):
    tq = _tile(t, 256)
    ck = _tile(t, 512)
    return pl.pallas_call(
        functools.partial(_indexer_body, tq=tq, ck=ck, n_sel=n_sel),
        grid=(b, t // tq),
        in_specs=[
            pl.BlockSpec((1, tq, IDX_HEADS * IDX_DIM), lambda bb, i: (bb, i, 0)),
            pl.BlockSpec((1, tq, LANES), lambda bb, i: (bb, i, 0)),
            pl.BlockSpec((1, t, LANES), lambda bb, i: (bb, 0, 0)),
            pl.BlockSpec((1, t, LANES), lambda bb, i: (bb, 0, 0)),
        ],
        out_specs=pl.BlockSpec((1, t, tq), lambda bb, i: (bb, 0, i)),
        out_shape=jax.ShapeDtypeStruct((b, t, t), jnp.int8),
        scratch_shapes=[pltpu.VMEM((t, tq), jnp.int32), pltpu.VMEM((IDX_HEADS, SUBLANES, tq), F32)],
        compiler_params=_params("parallel", "parallel"),
        name="indexer",
    )(qi.reshape(b, t, -1), small, k_even.reshape(b, t, LANES), k_odd.reshape(b, t, LANES))


ATTN_HEAD_GROUP = 4


def _attn_body(qi_ref, ki_ref, q_ref, k_ref, vt_ref, mask_ref, bias_ref, z_ref, o_ref, m_ref, acc_ref, *, tq):
    qi, ki = qi_ref[pl.program_id(1)], ki_ref[pl.program_id(1)]

    @pl.when(ki == 0)
    def _():
        m_ref[...] = jnp.full(m_ref.shape, MASK_NEG, F32)
        acc_ref[...] = jnp.zeros(acc_ref.shape, F32)

    def tile(near):
        madd = (1.0 - mask_ref[0].astype(F32)) * MASK_NEG
        off = jnp.where(ki == qi, 0, 1)
        for g in range(0, A_HEADS, ATTN_HEAD_GROUP):
            hs = range(g, g + ATTN_HEAD_GROUP)
            grp = slice(g, g + ATTN_HEAD_GROUP)
            head = lambda ref, h: ref[0, :, h * A_HEAD_DIM:(h + 1) * A_HEAD_DIM]
            k2 = jnp.stack([head(k_ref, h) for h in hs])
            q2 = jnp.stack([head(q_ref, h) for h in hs])
            s = lax.dot_general(k2, q2, (((2,), (2,)), ((0,), (0,))), preferred_element_type=F32) + madd[None]
            if near:
                s = s + bias_ref[off, grp].astype(F32)
            m_prev = m_ref[grp]
            m_new = jnp.maximum(m_prev, jnp.max(s, axis=1, keepdims=True))
            p = jnp.exp2(s - m_new).astype(BF16)
            vt2 = jnp.stack([vt_ref[h * V_ROWS:(h + 1) * V_ROWS, :] for h in hs])
            pv = lax.dot_general(vt2, p, (((2,), (1,)), ((0,), (0,))), preferred_element_type=F32)
            acc_ref[grp] = jnp.exp2(m_prev - m_new) * acc_ref[grp] + pv
            m_ref[grp] = m_new

    @pl.when(ki + 1 < qi)
    def _():
        tile(False)

    @pl.when(ki + 1 >= qi)
    def _():
        tile(True)

    @pl.when(ki == qi)
    def _():
        outs = []
        for h in range(A_HEADS):
            a = acc_ref[h]
            outs.append((a[:A_HEAD_DIM] / a[A_HEAD_DIM:A_HEAD_DIM + 1]).T)
        o_ref[0] = (jnp.concatenate(outs, axis=-1) * _silu(z_ref[0].astype(F32))).astype(o_ref.dtype)


def attention(q, k, vt, mask, bias, proj3d, b, t, tq):
    nq = t // tq
    pairs = [(i, j) for i in range(nq) for j in range(i + 1)]
    qi_tab = jnp.asarray([p[0] for p in pairs], jnp.int32)
    ki_tab = jnp.asarray([p[1] for p in pairs], jnp.int32)
    grid_spec = pltpu.PrefetchScalarGridSpec(
        num_scalar_prefetch=2,
        grid=(b, len(pairs)),
        in_specs=[
            pl.BlockSpec((1, tq, A_WIDTH), lambda bb, p, qi, ki: (bb, qi[p], 0)),
            pl.BlockSpec((1, tq, A_WIDTH), lambda bb, p, qi, ki: (bb, ki[p], 0)),
            pl.BlockSpec((A_HEADS * V_ROWS, tq), lambda bb, p, qi, ki: (0, bb * nq + ki[p])),
            pl.BlockSpec((1, tq, tq), lambda bb, p, qi, ki: (bb, ki[p], qi[p])),
            pl.BlockSpec(bias.shape, lambda bb, p, qi, ki: (0, 0, 0, 0)),
            pl.BlockSpec((1, tq, A_WIDTH), lambda bb, p, qi, ki: (bb, qi[p], COL_Z_A // A_WIDTH)),
        ],
        out_specs=pl.BlockSpec((1, tq, A_WIDTH), lambda bb, p, qi, ki: (bb, qi[p], 0)),
        scratch_shapes=[pltpu.VMEM((A_HEADS, 1, tq), F32), pltpu.VMEM((A_HEADS, V_ROWS, tq), F32)],
    )
    return pl.pallas_call(
        functools.partial(_attn_body, tq=tq),
        grid_spec=grid_spec,
        out_shape=jax.ShapeDtypeStruct((b, t, A_WIDTH), BF16),
        compiler_params=_params("parallel", "arbitrary"),
        name="dsa_attention",
    )(qi_tab, ki_tab, q.reshape(b, t, -1), k.reshape(b, t, -1), vt, mask, bias, proj3d)


def _gdn_prep_body(x_ref, halo_ref, sm_ref, cw_ref, alog_ref, dtb_ref,
                   q_ref, k_ref, v_ref, gcb_ref, bb_ref, grow_ref, xs_ref, *, tm):
    i = pl.program_id(1)
    hal = halo_ref[0].astype(F32)
    xs_ref[0:BF16_ROWS, :] = jnp.where(i > 0, hal, jnp.zeros_like(hal))
    xs_ref[BF16_ROWS:, :] = x_ref[0].astype(F32)
    y = jnp.zeros((tm, 3 * B_WIDTH), F32)
    for j in range(B_CONV):
        y = y + cw_ref[j:j + 1, :] * xs_ref[pl.ds(BF16_ROWS - (B_CONV - 1) + j, tm), :]
    y = _silu(y)
    for h in range(B_HEADS):
        sl = slice(h * LANES, (h + 1) * LANES)
        qh = y[:, sl]
        q_ref[0, :, sl] = qh * lax.rsqrt(jnp.sum(qh * qh, axis=-1, keepdims=True) + EPS) * (B_HEAD_DIM ** -0.5)
        kh = y[:, B_WIDTH + h * LANES:B_WIDTH + (h + 1) * LANES]
        k_ref[0, :, sl] = kh * lax.rsqrt(jnp.sum(kh * kh, axis=-1, keepdims=True) + EPS)
    v_ref[0] = y[:, 2 * B_WIDTH:]

    sm = sm_ref[0]
    xg = sm + dtb_ref[...]
    softplus = jnp.maximum(xg, 0.0) + jnp.log(1.0 + jnp.exp(-jnp.abs(xg)))
    g = -jnp.exp(alog_ref[...]) * softplus
    r = lax.broadcasted_iota(jnp.int32, (tm, tm), 0)
    c = lax.broadcasted_iota(jnp.int32, (tm, tm), 1)
    sh = int(math.log2(GDN_CHUNK))
    same_chunk = jnp.right_shift(r, sh) == jnp.right_shift(c, sh)
    tri = jnp.where(jnp.logical_and(same_chunk, c <= r), 1.0, 0.0).astype(F32)
    gc = lax.dot_general(tri, g, (((1,), (0,)), ((), ())), precision=HIGHEST, preferred_element_type=F32)
    beta = _sigmoid(sm)
    for h in range(B_HEADS):
        sl = slice(h * LANES, (h + 1) * LANES)
        gcb_ref[0, :, sl] = jnp.broadcast_to(gc[:, SM_ALPHA + h:SM_ALPHA + h + 1], (tm, LANES))
        bb_ref[0, :, sl] = jnp.broadcast_to(beta[:, SM_BETA + h:SM_BETA + h + 1], (tm, LANES))
    gct = gc.T
    for cc in range(tm // GDN_CHUNK):
        grow_ref[0, cc] = gct[SM_ALPHA:SM_ALPHA + B_HEADS, cc * GDN_CHUNK:(cc + 1) * GDN_CHUNK]


def gdn_prep(proj3d, small3d, conv_w, a_log, dt_bias, b, t):
    tm = _tile(t, 256)
    pad_row = lambda v: jnp.zeros((1, LANES), F32).at[0, SM_ALPHA:SM_ALPHA + B_HEADS].set(v)
    nc = tm // GDN_CHUNK
    act = jax.ShapeDtypeStruct((b, t, B_WIDTH), F32)
    full = lambda a: pl.BlockSpec(a.shape, lambda bb, i: (0,) * a.ndim)
    consts = [conv_w, pad_row(a_log), pad_row(dt_bias)]
    return pl.pallas_call(
        functools.partial(_gdn_prep_body, tm=tm),
        grid=(b, t // tm),
        in_specs=[
            pl.BlockSpec((1, tm, 3 * B_WIDTH), lambda bb, i: (bb, i, COL_QKV_B // (3 * B_WIDTH))),
            pl.BlockSpec((1, BF16_ROWS, 3 * B_WIDTH),
                         lambda bb, i: (bb, jnp.maximum(i * (tm // BF16_ROWS) - 1, 0), 0)),
            pl.BlockSpec((1, tm, LANES), lambda bb, i: (bb, i, 0)),
        ] + [full(c) for c in consts],
        out_specs=[pl.BlockSpec((1, tm, B_WIDTH), lambda bb, i: (bb, i, 0))] * 5
        + [pl.BlockSpec((1, nc, B_HEADS, GDN_CHUNK), lambda bb, i: (bb, i, 0, 0))],
        out_shape=[act] * 5 + [jax.ShapeDtypeStruct((b, t // GDN_CHUNK, B_HEADS, GDN_CHUNK), F32)],
        scratch_shapes=[pltpu.VMEM((tm + BF16_ROWS, 3 * B_WIDTH), F32)],
        compiler_params=_params("parallel", "parallel"),
        name="gdn_prep",
    )(proj3d, proj3d, small3d, *consts)


def _gdn_body(q_ref, k_ref, v_ref, gcb_ref, bb_ref, grow_ref, z_ref, on_ref, o_ref, s_ref, *, nc):
    @pl.when(pl.program_id(1) == 0)
    def _():
        s_ref[...] = jnp.zeros(s_ref.shape, F32)

    cs, nh = GDN_CHUNK, B_HEADS
    nb = nc * nh
    ri = lax.broadcasted_iota(jnp.int32, (nb, cs, cs), 1)
    ci = lax.broadcasted_iota(jnp.int32, (nb, cs, cs), 2)
    lower = ci <= ri
    strict = ci < ri
    eye = jnp.where(ci == ri, 1.0, 0.0).astype(F32)
    bf = lambda a: a.astype(BF16)
    bmm = lambda a, b: lax.dot_general(a, b, (((2,), (1,)), ((0,), (0,))), preferred_element_type=F32)
    bmm_nt = lambda a, b: lax.dot_general(a, b, (((2,), (2,)), ((0,), (0,))), preferred_element_type=F32)
    bmm_tn = lambda a, b: lax.dot_general(a, b, (((1,), (1,)), ((0,), (0,))), preferred_element_type=F32)

    def stack(ref):
        return jnp.stack([ref[0, c * cs:(c + 1) * cs, h * LANES:(h + 1) * LANES]
                          for c in range(nc) for h in range(nh)])

    q, k, v = stack(q_ref), stack(k_ref), stack(v_ref)
    gcb = stack(gcb_ref)
    beta = stack(bb_ref)
    grow = jnp.stack([grow_ref[0, c, h:h + 1, :] for c in range(nc) for h in range(nh)])
    diff = gcb[:, :, :cs] - grow
    decay = jnp.where(lower, jnp.exp(jnp.where(lower, diff, 0.0)), 0.0)
    eg = jnp.exp(gcb)
    glast = gcb[:, cs - 1:cs, :]
    kb = k * beta
    lmat = jnp.where(strict, bmm_nt(bf(kb), bf(k)) * decay, 0.0)
    n = -lmat
    tinv = eye + n
    for _ in range(int(math.log2(cs)) - 1):
        n = bmm(bf(n), bf(n))
        tinv = tinv + bmm(bf(tinv), bf(n))
    uw = bmm(bf(tinv), bf(jnp.concatenate([v * beta, kb * eg], axis=-1)))
    attn = bf(jnp.where(lower, bmm_nt(bf(q), bf(k)) * decay, 0.0))
    qg = bf(q * eg)
    kdec = bf(k * jnp.exp(glast - gcb))
    egl = jnp.exp(glast)

    s = s_ref[...]
    for c in range(nc):
        sl = slice(c * nh, (c + 1) * nh)
        sb = bf(s)
        v_new = uw[sl, :, :LANES] - bmm(bf(uw[sl, :, LANES:]), sb)
        o = bmm(qg[sl], sb) + bmm(attn[sl], bf(v_new))
        s = s * egl[sl] + bmm_tn(kdec[sl], bf(v_new))
        o = o * lax.rsqrt(jnp.mean(o * o, axis=-1, keepdims=True) + EPS) * on_ref[...]
        for h in range(nh):
            rows, cols = slice(c * cs, (c + 1) * cs), slice(h * LANES, (h + 1) * LANES)
            o_ref[0, rows, cols] = (o[h] * _silu(z_ref[0, rows, cols].astype(F32))).astype(o_ref.dtype)
    s_ref[...] = s


def gdn_scan(qh, kh, v, gcb, bb, grow, proj3d, o_norm, b, t):
    tt = _tile(t, 256)
    nc = tt // GDN_CHUNK
    blk = pl.BlockSpec((1, tt, B_WIDTH), lambda bb_, i: (bb_, i, 0))
    return pl.pallas_call(
        functools.partial(_gdn_body, nc=nc),
        grid=(b, t // tt),
        in_specs=[blk] * 5 + [
            pl.BlockSpec((1, nc, B_HEADS, GDN_CHUNK), lambda bb_, i: (bb_, i, 0, 0)),
            pl.BlockSpec((1, tt, B_WIDTH), lambda bb_, i: (bb_, i, COL_Z_B // B_WIDTH)),
            pl.BlockSpec((1, LANES), lambda bb_, i: (0, 0)),
        ],
        out_specs=blk,
        out_shape=jax.ShapeDtypeStruct((b, t, B_WIDTH), BF16),
        scratch_shapes=[pltpu.VMEM((B_HEADS, B_HEAD_DIM, B_HEAD_DIM), F32)],
        compiler_params=_params("parallel", "arbitrary"),
        name="gdn_scan",
    )(qh, kh, v, gcb, bb, grow, proj3d, o_norm.reshape(1, LANES))


C_HALO = 32
D_HALO = BF16_ROWS


def _cd_mix_body(a_ref, g_ref, ah_ref, gh_ref, zc_ref, bg_ref, cg_ref, ud_ref, cgh_ref, udh_ref, zd_ref,
                 dww_ref, dwb_ref, lnw_ref, lnb_ref, dcw_ref, yc_ref, yd_ref, us_ref, ds_ref, *, tm):
    i = pl.program_id(1)
    first = i == 0
    f32 = lambda r: r[0].astype(F32)
    uh = f32(ah_ref) * _sigmoid(f32(gh_ref))
    us_ref[0, 0:C_HALO, :] = jnp.where(first, jnp.zeros_like(uh), uh)
    us_ref[0, C_HALO:, :] = f32(a_ref) * _sigmoid(f32(g_ref))
    span = tm + C_HALO - SUBLANES
    for r in range(1, SUBLANES):
        us_ref[r, 0:span, :] = us_ref[0, pl.ds(r, span), :]
    u = jnp.zeros((tm, C_WIDTH), F32)
    for j in range(C_CONV):
        off = C_HALO - (C_CONV - 1) + j
        r, base = off % SUBLANES, off - off % SUBLANES
        u = u + dww_ref[j:j + 1, :] * us_ref[r, base:base + tm, :]
    u = u + dwb_ref[...]
    mu = jnp.mean(u, axis=-1, keepdims=True)
    uc = u - mu
    u = uc * lax.rsqrt(jnp.mean(uc * uc, axis=-1, keepdims=True) + EPS) * lnw_ref[...] + lnb_ref[...]
    yc_ref[0] = (_silu(u) * _silu(f32(zc_ref))).astype(yc_ref.dtype)

    dh = f32(cgh_ref) * f32(udh_ref)
    ds_ref[0:D_HALO, :] = jnp.where(first, jnp.zeros_like(dh), dh)
    ds_ref[D_HALO:, :] = f32(cg_ref) * f32(ud_ref)
    d = jnp.zeros((tm, D_WIDTH), F32)
    for j in range(D_CONV):
        d = d + dcw_ref[j:j + 1, :] * ds_ref[pl.ds(D_HALO - (D_CONV - 1) + j, tm), :]
    yd_ref[0] = (f32(bg_ref) * d * _silu(f32(zd_ref))).astype(yd_ref.dtype)


def cd_mix(proj3d, dw_w, dw_b, ln_w, ln_b, d_conv_w, b, t):
    tm = _tile(t, 256)
    w = C_WIDTH
    col = lambda n: pl.BlockSpec((1, tm, w), lambda bb, i, n=n: (bb, i, n))
    halo = lambda n, rows: pl.BlockSpec(
        (1, rows, w), lambda bb, i, n=n, rows=rows: (bb, jnp.maximum(i * (tm // rows) - 1, 0), n))
    row = lambda a: a.reshape(1, -1)
    full = lambda a: pl.BlockSpec(a.shape, lambda bb, i: (0,) * a.ndim)
    consts = [dw_w, row(dw_b), row(ln_w), row(ln_b), d_conv_w]
    out = jax.ShapeDtypeStruct((b, t, w), BF16)
    return pl.pallas_call(
        functools.partial(_cd_mix_body, tm=tm),
        grid=(b, t // tm),
        in_specs=[col(0), col(1), halo(0, C_HALO), halo(1, C_HALO), col(2), col(3), col(4), col(5),
                  halo(4, D_HALO), halo(5, D_HALO), col(6)] + [full(c) for c in consts],
        out_specs=[pl.BlockSpec((1, tm, w), lambda bb, i: (bb, i, 0))] * 2,
        out_shape=[out, out],
        scratch_shapes=[pltpu.VMEM((SUBLANES, tm + C_HALO, w), F32), pltpu.VMEM((tm + D_HALO, w), F32)],
        compiler_params=_params("parallel", "parallel"),
        name="cd_mix",
    )(*([proj3d] * 11), *consts)


def _reorder_ab_w_in(w):
    offs = [0]
    for s in AB_SPLITS:
        offs.append(offs[-1] + s)
    part = lambda n: w[:, offs[n]:offs[n + 1]]
    c_q, c_kv, k_idx, w_idx, z_a, qkv_b, beta_b, alpha_b, z_b = (part(n) for n in range(9))
    small = jnp.concatenate([k_idx, w_idx, beta_b, alpha_b], axis=1)
    small = jnp.pad(small, ((0, 0), (0, LANES - small.shape[1])))
    out = jnp.concatenate([qkv_b, z_a, z_b, c_q, c_kv], axis=1)
    return jnp.pad(out, ((0, 0), (0, AB_IN_PAD - out.shape[1]))).astype(BF16), small.astype(BF16)


def _ab_layer(x2d, b, t, norm_w, rel_bias, w_in, q_norm, w_uq, w_iq, kv_norm, w_uk, w_uv, q_gain, k_gain,
              ik_w, ik_b, conv_w, a_log, dt_bias, o_norm, w_out):
    proj, small = norm_matmul(x2d, norm_w, *_reorder_ab_w_in(w_in))
    proj3d, small3d = proj.reshape(b, t, AB_IN_PAD), small.reshape(b, t, LANES)
    q, k, v, qi, k_even, k_odd = dsa_prep(proj, small, q_norm, kv_norm, w_uq, w_iq, w_uk, w_uv, q_gain, k_gain,
                                          ik_w, ik_b)
    n_sel = min(TOPK_MAX, t // 4)
    mask = indexer_mask(qi, small3d, k_even, k_odd, b, t, n_sel)
    tq = _tile(t, 512)
    y_a = attention(q, k, v, mask, bias_tiles(rel_bias, tq), proj3d, b, t, tq)
    qh, kh, vv, gcb, bb, grow = gdn_prep(proj3d, small3d, conv_w, a_log, dt_bias, b, t)
    y_b = gdn_scan(qh, kh, vv, gcb, bb, grow, proj3d, o_norm, b, t)
    wo = w_out.astype(BF16)
    return out_proj(x2d, y_a.reshape(b * t, -1), y_b.reshape(b * t, -1), wo[:A_WIDTH], wo[A_WIDTH:])


def _cd_layer(x2d, b, t, norm_w, w_in, dw_w, dw_b, ln_w, ln_b, d_conv_w, w_out):
    proj = norm_matmul(x2d, norm_w, w_in.astype(BF16))
    y_c, y_d = cd_mix(proj.reshape(b, t, -1), dw_w, dw_b, ln_w, ln_b, d_conv_w, b, t)
    wo = w_out.astype(BF16)
    return out_proj(x2d, y_c.reshape(b * t, -1), y_d.reshape(b * t, -1), wo[:C_WIDTH], wo[C_WIDTH:])


def kernel(x, norm_w, rel_bias, ab_w_in, a_q_norm, a_w_uq, a_w_iq, a_kv_norm, a_w_uk, a_w_uv, a_q_gain,
           a_k_gain, a_ik_norm_w, a_ik_norm_b, b_conv_w, b_a_log, b_dt_bias, b_o_norm, ab_w_out, cd_w_in,
           c_dw_w, c_dw_b, c_ln_w, c_ln_b, d_conv_w, cd_w_out):
    b, t, d = x.shape
    depth = norm_w.shape[0]
    x2d = x.reshape(b * t, d)
    for i in range(depth):
        j = i // 2
        if i % 2 == 0:
            x2d = _ab_layer(x2d, b, t, norm_w[i], rel_bias, ab_w_in[j], a_q_norm[j], a_w_uq[j], a_w_iq[j],
                            a_kv_norm[j], a_w_uk[j], a_w_uv[j], a_q_gain[j], a_k_gain[j], a_ik_norm_w[j],
                            a_ik_norm_b[j], b_conv_w[j], b_a_log[j], b_dt_bias[j], b_o_norm[j], ab_w_out[j])
        else:
            x2d = _cd_layer(x2d, b, t, norm_w[i], cd_w_in[j], c_dw_w[j], c_dw_b[j], c_ln_w[j], c_ln_b[j],
                            d_conv_w[j], cd_w_out[j])
    return x2d.reshape(b, t, d)
```

```python
import functools
import math

import jax
import jax.numpy as jnp
from jax import lax
from jax.experimental import pallas as pl
from jax.experimental.pallas import tpu as pltpu

D_MODEL = 2048
A_HEADS = 8
A_HEAD_DIM = 128
A_WIDTH = A_HEADS * A_HEAD_DIM
A_Q_LORA = 512
A_KV_LORA = 256
IDX_HEADS = 16
IDX_DIM = 64
TOPK_MAX = 256
REL_BUCKETS = 32
REL_MAX_DIST = 128
B_HEADS = 8
B_HEAD_DIM = 128
B_WIDTH = B_HEADS * B_HEAD_DIM
B_CONV = 4
GDN_CHUNK = 64
C_WIDTH = 1024
C_CONV = 31
D_WIDTH = 1024
D_CONV = 3
EPS = 1e-6

AB_SPLITS = (A_Q_LORA, A_KV_LORA, IDX_DIM, IDX_HEADS, A_WIDTH, 3 * B_WIDTH, B_HEADS, B_HEADS, B_WIDTH)
AB_IN_PAD = 6144
SM_KIDX = 0
SM_WIDX = 64
SM_BETA = 80
SM_ALPHA = 88
COL_QKV_B = 0
COL_Z_A = 3072
COL_Z_B = 4096
COL_C_Q = 5120
COL_C_KV = 5632

LANES = 128
MASK_NEG = -1e30
INT_MIN = -(2 ** 31)
LOG2E = math.log2(math.e)
BF16_ROWS = 16
V_ROWS = A_HEAD_DIM + BF16_ROWS
VMEM_LIMIT = 56 * 1024 * 1024

F32 = jnp.float32
BF16 = jnp.bfloat16
HIGHEST = lax.Precision.HIGHEST


def _tile(n, pref):
    t = min(n, pref)
    assert n % t == 0, (n, t)
    return t


def _params(*sem):
    return pltpu.CompilerParams(dimension_semantics=sem, vmem_limit_bytes=VMEM_LIMIT)


def _dot(a, b):
    return jnp.dot(a, b, preferred_element_type=F32)


def _dot_nt(a, b):
    return lax.dot_general(a, b, (((1,), (1,)), ((), ())), preferred_element_type=F32)


def _dot_tn(a, b):
    return lax.dot_general(a, b, (((0,), (0,)), ((), ())), preferred_element_type=F32)


def _silu(x):
    return x * (1.0 / (1.0 + jnp.exp(-x)))


def _sigmoid(x):
    return 1.0 / (1.0 + jnp.exp(-x))


def _norm_matmul_body(*refs, with_f32_cols):
    if with_f32_cols:
        x_ref, nw_ref, w_ref, ws_ref, o_ref, os_ref, h_ref = refs
    else:
        x_ref, nw_ref, w_ref, o_ref, h_ref = refs

    @pl.when(pl.program_id(1) == 0)
    def _():
        x = x_ref[...]
        ms = jnp.mean(x * x, axis=-1, keepdims=True)
        h_ref[...] = (x * lax.rsqrt(ms + EPS) * nw_ref[...]).astype(h_ref.dtype)
        if with_f32_cols:
            os_ref[...] = _dot(h_ref[...], ws_ref[...])

    o_ref[...] = _dot(h_ref[...], w_ref[...]).astype(o_ref.dtype)


def norm_matmul(x2d, norm_w, w_bf16, w_f32_cols=None):
    m, k = x2d.shape
    n = w_bf16.shape[1]
    tm, tn = _tile(m, 1024), _tile(n, 1024)
    extra = w_f32_cols is not None
    in_specs = [
        pl.BlockSpec((tm, k), lambda i, j: (i, 0)),
        pl.BlockSpec((1, k), lambda i, j: (0, 0)),
        pl.BlockSpec((k, tn), lambda i, j: (0, j)),
    ]
    out_specs = [pl.BlockSpec((tm, tn), lambda i, j: (i, j))]
    out_shape = [jax.ShapeDtypeStruct((m, n), BF16)]
    args = [x2d, norm_w.reshape(1, k), w_bf16]
    if extra:
        in_specs.append(pl.BlockSpec((k, LANES), lambda i, j: (0, 0)))
        out_specs.append(pl.BlockSpec((tm, LANES), lambda i, j: (i, 0)))
        out_shape.append(jax.ShapeDtypeStruct((m, LANES), F32))
        args.append(w_f32_cols)
    res = pl.pallas_call(
        functools.partial(_norm_matmul_body, with_f32_cols=extra),
        grid=(m // tm, n // tn),
        in_specs=in_specs,
        out_specs=out_specs,
        out_shape=out_shape,
        scratch_shapes=[pltpu.VMEM((tm, k), BF16)],
        compiler_params=_params("parallel", "arbitrary"),
        name="norm_matmul",
    )(*args)
    return res if extra else res[0]


def _out_proj_body(x_ref, ya_ref, yb_ref, wa_ref, wb_ref, o_ref):
    o_ref[...] = x_ref[...] + _dot(ya_ref[...], wa_ref[...]) + _dot(yb_ref[...], wb_ref[...])


def out_proj(x2d, ya, yb, wa_bf16, wb_bf16):
    m, n = x2d.shape
    ka, kb = ya.shape[1], yb.shape[1]
    tm = _tile(m, 512)
    return pl.pallas_call(
        _out_proj_body,
        grid=(m // tm,),
        in_specs=[
            pl.BlockSpec((tm, n), lambda i: (i, 0)),
            pl.BlockSpec((tm, ka), lambda i: (i, 0)),
            pl.BlockSpec((tm, kb), lambda i: (i, 0)),
            pl.BlockSpec((ka, n), lambda i: (0, 0)),
            pl.BlockSpec((kb, n), lambda i: (0, 0)),
        ],
        out_specs=pl.BlockSpec((tm, n), lambda i: (i, 0)),
        out_shape=jax.ShapeDtypeStruct((m, n), F32),
        compiler_params=_params("parallel"),
        name="out_proj",
    )(x2d, ya, yb, wa_bf16, wb_bf16)


def _head_rmsnorm(y, gain_row, scale):
    outs = []
    for h in range(y.shape[1] // LANES):
        seg = y[:, h * LANES:(h + 1) * LANES]
        ms = jnp.mean(seg * seg, axis=-1, keepdims=True)
        outs.append(seg * lax.rsqrt(ms + EPS) * (gain_row * scale))
    return jnp.concatenate(outs, axis=-1)


def _dsa_prep_body(cq_ref, ckv_ref, sm_ref, qn_ref, kvn_ref, wuq_ref, wiq_ref, wuk_ref, wuv_ref,
                   qg_ref, kg_ref, ikw_ref, ikb_ref,
                   q_ref, k_ref, v_ref, qi_ref, ke_ref, ko_ref):
    cq = cq_ref[...].astype(F32)
    cq = cq * lax.rsqrt(jnp.mean(cq * cq, axis=-1, keepdims=True) + EPS) * qn_ref[...]
    cqb = cq.astype(BF16)
    q = _dot(cqb, wuq_ref[...])
    q_ref[...] = _head_rmsnorm(q, qg_ref[...], A_HEAD_DIM ** -0.5 * LOG2E).astype(q_ref.dtype)
    qi_ref[...] = _dot(cqb, wiq_ref[...]).astype(qi_ref.dtype)

    ckv = ckv_ref[...].astype(F32)
    ckv = ckv * lax.rsqrt(jnp.mean(ckv * ckv, axis=-1, keepdims=True) + EPS) * kvn_ref[...]
    ckvb = ckv.astype(BF16)
    k = _dot(ckvb, wuk_ref[...])
    k_ref[...] = _head_rmsnorm(k, kg_ref[...], 1.0).astype(k_ref.dtype)
    vt = _dot_nt(wuv_ref[...], ckvb).astype(v_ref.dtype)
    ones = jnp.ones((BF16_ROWS, vt.shape[1]), v_ref.dtype)
    for h in range(A_HEADS):
        v_ref[h * V_ROWS:h * V_ROWS + A_HEAD_DIM, :] = vt[h * A_HEAD_DIM:(h + 1) * A_HEAD_DIM, :]
        v_ref[h * V_ROWS + A_HEAD_DIM:(h + 1) * V_ROWS, :] = ones

    kx = sm_ref[...][:, SM_KIDX:SM_KIDX + IDX_DIM]
    mu = jnp.mean(kx, axis=-1, keepdims=True)
    kc = kx - mu
    kl = kc * lax.rsqrt(jnp.mean(kc * kc, axis=-1, keepdims=True) + EPS) * ikw_ref[...] + ikb_ref[...]
    z = jnp.zeros_like(kl)
    ke_ref[...] = jnp.concatenate([kl, z], axis=-1).astype(ke_ref.dtype)
    ko_ref[...] = jnp.concatenate([z, kl], axis=-1).astype(ko_ref.dtype)


def dsa_prep(proj, small, q_norm, kv_norm, w_uq, w_iq, w_uk, w_uv, q_gain, k_gain, ik_w, ik_b):
    m = proj.shape[0]
    tm = _tile(m, 512)
    row = lambda a: a.reshape(1, -1)
    full = lambda a: pl.BlockSpec(a.shape, lambda i: (0,) * a.ndim)
    consts = [row(q_norm), row(kv_norm), w_uq.astype(BF16), w_iq.astype(BF16), w_uk.astype(BF16),
              w_uv.T.astype(BF16), row(q_gain), row(k_gain), row(ik_w), row(ik_b)]
    outs = [jax.ShapeDtypeStruct((m, A_WIDTH), BF16)] * 2 + [
        jax.ShapeDtypeStruct((A_HEADS * V_ROWS, m), BF16),
        jax.ShapeDtypeStruct((m, IDX_HEADS * IDX_DIM), BF16),
        jax.ShapeDtypeStruct((m, LANES), BF16), jax.ShapeDtypeStruct((m, LANES), BF16)]
    out_specs = [pl.BlockSpec((tm, s.shape[1]), lambda i: (i, 0)) for s in outs]
    out_specs[2] = pl.BlockSpec((A_HEADS * V_ROWS, tm), lambda i: (0, i))
    return pl.pallas_call(
        _dsa_prep_body,
        grid=(m // tm,),
        in_specs=[
            pl.BlockSpec((tm, A_Q_LORA), lambda i: (i, COL_C_Q // A_Q_LORA)),
            pl.BlockSpec((tm, A_KV_LORA), lambda i: (i, COL_C_KV // A_KV_LORA)),
            pl.BlockSpec((tm, LANES), lambda i: (i, 0)),
        ] + [full(c) for c in consts],
        out_specs=out_specs,
        out_shape=outs,
        compiler_params=_params("parallel"),
        name="dsa_prep",
    )(proj, proj, small, *consts)


def _t5_bucket(dist):
    max_exact = REL_BUCKETS // 2
    large = max_exact + (jnp.log(jnp.maximum(dist, 1).astype(F32) / max_exact)
                         / math.log(REL_MAX_DIST / max_exact) * (REL_BUCKETS - max_exact)).astype(jnp.int32)
    large = jnp.minimum(large, REL_BUCKETS - 1)
    return jnp.where(dist < max_exact, dist, large)


def _bias_tiles_body(rb_ref, o_ref, *, t):
    h = pl.program_id(0)
    off = pl.program_id(1) * t
    dist = off + lax.broadcasted_iota(jnp.int32, (t, t), 1) - lax.broadcasted_iota(jnp.int32, (t, t), 0)
    bucket = _t5_bucket(jnp.maximum(dist, 0))
    acc = jnp.zeros((t, t), F32)
    for b in range(REL_BUCKETS):
        acc = jnp.where(bucket == b, rb_ref[b, h], acc)
    o_ref[0, 0] = ((acc - rb_ref[REL_BUCKETS - 1, h]) * LOG2E).astype(o_ref.dtype)


def bias_tiles(rel_bias, t):
    return pl.pallas_call(
        functools.partial(_bias_tiles_body, t=t),
        grid=(A_HEADS, 2),
        in_specs=[pl.BlockSpec(memory_space=pltpu.SMEM)],
        out_specs=pl.BlockSpec((1, 1, t, t), lambda h, c: (c, h, 0, 0)),
        out_shape=jax.ShapeDtypeStruct((2, A_HEADS, t, t), BF16),
        compiler_params=_params("parallel", "parallel"),
        name="bias_tiles",
    )(rel_bias)


SUBLANES = 8
MAX_SELECT_STEPS = 40
SELECT_STEPS_PER_CHECK = 4


def _key_of(x):
    bits = pltpu.bitcast(x, jnp.int32)
    key = jnp.where(bits < 0, bits ^ jnp.int32(0x7FFFFFFF), bits)
    return jnp.where(bits == jnp.int32(INT_MIN), 0, key)


def _indexer_body(qi_ref, sm_ref, ke_ref, ko_ref, mask_ref, key_ref, hi_ref, low_ref, w_ref, *, tq, ck, n_sel):
    i = pl.program_id(1)
    nk = ((i + 1) * tq + ck - 1) // ck
    groups = ck // SUBLANES
    scale = (IDX_HEADS ** -0.5) * (IDX_DIM ** -0.5)
    smt = sm_ref[0].T
    for h in range(IDX_HEADS):
        w_ref[h] = jnp.broadcast_to(smt[SM_WIDX + h:SM_WIDX + h + 1, :] * scale, (SUBLANES, tq))
    kloc = lax.broadcasted_iota(jnp.int32, (ck, tq), 0)
    qpos = i * tq + lax.broadcasted_iota(jnp.int32, (ck, tq), 1)
    int_max = jnp.int32(2 ** 31 - 1)

    def score_chunk(c, carry):
        kmin, kmax = carry
        ks = pl.multiple_of(c * ck, ck)
        ke = ke_ref[0, pl.ds(ks, ck), :]
        ko = ko_ref[0, pl.ds(ks, ck), :]
        acc = jnp.zeros((ck, tq), F32)
        for j in range(IDX_HEADS // 2):
            qp = qi_ref[0, :, j * LANES:(j + 1) * LANES]
            for par, kk in ((0, ke), (1, ko)):
                s = jnp.maximum(_dot_nt(kk, qp), 0.0)
                acc = acc + s * jnp.tile(w_ref[2 * j + par], (groups, 1))
        key = _key_of(acc)
        valid = c * ck + kloc <= qpos
        masked = jnp.where(valid, key, jnp.int32(INT_MIN))
        key_ref[pl.ds(ks, ck), :] = masked
        hi_ref[pl.ds(ks, ck), :] = jnp.right_shift(masked, 16).astype(jnp.int16)
        kmin = jnp.minimum(kmin, jnp.min(jnp.where(valid, key, int_max).reshape(groups, SUBLANES, tq), axis=0))
        kmax = jnp.maximum(kmax, jnp.max(masked.reshape(groups, SUBLANES, tq), axis=0))
        return kmin, kmax

    kmin, kmax = lax.fori_loop(0, nk, score_chunk, (jnp.full((SUBLANES, tq), int_max, jnp.int32),
                                                    jnp.full((SUBLANES, tq), INT_MIN, jnp.int32)))

    rows16 = ck // BF16_ROWS
    i16_min = -(2 ** 15)

    def scan16(a_ref, init, fn):
        def body(c, acc):
            ks = pl.multiple_of(c * ck, ck)
            return fn(acc, a_ref[pl.ds(ks, ck), :].reshape(rows16, BF16_ROWS, tq))

        return lax.fori_loop(0, nk, body, init)

    def count_ge16(a_ref, p):
        p16 = jnp.broadcast_to(p.astype(jnp.int16), (BF16_ROWS, tq))

        def fn(cnt, a):
            ind = jnp.where(a >= p16[None], jnp.int16(1), jnp.int16(0))
            for g in range(rows16):
                cnt = cnt + ind[g]
            return cnt

        cnt = scan16(a_ref, jnp.zeros((BF16_ROWS, tq), jnp.int16), fn)
        return jnp.sum(cnt.astype(jnp.int32), axis=0, keepdims=True)

    def max_le16(a_ref, h):
        h16 = jnp.broadcast_to(h.astype(jnp.int16), (BF16_ROWS, tq))

        def fn(acc, a):
            v = jnp.where(a <= h16[None], a, jnp.int16(i16_min))
            for g in range(rows16):
                acc = jnp.where(v[g] > acc, v[g], acc)
            return acc

        acc = scan16(a_ref, jnp.full((BF16_ROWS, tq), i16_min, jnp.int16), fn)
        return jnp.max(acc.astype(jnp.int32), axis=0, keepdims=True)

    def rank_select16(a_ref, r, skip, lo, hi, clo, chi, floor_count):
        def closed(lo, hi, clo):
            return skip | (clo == r) | (hi == lo + 1)

        def open_rows(lo, hi, clo, chi):
            return jnp.logical_not(closed(lo, hi, clo) | (chi == r - 1))

        def cond(st):
            it, lo, hi, clo, chi = st
            n_open = jnp.max(jnp.where(open_rows(lo, hi, clo, chi), 1, 0))
            return jnp.logical_and(it < MAX_SELECT_STEPS, n_open > 0)

        def step(st):
            it, lo, hi, clo, chi = st
            for _ in range(SELECT_STEPS_PER_CHECK):
                upd = open_rows(lo, hi, clo, chi)
                p = lo + jnp.maximum(jnp.right_shift(hi - lo, 1), 1)
                cnt = count_ge16(a_ref, p)
                ge = cnt >= r
                up, dn = upd & ge, upd & jnp.logical_not(ge)
                lo, hi = jnp.where(up, p, lo), jnp.where(dn, p, hi)
                clo, chi = jnp.where(up, cnt, clo), jnp.where(dn, cnt, chi)
            return it + SELECT_STEPS_PER_CHECK, lo, hi, clo, chi

        _, lo, hi, clo, chi = lax.while_loop(cond, step, (jnp.int32(0), lo, hi, clo, chi))
        last = jnp.logical_not(closed(lo, hi, clo))
        t_last = max_le16(a_ref, hi - 1)
        c_last = jnp.where(t_last == i16_min, floor_count, count_ge16(a_ref, t_last))
        return jnp.where(last, t_last, lo), jnp.where(last, c_last, clo), chi

    n_valid = i * tq + lax.broadcasted_iota(jnp.int32, (1, tq), 1) + 1
    few = n_valid <= n_sel
    zeros = jnp.zeros((1, tq), jnp.int32)
    hi_lo0 = jnp.right_shift(jnp.min(kmin, axis=0, keepdims=True), 16)
    hi_hi0 = jnp.right_shift(jnp.max(kmax, axis=0, keepdims=True), 16) + 1
    t_hi, c_ge_hi, c_gt_hi = rank_select16(hi_ref, n_sel, few, hi_lo0, hi_hi0, n_valid, zeros, n_valid)
    exact_hi = c_ge_hi == n_sel

    t_hi16 = jnp.broadcast_to(t_hi.astype(jnp.int16), (BF16_ROWS, tq))

    def fill_low(c, carry):
        ks = pl.multiple_of(c * ck, ck)
        low = ((key_ref[pl.ds(ks, ck), :] & 0xFFFF) - 2 ** 15).astype(jnp.int16).reshape(rows16, BF16_ROWS, tq)
        same = hi_ref[pl.ds(ks, ck), :].reshape(rows16, BF16_ROWS, tq) == t_hi16[None]
        low_ref[pl.ds(ks, ck), :] = jnp.where(same, low, jnp.int16(i16_min)).reshape(ck, tq)
        return carry

    lax.fori_loop(0, nk, fill_low, 0)
    in_bucket = c_ge_hi - c_gt_hi
    t_lo, c_ge_lo, c_gt_lo = rank_select16(low_ref, n_sel - c_gt_hi, few | exact_hi,
                                           jnp.full((1, tq), i16_min, jnp.int32), jnp.full((1, tq), 2 ** 15, jnp.int32),
                                           in_bucket, zeros, in_bucket)
    lo = jnp.where(exact_hi, t_hi * 2 ** 16, t_hi * 2 ** 16 + (t_lo + 2 ** 15))
    clo = jnp.where(exact_hi, n_sel, c_gt_hi + c_ge_lo)
    chi = c_gt_hi + c_gt_lo
    thr = jnp.where(few, jnp.int32(INT_MIN + 1), lo)
    thr8 = jnp.broadcast_to(thr, (SUBLANES, tq))
    tied = jnp.logical_not(few) & (clo > n_sel)
    need = n_sel - chi
    n_keys = mask_ref.shape[1]

    def tie_cut(_):
        def count_le(j):
            j8 = jnp.broadcast_to(j, (SUBLANES, tq))

            def body(c, cnt):
                ks = pl.multiple_of(c * ck, ck)
                kk = key_ref[pl.ds(ks, ck), :].reshape(groups, SUBLANES, tq)
                idx = (c * ck + kloc).reshape(groups, SUBLANES, tq)
                hit = jnp.where(kk == thr8[None], jnp.where(idx <= j8[None], 1, 0), 0)
                return cnt + jnp.sum(hit, axis=0)

            cnt = lax.fori_loop(0, nk, body, jnp.zeros((SUBLANES, tq), jnp.int32))
            return jnp.sum(cnt, axis=0, keepdims=True)

        def bisect(_, st):
            jlo, jhi = st
            mid = jlo + jnp.right_shift(jhi - jlo, 1)
            ok = count_le(mid) >= need
            return jnp.where(ok, jlo, mid), jnp.where(ok, mid, jhi)

        _, jhi = lax.fori_loop(0, n_keys.bit_length(), bisect,
                               (jnp.full((1, tq), -1, jnp.int32), jnp.full((1, tq), n_keys - 1, jnp.int32)))
        return jnp.where(tied, jhi, int_max)

    jcut = lax.cond(jnp.max(jnp.where(tied, 1, 0)) > 0, tie_cut, lambda _: jnp.full((1, tq), int_max, jnp.int32), 0)
    jcut8 = jnp.broadcast_to(jcut, (SUBLANES, tq))

    def write_chunk(c, carry):
        ks = pl.multiple_of(c * ck, ck)
        kk = key_ref[pl.ds(ks, ck), :].reshape(groups, SUBLANES, tq)
        idx = (c * ck + kloc).reshape(groups, SUBLANES, tq)
        at_thr = jnp.where(kk == thr8[None], jnp.where(idx <= jcut8[None], 1, 0), 0)
        sel = jnp.where(kk > thr8[None], 1, at_thr)
        mask_ref[0, pl.ds(ks, ck), :] = sel.reshape(ck, tq).astype(mask_ref.dtype)
        return carry

    lax.fori_loop(0, nk, write_chunk, 0)

    def zero_chunk(c, carry):
        ks = pl.multiple_of(c * ck, ck)
        mask_ref[0, pl.ds(ks, ck), :] = jnp.zeros((ck, tq), mask_ref.dtype)
        return carry

    lax.fori_loop(nk, mask_ref.shape[1] // ck, zero_chunk, 0)


def indexer_mask(qi, small, k_even, k_odd, b, t, n_sel):
    tq = _tile(t, 256)
    ck = _tile(t, 512)
    return pl.pallas_call(
        functools.partial(_indexer_body, tq=tq, ck=ck, n_sel=n_sel),
        grid=(b, t // tq),
        in_specs=[
            pl.BlockSpec((1, tq, IDX_HEADS * IDX_DIM), lambda bb, i: (bb, i, 0)),
            pl.BlockSpec((1, tq, LANES), lambda bb, i: (bb, i, 0)),
            pl.BlockSpec((1, t, LANES), lambda bb, i: (bb, 0, 0)),
            pl.BlockSpec((1, t, LANES), lambda bb, i: (bb, 0, 0)),
        ],
        out_specs=pl.BlockSpec((1, t, tq), lambda bb, i: (bb, 0, i)),
        out_shape=jax.ShapeDtypeStruct((b, t, t), jnp.int8),
        scratch_shapes=[pltpu.VMEM((t, tq), jnp.int32), pltpu.VMEM((t, tq), jnp.int16),
                        pltpu.VMEM((t, tq), jnp.int16), pltpu.VMEM((IDX_HEADS, SUBLANES, tq), F32)],
        compiler_params=_params("parallel", "parallel"),
        name="indexer",
    )(qi.reshape(b, t, -1), small, k_even.reshape(b, t, LANES), k_odd.reshape(b, t, LANES))


ATTN_HEAD_GROUP = 4


def _attn_body(qi_ref, ki_ref, q_ref, k_ref, vt_ref, mask_ref, bias_ref, z_ref, o_ref, m_ref, acc_ref, *, tq):
    qi, ki = qi_ref[pl.program_id(1)], ki_ref[pl.program_id(1)]

    @pl.when(ki == 0)
    def _():
        m_ref[...] = jnp.full(m_ref.shape, MASK_NEG, F32)
        acc_ref[...] = jnp.zeros(acc_ref.shape, F32)

    def tile(near):
        madd = (1.0 - mask_ref[0].astype(F32)) * MASK_NEG
        off = jnp.where(ki == qi, 0, 1)
        for g in range(0, A_HEADS, ATTN_HEAD_GROUP):
            hs = range(g, g + ATTN_HEAD_GROUP)
            grp = slice(g, g + ATTN_HEAD_GROUP)
            head = lambda ref, h: ref[0, :, h * A_HEAD_DIM:(h + 1) * A_HEAD_DIM]
            k2 = jnp.stack([head(k_ref, h) for h in hs])
            q2 = jnp.stack([head(q_ref, h) for h in hs])
            s = lax.dot_general(k2, q2, (((2,), (2,)), ((0,), (0,))), preferred_element_type=F32) + madd[None]
            if near:
                s = s + bias_ref[off, grp].astype(F32)
            m_prev = m_ref[grp]
            m_new = jnp.maximum(m_prev, jnp.max(s, axis=1, keepdims=True))
            p = jnp.exp2(s - m_new).astype(BF16)
            vt2 = jnp.stack([vt_ref[h * V_ROWS:(h + 1) * V_ROWS, :] for h in hs])
            pv = lax.dot_general(vt2, p, (((2,), (1,)), ((0,), (0,))), preferred_element_type=F32)
            acc_ref[grp] = jnp.exp2(m_prev - m_new) * acc_ref[grp] + pv
            m_ref[grp] = m_new

    @pl.when(ki + 1 < qi)
    def _():
        tile(False)

    @pl.when(ki + 1 >= qi)
    def _():
        tile(True)

    @pl.when(ki == qi)
    def _():
        outs = []
        for h in range(A_HEADS):
            a = acc_ref[h]
            outs.append((a[:A_HEAD_DIM] / a[A_HEAD_DIM:A_HEAD_DIM + 1]).T)
        o_ref[0] = (jnp.concatenate(outs, axis=-1) * _silu(z_ref[0].astype(F32))).astype(o_ref.dtype)


def attention(q, k, vt, mask, bias, proj3d, b, t, tq):
    nq = t // tq
    pairs = [(i, j) for i in range(nq) for j in range(i + 1)]
    qi_tab = jnp.asarray([p[0] for p in pairs], jnp.int32)
    ki_tab = jnp.asarray([p[1] for p in pairs], jnp.int32)
    grid_spec = pltpu.PrefetchScalarGridSpec(
        num_scalar_prefetch=2,
        grid=(b, len(pairs)),
        in_specs=[
            pl.BlockSpec((1, tq, A_WIDTH), lambda bb, p, qi, ki: (bb, qi[p], 0)),
            pl.BlockSpec((1, tq, A_WIDTH), lambda bb, p, qi, ki: (bb, ki[p], 0)),
            pl.BlockSpec((A_HEADS * V_ROWS, tq), lambda bb, p, qi, ki: (0, bb * nq + ki[p])),
            pl.BlockSpec((1, tq, tq), lambda bb, p, qi, ki: (bb, ki[p], qi[p])),
            pl.BlockSpec(bias.shape, lambda bb, p, qi, ki: (0, 0, 0, 0)),
            pl.BlockSpec((1, tq, A_WIDTH), lambda bb, p, qi, ki: (bb, qi[p], COL_Z_A // A_WIDTH)),
        ],
        out_specs=pl.BlockSpec((1, tq, A_WIDTH), lambda bb, p, qi, ki: (bb, qi[p], 0)),
        scratch_shapes=[pltpu.VMEM((A_HEADS, 1, tq), F32), pltpu.VMEM((A_HEADS, V_ROWS, tq), F32)],
    )
    return pl.pallas_call(
        functools.partial(_attn_body, tq=tq),
        grid_spec=grid_spec,
        out_shape=jax.ShapeDtypeStruct((b, t, A_WIDTH), BF16),
        compiler_params=_params("parallel", "arbitrary"),
        name="dsa_attention",
    )(qi_tab, ki_tab, q.reshape(b, t, -1), k.reshape(b, t, -1), vt, mask, bias, proj3d)


def _gdn_prep_body(x_ref, halo_ref, sm_ref, cw_ref, alog_ref, dtb_ref,
                   q_ref, k_ref, v_ref, gcb_ref, bb_ref, grow_ref, xs_ref, *, tm):
    i = pl.program_id(1)
    hal = halo_ref[0].astype(F32)
    xs_ref[0:BF16_ROWS, :] = jnp.where(i > 0, hal, jnp.zeros_like(hal))
    xs_ref[BF16_ROWS:, :] = x_ref[0].astype(F32)
    y = jnp.zeros((tm, 3 * B_WIDTH), F32)
    for j in range(B_CONV):
        y = y + cw_ref[j:j + 1, :] * xs_ref[pl.ds(BF16_ROWS - (B_CONV - 1) + j, tm), :]
    y = _silu(y)
    for h in range(B_HEADS):
        sl = slice(h * LANES, (h + 1) * LANES)
        qh = y[:, sl]
        q_ref[0, :, sl] = qh * lax.rsqrt(jnp.sum(qh * qh, axis=-1, keepdims=True) + EPS) * (B_HEAD_DIM ** -0.5)
        kh = y[:, B_WIDTH + h * LANES:B_WIDTH + (h + 1) * LANES]
        k_ref[0, :, sl] = kh * lax.rsqrt(jnp.sum(kh * kh, axis=-1, keepdims=True) + EPS)
    v_ref[0] = y[:, 2 * B_WIDTH:]

    sm = sm_ref[0]
    xg = sm + dtb_ref[...]
    softplus = jnp.maximum(xg, 0.0) + jnp.log(1.0 + jnp.exp(-jnp.abs(xg)))
    g = -jnp.exp(alog_ref[...]) * softplus
    r = lax.broadcasted_iota(jnp.int32, (tm, tm), 0)
    c = lax.broadcasted_iota(jnp.int32, (tm, tm), 1)
    sh = int(math.log2(GDN_CHUNK))
    same_chunk = jnp.right_shift(r, sh) == jnp.right_shift(c, sh)
    tri = jnp.where(jnp.logical_and(same_chunk, c <= r), 1.0, 0.0).astype(F32)
    gc = lax.dot_general(tri, g, (((1,), (0,)), ((), ())), precision=HIGHEST, preferred_element_type=F32)
    beta = _sigmoid(sm)
    for h in range(B_HEADS):
        sl = slice(h * LANES, (h + 1) * LANES)
        gcb_ref[0, :, sl] = jnp.broadcast_to(gc[:, SM_ALPHA + h:SM_ALPHA + h + 1], (tm, LANES))
        bb_ref[0, :, sl] = jnp.broadcast_to(beta[:, SM_BETA + h:SM_BETA + h + 1], (tm, LANES))
    gct = gc.T
    for cc in range(tm // GDN_CHUNK):
        grow_ref[0, cc] = gct[SM_ALPHA:SM_ALPHA + B_HEADS, cc * GDN_CHUNK:(cc + 1) * GDN_CHUNK]


def gdn_prep(proj3d, small3d, conv_w, a_log, dt_bias, b, t):
    tm = _tile(t, 256)
    pad_row = lambda v: jnp.zeros((1, LANES), F32).at[0, SM_ALPHA:SM_ALPHA + B_HEADS].set(v)
    nc = tm // GDN_CHUNK
    act = jax.ShapeDtypeStruct((b, t, B_WIDTH), F32)
    full = lambda a: pl.BlockSpec(a.shape, lambda bb, i: (0,) * a.ndim)
    consts = [conv_w, pad_row(a_log), pad_row(dt_bias)]
    return pl.pallas_call(
        functools.partial(_gdn_prep_body, tm=tm),
        grid=(b, t // tm),
        in_specs=[
            pl.BlockSpec((1, tm, 3 * B_WIDTH), lambda bb, i: (bb, i, COL_QKV_B // (3 * B_WIDTH))),
            pl.BlockSpec((1, BF16_ROWS, 3 * B_WIDTH),
                         lambda bb, i: (bb, jnp.maximum(i * (tm // BF16_ROWS) - 1, 0), 0)),
            pl.BlockSpec((1, tm, LANES), lambda bb, i: (bb, i, 0)),
        ] + [full(c) for c in consts],
        out_specs=[pl.BlockSpec((1, tm, B_WIDTH), lambda bb, i: (bb, i, 0))] * 5
        + [pl.BlockSpec((1, nc, B_HEADS, GDN_CHUNK), lambda bb, i: (bb, i, 0, 0))],
        out_shape=[act] * 5 + [jax.ShapeDtypeStruct((b, t // GDN_CHUNK, B_HEADS, GDN_CHUNK), F32)],
        scratch_shapes=[pltpu.VMEM((tm + BF16_ROWS, 3 * B_WIDTH), F32)],
        compiler_params=_params("parallel", "parallel"),
        name="gdn_prep",
    )(proj3d, proj3d, small3d, *consts)


def _gdn_body(q_ref, k_ref, v_ref, gcb_ref, bb_ref, grow_ref, z_ref, on_ref, o_ref, s_ref, *, nc):
    @pl.when(pl.program_id(1) == 0)
    def _():
        s_ref[...] = jnp.zeros(s_ref.shape, F32)

    cs, nh = GDN_CHUNK, B_HEADS
    nb = nc * nh
    ri = lax.broadcasted_iota(jnp.int32, (nb, cs, cs), 1)
    ci = lax.broadcasted_iota(jnp.int32, (nb, cs, cs), 2)
    lower = ci <= ri
    strict = ci < ri
    eye = jnp.where(ci == ri, 1.0, 0.0).astype(F32)
    bf = lambda a: a.astype(BF16)
    bmm = lambda a, b: lax.dot_general(a, b, (((2,), (1,)), ((0,), (0,))), preferred_element_type=F32)
    bmm_nt = lambda a, b: lax.dot_general(a, b, (((2,), (2,)), ((0,), (0,))), preferred_element_type=F32)
    bmm_tn = lambda a, b: lax.dot_general(a, b, (((1,), (1,)), ((0,), (0,))), preferred_element_type=F32)

    def stack(ref):
        return jnp.stack([ref[0, c * cs:(c + 1) * cs, h * LANES:(h + 1) * LANES]
                          for c in range(nc) for h in range(nh)])

    q, k, v = stack(q_ref), stack(k_ref), stack(v_ref)
    gcb = stack(gcb_ref)
    beta = stack(bb_ref)
    grow = jnp.stack([grow_ref[0, c, h:h + 1, :] for c in range(nc) for h in range(nh)])
    diff = gcb[:, :, :cs] - grow
    decay = jnp.where(lower, jnp.exp(jnp.where(lower, diff, 0.0)), 0.0)
    eg = jnp.exp(gcb)
    glast = gcb[:, cs - 1:cs, :]
    kb = k * beta
    lmat = jnp.where(strict, bmm_nt(bf(kb), bf(k)) * decay, 0.0)
    n = -lmat
    tinv = eye + n
    for _ in range(int(math.log2(cs)) - 1):
        n = bmm(bf(n), bf(n))
        tinv = tinv + bmm(bf(tinv), bf(n))
    uw = bmm(bf(tinv), bf(jnp.concatenate([v * beta, kb * eg], axis=-1)))
    attn = bf(jnp.where(lower, bmm_nt(bf(q), bf(k)) * decay, 0.0))
    qg = bf(q * eg)
    kdec = bf(k * jnp.exp(glast - gcb))
    egl = jnp.exp(glast)

    s = s_ref[...]
    for c in range(nc):
        sl = slice(c * nh, (c + 1) * nh)
        sb = bf(s)
        v_new = uw[sl, :, :LANES] - bmm(bf(uw[sl, :, LANES:]), sb)
        o = bmm(qg[sl], sb) + bmm(attn[sl], bf(v_new))
        s = s * egl[sl] + bmm_tn(kdec[sl], bf(v_new))
        o = o * lax.rsqrt(jnp.mean(o * o, axis=-1, keepdims=True) + EPS) * on_ref[...]
        for h in range(nh):
            rows, cols = slice(c * cs, (c + 1) * cs), slice(h * LANES, (h + 1) * LANES)
            o_ref[0, rows, cols] = (o[h] * _silu(z_ref[0, rows, cols].astype(F32))).astype(o_ref.dtype)
    s_ref[...] = s


def gdn_scan(qh, kh, v, gcb, bb, grow, proj3d, o_norm, b, t):
    tt = _tile(t, 256)
    nc = tt // GDN_CHUNK
    blk = pl.BlockSpec((1, tt, B_WIDTH), lambda bb_, i: (bb_, i, 0))
    return pl.pallas_call(
        functools.partial(_gdn_body, nc=nc),
        grid=(b, t // tt),
        in_specs=[blk] * 5 + [
            pl.BlockSpec((1, nc, B_HEADS, GDN_CHUNK), lambda bb_, i: (bb_, i, 0, 0)),
            pl.BlockSpec((1, tt, B_WIDTH), lambda bb_, i: (bb_, i, COL_Z_B // B_WIDTH)),
            pl.BlockSpec((1, LANES), lambda bb_, i: (0, 0)),
        ],
        out_specs=blk,
        out_shape=jax.ShapeDtypeStruct((b, t, B_WIDTH), BF16),
        scratch_shapes=[pltpu.VMEM((B_HEADS, B_HEAD_DIM, B_HEAD_DIM), F32)],
        compiler_params=_params("parallel", "arbitrary"),
        name="gdn_scan",
    )(qh, kh, v, gcb, bb, grow, proj3d, o_norm.reshape(1, LANES))


C_HALO = 32
D_HALO = BF16_ROWS


def _cd_mix_body(a_ref, g_ref, ah_ref, gh_ref, zc_ref, bg_ref, cg_ref, ud_ref, cgh_ref, udh_ref, zd_ref,
                 dww_ref, dwb_ref, lnw_ref, lnb_ref, dcw_ref, yc_ref, yd_ref, us_ref, ds_ref, *, tm):
    i = pl.program_id(1)
    first = i == 0
    f32 = lambda r: r[0].astype(F32)
    uh = f32(ah_ref) * _sigmoid(f32(gh_ref))
    us_ref[0, 0:C_HALO, :] = jnp.where(first, jnp.zeros_like(uh), uh)
    us_ref[0, C_HALO:, :] = f32(a_ref) * _sigmoid(f32(g_ref))
    span = tm + C_HALO - SUBLANES
    for r in range(1, SUBLANES):
        us_ref[r, 0:span, :] = us_ref[0, pl.ds(r, span), :]
    u = jnp.zeros((tm, C_WIDTH), F32)
    for j in range(C_CONV):
        off = C_HALO - (C_CONV - 1) + j
        r, base = off % SUBLANES, off - off % SUBLANES
        u = u + dww_ref[j:j + 1, :] * us_ref[r, base:base + tm, :]
    u = u + dwb_ref[...]
    mu = jnp.mean(u, axis=-1, keepdims=True)
    uc = u - mu
    u = uc * lax.rsqrt(jnp.mean(uc * uc, axis=-1, keepdims=True) + EPS) * lnw_ref[...] + lnb_ref[...]
    yc_ref[0] = (_silu(u) * _silu(f32(zc_ref))).astype(yc_ref.dtype)

    dh = f32(cgh_ref) * f32(udh_ref)
    ds_ref[0:D_HALO, :] = jnp.where(first, jnp.zeros_like(dh), dh)
    ds_ref[D_HALO:, :] = f32(cg_ref) * f32(ud_ref)
    d = jnp.zeros((tm, D_WIDTH), F32)
    for j in range(D_CONV):
        d = d + dcw_ref[j:j + 1, :] * ds_ref[pl.ds(D_HALO - (D_CONV - 1) + j, tm), :]
    yd_ref[0] = (f32(bg_ref) * d * _silu(f32(zd_ref))).astype(yd_ref.dtype)


def cd_mix(proj3d, dw_w, dw_b, ln_w, ln_b, d_conv_w, b, t):
    tm = _tile(t, 256)
    w = C_WIDTH
    col = lambda n: pl.BlockSpec((1, tm, w), lambda bb, i, n=n: (bb, i, n))
    halo = lambda n, rows: pl.BlockSpec(
        (1, rows, w), lambda bb, i, n=n, rows=rows: (bb, jnp.maximum(i * (tm // rows) - 1, 0), n))
    row = lambda a: a.reshape(1, -1)
    full = lambda a: pl.BlockSpec(a.shape, lambda bb, i: (0,) * a.ndim)
    consts = [dw_w, row(dw_b), row(ln_w), row(ln_b), d_conv_w]
    out = jax.ShapeDtypeStruct((b, t, w), BF16)
    return pl.pallas_call(
        functools.partial(_cd_mix_body, tm=tm),
        grid=(b, t // tm),
        in_specs=[col(0), col(1), halo(0, C_HALO), halo(1, C_HALO), col(2), col(3), col(4), col(5),
                  halo(4, D_HALO), halo(5, D_HALO), col(6)] + [full(c) for c in consts],
        out_specs=[pl.BlockSpec((1, tm, w), lambda bb, i: (bb, i, 0))] * 2,
        out_shape=[out, out],
        scratch_shapes=[pltpu.VMEM((SUBLANES, tm + C_HALO, w), F32), pltpu.VMEM((tm + D_HALO, w), F32)],
        compiler_params=_params("parallel", "parallel"),
        name="cd_mix",
    )(*([proj3d] * 11), *consts)


def _reorder_ab_w_in(w):
    offs = [0]
    for s in AB_SPLITS:
        offs.append(offs[-1] + s)
    part = lambda n: w[:, offs[n]:offs[n + 1]]
    c_q, c_kv, k_idx, w_idx, z_a, qkv_b, beta_b, alpha_b, z_b = (part(n) for n in range(9))
    small = jnp.concatenate([k_idx, w_idx, beta_b, alpha_b], axis=1)
    small = jnp.pad(small, ((0, 0), (0, LANES - small.shape[1])))
    out = jnp.concatenate([qkv_b, z_a, z_b, c_q, c_kv], axis=1)
    return jnp.pad(out, ((0, 0), (0, AB_IN_PAD - out.shape[1]))).astype(BF16), small.astype(BF16)


def _ab_layer(x2d, b, t, norm_w, rel_bias, w_in, q_norm, w_uq, w_iq, kv_norm, w_uk, w_uv, q_gain, k_gain,
              ik_w, ik_b, conv_w, a_log, dt_bias, o_norm, w_out):
    proj, small = norm_matmul(x2d, norm_w, *_reorder_ab_w_in(w_in))
    proj3d, small3d = proj.reshape(b, t, AB_IN_PAD), small.reshape(b, t, LANES)
    q, k, v, qi, k_even, k_odd = dsa_prep(proj, small, q_norm, kv_norm, w_uq, w_iq, w_uk, w_uv, q_gain, k_gain,
                                          ik_w, ik_b)
    n_sel = min(TOPK_MAX, t // 4)
    mask = indexer_mask(qi, small3d, k_even, k_odd, b, t, n_sel)
    tq = _tile(t, 512)
    y_a = attention(q, k, v, mask, bias_tiles(rel_bias, tq), proj3d, b, t, tq)
    qh, kh, vv, gcb, bb, grow = gdn_prep(proj3d, small3d, conv_w, a_log, dt_bias, b, t)
    y_b = gdn_scan(qh, kh, vv, gcb, bb, grow, proj3d, o_norm, b, t)
    wo = w_out.astype(BF16)
    return out_proj(x2d, y_a.reshape(b * t, -1), y_b.reshape(b * t, -1), wo[:A_WIDTH], wo[A_WIDTH:])


def _cd_layer(x2d, b, t, norm_w, w_in, dw_w, dw_b, ln_w, ln_b, d_conv_w, w_out):
    proj = norm_matmul(x2d, norm_w, w_in.astype(BF16))
    y_c, y_d = cd_mix(proj.reshape(b, t, -1), dw_w, dw_b, ln_w, ln_b, d_conv_w, b, t)
    wo = w_out.astype(BF16)
    return out_proj(x2d, y_c.reshape(b * t, -1), y_d.reshape(b * t, -1), wo[:C_WIDTH], wo[C_WIDTH:])


def kernel(x, norm_w, rel_bias, ab_w_in, a_q_norm, a_w_uq, a_w_iq, a_kv_norm, a_w_uk, a_w_uv, a_q_gain,
           a_k_gain, a_ik_norm_w, a_ik_norm_b, b_conv_w, b_a_log, b_dt_bias, b_o_norm, ab_w_out, cd_w_in,
           c_dw_w, c_dw_b, c_ln_w, c_ln_b, d_conv_w, cd_w_out):
    b, t, d = x.shape
    depth = norm_w.shape[0]
    x2d = x.reshape(b * t, d)
    for i in range(depth):
        j = i // 2
        if i % 2 == 0:
            x2d = _ab_layer(x2d, b, t, norm_w[i], rel_bias, ab_w_in[j], a_q_norm[j], a_w_uq[j], a_w_iq[j],
                            a_kv_norm[j], a_w_uk[j], a_w_uv[j], a_q_gain[j], a_k_gain[j], a_ik_norm_w[j],
                            a_ik_norm_b[j], b_conv_w[j], b_a_log[j], b_dt_bias[j], b_o_norm[j], ab_w_out[j])
        else:
            x2d = _cd_layer(x2d, b, t, norm_w[i], cd_w_in[j], c_dw_w[j], c_dw_b[j], c_ln_w[j], c_ln_b[j],
                            d_conv_w[j], cd_w_out[j])
    return x2d.reshape(b, t, d)
```

```python
import functools
import math

import jax
import jax.numpy as jnp
from jax import lax
from jax.experimental import pallas as pl
from jax.experimental.pallas import tpu as pltpu

A_HEADS = 8
A_HEAD_DIM = 128
A_WIDTH = A_HEADS * A_HEAD_DIM
A_Q_LORA = 512
A_KV_LORA = 256
IDX_HEADS = 16
IDX_DIM = 64
TOPK_MAX = 256
REL_BUCKETS = 32
REL_MAX_DIST = 128
B_HEADS = 8
B_HEAD_DIM = 128
B_WIDTH = B_HEADS * B_HEAD_DIM
B_CONV = 4
GDN_CHUNK = 64
C_WIDTH = 1024
C_CONV = 31
D_WIDTH = 1024
D_CONV = 3
EPS = 1e-6

AB_SPLITS = (A_Q_LORA, A_KV_LORA, IDX_DIM, IDX_HEADS, A_WIDTH, 3 * B_WIDTH, B_HEADS, B_HEADS, B_WIDTH)
AB_IN_PAD = 6144
SM_KIDX = 0
SM_WIDX = 64
SM_BETA = 80
SM_ALPHA = 88
COL_QKV_B = 0
COL_Z_A = 3072
COL_Z_B = 4096
COL_C_Q = 5120
COL_C_KV = 5632

LANES = 128
MASK_NEG = -1e30
INT_MIN = -(2 ** 31)
LOG2E = math.log2(math.e)
BF16_ROWS = 16
V_ROWS = A_HEAD_DIM + BF16_ROWS
VMEM_LIMIT = 56 * 1024 * 1024

F32 = jnp.float32
BF16 = jnp.bfloat16
HIGHEST = lax.Precision.HIGHEST


def _tile(n, pref):
    t = min(n, pref)
    assert n % t == 0, (n, t)
    return t


def _params(*sem):
    return pltpu.CompilerParams(dimension_semantics=sem, vmem_limit_bytes=VMEM_LIMIT)


def _dot(a, b):
    return jnp.dot(a, b, preferred_element_type=F32)


def _dot_nt(a, b):
    return lax.dot_general(a, b, (((1,), (1,)), ((), ())), preferred_element_type=F32)


def _silu(x):
    return x * (1.0 / (1.0 + jnp.exp(-x)))


def _sigmoid(x):
    return 1.0 / (1.0 + jnp.exp(-x))


def _norm_matmul_body(*refs, with_f32_cols):
    if with_f32_cols:
        x_ref, nw_ref, w_ref, ws_ref, o_ref, os_ref, h_ref = refs
    else:
        x_ref, nw_ref, w_ref, o_ref, h_ref = refs

    @pl.when(pl.program_id(1) == 0)
    def _():
        x = x_ref[...]
        ms = jnp.mean(x * x, axis=-1, keepdims=True)
        h_ref[...] = (x * lax.rsqrt(ms + EPS) * nw_ref[...]).astype(h_ref.dtype)
        if with_f32_cols:
            os_ref[...] = _dot(h_ref[...], ws_ref[...])

    o_ref[...] = _dot(h_ref[...], w_ref[...]).astype(o_ref.dtype)


def norm_matmul(x2d, norm_w, w_bf16, w_f32_cols=None):
    m, k = x2d.shape
    n = w_bf16.shape[1]
    tm, tn = _tile(m, 1024), _tile(n, 1024)
    extra = w_f32_cols is not None
    in_specs = [
        pl.BlockSpec((tm, k), lambda i, j: (i, 0)),
        pl.BlockSpec((1, k), lambda i, j: (0, 0)),
        pl.BlockSpec((k, tn), lambda i, j: (0, j)),
    ]
    out_specs = [pl.BlockSpec((tm, tn), lambda i, j: (i, j))]
    out_shape = [jax.ShapeDtypeStruct((m, n), BF16)]
    args = [x2d, norm_w.reshape(1, k), w_bf16]
    if extra:
        in_specs.append(pl.BlockSpec((k, LANES), lambda i, j: (0, 0)))
        out_specs.append(pl.BlockSpec((tm, LANES), lambda i, j: (i, 0)))
        out_shape.append(jax.ShapeDtypeStruct((m, LANES), F32))
        args.append(w_f32_cols)
    res = pl.pallas_call(
        functools.partial(_norm_matmul_body, with_f32_cols=extra),
        grid=(m // tm, n // tn),
        in_specs=in_specs,
        out_specs=out_specs,
        out_shape=out_shape,
        scratch_shapes=[pltpu.VMEM((tm, k), BF16)],
        compiler_params=_params("parallel", "arbitrary"),
        name="norm_matmul",
    )(*args)
    return res if extra else res[0]


def _out_proj_body(x_ref, ya_ref, yb_ref, wa_ref, wb_ref, o_ref):
    o_ref[...] = x_ref[...] + _dot(ya_ref[...], wa_ref[...]) + _dot(yb_ref[...], wb_ref[...])


def out_proj(x2d, ya, yb, wa_bf16, wb_bf16):
    m, n = x2d.shape
    ka, kb = ya.shape[1], yb.shape[1]
    tm = _tile(m, 512)
    return pl.pallas_call(
        _out_proj_body,
        grid=(m // tm,),
        in_specs=[
            pl.BlockSpec((tm, n), lambda i: (i, 0)),
            pl.BlockSpec((tm, ka), lambda i: (i, 0)),
            pl.BlockSpec((tm, kb), lambda i: (i, 0)),
            pl.BlockSpec((ka, n), lambda i: (0, 0)),
            pl.BlockSpec((kb, n), lambda i: (0, 0)),
        ],
        out_specs=pl.BlockSpec((tm, n), lambda i: (i, 0)),
        out_shape=jax.ShapeDtypeStruct((m, n), F32),
        compiler_params=_params("parallel"),
        name="out_proj",
    )(x2d, ya, yb, wa_bf16, wb_bf16)


def _head_rmsnorm(y, gain_row, scale):
    outs = []
    for h in range(y.shape[1] // LANES):
        seg = y[:, h * LANES:(h + 1) * LANES]
        ms = jnp.mean(seg * seg, axis=-1, keepdims=True)
        outs.append(seg * lax.rsqrt(ms + EPS) * (gain_row * scale))
    return jnp.concatenate(outs, axis=-1)


def _dsa_prep_body(cq_ref, ckv_ref, sm_ref, qn_ref, kvn_ref, wuq_ref, wiq_ref, wuk_ref, wuv_ref,
                   qg_ref, kg_ref, ikw_ref, ikb_ref,
                   q_ref, k_ref, v_ref, qi_ref, ke_ref, ko_ref):
    cq = cq_ref[...].astype(F32)
    cq = cq * lax.rsqrt(jnp.mean(cq * cq, axis=-1, keepdims=True) + EPS) * qn_ref[...]
    cqb = cq.astype(BF16)
    q = _dot(cqb, wuq_ref[...])
    q_ref[...] = _head_rmsnorm(q, qg_ref[...], A_HEAD_DIM ** -0.5 * LOG2E).astype(q_ref.dtype)
    qi_ref[...] = _dot(cqb, wiq_ref[...]).astype(qi_ref.dtype)

    ckv = ckv_ref[...].astype(F32)
    ckv = ckv * lax.rsqrt(jnp.mean(ckv * ckv, axis=-1, keepdims=True) + EPS) * kvn_ref[...]
    ckvb = ckv.astype(BF16)
    k = _dot(ckvb, wuk_ref[...])
    k_ref[...] = _head_rmsnorm(k, kg_ref[...], 1.0).astype(k_ref.dtype)
    vt = _dot_nt(wuv_ref[...], ckvb).astype(v_ref.dtype)
    ones = jnp.ones((BF16_ROWS, vt.shape[1]), v_ref.dtype)
    for h in range(A_HEADS):
        v_ref[h * V_ROWS:h * V_ROWS + A_HEAD_DIM, :] = vt[h * A_HEAD_DIM:(h + 1) * A_HEAD_DIM, :]
        v_ref[h * V_ROWS + A_HEAD_DIM:(h + 1) * V_ROWS, :] = ones

    kx = sm_ref[...][:, SM_KIDX:SM_KIDX + IDX_DIM]
    mu = jnp.mean(kx, axis=-1, keepdims=True)
    kc = kx - mu
    kl = kc * lax.rsqrt(jnp.mean(kc * kc, axis=-1, keepdims=True) + EPS) * ikw_ref[...] + ikb_ref[...]
    z = jnp.zeros_like(kl)
    ke_ref[...] = jnp.concatenate([kl, z], axis=-1).astype(ke_ref.dtype)
    ko_ref[...] = jnp.concatenate([z, kl], axis=-1).astype(ko_ref.dtype)


def dsa_prep(proj, small, q_norm, kv_norm, w_uq, w_iq, w_uk, w_uv, q_gain, k_gain, ik_w, ik_b):
    m = proj.shape[0]
    tm = _tile(m, 512)
    row = lambda a: a.reshape(1, -1)
    full = lambda a: pl.BlockSpec(a.shape, lambda i: (0,) * a.ndim)
    consts = [row(q_norm), row(kv_norm), w_uq.astype(BF16), w_iq.astype(BF16), w_uk.astype(BF16),
              w_uv.T.astype(BF16), row(q_gain), row(k_gain), row(ik_w), row(ik_b)]
    outs = [jax.ShapeDtypeStruct((m, A_WIDTH), BF16)] * 2 + [
        jax.ShapeDtypeStruct((A_HEADS * V_ROWS, m), BF16),
        jax.ShapeDtypeStruct((m, IDX_HEADS * IDX_DIM), BF16),
        jax.ShapeDtypeStruct((m, LANES), BF16), jax.ShapeDtypeStruct((m, LANES), BF16)]
    out_specs = [pl.BlockSpec((tm, s.shape[1]), lambda i: (i, 0)) for s in outs]
    out_specs[2] = pl.BlockSpec((A_HEADS * V_ROWS, tm), lambda i: (0, i))
    return pl.pallas_call(
        _dsa_prep_body,
        grid=(m // tm,),
        in_specs=[
            pl.BlockSpec((tm, A_Q_LORA), lambda i: (i, COL_C_Q // A_Q_LORA)),
            pl.BlockSpec((tm, A_KV_LORA), lambda i: (i, COL_C_KV // A_KV_LORA)),
            pl.BlockSpec((tm, LANES), lambda i: (i, 0)),
        ] + [full(c) for c in consts],
        out_specs=out_specs,
        out_shape=outs,
        compiler_params=_params("parallel"),
        name="dsa_prep",
    )(proj, proj, small, *consts)


def _t5_bucket(dist):
    max_exact = REL_BUCKETS // 2
    large = max_exact + (jnp.log(jnp.maximum(dist, 1).astype(F32) / max_exact)
                         / math.log(REL_MAX_DIST / max_exact) * (REL_BUCKETS - max_exact)).astype(jnp.int32)
    large = jnp.minimum(large, REL_BUCKETS - 1)
    return jnp.where(dist < max_exact, dist, large)


def _bias_tiles_body(rb_ref, o_ref, *, t):
    h = pl.program_id(0)
    off = pl.program_id(1) * t
    dist = off + lax.broadcasted_iota(jnp.int32, (t, t), 1) - lax.broadcasted_iota(jnp.int32, (t, t), 0)
    bucket = _t5_bucket(jnp.maximum(dist, 0))
    acc = jnp.zeros((t, t), F32)
    for b in range(REL_BUCKETS):
        acc = jnp.where(bucket == b, rb_ref[b, h], acc)
    o_ref[0, 0] = ((acc - rb_ref[REL_BUCKETS - 1, h]) * LOG2E).astype(o_ref.dtype)


def bias_tiles(rel_bias, t):
    return pl.pallas_call(
        functools.partial(_bias_tiles_body, t=t),
        grid=(A_HEADS, 2),
        in_specs=[pl.BlockSpec(memory_space=pltpu.SMEM)],
        out_specs=pl.BlockSpec((1, 1, t, t), lambda h, c: (c, h, 0, 0)),
        out_shape=jax.ShapeDtypeStruct((2, A_HEADS, t, t), BF16),
        compiler_params=_params("parallel", "parallel"),
        name="bias_tiles",
    )(rel_bias)


SUBLANES = 8
MAX_SELECT_STEPS = 24
SELECT_STEPS_PER_CHECK = 4


def _key_of(x):
    bits = pltpu.bitcast(x, jnp.int32)
    key = jnp.where(bits < 0, bits ^ jnp.int32(0x7FFFFFFF), bits)
    return jnp.where(bits == jnp.int32(INT_MIN), 0, key)


def _indexer_body(qi_ref, sm_ref, ke_ref, ko_ref, mask_ref, key_ref, hi_ref, low_ref, w_ref, *, tq, ck, n_sel):
    i = pl.program_id(1)
    nk = ((i + 1) * tq + ck - 1) // ck
    groups = ck // SUBLANES
    scale = (IDX_HEADS ** -0.5) * (IDX_DIM ** -0.5)
    smt = sm_ref[0].T
    for h in range(IDX_HEADS):
        w_ref[h] = jnp.broadcast_to(smt[SM_WIDX + h:SM_WIDX + h + 1, :] * scale, (SUBLANES, tq))
    kloc = lax.broadcasted_iota(jnp.int32, (ck, tq), 0)
    qpos = i * tq + lax.broadcasted_iota(jnp.int32, (ck, tq), 1)
    int_max = jnp.int32(2 ** 31 - 1)

    def score_chunk(c, carry):
        kmin, kmax = carry
        ks = pl.multiple_of(c * ck, ck)
        ke = ke_ref[0, pl.ds(ks, ck), :]
        ko = ko_ref[0, pl.ds(ks, ck), :]
        acc = jnp.zeros((ck, tq), F32)
        for j in range(IDX_HEADS // 2):
            qp = qi_ref[0, :, j * LANES:(j + 1) * LANES]
            for par, kk in ((0, ke), (1, ko)):
                s = jnp.maximum(_dot_nt(kk, qp), 0.0)
                acc = acc + s * jnp.tile(w_ref[2 * j + par], (groups, 1))
        key = _key_of(acc)
        valid = c * ck + kloc <= qpos
        masked = jnp.where(valid, key, jnp.int32(INT_MIN))
        key_ref[pl.ds(ks, ck), :] = masked
        hi_ref[pl.ds(ks, ck), :] = jnp.right_shift(masked, 16).astype(jnp.int16)
        kmin = jnp.minimum(kmin, jnp.min(jnp.where(valid, key, int_max).reshape(groups, SUBLANES, tq), axis=0))
        kmax = jnp.maximum(kmax, jnp.max(masked.reshape(groups, SUBLANES, tq), axis=0))
        return kmin, kmax

    kmin, kmax = lax.fori_loop(0, nk, score_chunk, (jnp.full((SUBLANES, tq), int_max, jnp.int32),
                                                    jnp.full((SUBLANES, tq), INT_MIN, jnp.int32)))

    rows16 = ck // BF16_ROWS
    i16_min = -(2 ** 15)

    def scan16(a_ref, init, fn):
        def body(c, acc):
            ks = pl.multiple_of(c * ck, ck)
            return fn(acc, a_ref[pl.ds(ks, ck), :].reshape(rows16, BF16_ROWS, tq))

        return lax.fori_loop(0, nk, body, init)

    def count_ge16(a_ref, p):
        p16 = jnp.broadcast_to(p.astype(jnp.int16), (BF16_ROWS, tq))

        def fn(cnt, a):
            ind = jnp.where(a >= p16[None], jnp.int16(1), jnp.int16(0))
            for g in range(rows16):
                cnt = cnt + ind[g]
            return cnt

        cnt = scan16(a_ref, jnp.zeros((BF16_ROWS, tq), jnp.int16), fn)
        return jnp.sum(cnt.astype(jnp.int32), axis=0, keepdims=True)

    def max_le16(a_ref, h):
        h16 = jnp.broadcast_to(h.astype(jnp.int16), (BF16_ROWS, tq))

        def fn(acc, a):
            v = jnp.where(a <= h16[None], a, jnp.int16(i16_min))
            for g in range(rows16):
                acc = jnp.where(v[g] > acc, v[g], acc)
            return acc

        acc = scan16(a_ref, jnp.full((BF16_ROWS, tq), i16_min, jnp.int16), fn)
        return jnp.max(acc.astype(jnp.int32), axis=0, keepdims=True)

    def rank_select16(a_ref, r, skip, lo, hi, clo, chi, floor_count):
        def closed(lo, hi, clo):
            return skip | (clo == r) | (hi == lo + 1)

        def open_rows(lo, hi, clo, chi):
            return jnp.logical_not(closed(lo, hi, clo) | (chi == r - 1))

        def cond(st):
            it, lo, hi, clo, chi = st
            n_open = jnp.max(jnp.where(open_rows(lo, hi, clo, chi), 1, 0))
            return jnp.logical_and(it < MAX_SELECT_STEPS, n_open > 0)

        def step(st):
            it, lo, hi, clo, chi = st
            for _ in range(SELECT_STEPS_PER_CHECK):
                upd = open_rows(lo, hi, clo, chi)
                p = lo + jnp.maximum(jnp.right_shift(hi - lo, 1), 1)
                cnt = count_ge16(a_ref, p)
                ge = cnt >= r
                up, dn = upd & ge, upd & jnp.logical_not(ge)
                lo, hi = jnp.where(up, p, lo), jnp.where(dn, p, hi)
                clo, chi = jnp.where(up, cnt, clo), jnp.where(dn, cnt, chi)
            return it + SELECT_STEPS_PER_CHECK, lo, hi, clo, chi

        _, lo, hi, clo, chi = lax.while_loop(cond, step, (jnp.int32(0), lo, hi, clo, chi))
        last = jnp.logical_not(closed(lo, hi, clo))
        t_last = max_le16(a_ref, hi - 1)
        c_last = jnp.where(t_last == i16_min, floor_count, count_ge16(a_ref, t_last))
        return jnp.where(last, t_last, lo), jnp.where(last, c_last, clo), chi

    n_valid = i * tq + lax.broadcasted_iota(jnp.int32, (1, tq), 1) + 1
    few = n_valid <= n_sel
    zeros = jnp.zeros((1, tq), jnp.int32)
    hi_lo0 = jnp.right_shift(jnp.min(kmin, axis=0, keepdims=True), 16)
    hi_hi0 = jnp.right_shift(jnp.max(kmax, axis=0, keepdims=True), 16) + 1
    t_hi, c_ge_hi, c_gt_hi = rank_select16(hi_ref, n_sel, few, hi_lo0, hi_hi0, n_valid, zeros, n_valid)
    exact_hi = c_ge_hi == n_sel

    t_hi16 = jnp.broadcast_to(t_hi.astype(jnp.int16), (BF16_ROWS, tq))

    def fill_low(c, carry):
        ks = pl.multiple_of(c * ck, ck)
        low = ((key_ref[pl.ds(ks, ck), :] & 0xFFFF) - 2 ** 15).astype(jnp.int16).reshape(rows16, BF16_ROWS, tq)
        same = hi_ref[pl.ds(ks, ck), :].reshape(rows16, BF16_ROWS, tq) == t_hi16[None]
        low_ref[pl.ds(ks, ck), :] = jnp.where(same, low, jnp.int16(i16_min)).reshape(ck, tq)
        return carry

    lax.fori_loop(0, nk, fill_low, 0)
    in_bucket = c_ge_hi - c_gt_hi
    t_lo, c_ge_lo, c_gt_lo = rank_select16(low_ref, n_sel - c_gt_hi, few | exact_hi,
                                           jnp.full((1, tq), i16_min, jnp.int32), jnp.full((1, tq), 2 ** 15, jnp.int32),
                                           in_bucket, zeros, in_bucket)
    lo = jnp.where(exact_hi, t_hi * 2 ** 16, t_hi * 2 ** 16 + (t_lo + 2 ** 15))
    clo = jnp.where(exact_hi, n_sel, c_gt_hi + c_ge_lo)
    chi = c_gt_hi + c_gt_lo
    thr = jnp.where(few, jnp.int32(INT_MIN + 1), lo)
    thr8 = jnp.broadcast_to(thr, (SUBLANES, tq))
    tied = jnp.logical_not(few) & (clo > n_sel)
    need = n_sel - chi
    n_keys = mask_ref.shape[1]

    def tie_cut(_):
        def count_le(j):
            j8 = jnp.broadcast_to(j, (SUBLANES, tq))

            def body(c, cnt):
                ks = pl.multiple_of(c * ck, ck)
                kk = key_ref[pl.ds(ks, ck), :].reshape(groups, SUBLANES, tq)
                idx = (c * ck + kloc).reshape(groups, SUBLANES, tq)
                hit = jnp.where(kk == thr8[None], jnp.where(idx <= j8[None], 1, 0), 0)
                return cnt + jnp.sum(hit, axis=0)

            cnt = lax.fori_loop(0, nk, body, jnp.zeros((SUBLANES, tq), jnp.int32))
            return jnp.sum(cnt, axis=0, keepdims=True)

        def bisect(_, st):
            jlo, jhi = st
            mid = jlo + jnp.right_shift(jhi - jlo, 1)
            ok = count_le(mid) >= need
            return jnp.where(ok, jlo, mid), jnp.where(ok, mid, jhi)

        _, jhi = lax.fori_loop(0, n_keys.bit_length(), bisect,
                               (jnp.full((1, tq), -1, jnp.int32), jnp.full((1, tq), n_keys - 1, jnp.int32)))
        return jnp.where(tied, jhi, int_max)

    jcut = lax.cond(jnp.max(jnp.where(tied, 1, 0)) > 0, tie_cut, lambda _: jnp.full((1, tq), int_max, jnp.int32), 0)
    jcut8 = jnp.broadcast_to(jcut, (SUBLANES, tq))

    def write_chunk(c, carry):
        ks = pl.multiple_of(c * ck, ck)
        kk = key_ref[pl.ds(ks, ck), :].reshape(groups, SUBLANES, tq)
        idx = (c * ck + kloc).reshape(groups, SUBLANES, tq)
        at_thr = jnp.where(kk == thr8[None], jnp.where(idx <= jcut8[None], 1, 0), 0)
        sel = jnp.where(kk > thr8[None], 1, at_thr)
        mask_ref[0, pl.ds(ks, ck), :] = sel.reshape(ck, tq).astype(mask_ref.dtype)
        return carry

    lax.fori_loop(0, nk, write_chunk, 0)

    def zero_chunk(c, carry):
        ks = pl.multiple_of(c * ck, ck)
        mask_ref[0, pl.ds(ks, ck), :] = jnp.zeros((ck, tq), mask_ref.dtype)
        return carry

    lax.fori_loop(nk, mask_ref.shape[1] // ck, zero_chunk, 0)


def indexer_mask(qi, small, k_even, k_odd, b, t, n_sel):
    tq = _tile(t, 512)
    ck = _tile(t, 512)
    return pl.pallas_call(
        functools.partial(_indexer_body, tq=tq, ck=ck, n_sel=n_sel),
        grid=(b, t // tq),
        in_specs=[
            pl.BlockSpec((1, tq, IDX_HEADS * IDX_DIM), lambda bb, i: (bb, i, 0)),
            pl.BlockSpec((1, tq, LANES), lambda bb, i: (bb, i, 0)),
            pl.BlockSpec((1, t, LANES), lambda bb, i: (bb, 0, 0)),
            pl.BlockSpec((1, t, LANES), lambda bb, i: (bb, 0, 0)),
        ],
        out_specs=pl.BlockSpec((1, t, tq), lambda bb, i: (bb, 0, i)),
        out_shape=jax.ShapeDtypeStruct((b, t, t), jnp.int8),
        scratch_shapes=[pltpu.VMEM((t, tq), jnp.int32), pltpu.VMEM((t, tq), jnp.int16),
                        pltpu.VMEM((t, tq), jnp.int16), pltpu.VMEM((IDX_HEADS, SUBLANES, tq), F32)],
        compiler_params=_params("parallel", "parallel"),
        name="indexer",
    )(qi.reshape(b, t, -1), small, k_even.reshape(b, t, LANES), k_odd.reshape(b, t, LANES))


ATTN_HEAD_GROUP = 4


def _attn_body(qi_ref, ki_ref, q_ref, k_ref, vt_ref, mask_ref, bias_ref, z_ref, o_ref, m_ref, acc_ref, *, tq):
    qi, ki = qi_ref[pl.program_id(1)], ki_ref[pl.program_id(1)]

    @pl.when(ki == 0)
    def _():
        m_ref[...] = jnp.full(m_ref.shape, MASK_NEG, F32)
        acc_ref[...] = jnp.zeros(acc_ref.shape, F32)

    def tile(near):
        madd = (1.0 - mask_ref[0].astype(F32)) * MASK_NEG
        off = jnp.where(ki == qi, 0, 1)
        for g in range(0, A_HEADS, ATTN_HEAD_GROUP):
            hs = range(g, g + ATTN_HEAD_GROUP)
            grp = slice(g, g + ATTN_HEAD_GROUP)
            head = lambda ref, h: ref[0, :, h * A_HEAD_DIM:(h + 1) * A_HEAD_DIM]
            k2 = jnp.stack([head(k_ref, h) for h in hs])
            q2 = jnp.stack([head(q_ref, h) for h in hs])
            s = lax.dot_general(k2, q2, (((2,), (2,)), ((0,), (0,))), preferred_element_type=F32) + madd[None]
            if near:
                s = s + bias_ref[off, grp].astype(F32)
            m_prev = m_ref[grp]
            m_new = jnp.maximum(m_prev, jnp.max(s, axis=1, keepdims=True))
            p = jnp.exp2(s - m_new).astype(BF16)
            vt2 = jnp.stack([vt_ref[h * V_ROWS:(h + 1) * V_ROWS, :] for h in hs])
            pv = lax.dot_general(vt2, p, (((2,), (1,)), ((0,), (0,))), preferred_element_type=F32)
            acc_ref[grp] = jnp.exp2(m_prev - m_new) * acc_ref[grp] + pv
            m_ref[grp] = m_new

    @pl.when(ki + 1 < qi)
    def _():
        tile(False)

    @pl.when(ki + 1 >= qi)
    def _():
        tile(True)

    @pl.when(ki == qi)
    def _():
        outs = []
        for h in range(A_HEADS):
            a = acc_ref[h]
            outs.append((a[:A_HEAD_DIM] / a[A_HEAD_DIM:A_HEAD_DIM + 1]).T)
        o_ref[0] = (jnp.concatenate(outs, axis=-1) * _silu(z_ref[0].astype(F32))).astype(o_ref.dtype)


def attention(q, k, vt, mask, bias, proj3d, b, t, tq):
    nq = t // tq
    pairs = [(i, j) for i in range(nq) for j in range(i + 1)]
    qi_tab = jnp.asarray([p[0] for p in pairs], jnp.int32)
    ki_tab = jnp.asarray([p[1] for p in pairs], jnp.int32)
    grid_spec = pltpu.PrefetchScalarGridSpec(
        num_scalar_prefetch=2,
        grid=(b, len(pairs)),
        in_specs=[
            pl.BlockSpec((1, tq, A_WIDTH), lambda bb, p, qi, ki: (bb, qi[p], 0)),
            pl.BlockSpec((1, tq, A_WIDTH), lambda bb, p, qi, ki: (bb, ki[p], 0)),
            pl.BlockSpec((A_HEADS * V_ROWS, tq), lambda bb, p, qi, ki: (0, bb * nq + ki[p])),
            pl.BlockSpec((1, tq, tq), lambda bb, p, qi, ki: (bb, ki[p], qi[p])),
            pl.BlockSpec(bias.shape, lambda bb, p, qi, ki: (0, 0, 0, 0)),
            pl.BlockSpec((1, tq, A_WIDTH), lambda bb, p, qi, ki: (bb, qi[p], COL_Z_A // A_WIDTH)),
        ],
        out_specs=pl.BlockSpec((1, tq, A_WIDTH), lambda bb, p, qi, ki: (bb, qi[p], 0)),
        scratch_shapes=[pltpu.VMEM((A_HEADS, 1, tq), F32), pltpu.VMEM((A_HEADS, V_ROWS, tq), F32)],
    )
    return pl.pallas_call(
        functools.partial(_attn_body, tq=tq),
        grid_spec=grid_spec,
        out_shape=jax.ShapeDtypeStruct((b, t, A_WIDTH), BF16),
        compiler_params=_params("parallel", "arbitrary"),
        name="dsa_attention",
    )(qi_tab, ki_tab, q.reshape(b, t, -1), k.reshape(b, t, -1), vt, mask, bias, proj3d)


def _gdn_prep_body(x_ref, halo_ref, sm_ref, cw_ref, alog_ref, dtb_ref,
                   q_ref, k_ref, v_ref, gcb_ref, bb_ref, grow_ref, xs_ref, *, tm):
    i = pl.program_id(1)
    hal = halo_ref[0].astype(F32)
    xs_ref[0:BF16_ROWS, :] = jnp.where(i > 0, hal, jnp.zeros_like(hal))
    xs_ref[BF16_ROWS:, :] = x_ref[0].astype(F32)
    y = jnp.zeros((tm, 3 * B_WIDTH), F32)
    for j in range(B_CONV):
        y = y + cw_ref[j:j + 1, :] * xs_ref[pl.ds(BF16_ROWS - (B_CONV - 1) + j, tm), :]
    y = _silu(y)
    for h in range(B_HEADS):
        sl = slice(h * LANES, (h + 1) * LANES)
        qh = y[:, sl]
        q_ref[0, :, sl] = qh * lax.rsqrt(jnp.sum(qh * qh, axis=-1, keepdims=True) + EPS) * (B_HEAD_DIM ** -0.5)
        kh = y[:, B_WIDTH + h * LANES:B_WIDTH + (h + 1) * LANES]
        k_ref[0, :, sl] = kh * lax.rsqrt(jnp.sum(kh * kh, axis=-1, keepdims=True) + EPS)
    v_ref[0] = y[:, 2 * B_WIDTH:]

    sm = sm_ref[0]
    xg = sm + dtb_ref[...]
    softplus = jnp.maximum(xg, 0.0) + jnp.log(1.0 + jnp.exp(-jnp.abs(xg)))
    g = -jnp.exp(alog_ref[...]) * softplus
    r = lax.broadcasted_iota(jnp.int32, (tm, tm), 0)
    c = lax.broadcasted_iota(jnp.int32, (tm, tm), 1)
    sh = int(math.log2(GDN_CHUNK))
    same_chunk = jnp.right_shift(r, sh) == jnp.right_shift(c, sh)
    tri = jnp.where(jnp.logical_and(same_chunk, c <= r), 1.0, 0.0).astype(F32)
    gc = lax.dot_general(tri, g, (((1,), (0,)), ((), ())), precision=HIGHEST, preferred_element_type=F32)
    beta = _sigmoid(sm)
    for h in range(B_HEADS):
        sl = slice(h * LANES, (h + 1) * LANES)
        gcb_ref[0, :, sl] = jnp.broadcast_to(gc[:, SM_ALPHA + h:SM_ALPHA + h + 1], (tm, LANES))
        bb_ref[0, :, sl] = jnp.broadcast_to(beta[:, SM_BETA + h:SM_BETA + h + 1], (tm, LANES))
    gct = gc.T
    for cc in range(tm // GDN_CHUNK):
        grow_ref[0, cc] = gct[SM_ALPHA:SM_ALPHA + B_HEADS, cc * GDN_CHUNK:(cc + 1) * GDN_CHUNK]


def gdn_prep(proj3d, small3d, conv_w, a_log, dt_bias, b, t):
    tm = _tile(t, 256)
    pad_row = lambda v: jnp.zeros((1, LANES), F32).at[0, SM_ALPHA:SM_ALPHA + B_HEADS].set(v)
    nc = tm // GDN_CHUNK
    act = jax.ShapeDtypeStruct((b, t, B_WIDTH), F32)
    full = lambda a: pl.BlockSpec(a.shape, lambda bb, i: (0,) * a.ndim)
    consts = [conv_w, pad_row(a_log), pad_row(dt_bias)]
    return pl.pallas_call(
        functools.partial(_gdn_prep_body, tm=tm),
        grid=(b, t // tm),
        in_specs=[
            pl.BlockSpec((1, tm, 3 * B_WIDTH), lambda bb, i: (bb, i, COL_QKV_B // (3 * B_WIDTH))),
            pl.BlockSpec((1, BF16_ROWS, 3 * B_WIDTH),
                         lambda bb, i: (bb, jnp.maximum(i * (tm // BF16_ROWS) - 1, 0), 0)),
            pl.BlockSpec((1, tm, LANES), lambda bb, i: (bb, i, 0)),
        ] + [full(c) for c in consts],
        out_specs=[pl.BlockSpec((1, tm, B_WIDTH), lambda bb, i: (bb, i, 0))] * 5
        + [pl.BlockSpec((1, nc, B_HEADS, GDN_CHUNK), lambda bb, i: (bb, i, 0, 0))],
        out_shape=[act] * 5 + [jax.ShapeDtypeStruct((b, t // GDN_CHUNK, B_HEADS, GDN_CHUNK), F32)],
        scratch_shapes=[pltpu.VMEM((tm + BF16_ROWS, 3 * B_WIDTH), F32)],
        compiler_params=_params("parallel", "parallel"),
        name="gdn_prep",
    )(proj3d, proj3d, small3d, *consts)


def _gdn_body(q_ref, k_ref, v_ref, gcb_ref, bb_ref, grow_ref, z_ref, on_ref, o_ref, s_ref, *, nc):
    @pl.when(pl.program_id(1) == 0)
    def _():
        s_ref[...] = jnp.zeros(s_ref.shape, F32)

    cs, nh = GDN_CHUNK, B_HEADS
    nb = nc * nh
    ri = lax.broadcasted_iota(jnp.int32, (nb, cs, cs), 1)
    ci = lax.broadcasted_iota(jnp.int32, (nb, cs, cs), 2)
    lower = ci <= ri
    strict = ci < ri
    eye = jnp.where(ci == ri, 1.0, 0.0).astype(F32)
    bf = lambda a: a.astype(BF16)
    bmm = lambda a, b: lax.dot_general(a, b, (((2,), (1,)), ((0,), (0,))), preferred_element_type=F32)
    bmm_nt = lambda a, b: lax.dot_general(a, b, (((2,), (2,)), ((0,), (0,))), preferred_element_type=F32)
    bmm_tn = lambda a, b: lax.dot_general(a, b, (((1,), (1,)), ((0,), (0,))), preferred_element_type=F32)

    def stack(ref):
        return jnp.stack([ref[0, c * cs:(c + 1) * cs, h * LANES:(h + 1) * LANES]
                          for c in range(nc) for h in range(nh)])

    q, k, v = stack(q_ref), stack(k_ref), stack(v_ref)
    gcb = stack(gcb_ref)
    beta = stack(bb_ref)
    grow = jnp.stack([grow_ref[0, c, h:h + 1, :] for c in range(nc) for h in range(nh)])
    diff = gcb[:, :, :cs] - grow
    decay = jnp.where(lower, jnp.exp(jnp.where(lower, diff, 0.0)), 0.0)
    eg = jnp.exp(gcb)
    glast = gcb[:, cs - 1:cs, :]
    kb = k * beta
    lmat = jnp.where(strict, bmm_nt(bf(kb), bf(k)) * decay, 0.0)
    n = -lmat
    tinv = eye + n
    for _ in range(int(math.log2(cs)) - 1):
        n = bmm(bf(n), bf(n))
        tinv = tinv + bmm(bf(tinv), bf(n))
    uw = bmm(bf(tinv), bf(jnp.concatenate([v * beta, kb * eg], axis=-1)))
    attn = bf(jnp.where(lower, bmm_nt(bf(q), bf(k)) * decay, 0.0))
    qg = bf(q * eg)
    kdec = bf(k * jnp.exp(glast - gcb))
    egl = jnp.exp(glast)

    s = s_ref[...]
    for c in range(nc):
        sl = slice(c * nh, (c + 1) * nh)
        sb = bf(s)
        v_new = uw[sl, :, :LANES] - bmm(bf(uw[sl, :, LANES:]), sb)
        o = bmm(qg[sl], sb) + bmm(attn[sl], bf(v_new))
        s = s * egl[sl] + bmm_tn(kdec[sl], bf(v_new))
        o = o * lax.rsqrt(jnp.mean(o * o, axis=-1, keepdims=True) + EPS) * on_ref[...]
        for h in range(nh):
            rows, cols = slice(c * cs, (c + 1) * cs), slice(h * LANES, (h + 1) * LANES)
            o_ref[0, rows, cols] = (o[h] * _silu(z_ref[0, rows, cols].astype(F32))).astype(o_ref.dtype)
    s_ref[...] = s


def gdn_scan(qh, kh, v, gcb, bb, grow, proj3d, o_norm, b, t):
    tt = _tile(t, 256)
    nc = tt // GDN_CHUNK
    blk = pl.BlockSpec((1, tt, B_WIDTH), lambda bb_, i: (bb_, i, 0))
    return pl.pallas_call(
        functools.partial(_gdn_body, nc=nc),
        grid=(b, t // tt),
        in_specs=[blk] * 5 + [
            pl.BlockSpec((1, nc, B_HEADS, GDN_CHUNK), lambda bb_, i: (bb_, i, 0, 0)),
            pl.BlockSpec((1, tt, B_WIDTH), lambda bb_, i: (bb_, i, COL_Z_B // B_WIDTH)),
            pl.BlockSpec((1, LANES), lambda bb_, i: (0, 0)),
        ],
        out_specs=blk,
        out_shape=jax.ShapeDtypeStruct((b, t, B_WIDTH), BF16),
        scratch_shapes=[pltpu.VMEM((B_HEADS, B_HEAD_DIM, B_HEAD_DIM), F32)],
        compiler_params=_params("parallel", "arbitrary"),
        name="gdn_scan",
    )(qh, kh, v, gcb, bb, grow, proj3d, o_norm.reshape(1, LANES))


C_HALO = 32
D_HALO = BF16_ROWS


def _cd_mix_body(a_ref, g_ref, ah_ref, gh_ref, zc_ref, bg_ref, cg_ref, ud_ref, cgh_ref, udh_ref, zd_ref,
                 dww_ref, dwb_ref, lnw_ref, lnb_ref, dcw_ref, yc_ref, yd_ref, us_ref, ds_ref, *, tm):
    i = pl.program_id(1)
    first = i == 0
    f32 = lambda r: r[0].astype(F32)
    uh = f32(ah_ref) * _sigmoid(f32(gh_ref))
    us_ref[0, 0:C_HALO, :] = jnp.where(first, jnp.zeros_like(uh), uh)
    us_ref[0, C_HALO:, :] = f32(a_ref) * _sigmoid(f32(g_ref))
    span = tm + C_HALO - SUBLANES
    for r in range(1, SUBLANES):
        us_ref[r, 0:span, :] = us_ref[0, pl.ds(r, span), :]
    u = jnp.zeros((tm, C_WIDTH), F32)
    for j in range(C_CONV):
        off = C_HALO - (C_CONV - 1) + j
        r, base = off % SUBLANES, off - off % SUBLANES
        u = u + dww_ref[j:j + 1, :] * us_ref[r, base:base + tm, :]
    u = u + dwb_ref[...]
    mu = jnp.mean(u, axis=-1, keepdims=True)
    uc = u - mu
    u = uc * lax.rsqrt(jnp.mean(uc * uc, axis=-1, keepdims=True) + EPS) * lnw_ref[...] + lnb_ref[...]
    yc_ref[0] = (_silu(u) * _silu(f32(zc_ref))).astype(yc_ref.dtype)

    dh = f32(cgh_ref) * f32(udh_ref)
    ds_ref[0:D_HALO, :] = jnp.where(first, jnp.zeros_like(dh), dh)
    ds_ref[D_HALO:, :] = f32(cg_ref) * f32(ud_ref)
    d = jnp.zeros((tm, D_WIDTH), F32)
    for j in range(D_CONV):
        d = d + dcw_ref[j:j + 1, :] * ds_ref[pl.ds(D_HALO - (D_CONV - 1) + j, tm), :]
    yd_ref[0] = (f32(bg_ref) * d * _silu(f32(zd_ref))).astype(yd_ref.dtype)


def cd_mix(proj3d, dw_w, dw_b, ln_w, ln_b, d_conv_w, b, t):
    tm = _tile(t, 256)
    w = C_WIDTH
    col = lambda n: pl.BlockSpec((1, tm, w), lambda bb, i, n=n: (bb, i, n))
    halo = lambda n, rows: pl.BlockSpec(
        (1, rows, w), lambda bb, i, n=n, rows=rows: (bb, jnp.maximum(i * (tm // rows) - 1, 0), n))
    row = lambda a: a.reshape(1, -1)
    full = lambda a: pl.BlockSpec(a.shape, lambda bb, i: (0,) * a.ndim)
    consts = [dw_w, row(dw_b), row(ln_w), row(ln_b), d_conv_w]
    out = jax.ShapeDtypeStruct((b, t, w), BF16)
    return pl.pallas_call(
        functools.partial(_cd_mix_body, tm=tm),
        grid=(b, t // tm),
        in_specs=[col(0), col(1), halo(0, C_HALO), halo(1, C_HALO), col(2), col(3), col(4), col(5),
                  halo(4, D_HALO), halo(5, D_HALO), col(6)] + [full(c) for c in consts],
        out_specs=[pl.BlockSpec((1, tm, w), lambda bb, i: (bb, i, 0))] * 2,
        out_shape=[out, out],
        scratch_shapes=[pltpu.VMEM((SUBLANES, tm + C_HALO, w), F32), pltpu.VMEM((tm + D_HALO, w), F32)],
        compiler_params=_params("parallel", "parallel"),
        name="cd_mix",
    )(*([proj3d] * 11), *consts)


def _reorder_ab_w_in(w):
    offs = [0]
    for s in AB_SPLITS:
        offs.append(offs[-1] + s)
    part = lambda n: w[:, offs[n]:offs[n + 1]]
    c_q, c_kv, k_idx, w_idx, z_a, qkv_b, beta_b, alpha_b, z_b = (part(n) for n in range(9))
    small = jnp.concatenate([k_idx, w_idx, beta_b, alpha_b], axis=1)
    small = jnp.pad(small, ((0, 0), (0, LANES - small.shape[1])))
    out = jnp.concatenate([qkv_b, z_a, z_b, c_q, c_kv], axis=1)
    return jnp.pad(out, ((0, 0), (0, AB_IN_PAD - out.shape[1]))).astype(BF16), small.astype(BF16)


def _ab_layer(x2d, b, t, norm_w, rel_bias, w_in, q_norm, w_uq, w_iq, kv_norm, w_uk, w_uv, q_gain, k_gain,
              ik_w, ik_b, conv_w, a_log, dt_bias, o_norm, w_out):
    proj, small = norm_matmul(x2d, norm_w, *_reorder_ab_w_in(w_in))
    proj3d, small3d = proj.reshape(b, t, AB_IN_PAD), small.reshape(b, t, LANES)
    q, k, v, qi, k_even, k_odd = dsa_prep(proj, small, q_norm, kv_norm, w_uq, w_iq, w_uk, w_uv, q_gain, k_gain,
                                          ik_w, ik_b)
    n_sel = min(TOPK_MAX, t // 4)
    mask = indexer_mask(qi, small3d, k_even, k_odd, b, t, n_sel)
    tq = _tile(t, 512)
    y_a = attention(q, k, v, mask, bias_tiles(rel_bias, tq), proj3d, b, t, tq)
    qh, kh, vv, gcb, bb, grow = gdn_prep(proj3d, small3d, conv_w, a_log, dt_bias, b, t)
    y_b = gdn_scan(qh, kh, vv, gcb, bb, grow, proj3d, o_norm, b, t)
    wo = w_out.astype(BF16)
    return out_proj(x2d, y_a.reshape(b * t, -1), y_b.reshape(b * t, -1), wo[:A_WIDTH], wo[A_WIDTH:])


def _cd_layer(x2d, b, t, norm_w, w_in, dw_w, dw_b, ln_w, ln_b, d_conv_w, w_out):
    proj = norm_matmul(x2d, norm_w, w_in.astype(BF16))
    y_c, y_d = cd_mix(proj.reshape(b, t, -1), dw_w, dw_b, ln_w, ln_b, d_conv_w, b, t)
    wo = w_out.astype(BF16)
    return out_proj(x2d, y_c.reshape(b * t, -1), y_d.reshape(b * t, -1), wo[:C_WIDTH], wo[C_WIDTH:])


def kernel(x, norm_w, rel_bias, ab_w_in, a_q_norm, a_w_uq, a_w_iq, a_kv_norm, a_w_uk, a_w_uv, a_q_gain,
           a_k_gain, a_ik_norm_w, a_ik_norm_b, b_conv_w, b_a_log, b_dt_bias, b_o_norm, ab_w_out, cd_w_in,
           c_dw_w, c_dw_b, c_ln_w, c_ln_b, d_conv_w, cd_w_out):
    b, t, d = x.shape
    depth = norm_w.shape[0]
    x2d = x.reshape(b * t, d)
    for i in range(depth):
        j = i // 2
        if i % 2 == 0:
            x2d = _ab_layer(x2d, b, t, norm_w[i], rel_bias, ab_w_in[j], a_q_norm[j], a_w_uq[j], a_w_iq[j],
                            a_kv_norm[j], a_w_uk[j], a_w_uv[j], a_q_gain[j], a_k_gain[j], a_ik_norm_w[j],
                            a_ik_norm_b[j], b_conv_w[j], b_a_log[j], b_dt_bias[j], b_o_norm[j], ab_w_out[j])
        else:
            x2d = _cd_layer(x2d, b, t, norm_w[i], cd_w_in[j], c_dw_w[j], c_dw_b[j], c_ln_w[j], c_ln_b[j],
                            d_conv_w[j], cd_w_out[j])
    return x2d.reshape(b, t, d)
```

```python
import functools
import math

import jax
import jax.numpy as jnp
from jax import lax
from jax.experimental import pallas as pl
from jax.experimental.pallas import tpu as pltpu

A_HEADS = 8
A_HEAD_DIM = 128
A_WIDTH = A_HEADS * A_HEAD_DIM
A_Q_LORA = 512
A_KV_LORA = 256
IDX_HEADS = 16
IDX_DIM = 64
TOPK_MAX = 256
REL_BUCKETS = 32
REL_MAX_DIST = 128
B_HEADS = 8
B_HEAD_DIM = 128
B_WIDTH = B_HEADS * B_HEAD_DIM
B_CONV = 4
GDN_CHUNK = 64
C_WIDTH = 1024
C_CONV = 31
D_WIDTH = 1024
D_CONV = 3
EPS = 1e-6

AB_SPLITS = (A_Q_LORA, A_KV_LORA, IDX_DIM, IDX_HEADS, A_WIDTH, 3 * B_WIDTH, B_HEADS, B_HEADS, B_WIDTH)
AB_IN_PAD = 6144
SM_KIDX = 0
SM_WIDX = 64
SM_BETA = 80
SM_ALPHA = 88
COL_QKV_B = 0
COL_Z_A = 3072
COL_Z_B = 4096
COL_C_Q = 5120
COL_C_KV = 5632

LANES = 128
MASK_NEG = -1e30
INT_MIN = -(2 ** 31)
LOG2E = math.log2(math.e)
BF16_ROWS = 16
V_ROWS = A_HEAD_DIM + BF16_ROWS
VMEM_LIMIT = 56 * 1024 * 1024

F32 = jnp.float32
BF16 = jnp.bfloat16
HIGHEST = lax.Precision.HIGHEST


def _tile(n, pref):
    t = min(n, pref)
    assert n % t == 0, (n, t)
    return t


def _params(*sem):
    return pltpu.CompilerParams(dimension_semantics=sem, vmem_limit_bytes=VMEM_LIMIT)


def _dot(a, b):
    return jnp.dot(a, b, preferred_element_type=F32)


def _dot_nt(a, b):
    return lax.dot_general(a, b, (((1,), (1,)), ((), ())), preferred_element_type=F32)


def _silu(x):
    return x * (1.0 / (1.0 + jnp.exp(-x)))


def _sigmoid(x):
    return 1.0 / (1.0 + jnp.exp(-x))


def _norm_matmul_body(*refs, with_f32_cols):
    if with_f32_cols:
        x_ref, nw_ref, w_ref, ws_ref, o_ref, os_ref, h_ref = refs
    else:
        x_ref, nw_ref, w_ref, o_ref, h_ref = refs

    @pl.when(pl.program_id(1) == 0)
    def _():
        x = x_ref[...]
        ms = jnp.mean(x * x, axis=-1, keepdims=True)
        h_ref[...] = (x * lax.rsqrt(ms + EPS) * nw_ref[...]).astype(h_ref.dtype)
        if with_f32_cols:
            os_ref[...] = _dot(h_ref[...], ws_ref[...])

    o_ref[...] = _dot(h_ref[...], w_ref[...]).astype(o_ref.dtype)


def norm_matmul(x2d, norm_w, w_bf16, w_f32_cols=None):
    m, k = x2d.shape
    n = w_bf16.shape[1]
    tm, tn = _tile(m, 1024), _tile(n, 1024)
    extra = w_f32_cols is not None
    in_specs = [
        pl.BlockSpec((tm, k), lambda i, j: (i, 0)),
        pl.BlockSpec((1, k), lambda i, j: (0, 0)),
        pl.BlockSpec((k, tn), lambda i, j: (0, j)),
    ]
    out_specs = [pl.BlockSpec((tm, tn), lambda i, j: (i, j))]
    out_shape = [jax.ShapeDtypeStruct((m, n), BF16)]
    args = [x2d, norm_w.reshape(1, k), w_bf16]
    if extra:
        in_specs.append(pl.BlockSpec((k, LANES), lambda i, j: (0, 0)))
        out_specs.append(pl.BlockSpec((tm, LANES), lambda i, j: (i, 0)))
        out_shape.append(jax.ShapeDtypeStruct((m, LANES), F32))
        args.append(w_f32_cols)
    res = pl.pallas_call(
        functools.partial(_norm_matmul_body, with_f32_cols=extra),
        grid=(m // tm, n // tn),
        in_specs=in_specs,
        out_specs=out_specs,
        out_shape=out_shape,
        scratch_shapes=[pltpu.VMEM((tm, k), BF16)],
        compiler_params=_params("parallel", "arbitrary"),
        name="norm_matmul",
    )(*args)
    return res if extra else res[0]


def _out_proj_body(x_ref, ya_ref, yb_ref, wa_ref, wb_ref, o_ref):
    o_ref[...] = x_ref[...] + _dot(ya_ref[...], wa_ref[...]) + _dot(yb_ref[...], wb_ref[...])


def out_proj(x2d, ya, yb, wa_bf16, wb_bf16):
    m, n = x2d.shape
    ka, kb = ya.shape[1], yb.shape[1]
    tm = _tile(m, 512)
    return pl.pallas_call(
        _out_proj_body,
        grid=(m // tm,),
        in_specs=[
            pl.BlockSpec((tm, n), lambda i: (i, 0)),
            pl.BlockSpec((tm, ka), lambda i: (i, 0)),
            pl.BlockSpec((tm, kb), lambda i: (i, 0)),
            pl.BlockSpec((ka, n), lambda i: (0, 0)),
            pl.BlockSpec((kb, n), lambda i: (0, 0)),
        ],
        out_specs=pl.BlockSpec((tm, n), lambda i: (i, 0)),
        out_shape=jax.ShapeDtypeStruct((m, n), F32),
        compiler_params=_params("parallel"),
        name="out_proj",
    )(x2d, ya, yb, wa_bf16, wb_bf16)


def _head_rmsnorm(y, gain_row, scale):
    outs = []
    for h in range(y.shape[1] // LANES):
        seg = y[:, h * LANES:(h + 1) * LANES]
        ms = jnp.mean(seg * seg, axis=-1, keepdims=True)
        outs.append(seg * lax.rsqrt(ms + EPS) * (gain_row * scale))
    return jnp.concatenate(outs, axis=-1)


def _dsa_prep_body(cq_ref, ckv_ref, sm_ref, qn_ref, kvn_ref, wuq_ref, wiq_ref, wuk_ref, wuv_ref,
                   qg_ref, kg_ref, ikw_ref, ikb_ref,
                   q_ref, k_ref, v_ref, qi_ref, ke_ref, ko_ref):
    cq = cq_ref[...].astype(F32)
    cq = cq * lax.rsqrt(jnp.mean(cq * cq, axis=-1, keepdims=True) + EPS) * qn_ref[...]
    cqb = cq.astype(BF16)
    q = _dot(cqb, wuq_ref[...])
    q_ref[...] = _head_rmsnorm(q, qg_ref[...], A_HEAD_DIM ** -0.5 * LOG2E).astype(q_ref.dtype)
    qi_ref[...] = _dot(cqb, wiq_ref[...]).astype(qi_ref.dtype)

    ckv = ckv_ref[...].astype(F32)
    ckv = ckv * lax.rsqrt(jnp.mean(ckv * ckv, axis=-1, keepdims=True) + EPS) * kvn_ref[...]
    ckvb = ckv.astype(BF16)
    k = _dot(ckvb, wuk_ref[...])
    k_ref[...] = _head_rmsnorm(k, kg_ref[...], 1.0).astype(k_ref.dtype)
    vt = _dot_nt(wuv_ref[...], ckvb).astype(v_ref.dtype)
    ones = jnp.ones((BF16_ROWS, vt.shape[1]), v_ref.dtype)
    for h in range(A_HEADS):
        v_ref[h * V_ROWS:h * V_ROWS + A_HEAD_DIM, :] = vt[h * A_HEAD_DIM:(h + 1) * A_HEAD_DIM, :]
        v_ref[h * V_ROWS + A_HEAD_DIM:(h + 1) * V_ROWS, :] = ones

    kx = sm_ref[...][:, SM_KIDX:SM_KIDX + IDX_DIM]
    mu = jnp.mean(kx, axis=-1, keepdims=True)
    kc = kx - mu
    kl = kc * lax.rsqrt(jnp.mean(kc * kc, axis=-1, keepdims=True) + EPS) * ikw_ref[...] + ikb_ref[...]
    z = jnp.zeros_like(kl)
    ke_ref[...] = jnp.concatenate([kl, z], axis=-1).astype(ke_ref.dtype)
    ko_ref[...] = jnp.concatenate([z, kl], axis=-1).astype(ko_ref.dtype)


def dsa_prep(proj, small, q_norm, kv_norm, w_uq, w_iq, w_uk, w_uv, q_gain, k_gain, ik_w, ik_b):
    m = proj.shape[0]
    tm = _tile(m, 512)
    row = lambda a: a.reshape(1, -1)
    full = lambda a: pl.BlockSpec(a.shape, lambda i: (0,) * a.ndim)
    consts = [row(q_norm), row(kv_norm), w_uq.astype(BF16), w_iq.astype(BF16), w_uk.astype(BF16),
              w_uv.T.astype(BF16), row(q_gain), row(k_gain), row(ik_w), row(ik_b)]
    outs = [jax.ShapeDtypeStruct((m, A_WIDTH), BF16)] * 2 + [
        jax.ShapeDtypeStruct((A_HEADS * V_ROWS, m), BF16),
        jax.ShapeDtypeStruct((m, IDX_HEADS * IDX_DIM), BF16),
        jax.ShapeDtypeStruct((m, LANES), BF16), jax.ShapeDtypeStruct((m, LANES), BF16)]
    out_specs = [pl.BlockSpec((tm, s.shape[1]), lambda i: (i, 0)) for s in outs]
    out_specs[2] = pl.BlockSpec((A_HEADS * V_ROWS, tm), lambda i: (0, i))
    return pl.pallas_call(
        _dsa_prep_body,
        grid=(m // tm,),
        in_specs=[
            pl.BlockSpec((tm, A_Q_LORA), lambda i: (i, COL_C_Q // A_Q_LORA)),
            pl.BlockSpec((tm, A_KV_LORA), lambda i: (i, COL_C_KV // A_KV_LORA)),
            pl.BlockSpec((tm, LANES), lambda i: (i, 0)),
        ] + [full(c) for c in consts],
        out_specs=out_specs,
        out_shape=outs,
        compiler_params=_params("parallel"),
        name="dsa_prep",
    )(proj, proj, small, *consts)


def _t5_bucket(dist):
    max_exact = REL_BUCKETS // 2
    large = max_exact + (jnp.log(jnp.maximum(dist, 1).astype(F32) / max_exact)
                         / math.log(REL_MAX_DIST / max_exact) * (REL_BUCKETS - max_exact)).astype(jnp.int32)
    large = jnp.minimum(large, REL_BUCKETS - 1)
    return jnp.where(dist < max_exact, dist, large)


def _bias_tiles_body(rb_ref, o_ref, *, t):
    h = pl.program_id(0)
    off = pl.program_id(1) * t
    dist = off + lax.broadcasted_iota(jnp.int32, (t, t), 1) - lax.broadcasted_iota(jnp.int32, (t, t), 0)
    bucket = _t5_bucket(jnp.maximum(dist, 0))
    acc = jnp.zeros((t, t), F32)
    for b in range(REL_BUCKETS):
        acc = jnp.where(bucket == b, rb_ref[b, h], acc)
    o_ref[0, 0] = ((acc - rb_ref[REL_BUCKETS - 1, h]) * LOG2E).astype(o_ref.dtype)


def bias_tiles(rel_bias, t):
    return pl.pallas_call(
        functools.partial(_bias_tiles_body, t=t),
        grid=(A_HEADS, 2),
        in_specs=[pl.BlockSpec(memory_space=pltpu.SMEM)],
        out_specs=pl.BlockSpec((1, 1, t, t), lambda h, c: (c, h, 0, 0)),
        out_shape=jax.ShapeDtypeStruct((2, A_HEADS, t, t), BF16),
        compiler_params=_params("parallel", "parallel"),
        name="bias_tiles",
    )(rel_bias)


SUBLANES = 8
MAX_SELECT_STEPS = 24
SELECT_STEPS_PER_CHECK = 4


def _key_of(x):
    bits = pltpu.bitcast(x, jnp.int32)
    key = jnp.where(bits < 0, bits ^ jnp.int32(0x7FFFFFFF), bits)
    return jnp.where(bits == jnp.int32(INT_MIN), 0, key)


def _indexer_body(qi_ref, sm_ref, ke_ref, ko_ref, mask_ref, key_ref, hi_ref, low_ref, w_ref, *, tq, ck, n_sel):
    i = pl.program_id(1)
    nk = ((i + 1) * tq + ck - 1) // ck
    groups = ck // SUBLANES
    scale = (IDX_HEADS ** -0.5) * (IDX_DIM ** -0.5)
    smt = sm_ref[0].T
    for h in range(IDX_HEADS):
        w_ref[h] = jnp.broadcast_to(smt[SM_WIDX + h:SM_WIDX + h + 1, :] * scale, (SUBLANES, tq))
    kloc = lax.broadcasted_iota(jnp.int32, (ck, tq), 0)
    qpos = i * tq + lax.broadcasted_iota(jnp.int32, (ck, tq), 1)
    int_max = jnp.int32(2 ** 31 - 1)

    def score_chunk(c, carry):
        kmin, kmax = carry
        ks = pl.multiple_of(c * ck, ck)
        ke = ke_ref[0, pl.ds(ks, ck), :]
        ko = ko_ref[0, pl.ds(ks, ck), :]
        acc = jnp.zeros((ck, tq), F32)
        for j in range(IDX_HEADS // 2):
            qp = qi_ref[0, :, j * LANES:(j + 1) * LANES]
            for par, kk in ((0, ke), (1, ko)):
                s = jnp.maximum(_dot_nt(kk, qp), 0.0)
                acc = acc + s * jnp.tile(w_ref[2 * j + par], (groups, 1))
        key = _key_of(acc)
        valid = c * ck + kloc <= qpos
        masked = jnp.where(valid, key, jnp.int32(INT_MIN))
        key_ref[pl.ds(ks, ck), :] = masked
        hi_ref[pl.ds(ks, ck), :] = jnp.right_shift(masked, 16).astype(jnp.int16)
        kmin = jnp.minimum(kmin, jnp.min(jnp.where(valid, key, int_max).reshape(groups, SUBLANES, tq), axis=0))
        kmax = jnp.maximum(kmax, jnp.max(masked.reshape(groups, SUBLANES, tq), axis=0))
        return kmin, kmax

    kmin, kmax = lax.fori_loop(0, nk, score_chunk, (jnp.full((SUBLANES, tq), int_max, jnp.int32),
                                                    jnp.full((SUBLANES, tq), INT_MIN, jnp.int32)))

    rows16 = ck // BF16_ROWS
    i16_min = -(2 ** 15)

    def scan16(a_ref, init, fn):
        def body(c, acc):
            ks = pl.multiple_of(c * ck, ck)
            return fn(acc, a_ref[pl.ds(ks, ck), :].reshape(rows16, BF16_ROWS, tq))

        return lax.fori_loop(0, nk, body, init)

    def count_ge16(a_ref, p):
        p16 = jnp.broadcast_to(p.astype(jnp.int16), (BF16_ROWS, tq))

        def fn(cnt, a):
            ind = jnp.where(a >= p16[None], jnp.int16(1), jnp.int16(0))
            for g in range(rows16):
                cnt = cnt + ind[g]
            return cnt

        cnt = scan16(a_ref, jnp.zeros((BF16_ROWS, tq), jnp.int16), fn)
        return jnp.sum(cnt.astype(jnp.int32), axis=0, keepdims=True)

    def max_le16(a_ref, h):
        h16 = jnp.broadcast_to(h.astype(jnp.int16), (BF16_ROWS, tq))

        def fn(acc, a):
            v = jnp.where(a <= h16[None], a, jnp.int16(i16_min))
            for g in range(rows16):
                acc = jnp.where(v[g] > acc, v[g], acc)
            return acc

        acc = scan16(a_ref, jnp.full((BF16_ROWS, tq), i16_min, jnp.int16), fn)
        return jnp.max(acc.astype(jnp.int32), axis=0, keepdims=True)

    def rank_select16(a_ref, r, skip, lo, hi, clo, chi, floor_count):
        def closed(lo, hi, clo):
            return skip | (clo == r) | (hi == lo + 1)

        def open_rows(lo, hi, clo, chi):
            return jnp.logical_not(closed(lo, hi, clo) | (chi == r - 1))

        def cond(st):
            it, lo, hi, clo, chi = st
            n_open = jnp.max(jnp.where(open_rows(lo, hi, clo, chi), 1, 0))
            return jnp.logical_and(it < MAX_SELECT_STEPS, n_open > 0)

        def step(st):
            it, lo, hi, clo, chi = st
            for _ in range(SELECT_STEPS_PER_CHECK):
                upd = open_rows(lo, hi, clo, chi)
                p = lo + jnp.maximum(jnp.right_shift(hi - lo, 1), 1)
                cnt = count_ge16(a_ref, p)
                ge = cnt >= r
                up, dn = upd & ge, upd & jnp.logical_not(ge)
                lo, hi = jnp.where(up, p, lo), jnp.where(dn, p, hi)
                clo, chi = jnp.where(up, cnt, clo), jnp.where(dn, cnt, chi)
            return it + SELECT_STEPS_PER_CHECK, lo, hi, clo, chi

        _, lo, hi, clo, chi = lax.while_loop(cond, step, (jnp.int32(0), lo, hi, clo, chi))
        last = jnp.logical_not(closed(lo, hi, clo))
        t_last = max_le16(a_ref, hi - 1)
        c_last = jnp.where(t_last == i16_min, floor_count, count_ge16(a_ref, t_last))
        return jnp.where(last, t_last, lo), jnp.where(last, c_last, clo), chi

    n_valid = i * tq + lax.broadcasted_iota(jnp.int32, (1, tq), 1) + 1
    few = n_valid <= n_sel
    zeros = jnp.zeros((1, tq), jnp.int32)
    hi_lo0 = jnp.right_shift(jnp.min(kmin, axis=0, keepdims=True), 16)
    hi_hi0 = jnp.right_shift(jnp.max(kmax, axis=0, keepdims=True), 16) + 1
    t_hi, c_ge_hi, c_gt_hi = rank_select16(hi_ref, n_sel, few, hi_lo0, hi_hi0, n_valid, zeros, n_valid)
    exact_hi = c_ge_hi == n_sel

    t_hi16 = jnp.broadcast_to(t_hi.astype(jnp.int16), (BF16_ROWS, tq))

    def fill_low(c, carry):
        ks = pl.multiple_of(c * ck, ck)
        low = ((key_ref[pl.ds(ks, ck), :] & 0xFFFF) - 2 ** 15).astype(jnp.int16).reshape(rows16, BF16_ROWS, tq)
        same = hi_ref[pl.ds(ks, ck), :].reshape(rows16, BF16_ROWS, tq) == t_hi16[None]
        low_ref[pl.ds(ks, ck), :] = jnp.where(same, low, jnp.int16(i16_min)).reshape(ck, tq)
        return carry

    lax.fori_loop(0, nk, fill_low, 0)
    in_bucket = c_ge_hi - c_gt_hi
    t_lo, c_ge_lo, c_gt_lo = rank_select16(low_ref, n_sel - c_gt_hi, few | exact_hi,
                                           jnp.full((1, tq), i16_min, jnp.int32), jnp.full((1, tq), 2 ** 15, jnp.int32),
                                           in_bucket, zeros, in_bucket)
    lo = jnp.where(exact_hi, t_hi * 2 ** 16, t_hi * 2 ** 16 + (t_lo + 2 ** 15))
    clo = jnp.where(exact_hi, n_sel, c_gt_hi + c_ge_lo)
    chi = c_gt_hi + c_gt_lo
    thr = jnp.where(few, jnp.int32(INT_MIN + 1), lo)
    thr8 = jnp.broadcast_to(thr, (SUBLANES, tq))
    tied = jnp.logical_not(few) & (clo > n_sel)
    need = n_sel - chi
    n_keys = mask_ref.shape[1]

    def tie_cut(_):
        def count_le(j):
            j8 = jnp.broadcast_to(j, (SUBLANES, tq))

            def body(c, cnt):
                ks = pl.multiple_of(c * ck, ck)
                kk = key_ref[pl.ds(ks, ck), :].reshape(groups, SUBLANES, tq)
                idx = (c * ck + kloc).reshape(groups, SUBLANES, tq)
                hit = jnp.where(kk == thr8[None], jnp.where(idx <= j8[None], 1, 0), 0)
                return cnt + jnp.sum(hit, axis=0)

            cnt = lax.fori_loop(0, nk, body, jnp.zeros((SUBLANES, tq), jnp.int32))
            return jnp.sum(cnt, axis=0, keepdims=True)

        def bisect(_, st):
            jlo, jhi = st
            mid = jlo + jnp.right_shift(jhi - jlo, 1)
            ok = count_le(mid) >= need
            return jnp.where(ok, jlo, mid), jnp.where(ok, mid, jhi)

        _, jhi = lax.fori_loop(0, n_keys.bit_length(), bisect,
                               (jnp.full((1, tq), -1, jnp.int32), jnp.full((1, tq), n_keys - 1, jnp.int32)))
        return jnp.where(tied, jhi, int_max)

    jcut = lax.cond(jnp.max(jnp.where(tied, 1, 0)) > 0, tie_cut, lambda _: jnp.full((1, tq), int_max, jnp.int32), 0)
    jcut8 = jnp.broadcast_to(jcut, (SUBLANES, tq))

    def write_chunk(c, carry):
        ks = pl.multiple_of(c * ck, ck)
        kk = key_ref[pl.ds(ks, ck), :].reshape(groups, SUBLANES, tq)
        idx = (c * ck + kloc).reshape(groups, SUBLANES, tq)
        at_thr = jnp.where(kk == thr8[None], jnp.where(idx <= jcut8[None], 1, 0), 0)
        sel = jnp.where(kk > thr8[None], 1, at_thr)
        mask_ref[0, pl.ds(ks, ck), :] = sel.reshape(ck, tq).astype(mask_ref.dtype)
        return carry

    lax.fori_loop(0, nk, write_chunk, 0)

    def zero_chunk(c, carry):
        ks = pl.multiple_of(c * ck, ck)
        mask_ref[0, pl.ds(ks, ck), :] = jnp.zeros((ck, tq), mask_ref.dtype)
        return carry

    lax.fori_loop(nk, mask_ref.shape[1] // ck, zero_chunk, 0)


def indexer_mask(qi, small, k_even, k_odd, b, t, n_sel):
    tq = _tile(t, 512)
    ck = _tile(t, 512)
    return pl.pallas_call(
        functools.partial(_indexer_body, tq=tq, ck=ck, n_sel=n_sel),
        grid=(b, t // tq),
        in_specs=[
            pl.BlockSpec((1, tq, IDX_HEADS * IDX_DIM), lambda bb, i: (bb, i, 0)),
            pl.BlockSpec((1, tq, LANES), lambda bb, i: (bb, i, 0)),
            pl.BlockSpec((1, t, LANES), lambda bb, i: (bb, 0, 0)),
            pl.BlockSpec((1, t, LANES), lambda bb, i: (bb, 0, 0)),
        ],
        out_specs=pl.BlockSpec((1, t, tq), lambda bb, i: (bb, 0, i)),
        out_shape=jax.ShapeDtypeStruct((b, t, t), jnp.int8),
        scratch_shapes=[pltpu.VMEM((t, tq), jnp.int32), pltpu.VMEM((t, tq), jnp.int16),
                        pltpu.VMEM((t, tq), jnp.int16), pltpu.VMEM((IDX_HEADS, SUBLANES, tq), F32)],
        compiler_params=_params("parallel", "parallel"),
        name="indexer",
    )(qi.reshape(b, t, -1), small, k_even.reshape(b, t, LANES), k_odd.reshape(b, t, LANES))


ATTN_HEAD_GROUP = 4


def _attn_body(qi_ref, ki_ref, fix_ref, q_ref, k_ref, vt_ref, mask_ref, bias_ref, z_ref, m0_ref, o_ref, m_ref, acc_ref,
               *, tq):
    qi, ki = qi_ref[pl.program_id(1)], ki_ref[pl.program_id(1)]
    fixed = fix_ref[0] == 1

    @pl.when(ki == 0)
    def _():
        m_ref[...] = jnp.full(m_ref.shape, MASK_NEG, F32)
        acc_ref[...] = jnp.zeros(acc_ref.shape, F32)

    def tile(near, fixed_shift):
        madd = (1.0 - mask_ref[0].astype(F32)) * MASK_NEG
        if fixed_shift:
            madd = madd - m0_ref[0, 0]
        off = jnp.where(ki == qi, 0, 1)
        for g in range(0, A_HEADS, ATTN_HEAD_GROUP):
            hs = range(g, g + ATTN_HEAD_GROUP)
            grp = slice(g, g + ATTN_HEAD_GROUP)
            head = lambda ref, h: ref[0, :, h * A_HEAD_DIM:(h + 1) * A_HEAD_DIM]
            k2 = jnp.stack([head(k_ref, h) for h in hs])
            q2 = jnp.stack([head(q_ref, h) for h in hs])
            s = lax.dot_general(k2, q2, (((2,), (2,)), ((0,), (0,))), preferred_element_type=F32) + madd[None]
            if near:
                s = s + bias_ref[off, grp].astype(F32)
            vt2 = jnp.stack([vt_ref[h * V_ROWS:(h + 1) * V_ROWS, :] for h in hs])
            pv = lambda p: lax.dot_general(vt2, p, (((2,), (1,)), ((0,), (0,))), preferred_element_type=F32)
            if fixed_shift:
                acc_ref[grp] = acc_ref[grp] + pv(jnp.exp2(s).astype(BF16))
            else:
                m_prev = m_ref[grp]
                m_new = jnp.maximum(m_prev, jnp.max(s, axis=1, keepdims=True))
                acc_ref[grp] = jnp.exp2(m_prev - m_new) * acc_ref[grp] + pv(jnp.exp2(s - m_new).astype(BF16))
                m_ref[grp] = m_new

    far = ki + 1 < qi
    for near in (False, True):
        for fixed_shift in (False, True):
            cond = jnp.logical_and(jnp.logical_not(far) if near else far,
                                   fixed if fixed_shift else jnp.logical_not(fixed))
            pl.when(cond)(functools.partial(tile, near, fixed_shift))

    @pl.when(ki == qi)
    def _():
        outs = []
        for h in range(A_HEADS):
            a = acc_ref[h]
            outs.append((a[:A_HEAD_DIM] / a[A_HEAD_DIM:A_HEAD_DIM + 1]).T)
        o_ref[0] = (jnp.concatenate(outs, axis=-1) * _silu(z_ref[0].astype(F32))).astype(o_ref.dtype)


MAX_LOGIT_SPAN = 100.0


def softmax_shift(q_gain, k_gain, rel_bias):
    slack = 1.02
    qk = (A_HEAD_DIM ** 0.5) * LOG2E * jnp.max(jnp.abs(q_gain)) * jnp.max(jnp.abs(k_gain)) * slack
    b2 = (rel_bias - rel_bias[REL_BUCKETS - 1]) * (LOG2E * slack)
    shift = qk + jnp.max(b2)
    span = 2.0 * qk + jnp.max(b2) - jnp.min(b2)
    return shift.astype(F32), (span <= MAX_LOGIT_SPAN).astype(jnp.int32)


def attention(q, k, vt, mask, bias, proj3d, shift, use_shift, b, t, tq):
    nq = t // tq
    pairs = [(i, j) for i in range(nq) for j in range(i + 1)]
    qi_tab = jnp.asarray([p[0] for p in pairs], jnp.int32)
    ki_tab = jnp.asarray([p[1] for p in pairs], jnp.int32)
    grid_spec = pltpu.PrefetchScalarGridSpec(
        num_scalar_prefetch=3,
        grid=(b, len(pairs)),
        in_specs=[
            pl.BlockSpec((1, tq, A_WIDTH), lambda bb, p, qi, ki, fx: (bb, qi[p], 0)),
            pl.BlockSpec((1, tq, A_WIDTH), lambda bb, p, qi, ki, fx: (bb, ki[p], 0)),
            pl.BlockSpec((A_HEADS * V_ROWS, tq), lambda bb, p, qi, ki, fx: (0, bb * nq + ki[p])),
            pl.BlockSpec((1, tq, tq), lambda bb, p, qi, ki, fx: (bb, ki[p], qi[p])),
            pl.BlockSpec(bias.shape, lambda bb, p, qi, ki, fx: (0, 0, 0, 0)),
            pl.BlockSpec((1, tq, A_WIDTH), lambda bb, p, qi, ki, fx: (bb, qi[p], COL_Z_A // A_WIDTH)),
            pl.BlockSpec(memory_space=pltpu.SMEM),
        ],
        out_specs=pl.BlockSpec((1, tq, A_WIDTH), lambda bb, p, qi, ki, fx: (bb, qi[p], 0)),
        scratch_shapes=[pltpu.VMEM((A_HEADS, 1, tq), F32), pltpu.VMEM((A_HEADS, V_ROWS, tq), F32)],
    )
    return pl.pallas_call(
        functools.partial(_attn_body, tq=tq),
        grid_spec=grid_spec,
        out_shape=jax.ShapeDtypeStruct((b, t, A_WIDTH), BF16),
        compiler_params=_params("parallel", "arbitrary"),
        name="dsa_attention",
    )(qi_tab, ki_tab, use_shift.reshape(1), q.reshape(b, t, -1), k.reshape(b, t, -1), vt, mask, bias, proj3d,
      shift.reshape(1, 1))


def _gdn_prep_body(x_ref, halo_ref, sm_ref, cw_ref, alog_ref, dtb_ref,
                   q_ref, k_ref, v_ref, gcb_ref, bb_ref, grow_ref, xs_ref, *, tm):
    i = pl.program_id(1)
    hal = halo_ref[0].astype(F32)
    xs_ref[0:BF16_ROWS, :] = jnp.where(i > 0, hal, jnp.zeros_like(hal))
    xs_ref[BF16_ROWS:, :] = x_ref[0].astype(F32)
    y = jnp.zeros((tm, 3 * B_WIDTH), F32)
    for j in range(B_CONV):
        y = y + cw_ref[j:j + 1, :] * xs_ref[pl.ds(BF16_ROWS - (B_CONV - 1) + j, tm), :]
    y = _silu(y)
    for h in range(B_HEADS):
        sl = slice(h * LANES, (h + 1) * LANES)
        qh = y[:, sl]
        q_ref[0, :, sl] = qh * lax.rsqrt(jnp.sum(qh * qh, axis=-1, keepdims=True) + EPS) * (B_HEAD_DIM ** -0.5)
        kh = y[:, B_WIDTH + h * LANES:B_WIDTH + (h + 1) * LANES]
        k_ref[0, :, sl] = kh * lax.rsqrt(jnp.sum(kh * kh, axis=-1, keepdims=True) + EPS)
    v_ref[0] = y[:, 2 * B_WIDTH:]

    sm = sm_ref[0]
    xg = sm + dtb_ref[...]
    softplus = jnp.maximum(xg, 0.0) + jnp.log(1.0 + jnp.exp(-jnp.abs(xg)))
    g = -jnp.exp(alog_ref[...]) * softplus
    r = lax.broadcasted_iota(jnp.int32, (tm, tm), 0)
    c = lax.broadcasted_iota(jnp.int32, (tm, tm), 1)
    sh = int(math.log2(GDN_CHUNK))
    same_chunk = jnp.right_shift(r, sh) == jnp.right_shift(c, sh)
    tri = jnp.where(jnp.logical_and(same_chunk, c <= r), 1.0, 0.0).astype(F32)
    gc = lax.dot_general(tri, g, (((1,), (0,)), ((), ())), precision=HIGHEST, preferred_element_type=F32)
    beta = _sigmoid(sm)
    for h in range(B_HEADS):
        sl = slice(h * LANES, (h + 1) * LANES)
        gcb_ref[0, :, sl] = jnp.broadcast_to(gc[:, SM_ALPHA + h:SM_ALPHA + h + 1], (tm, LANES))
        bb_ref[0, :, sl] = jnp.broadcast_to(beta[:, SM_BETA + h:SM_BETA + h + 1], (tm, LANES))
    gct = gc.T
    for cc in range(tm // GDN_CHUNK):
        grow_ref[0, cc] = gct[SM_ALPHA:SM_ALPHA + B_HEADS, cc * GDN_CHUNK:(cc + 1) * GDN_CHUNK]


def gdn_prep(proj3d, small3d, conv_w, a_log, dt_bias, b, t):
    tm = _tile(t, 256)
    pad_row = lambda v: jnp.zeros((1, LANES), F32).at[0, SM_ALPHA:SM_ALPHA + B_HEADS].set(v)
    nc = tm // GDN_CHUNK
    act = jax.ShapeDtypeStruct((b, t, B_WIDTH), F32)
    full = lambda a: pl.BlockSpec(a.shape, lambda bb, i: (0,) * a.ndim)
    consts = [conv_w, pad_row(a_log), pad_row(dt_bias)]
    return pl.pallas_call(
        functools.partial(_gdn_prep_body, tm=tm),
        grid=(b, t // tm),
        in_specs=[
            pl.BlockSpec((1, tm, 3 * B_WIDTH), lambda bb, i: (bb, i, COL_QKV_B // (3 * B_WIDTH))),
            pl.BlockSpec((1, BF16_ROWS, 3 * B_WIDTH),
                         lambda bb, i: (bb, jnp.maximum(i * (tm // BF16_ROWS) - 1, 0), 0)),
            pl.BlockSpec((1, tm, LANES), lambda bb, i: (bb, i, 0)),
        ] + [full(c) for c in consts],
        out_specs=[pl.BlockSpec((1, tm, B_WIDTH), lambda bb, i: (bb, i, 0))] * 5
        + [pl.BlockSpec((1, nc, B_HEADS, GDN_CHUNK), lambda bb, i: (bb, i, 0, 0))],
        out_shape=[act] * 5 + [jax.ShapeDtypeStruct((b, t // GDN_CHUNK, B_HEADS, GDN_CHUNK), F32)],
        scratch_shapes=[pltpu.VMEM((tm + BF16_ROWS, 3 * B_WIDTH), F32)],
        compiler_params=_params("parallel", "parallel"),
        name="gdn_prep",
    )(proj3d, proj3d, small3d, *consts)


def _gdn_body(q_ref, k_ref, v_ref, gcb_ref, bb_ref, grow_ref, z_ref, on_ref, o_ref, s_ref, *, nc):
    @pl.when(pl.program_id(1) == 0)
    def _():
        s_ref[...] = jnp.zeros(s_ref.shape, F32)

    cs, nh = GDN_CHUNK, B_HEADS
    nb = nc * nh
    ri = lax.broadcasted_iota(jnp.int32, (nb, cs, cs), 1)
    ci = lax.broadcasted_iota(jnp.int32, (nb, cs, cs), 2)
    lower = ci <= ri
    strict = ci < ri
    eye = jnp.where(ci == ri, 1.0, 0.0).astype(F32)
    bf = lambda a: a.astype(BF16)
    bmm = lambda a, b: lax.dot_general(a, b, (((2,), (1,)), ((0,), (0,))), preferred_element_type=F32)
    bmm_nt = lambda a, b: lax.dot_general(a, b, (((2,), (2,)), ((0,), (0,))), preferred_element_type=F32)
    bmm_tn = lambda a, b: lax.dot_general(a, b, (((1,), (1,)), ((0,), (0,))), preferred_element_type=F32)

    def stack(ref):
        return jnp.stack([ref[0, c * cs:(c + 1) * cs, h * LANES:(h + 1) * LANES]
                          for c in range(nc) for h in range(nh)])

    q, k, v = stack(q_ref), stack(k_ref), stack(v_ref)
    gcb = stack(gcb_ref)
    beta = stack(bb_ref)
    grow = jnp.stack([grow_ref[0, c, h:h + 1, :] for c in range(nc) for h in range(nh)])
    diff = gcb[:, :, :cs] - grow
    decay = jnp.where(lower, jnp.exp(jnp.where(lower, diff, 0.0)), 0.0)
    eg = jnp.exp(gcb)
    glast = gcb[:, cs - 1:cs, :]
    kb = k * beta
    lmat = jnp.where(strict, bmm_nt(bf(kb), bf(k)) * decay, 0.0)
    n = -lmat
    tinv = eye + n
    for _ in range(int(math.log2(cs)) - 1):
        n = bmm(bf(n), bf(n))
        tinv = tinv + bmm(bf(tinv), bf(n))
    uw = bmm(bf(tinv), bf(jnp.concatenate([v * beta, kb * eg], axis=-1)))
    attn = bf(jnp.where(lower, bmm_nt(bf(q), bf(k)) * decay, 0.0))
    qg = bf(q * eg)
    kdec = bf(k * jnp.exp(glast - gcb))
    egl = jnp.exp(glast)

    s = s_ref[...]
    for c in range(nc):
        sl = slice(c * nh, (c + 1) * nh)
        sb = bf(s)
        v_new = uw[sl, :, :LANES] - bmm(bf(uw[sl, :, LANES:]), sb)
        o = bmm(qg[sl], sb) + bmm(attn[sl], bf(v_new))
        s = s * egl[sl] + bmm_tn(kdec[sl], bf(v_new))
        o = o * lax.rsqrt(jnp.mean(o * o, axis=-1, keepdims=True) + EPS) * on_ref[...]
        for h in range(nh):
            rows, cols = slice(c * cs, (c + 1) * cs), slice(h * LANES, (h + 1) * LANES)
            o_ref[0, rows, cols] = (o[h] * _silu(z_ref[0, rows, cols].astype(F32))).astype(o_ref.dtype)
    s_ref[...] = s


def gdn_scan(qh, kh, v, gcb, bb, grow, proj3d, o_norm, b, t):
    tt = _tile(t, 256)
    nc = tt // GDN_CHUNK
    blk = pl.BlockSpec((1, tt, B_WIDTH), lambda bb_, i: (bb_, i, 0))
    return pl.pallas_call(
        functools.partial(_gdn_body, nc=nc),
        grid=(b, t // tt),
        in_specs=[blk] * 5 + [
            pl.BlockSpec((1, nc, B_HEADS, GDN_CHUNK), lambda bb_, i: (bb_, i, 0, 0)),
            pl.BlockSpec((1, tt, B_WIDTH), lambda bb_, i: (bb_, i, COL_Z_B // B_WIDTH)),
            pl.BlockSpec((1, LANES), lambda bb_, i: (0, 0)),
        ],
        out_specs=blk,
        out_shape=jax.ShapeDtypeStruct((b, t, B_WIDTH), BF16),
        scratch_shapes=[pltpu.VMEM((B_HEADS, B_HEAD_DIM, B_HEAD_DIM), F32)],
        compiler_params=_params("parallel", "arbitrary"),
        name="gdn_scan",
    )(qh, kh, v, gcb, bb, grow, proj3d, o_norm.reshape(1, LANES))


C_HALO = 32
D_HALO = BF16_ROWS


def _cd_mix_body(a_ref, g_ref, ah_ref, gh_ref, zc_ref, bg_ref, cg_ref, ud_ref, cgh_ref, udh_ref, zd_ref,
                 dww_ref, dwb_ref, lnw_ref, lnb_ref, dcw_ref, yc_ref, yd_ref, us_ref, ds_ref, *, tm):
    i = pl.program_id(1)
    first = i == 0
    f32 = lambda r: r[0].astype(F32)
    uh = f32(ah_ref) * _sigmoid(f32(gh_ref))
    us_ref[0, 0:C_HALO, :] = jnp.where(first, jnp.zeros_like(uh), uh)
    us_ref[0, C_HALO:, :] = f32(a_ref) * _sigmoid(f32(g_ref))
    span = tm + C_HALO - SUBLANES
    for r in range(1, SUBLANES):
        us_ref[r, 0:span, :] = us_ref[0, pl.ds(r, span), :]
    u = jnp.zeros((tm, C_WIDTH), F32)
    for j in range(C_CONV):
        off = C_HALO - (C_CONV - 1) + j
        r, base = off % SUBLANES, off - off % SUBLANES
        u = u + dww_ref[j:j + 1, :] * us_ref[r, base:base + tm, :]
    u = u + dwb_ref[...]
    mu = jnp.mean(u, axis=-1, keepdims=True)
    uc = u - mu
    u = uc * lax.rsqrt(jnp.mean(uc * uc, axis=-1, keepdims=True) + EPS) * lnw_ref[...] + lnb_ref[...]
    yc_ref[0] = (_silu(u) * _silu(f32(zc_ref))).astype(yc_ref.dtype)

    dh = f32(cgh_ref) * f32(udh_ref)
    ds_ref[0:D_HALO, :] = jnp.where(first, jnp.zeros_like(dh), dh)
    ds_ref[D_HALO:, :] = f32(cg_ref) * f32(ud_ref)
    d = jnp.zeros((tm, D_WIDTH), F32)
    for j in range(D_CONV):
        d = d + dcw_ref[j:j + 1, :] * ds_ref[pl.ds(D_HALO - (D_CONV - 1) + j, tm), :]
    yd_ref[0] = (f32(bg_ref) * d * _silu(f32(zd_ref))).astype(yd_ref.dtype)


def cd_mix(proj3d, dw_w, dw_b, ln_w, ln_b, d_conv_w, b, t):
    tm = _tile(t, 256)
    w = C_WIDTH
    col = lambda n: pl.BlockSpec((1, tm, w), lambda bb, i, n=n: (bb, i, n))
    halo = lambda n, rows: pl.BlockSpec(
        (1, rows, w), lambda bb, i, n=n, rows=rows: (bb, jnp.maximum(i * (tm // rows) - 1, 0), n))
    row = lambda a: a.reshape(1, -1)
    full = lambda a: pl.BlockSpec(a.shape, lambda bb, i: (0,) * a.ndim)
    consts = [dw_w, row(dw_b), row(ln_w), row(ln_b), d_conv_w]
    out = jax.ShapeDtypeStruct((b, t, w), BF16)
    return pl.pallas_call(
        functools.partial(_cd_mix_body, tm=tm),
        grid=(b, t // tm),
        in_specs=[col(0), col(1), halo(0, C_HALO), halo(1, C_HALO), col(2), col(3), col(4), col(5),
                  halo(4, D_HALO), halo(5, D_HALO), col(6)] + [full(c) for c in consts],
        out_specs=[pl.BlockSpec((1, tm, w), lambda bb, i: (bb, i, 0))] * 2,
        out_shape=[out, out],
        scratch_shapes=[pltpu.VMEM((SUBLANES, tm + C_HALO, w), F32), pltpu.VMEM((tm + D_HALO, w), F32)],
        compiler_params=_params("parallel", "parallel"),
        name="cd_mix",
    )(*([proj3d] * 11), *consts)


def _reorder_ab_w_in(w):
    offs = [0]
    for s in AB_SPLITS:
        offs.append(offs[-1] + s)
    part = lambda n: w[:, offs[n]:offs[n + 1]]
    c_q, c_kv, k_idx, w_idx, z_a, qkv_b, beta_b, alpha_b, z_b = (part(n) for n in range(9))
    small = jnp.concatenate([k_idx, w_idx, beta_b, alpha_b], axis=1)
    small = jnp.pad(small, ((0, 0), (0, LANES - small.shape[1])))
    out = jnp.concatenate([qkv_b, z_a, z_b, c_q, c_kv], axis=1)
    return jnp.pad(out, ((0, 0), (0, AB_IN_PAD - out.shape[1]))).astype(BF16), small.astype(BF16)


def _ab_layer(x2d, b, t, norm_w, rel_bias, w_in, q_norm, w_uq, w_iq, kv_norm, w_uk, w_uv, q_gain, k_gain,
              ik_w, ik_b, conv_w, a_log, dt_bias, o_norm, w_out):
    proj, small = norm_matmul(x2d, norm_w, *_reorder_ab_w_in(w_in))
    proj3d, small3d = proj.reshape(b, t, AB_IN_PAD), small.reshape(b, t, LANES)
    q, k, v, qi, k_even, k_odd = dsa_prep(proj, small, q_norm, kv_norm, w_uq, w_iq, w_uk, w_uv, q_gain, k_gain,
                                          ik_w, ik_b)
    n_sel = min(TOPK_MAX, t // 4)
    mask = indexer_mask(qi, small3d, k_even, k_odd, b, t, n_sel)
    tq = _tile(t, 512)
    shift, use_shift = softmax_shift(q_gain, k_gain, rel_bias)
    y_a = attention(q, k, v, mask, bias_tiles(rel_bias, tq), proj3d, shift, use_shift, b, t, tq)
    qh, kh, vv, gcb, bb, grow = gdn_prep(proj3d, small3d, conv_w, a_log, dt_bias, b, t)
    y_b = gdn_scan(qh, kh, vv, gcb, bb, grow, proj3d, o_norm, b, t)
    wo = w_out.astype(BF16)
    return out_proj(x2d, y_a.reshape(b * t, -1), y_b.reshape(b * t, -1), wo[:A_WIDTH], wo[A_WIDTH:])


def _cd_layer(x2d, b, t, norm_w, w_in, dw_w, dw_b, ln_w, ln_b, d_conv_w, w_out):
    proj = norm_matmul(x2d, norm_w, w_in.astype(BF16))
    y_c, y_d = cd_mix(proj.reshape(b, t, -1), dw_w, dw_b, ln_w, ln_b, d_conv_w, b, t)
    wo = w_out.astype(BF16)
    return out_proj(x2d, y_c.reshape(b * t, -1), y_d.reshape(b * t, -1), wo[:C_WIDTH], wo[C_WIDTH:])


def kernel(x, norm_w, rel_bias, ab_w_in, a_q_norm, a_w_uq, a_w_iq, a_kv_norm, a_w_uk, a_w_uv, a_q_gain,
           a_k_gain, a_ik_norm_w, a_ik_norm_b, b_conv_w, b_a_log, b_dt_bias, b_o_norm, ab_w_out, cd_w_in,
           c_dw_w, c_dw_b, c_ln_w, c_ln_b, d_conv_w, cd_w_out):
    b, t, d = x.shape
    depth = norm_w.shape[0]
    x2d = x.reshape(b * t, d)
    for i in range(depth):
        j = i // 2
        if i % 2 == 0:
            x2d = _ab_layer(x2d, b, t, norm_w[i], rel_bias, ab_w_in[j], a_q_norm[j], a_w_uq[j], a_w_iq[j],
                            a_kv_norm[j], a_w_uk[j], a_w_uv[j], a_q_gain[j], a_k_gain[j], a_ik_norm_w[j],
                            a_ik_norm_b[j], b_conv_w[j], b_a_log[j], b_dt_bias[j], b_o_norm[j], ab_w_out[j])
        else:
            x2d = _cd_layer(x2d, b, t, norm_w[i], cd_w_in[j], c_dw_w[j], c_dw_b[j], c_ln_w[j], c_ln_b[j],
                            d_conv_w[j], cd_w_out[j])
    return x2d.reshape(b, t, d)
```

```python
import functools
import math

import jax
import jax.numpy as jnp
from jax import lax
from jax.experimental import pallas as pl
from jax.experimental.pallas import tpu as pltpu

A_HEADS = 8
A_HEAD_DIM = 128
A_WIDTH = A_HEADS * A_HEAD_DIM
A_Q_LORA = 512
A_KV_LORA = 256
IDX_HEADS = 16
IDX_DIM = 64
TOPK_MAX = 256
REL_BUCKETS = 32
REL_MAX_DIST = 128
B_HEADS = 8
B_HEAD_DIM = 128
B_WIDTH = B_HEADS * B_HEAD_DIM
B_CONV = 4
GDN_CHUNK = 64
C_WIDTH = 1024
C_CONV = 31
D_WIDTH = 1024
D_CONV = 3
EPS = 1e-6

AB_SPLITS = (A_Q_LORA, A_KV_LORA, IDX_DIM, IDX_HEADS, A_WIDTH, 3 * B_WIDTH, B_HEADS, B_HEADS, B_WIDTH)
AB_IN_PAD = 6144
SM_KIDX = 0
SM_WIDX = 64
SM_BETA = 80
SM_ALPHA = 88
COL_QKV_B = 0
COL_Z_A = 3072
COL_Z_B = 4096
COL_C_Q = 5120
COL_C_KV = 5632

LANES = 128
MASK_NEG = -1e30
INT_MIN = -(2 ** 31)
LOG2E = math.log2(math.e)
BF16_ROWS = 16
V_ROWS = A_HEAD_DIM + BF16_ROWS
VMEM_LIMIT = 56 * 1024 * 1024

F32 = jnp.float32
BF16 = jnp.bfloat16
HIGHEST = lax.Precision.HIGHEST


def _tile(n, pref):
    t = min(n, pref)
    assert n % t == 0, (n, t)
    return t


def _params(*sem):
    return pltpu.CompilerParams(dimension_semantics=sem, vmem_limit_bytes=VMEM_LIMIT)


def _dot(a, b):
    return jnp.dot(a, b, preferred_element_type=F32)


def _dot_nt(a, b):
    return lax.dot_general(a, b, (((1,), (1,)), ((), ())), preferred_element_type=F32)


def _silu(x):
    return x * (1.0 / (1.0 + jnp.exp(-x)))


def _sigmoid(x):
    return 1.0 / (1.0 + jnp.exp(-x))


def _norm_matmul_body(*refs, with_f32_cols):
    if with_f32_cols:
        x_ref, nw_ref, w_ref, ws_ref, o_ref, os_ref, h_ref = refs
    else:
        x_ref, nw_ref, w_ref, o_ref, h_ref = refs

    @pl.when(pl.program_id(1) == 0)
    def _():
        x = x_ref[...]
        ms = jnp.mean(x * x, axis=-1, keepdims=True)
        h_ref[...] = (x * lax.rsqrt(ms + EPS) * nw_ref[...]).astype(h_ref.dtype)
        if with_f32_cols:
            os_ref[...] = _dot(h_ref[...], ws_ref[...])

    o_ref[...] = _dot(h_ref[...], w_ref[...]).astype(o_ref.dtype)


def norm_matmul(x2d, norm_w, w_bf16, w_f32_cols=None):
    m, k = x2d.shape
    n = w_bf16.shape[1]
    tm, tn = _tile(m, 1024), _tile(n, 1024)
    extra = w_f32_cols is not None
    in_specs = [
        pl.BlockSpec((tm, k), lambda i, j: (i, 0)),
        pl.BlockSpec((1, k), lambda i, j: (0, 0)),
        pl.BlockSpec((k, tn), lambda i, j: (0, j)),
    ]
    out_specs = [pl.BlockSpec((tm, tn), lambda i, j: (i, j))]
    out_shape = [jax.ShapeDtypeStruct((m, n), BF16)]
    args = [x2d, norm_w.reshape(1, k), w_bf16]
    if extra:
        in_specs.append(pl.BlockSpec((k, LANES), lambda i, j: (0, 0)))
        out_specs.append(pl.BlockSpec((tm, LANES), lambda i, j: (i, 0)))
        out_shape.append(jax.ShapeDtypeStruct((m, LANES), F32))
        args.append(w_f32_cols)
    res = pl.pallas_call(
        functools.partial(_norm_matmul_body, with_f32_cols=extra),
        grid=(m // tm, n // tn),
        in_specs=in_specs,
        out_specs=out_specs,
        out_shape=out_shape,
        scratch_shapes=[pltpu.VMEM((tm, k), BF16)],
        compiler_params=_params("parallel", "arbitrary"),
        name="norm_matmul",
    )(*args)
    return res if extra else res[0]


def _out_proj_body(x_ref, ya_ref, yb_ref, wa_ref, wb_ref, o_ref):
    o_ref[...] = x_ref[...] + _dot(ya_ref[...], wa_ref[...]) + _dot(yb_ref[...], wb_ref[...])


def out_proj(x2d, ya, yb, wa_bf16, wb_bf16):
    m, n = x2d.shape
    ka, kb = ya.shape[1], yb.shape[1]
    tm = _tile(m, 512)
    return pl.pallas_call(
        _out_proj_body,
        grid=(m // tm,),
        in_specs=[
            pl.BlockSpec((tm, n), lambda i: (i, 0)),
            pl.BlockSpec((tm, ka), lambda i: (i, 0)),
            pl.BlockSpec((tm, kb), lambda i: (i, 0)),
            pl.BlockSpec((ka, n), lambda i: (0, 0)),
            pl.BlockSpec((kb, n), lambda i: (0, 0)),
        ],
        out_specs=pl.BlockSpec((tm, n), lambda i: (i, 0)),
        out_shape=jax.ShapeDtypeStruct((m, n), F32),
        compiler_params=_params("parallel"),
        name="out_proj",
    )(x2d, ya, yb, wa_bf16, wb_bf16)


def _head_rmsnorm(y, gain_row, scale):
    outs = []
    for h in range(y.shape[1] // LANES):
        seg = y[:, h * LANES:(h + 1) * LANES]
        ms = jnp.mean(seg * seg, axis=-1, keepdims=True)
        outs.append(seg * lax.rsqrt(ms + EPS) * (gain_row * scale))
    return jnp.concatenate(outs, axis=-1)


def _dsa_prep_body(cq_ref, ckv_ref, sm_ref, qn_ref, kvn_ref, wuq_ref, wiq_ref, wuk_ref, wuv_ref,
                   qg_ref, kg_ref, ikw_ref, ikb_ref,
                   q_ref, k_ref, v_ref, qi_ref, ke_ref, ko_ref):
    cq = cq_ref[...].astype(F32)
    cq = cq * lax.rsqrt(jnp.mean(cq * cq, axis=-1, keepdims=True) + EPS) * qn_ref[...]
    cqb = cq.astype(BF16)
    q = _dot(cqb, wuq_ref[...])
    q_ref[...] = _head_rmsnorm(q, qg_ref[...], A_HEAD_DIM ** -0.5 * LOG2E).astype(q_ref.dtype)
    qi_ref[...] = _dot(cqb, wiq_ref[...]).astype(qi_ref.dtype)

    ckv = ckv_ref[...].astype(F32)
    ckv = ckv * lax.rsqrt(jnp.mean(ckv * ckv, axis=-1, keepdims=True) + EPS) * kvn_ref[...]
    ckvb = ckv.astype(BF16)
    k = _dot(ckvb, wuk_ref[...])
    k_ref[...] = _head_rmsnorm(k, kg_ref[...], 1.0).astype(k_ref.dtype)
    vt = _dot_nt(wuv_ref[...], ckvb).astype(v_ref.dtype)
    ones = jnp.ones((BF16_ROWS, vt.shape[1]), v_ref.dtype)
    for h in range(A_HEADS):
        v_ref[h * V_ROWS:h * V_ROWS + A_HEAD_DIM, :] = vt[h * A_HEAD_DIM:(h + 1) * A_HEAD_DIM, :]
        v_ref[h * V_ROWS + A_HEAD_DIM:(h + 1) * V_ROWS, :] = ones

    kx = sm_ref[...][:, SM_KIDX:SM_KIDX + IDX_DIM]
    mu = jnp.mean(kx, axis=-1, keepdims=True)
    kc = kx - mu
    kl = kc * lax.rsqrt(jnp.mean(kc * kc, axis=-1, keepdims=True) + EPS) * ikw_ref[...] + ikb_ref[...]
    z = jnp.zeros_like(kl)
    ke_ref[...] = jnp.concatenate([kl, z], axis=-1).astype(ke_ref.dtype)
    ko_ref[...] = jnp.concatenate([z, kl], axis=-1).astype(ko_ref.dtype)


def dsa_prep(proj, small, q_norm, kv_norm, w_uq, w_iq, w_uk, w_uv, q_gain, k_gain, ik_w, ik_b):
    m = proj.shape[0]
    tm = _tile(m, 512)
    row = lambda a: a.reshape(1, -1)
    full = lambda a: pl.BlockSpec(a.shape, lambda i: (0,) * a.ndim)
    consts = [row(q_norm), row(kv_norm), w_uq.astype(BF16), w_iq.astype(BF16), w_uk.astype(BF16),
              w_uv.T.astype(BF16), row(q_gain), row(k_gain), row(ik_w), row(ik_b)]
    outs = [jax.ShapeDtypeStruct((m, A_WIDTH), BF16)] * 2 + [
        jax.ShapeDtypeStruct((A_HEADS * V_ROWS, m), BF16),
        jax.ShapeDtypeStruct((m, IDX_HEADS * IDX_DIM), BF16),
        jax.ShapeDtypeStruct((m, LANES), BF16), jax.ShapeDtypeStruct((m, LANES), BF16)]
    out_specs = [pl.BlockSpec((tm, s.shape[1]), lambda i: (i, 0)) for s in outs]
    out_specs[2] = pl.BlockSpec((A_HEADS * V_ROWS, tm), lambda i: (0, i))
    return pl.pallas_call(
        _dsa_prep_body,
        grid=(m // tm,),
        in_specs=[
            pl.BlockSpec((tm, A_Q_LORA), lambda i: (i, COL_C_Q // A_Q_LORA)),
            pl.BlockSpec((tm, A_KV_LORA), lambda i: (i, COL_C_KV // A_KV_LORA)),
            pl.BlockSpec((tm, LANES), lambda i: (i, 0)),
        ] + [full(c) for c in consts],
        out_specs=out_specs,
        out_shape=outs,
        compiler_params=_params("parallel"),
        name="dsa_prep",
    )(proj, proj, small, *consts)


def _t5_bucket(dist):
    max_exact = REL_BUCKETS // 2
    large = max_exact + (jnp.log(jnp.maximum(dist, 1).astype(F32) / max_exact)
                         / math.log(REL_MAX_DIST / max_exact) * (REL_BUCKETS - max_exact)).astype(jnp.int32)
    large = jnp.minimum(large, REL_BUCKETS - 1)
    return jnp.where(dist < max_exact, dist, large)


def _bias_tiles_body(rb_ref, o_ref, *, t):
    h = pl.program_id(0)
    off = pl.program_id(1) * t
    dist = off + lax.broadcasted_iota(jnp.int32, (t, t), 1) - lax.broadcasted_iota(jnp.int32, (t, t), 0)
    bucket = _t5_bucket(jnp.maximum(dist, 0))
    acc = jnp.zeros((t, t), F32)
    for b in range(REL_BUCKETS):
        acc = jnp.where(bucket == b, rb_ref[b, h], acc)
    o_ref[0, 0] = ((acc - rb_ref[REL_BUCKETS - 1, h]) * LOG2E).astype(o_ref.dtype)


def bias_tiles(rel_bias, t):
    return pl.pallas_call(
        functools.partial(_bias_tiles_body, t=t),
        grid=(A_HEADS, 2),
        in_specs=[pl.BlockSpec(memory_space=pltpu.SMEM)],
        out_specs=pl.BlockSpec((1, 1, t, t), lambda h, c: (c, h, 0, 0)),
        out_shape=jax.ShapeDtypeStruct((2, A_HEADS, t, t), BF16),
        compiler_params=_params("parallel", "parallel"),
        name="bias_tiles",
    )(rel_bias)


SUBLANES = 8
MAX_SELECT_STEPS = 24
SELECT_STEPS_PER_CHECK = 4
HALF_BITS = 16
HALF_SPAN = 2 ** HALF_BITS
I16_MIN = -(2 ** (HALF_BITS - 1))


def _key_of(x):
    bits = pltpu.bitcast(x, jnp.int32)
    key = jnp.where(bits < 0, bits ^ jnp.int32(0x7FFFFFFF), bits)
    return jnp.where(bits == jnp.int32(INT_MIN), 0, key)


def _indexer_body(qi_ref, sm_ref, ke_ref, ko_ref, mask_ref, key_ref, hi_ref, low_ref, w_ref, *, tq, ck, n_sel):
    i = pl.program_id(1)
    nk = ((i + 1) * tq + ck - 1) // ck
    groups = ck // SUBLANES
    scale = (IDX_HEADS ** -0.5) * (IDX_DIM ** -0.5)
    smt = sm_ref[0].T
    for h in range(IDX_HEADS):
        w_ref[h] = jnp.broadcast_to(smt[SM_WIDX + h:SM_WIDX + h + 1, :] * scale, (SUBLANES, tq))
    kloc = lax.broadcasted_iota(jnp.int32, (ck, tq), 0)
    qpos = i * tq + lax.broadcasted_iota(jnp.int32, (ck, tq), 1)
    int_max = jnp.int32(2 ** 31 - 1)

    def score_chunk(c, carry):
        kmin, kmax = carry
        ks = pl.multiple_of(c * ck, ck)
        ke = ke_ref[0, pl.ds(ks, ck), :]
        ko = ko_ref[0, pl.ds(ks, ck), :]
        acc = jnp.zeros((ck, tq), F32)
        for j in range(IDX_HEADS // 2):
            qp = qi_ref[0, :, j * LANES:(j + 1) * LANES]
            for par, kk in ((0, ke), (1, ko)):
                s = jnp.maximum(_dot_nt(kk, qp), 0.0)
                acc = acc + s * jnp.tile(w_ref[2 * j + par], (groups, 1))
        key = _key_of(acc)
        valid = c * ck + kloc <= qpos
        masked = jnp.where(valid, key, jnp.int32(INT_MIN))
        key_ref[pl.ds(ks, ck), :] = masked
        hi_ref[pl.ds(ks, ck), :] = jnp.right_shift(masked, HALF_BITS).astype(jnp.int16)
        kmin = jnp.minimum(kmin, jnp.min(jnp.where(valid, key, int_max).reshape(groups, SUBLANES, tq), axis=0))
        kmax = jnp.maximum(kmax, jnp.max(masked.reshape(groups, SUBLANES, tq), axis=0))
        return kmin, kmax

    kmin, kmax = lax.fori_loop(0, nk, score_chunk, (jnp.full((SUBLANES, tq), int_max, jnp.int32),
                                                    jnp.full((SUBLANES, tq), INT_MIN, jnp.int32)))

    rows16 = ck // BF16_ROWS
    i16_min = I16_MIN

    def scan16(a_ref, init, fn):
        def body(c, acc):
            ks = pl.multiple_of(c * ck, ck)
            return fn(acc, a_ref[pl.ds(ks, ck), :].reshape(rows16, BF16_ROWS, tq))

        return lax.fori_loop(0, nk, body, init)

    def count_ge16(a_ref, p):
        p16 = jnp.broadcast_to(p.astype(jnp.int16), (BF16_ROWS, tq))

        def fn(cnt, a):
            ind = jnp.where(a >= p16[None], jnp.int16(1), jnp.int16(0))
            for g in range(rows16):
                cnt = cnt + ind[g]
            return cnt

        cnt = scan16(a_ref, jnp.zeros((BF16_ROWS, tq), jnp.int16), fn)
        return jnp.sum(cnt.astype(jnp.int32), axis=0, keepdims=True)

    def max_le16(a_ref, h):
        h16 = jnp.broadcast_to(h.astype(jnp.int16), (BF16_ROWS, tq))

        def fn(acc, a):
            v = jnp.where(a <= h16[None], a, jnp.int16(i16_min))
            for g in range(rows16):
                acc = jnp.where(v[g] > acc, v[g], acc)
            return acc

        acc = scan16(a_ref, jnp.full((BF16_ROWS, tq), i16_min, jnp.int16), fn)
        return jnp.max(acc.astype(jnp.int32), axis=0, keepdims=True)

    def rank_select16(a_ref, r, skip, lo, hi, clo, chi, floor_count):
        def closed(lo, hi, clo):
            return skip | (clo == r) | (hi == lo + 1)

        def open_rows(lo, hi, clo, chi):
            return jnp.logical_not(closed(lo, hi, clo) | (chi == r - 1))

        def cond(st):
            it, lo, hi, clo, chi = st
            n_open = jnp.max(jnp.where(open_rows(lo, hi, clo, chi), 1, 0))
            return jnp.logical_and(it < MAX_SELECT_STEPS, n_open > 0)

        def step(st):
            it, lo, hi, clo, chi = st
            for _ in range(SELECT_STEPS_PER_CHECK):
                upd = open_rows(lo, hi, clo, chi)
                p = lo + jnp.maximum(jnp.right_shift(hi - lo, 1), 1)
                cnt = count_ge16(a_ref, p)
                ge = cnt >= r
                up, dn = upd & ge, upd & jnp.logical_not(ge)
                lo, hi = jnp.where(up, p, lo), jnp.where(dn, p, hi)
                clo, chi = jnp.where(up, cnt, clo), jnp.where(dn, cnt, chi)
            return it + SELECT_STEPS_PER_CHECK, lo, hi, clo, chi

        _, lo, hi, clo, chi = lax.while_loop(cond, step, (jnp.int32(0), lo, hi, clo, chi))
        last = jnp.logical_not(closed(lo, hi, clo))
        t_last = max_le16(a_ref, hi - 1)
        c_last = jnp.where(t_last == i16_min, floor_count, count_ge16(a_ref, t_last))
        return jnp.where(last, t_last, lo), jnp.where(last, c_last, clo), chi

    n_valid = i * tq + lax.broadcasted_iota(jnp.int32, (1, tq), 1) + 1
    few = n_valid <= n_sel
    zeros = jnp.zeros((1, tq), jnp.int32)
    hi_lo0 = jnp.right_shift(jnp.min(kmin, axis=0, keepdims=True), HALF_BITS)
    hi_hi0 = jnp.right_shift(jnp.max(kmax, axis=0, keepdims=True), HALF_BITS) + 1
    t_hi, c_ge_hi, c_gt_hi = rank_select16(hi_ref, n_sel, few, hi_lo0, hi_hi0, n_valid, zeros, n_valid)
    exact_hi = c_ge_hi == n_sel

    t_hi16 = jnp.broadcast_to(t_hi.astype(jnp.int16), (BF16_ROWS, tq))

    def fill_low(c, carry):
        ks = pl.multiple_of(c * ck, ck)
        low = (key_ref[pl.ds(ks, ck), :] & (HALF_SPAN - 1)) + I16_MIN
        low = low.astype(jnp.int16).reshape(rows16, BF16_ROWS, tq)
        same = hi_ref[pl.ds(ks, ck), :].reshape(rows16, BF16_ROWS, tq) == t_hi16[None]
        low_ref[pl.ds(ks, ck), :] = jnp.where(same, low, jnp.int16(i16_min)).reshape(ck, tq)
        return carry

    lax.fori_loop(0, nk, fill_low, 0)
    in_bucket = c_ge_hi - c_gt_hi
    t_lo, c_ge_lo, c_gt_lo = rank_select16(low_ref, n_sel - c_gt_hi, few | exact_hi,
                                           jnp.full((1, tq), I16_MIN, jnp.int32), jnp.full((1, tq), -I16_MIN, jnp.int32),
                                           in_bucket, zeros, in_bucket)
    lo = jnp.where(exact_hi, t_hi * HALF_SPAN, t_hi * HALF_SPAN + (t_lo - I16_MIN))
    clo = jnp.where(exact_hi, n_sel, c_gt_hi + c_ge_lo)
    chi = c_gt_hi + c_gt_lo
    thr = jnp.where(few, jnp.int32(INT_MIN + 1), lo)
    thr8 = jnp.broadcast_to(thr, (SUBLANES, tq))
    tied = jnp.logical_not(few) & (clo > n_sel)
    need = n_sel - chi
    n_keys = mask_ref.shape[1]

    def tie_cut(_):
        def count_le(j):
            j8 = jnp.broadcast_to(j, (SUBLANES, tq))

            def body(c, cnt):
                ks = pl.multiple_of(c * ck, ck)
                kk = key_ref[pl.ds(ks, ck), :].reshape(groups, SUBLANES, tq)
                idx = (c * ck + kloc).reshape(groups, SUBLANES, tq)
                hit = jnp.where(kk == thr8[None], jnp.where(idx <= j8[None], 1, 0), 0)
                return cnt + jnp.sum(hit, axis=0)

            cnt = lax.fori_loop(0, nk, body, jnp.zeros((SUBLANES, tq), jnp.int32))
            return jnp.sum(cnt, axis=0, keepdims=True)

        def bisect(_, st):
            jlo, jhi = st
            mid = jlo + jnp.right_shift(jhi - jlo, 1)
            ok = count_le(mid) >= need
            return jnp.where(ok, jlo, mid), jnp.where(ok, mid, jhi)

        _, jhi = lax.fori_loop(0, n_keys.bit_length(), bisect,
                               (jnp.full((1, tq), -1, jnp.int32), jnp.full((1, tq), n_keys - 1, jnp.int32)))
        return jnp.where(tied, jhi, int_max)

    jcut = lax.cond(jnp.max(jnp.where(tied, 1, 0)) > 0, tie_cut, lambda _: jnp.full((1, tq), int_max, jnp.int32), 0)
    jcut8 = jnp.broadcast_to(jcut, (SUBLANES, tq))

    def write_chunk(c, carry):
        ks = pl.multiple_of(c * ck, ck)
        kk = key_ref[pl.ds(ks, ck), :].reshape(groups, SUBLANES, tq)
        idx = (c * ck + kloc).reshape(groups, SUBLANES, tq)
        at_thr = jnp.where(kk == thr8[None], jnp.where(idx <= jcut8[None], 1, 0), 0)
        sel = jnp.where(kk > thr8[None], 1, at_thr)
        mask_ref[0, pl.ds(ks, ck), :] = sel.reshape(ck, tq).astype(mask_ref.dtype)
        return carry

    lax.fori_loop(0, nk, write_chunk, 0)

    def zero_chunk(c, carry):
        ks = pl.multiple_of(c * ck, ck)
        mask_ref[0, pl.ds(ks, ck), :] = jnp.zeros((ck, tq), mask_ref.dtype)
        return carry

    lax.fori_loop(nk, mask_ref.shape[1] // ck, zero_chunk, 0)


def indexer_mask(qi, small, k_even, k_odd, b, t, n_sel):
    tq = _tile(t, 512)
    ck = _tile(t, 512)
    return pl.pallas_call(
        functools.partial(_indexer_body, tq=tq, ck=ck, n_sel=n_sel),
        grid=(b, t // tq),
        in_specs=[
            pl.BlockSpec((1, tq, IDX_HEADS * IDX_DIM), lambda bb, i: (bb, i, 0)),
            pl.BlockSpec((1, tq, LANES), lambda bb, i: (bb, i, 0)),
            pl.BlockSpec((1, t, LANES), lambda bb, i: (bb, 0, 0)),
            pl.BlockSpec((1, t, LANES), lambda bb, i: (bb, 0, 0)),
        ],
        out_specs=pl.BlockSpec((1, t, tq), lambda bb, i: (bb, 0, i)),
        out_shape=jax.ShapeDtypeStruct((b, t, t), jnp.int8),
        scratch_shapes=[pltpu.VMEM((t, tq), jnp.int32), pltpu.VMEM((t, tq), jnp.int16),
                        pltpu.VMEM((t, tq), jnp.int16), pltpu.VMEM((IDX_HEADS, SUBLANES, tq), F32)],
        compiler_params=_params("parallel", "parallel"),
        name="indexer",
    )(qi.reshape(b, t, -1), small, k_even.reshape(b, t, LANES), k_odd.reshape(b, t, LANES))


ATTN_HEAD_GROUP = 4


def _attn_body(qi_ref, ki_ref, fix_ref, q_ref, k_ref, vt_ref, mask_ref, bias_ref, z_ref, m0_ref, o_ref, m_ref, acc_ref,
               *, tq):
    qi, ki = qi_ref[pl.program_id(1)], ki_ref[pl.program_id(1)]
    fixed = fix_ref[0] == 1

    @pl.when(ki == 0)
    def _():
        m_ref[...] = jnp.full(m_ref.shape, MASK_NEG, F32)
        acc_ref[...] = jnp.zeros(acc_ref.shape, F32)

    def tile(near, fixed_shift):
        madd = (1.0 - mask_ref[0].astype(F32)) * MASK_NEG
        if fixed_shift:
            madd = madd - m0_ref[0, 0]
        off = jnp.where(ki == qi, 0, 1)
        for g in range(0, A_HEADS, ATTN_HEAD_GROUP):
            hs = range(g, g + ATTN_HEAD_GROUP)
            grp = slice(g, g + ATTN_HEAD_GROUP)
            head = lambda ref, h: ref[0, :, h * A_HEAD_DIM:(h + 1) * A_HEAD_DIM]
            k2 = jnp.stack([head(k_ref, h) for h in hs])
            q2 = jnp.stack([head(q_ref, h) for h in hs])
            s = lax.dot_general(k2, q2, (((2,), (2,)), ((0,), (0,))), preferred_element_type=F32) + madd[None]
            if near:
                s = s + bias_ref[off, grp].astype(F32)
            vt2 = jnp.stack([vt_ref[h * V_ROWS:(h + 1) * V_ROWS, :] for h in hs])
            pv = lambda p: lax.dot_general(vt2, p, (((2,), (1,)), ((0,), (0,))), preferred_element_type=F32)
            if fixed_shift:
                acc_ref[grp] = acc_ref[grp] + pv(jnp.exp2(s).astype(BF16))
            else:
                m_prev = m_ref[grp]
                m_new = jnp.maximum(m_prev, jnp.max(s, axis=1, keepdims=True))
                acc_ref[grp] = jnp.exp2(m_prev - m_new) * acc_ref[grp] + pv(jnp.exp2(s - m_new).astype(BF16))
                m_ref[grp] = m_new

    far = ki + 1 < qi
    for near in (False, True):
        for fixed_shift in (False, True):
            cond = jnp.logical_and(jnp.logical_not(far) if near else far,
                                   fixed if fixed_shift else jnp.logical_not(fixed))
            pl.when(cond)(functools.partial(tile, near, fixed_shift))

    @pl.when(ki == qi)
    def _():
        outs = []
        for h in range(A_HEADS):
            a = acc_ref[h]
            outs.append((a[:A_HEAD_DIM] / a[A_HEAD_DIM:A_HEAD_DIM + 1]).T)
        o_ref[0] = (jnp.concatenate(outs, axis=-1) * _silu(z_ref[0].astype(F32))).astype(o_ref.dtype)


MAX_LOGIT_SPAN = 100.0


def softmax_shift(q_gain, k_gain, rel_bias):
    slack = 1.02
    qk = (A_HEAD_DIM ** 0.5) * LOG2E * jnp.max(jnp.abs(q_gain)) * jnp.max(jnp.abs(k_gain)) * slack
    b2 = (rel_bias - rel_bias[REL_BUCKETS - 1]) * (LOG2E * slack)
    shift = qk + jnp.max(b2)
    span = 2.0 * qk + jnp.max(b2) - jnp.min(b2)
    return shift.astype(F32), (span <= MAX_LOGIT_SPAN).astype(jnp.int32)


def attention(q, k, vt, mask, bias, proj3d, shift, use_shift, b, t, tq):
    nq = t // tq
    pairs = [(i, j) for i in range(nq) for j in range(i + 1)]
    qi_tab = jnp.asarray([p[0] for p in pairs], jnp.int32)
    ki_tab = jnp.asarray([p[1] for p in pairs], jnp.int32)
    grid_spec = pltpu.PrefetchScalarGridSpec(
        num_scalar_prefetch=3,
        grid=(b, len(pairs)),
        in_specs=[
            pl.BlockSpec((1, tq, A_WIDTH), lambda bb, p, qi, ki, fx: (bb, qi[p], 0)),
            pl.BlockSpec((1, tq, A_WIDTH), lambda bb, p, qi, ki, fx: (bb, ki[p], 0)),
            pl.BlockSpec((A_HEADS * V_ROWS, tq), lambda bb, p, qi, ki, fx: (0, bb * nq + ki[p])),
            pl.BlockSpec((1, tq, tq), lambda bb, p, qi, ki, fx: (bb, ki[p], qi[p])),
            pl.BlockSpec(bias.shape, lambda bb, p, qi, ki, fx: (0, 0, 0, 0)),
            pl.BlockSpec((1, tq, A_WIDTH), lambda bb, p, qi, ki, fx: (bb, qi[p], COL_Z_A // A_WIDTH)),
            pl.BlockSpec(memory_space=pltpu.SMEM),
        ],
        out_specs=pl.BlockSpec((1, tq, A_WIDTH), lambda bb, p, qi, ki, fx: (bb, qi[p], 0)),
        scratch_shapes=[pltpu.VMEM((A_HEADS, 1, tq), F32), pltpu.VMEM((A_HEADS, V_ROWS, tq), F32)],
    )
    return pl.pallas_call(
        functools.partial(_attn_body, tq=tq),
        grid_spec=grid_spec,
        out_shape=jax.ShapeDtypeStruct((b, t, A_WIDTH), BF16),
        compiler_params=_params("parallel", "arbitrary"),
        name="dsa_attention",
    )(qi_tab, ki_tab, use_shift.reshape(1), q.reshape(b, t, -1), k.reshape(b, t, -1), vt, mask, bias, proj3d,
      shift.reshape(1, 1))


def _gdn_prep_body(x_ref, halo_ref, sm_ref, cw_ref, alog_ref, dtb_ref,
                   q_ref, k_ref, v_ref, gcb_ref, bb_ref, grow_ref, *, tm):
    i = pl.program_id(1)
    x = x_ref[0]
    hal = halo_ref[0]
    hal = jnp.where(i > 0, hal, jnp.zeros_like(hal))
    rt = lax.broadcasted_iota(jnp.int32, (tm, tm), 0)
    ct = lax.broadcasted_iota(jnp.int32, (tm, tm), 1)
    rh = lax.broadcasted_iota(jnp.int32, (SUBLANES, BF16_ROWS), 0)
    ch = lax.broadcasted_iota(jnp.int32, (SUBLANES, BF16_ROWS), 1)
    y = cw_ref[B_CONV - 1:B_CONV, :] * x.astype(F32)
    for j in range(B_CONV - 1):
        back = B_CONV - 1 - j
        main = _dot(jnp.where(ct == rt - back, 1.0, 0.0).astype(BF16), x)
        head = _dot(jnp.where(ch == rh + (BF16_ROWS - back), 1.0, 0.0).astype(BF16), hal)
        y = y + cw_ref[j:j + 1, :] * jnp.concatenate([main[:SUBLANES] + head, main[SUBLANES:]], axis=0)
    y = _silu(y)
    for h in range(B_HEADS):
        sl = slice(h * LANES, (h + 1) * LANES)
        qh = y[:, sl]
        q_ref[0, :, sl] = qh * lax.rsqrt(jnp.sum(qh * qh, axis=-1, keepdims=True) + EPS) * (B_HEAD_DIM ** -0.5)
        kh = y[:, B_WIDTH + h * LANES:B_WIDTH + (h + 1) * LANES]
        k_ref[0, :, sl] = kh * lax.rsqrt(jnp.sum(kh * kh, axis=-1, keepdims=True) + EPS)
    v_ref[0] = y[:, 2 * B_WIDTH:]

    sm = sm_ref[0]
    xg = sm + dtb_ref[...]
    softplus = jnp.maximum(xg, 0.0) + jnp.log(1.0 + jnp.exp(-jnp.abs(xg)))
    g = -jnp.exp(alog_ref[...]) * softplus
    r = lax.broadcasted_iota(jnp.int32, (tm, tm), 0)
    c = lax.broadcasted_iota(jnp.int32, (tm, tm), 1)
    sh = int(math.log2(GDN_CHUNK))
    same_chunk = jnp.right_shift(r, sh) == jnp.right_shift(c, sh)
    tri = jnp.where(jnp.logical_and(same_chunk, c <= r), 1.0, 0.0).astype(F32)
    gc = lax.dot_general(tri, g, (((1,), (0,)), ((), ())), precision=HIGHEST, preferred_element_type=F32)
    beta = _sigmoid(sm)
    for h in range(B_HEADS):
        sl = slice(h * LANES, (h + 1) * LANES)
        gcb_ref[0, :, sl] = jnp.broadcast_to(gc[:, SM_ALPHA + h:SM_ALPHA + h + 1], (tm, LANES))
        bb_ref[0, :, sl] = jnp.broadcast_to(beta[:, SM_BETA + h:SM_BETA + h + 1], (tm, LANES))
    gct = gc.T
    for cc in range(tm // GDN_CHUNK):
        grow_ref[0, cc] = gct[SM_ALPHA:SM_ALPHA + B_HEADS, cc * GDN_CHUNK:(cc + 1) * GDN_CHUNK]


def gdn_prep(proj3d, small3d, conv_w, a_log, dt_bias, b, t):
    tm = _tile(t, 256)
    pad_row = lambda v: jnp.zeros((1, LANES), F32).at[0, SM_ALPHA:SM_ALPHA + B_HEADS].set(v)
    nc = tm // GDN_CHUNK
    act = jax.ShapeDtypeStruct((b, t, B_WIDTH), F32)
    full = lambda a: pl.BlockSpec(a.shape, lambda bb, i: (0,) * a.ndim)
    consts = [conv_w, pad_row(a_log), pad_row(dt_bias)]
    return pl.pallas_call(
        functools.partial(_gdn_prep_body, tm=tm),
        grid=(b, t // tm),
        in_specs=[
            pl.BlockSpec((1, tm, 3 * B_WIDTH), lambda bb, i: (bb, i, COL_QKV_B // (3 * B_WIDTH))),
            pl.BlockSpec((1, BF16_ROWS, 3 * B_WIDTH),
                         lambda bb, i: (bb, jnp.maximum(i * (tm // BF16_ROWS) - 1, 0), 0)),
            pl.BlockSpec((1, tm, LANES), lambda bb, i: (bb, i, 0)),
        ] + [full(c) for c in consts],
        out_specs=[pl.BlockSpec((1, tm, B_WIDTH), lambda bb, i: (bb, i, 0))] * 5
        + [pl.BlockSpec((1, nc, B_HEADS, GDN_CHUNK), lambda bb, i: (bb, i, 0, 0))],
        out_shape=[act] * 5 + [jax.ShapeDtypeStruct((b, t // GDN_CHUNK, B_HEADS, GDN_CHUNK), F32)],
        compiler_params=_params("parallel", "parallel"),
        name="gdn_prep",
    )(proj3d, proj3d, small3d, *consts)


def _gdn_body(q_ref, k_ref, v_ref, gcb_ref, bb_ref, grow_ref, z_ref, on_ref, o_ref, s_ref, *, nc):
    @pl.when(pl.program_id(1) == 0)
    def _():
        s_ref[...] = jnp.zeros(s_ref.shape, F32)

    cs, nh = GDN_CHUNK, B_HEADS
    nb = nc * nh
    ri = lax.broadcasted_iota(jnp.int32, (nb, cs, cs), 1)
    ci = lax.broadcasted_iota(jnp.int32, (nb, cs, cs), 2)
    lower = ci <= ri
    strict = ci < ri
    eye = jnp.where(ci == ri, 1.0, 0.0).astype(F32)
    bf = lambda a: a.astype(BF16)
    bmm = lambda a, b: lax.dot_general(a, b, (((2,), (1,)), ((0,), (0,))), preferred_element_type=F32)
    bmm_nt = lambda a, b: lax.dot_general(a, b, (((2,), (2,)), ((0,), (0,))), preferred_element_type=F32)
    bmm_tn = lambda a, b: lax.dot_general(a, b, (((1,), (1,)), ((0,), (0,))), preferred_element_type=F32)

    def stack(ref):
        return jnp.stack([ref[0, c * cs:(c + 1) * cs, h * LANES:(h + 1) * LANES]
                          for c in range(nc) for h in range(nh)])

    q, k, v = stack(q_ref), stack(k_ref), stack(v_ref)
    gcb = stack(gcb_ref)
    beta = stack(bb_ref)
    grow = jnp.stack([grow_ref[0, c, h:h + 1, :] for c in range(nc) for h in range(nh)])
    diff = gcb[:, :, :cs] - grow
    decay = jnp.where(lower, jnp.exp(jnp.where(lower, diff, 0.0)), 0.0)
    eg = jnp.exp(gcb)
    glast = gcb[:, cs - 1:cs, :]
    kb = k * beta
    lmat = jnp.where(strict, bmm_nt(bf(kb), bf(k)) * decay, 0.0)
    n = -lmat
    tinv = eye + n
    for _ in range(int(math.log2(cs)) - 1):
        n = bmm(bf(n), bf(n))
        tinv = tinv + bmm(bf(tinv), bf(n))
    uw = bmm(bf(tinv), bf(jnp.concatenate([v * beta, kb * eg], axis=-1)))
    attn = bf(jnp.where(lower, bmm_nt(bf(q), bf(k)) * decay, 0.0))
    qg = bf(q * eg)
    kdec = bf(k * jnp.exp(glast - gcb))
    egl = jnp.exp(glast)

    s = s_ref[...]
    for c in range(nc):
        sl = slice(c * nh, (c + 1) * nh)
        sb = bf(s)
        v_new = uw[sl, :, :LANES] - bmm(bf(uw[sl, :, LANES:]), sb)
        o = bmm(qg[sl], sb) + bmm(attn[sl], bf(v_new))
        s = s * egl[sl] + bmm_tn(kdec[sl], bf(v_new))
        o = o * lax.rsqrt(jnp.mean(o * o, axis=-1, keepdims=True) + EPS) * on_ref[...]
        for h in range(nh):
            rows, cols = slice(c * cs, (c + 1) * cs), slice(h * LANES, (h + 1) * LANES)
            o_ref[0, rows, cols] = (o[h] * _silu(z_ref[0, rows, cols].astype(F32))).astype(o_ref.dtype)
    s_ref[...] = s


def gdn_scan(qh, kh, v, gcb, bb, grow, proj3d, o_norm, b, t):
    tt = _tile(t, 256)
    nc = tt // GDN_CHUNK
    blk = pl.BlockSpec((1, tt, B_WIDTH), lambda bb_, i: (bb_, i, 0))
    return pl.pallas_call(
        functools.partial(_gdn_body, nc=nc),
        grid=(b, t // tt),
        in_specs=[blk] * 5 + [
            pl.BlockSpec((1, nc, B_HEADS, GDN_CHUNK), lambda bb_, i: (bb_, i, 0, 0)),
            pl.BlockSpec((1, tt, B_WIDTH), lambda bb_, i: (bb_, i, COL_Z_B // B_WIDTH)),
            pl.BlockSpec((1, LANES), lambda bb_, i: (0, 0)),
        ],
        out_specs=blk,
        out_shape=jax.ShapeDtypeStruct((b, t, B_WIDTH), BF16),
        scratch_shapes=[pltpu.VMEM((B_HEADS, B_HEAD_DIM, B_HEAD_DIM), F32)],
        compiler_params=_params("parallel", "arbitrary"),
        name="gdn_scan",
    )(qh, kh, v, gcb, bb, grow, proj3d, o_norm.reshape(1, LANES))


C_HALO = 32
D_HALO = BF16_ROWS


def _cd_mix_body(a_ref, g_ref, ah_ref, gh_ref, zc_ref, bg_ref, cg_ref, ud_ref, cgh_ref, udh_ref, zd_ref,
                 dww_ref, dwb_ref, lnw_ref, lnb_ref, dcw_ref, yc_ref, yd_ref, us_ref, ds_ref, *, tm):
    i = pl.program_id(1)
    first = i == 0
    f32 = lambda r: r[0].astype(F32)
    uh = f32(ah_ref) * _sigmoid(f32(gh_ref))
    us_ref[0, 0:C_HALO, :] = jnp.where(first, jnp.zeros_like(uh), uh)
    us_ref[0, C_HALO:, :] = f32(a_ref) * _sigmoid(f32(g_ref))
    span = tm + C_HALO - SUBLANES
    for r in range(1, SUBLANES):
        us_ref[r, 0:span, :] = us_ref[0, pl.ds(r, span), :]
    u = jnp.zeros((tm, C_WIDTH), F32)
    for j in range(C_CONV):
        off = C_HALO - (C_CONV - 1) + j
        r, base = off % SUBLANES, off - off % SUBLANES
        u = u + dww_ref[j:j + 1, :] * us_ref[r, base:base + tm, :]
    u = u + dwb_ref[...]
    mu = jnp.mean(u, axis=-1, keepdims=True)
    uc = u - mu
    u = uc * lax.rsqrt(jnp.mean(uc * uc, axis=-1, keepdims=True) + EPS) * lnw_ref[...] + lnb_ref[...]
    yc_ref[0] = (_silu(u) * _silu(f32(zc_ref))).astype(yc_ref.dtype)

    dh = f32(cgh_ref) * f32(udh_ref)
    ds_ref[0:D_HALO, :] = jnp.where(first, jnp.zeros_like(dh), dh)
    ds_ref[D_HALO:, :] = f32(cg_ref) * f32(ud_ref)
    d = jnp.zeros((tm, D_WIDTH), F32)
    for j in range(D_CONV):
        d = d + dcw_ref[j:j + 1, :] * ds_ref[pl.ds(D_HALO - (D_CONV - 1) + j, tm), :]
    yd_ref[0] = (f32(bg_ref) * d * _silu(f32(zd_ref))).astype(yd_ref.dtype)


def cd_mix(proj3d, dw_w, dw_b, ln_w, ln_b, d_conv_w, b, t):
    tm = _tile(t, 256)
    w = C_WIDTH
    col = lambda n: pl.BlockSpec((1, tm, w), lambda bb, i, n=n: (bb, i, n))
    halo = lambda n, rows: pl.BlockSpec(
        (1, rows, w), lambda bb, i, n=n, rows=rows: (bb, jnp.maximum(i * (tm // rows) - 1, 0), n))
    row = lambda a: a.reshape(1, -1)
    full = lambda a: pl.BlockSpec(a.shape, lambda bb, i: (0,) * a.ndim)
    consts = [dw_w, row(dw_b), row(ln_w), row(ln_b), d_conv_w]
    out = jax.ShapeDtypeStruct((b, t, w), BF16)
    return pl.pallas_call(
        functools.partial(_cd_mix_body, tm=tm),
        grid=(b, t // tm),
        in_specs=[col(0), col(1), halo(0, C_HALO), halo(1, C_HALO), col(2), col(3), col(4), col(5),
                  halo(4, D_HALO), halo(5, D_HALO), col(6)] + [full(c) for c in consts],
        out_specs=[pl.BlockSpec((1, tm, w), lambda bb, i: (bb, i, 0))] * 2,
        out_shape=[out, out],
        scratch_shapes=[pltpu.VMEM((SUBLANES, tm + C_HALO, w), F32), pltpu.VMEM((tm + D_HALO, w), F32)],
        compiler_params=_params("parallel", "parallel"),
        name="cd_mix",
    )(*([proj3d] * 11), *consts)


def _reorder_ab_w_in(w):
    offs = [0]
    for s in AB_SPLITS:
        offs.append(offs[-1] + s)
    part = lambda n: w[:, offs[n]:offs[n + 1]]
    c_q, c_kv, k_idx, w_idx, z_a, qkv_b, beta_b, alpha_b, z_b = (part(n) for n in range(9))
    small = jnp.concatenate([k_idx, w_idx, beta_b, alpha_b], axis=1)
    small = jnp.pad(small, ((0, 0), (0, LANES - small.shape[1])))
    out = jnp.concatenate([qkv_b, z_a, z_b, c_q, c_kv], axis=1)
    return jnp.pad(out, ((0, 0), (0, AB_IN_PAD - out.shape[1]))).astype(BF16), small.astype(BF16)


def _ab_layer(x2d, b, t, norm_w, rel_bias, w_in, q_norm, w_uq, w_iq, kv_norm, w_uk, w_uv, q_gain, k_gain,
              ik_w, ik_b, conv_w, a_log, dt_bias, o_norm, w_out):
    proj, small = norm_matmul(x2d, norm_w, *_reorder_ab_w_in(w_in))
    proj3d, small3d = proj.reshape(b, t, AB_IN_PAD), small.reshape(b, t, LANES)
    q, k, v, qi, k_even, k_odd = dsa_prep(proj, small, q_norm, kv_norm, w_uq, w_iq, w_uk, w_uv, q_gain, k_gain,
                                          ik_w, ik_b)
    n_sel = min(TOPK_MAX, t // 4)
    mask = indexer_mask(qi, small3d, k_even, k_odd, b, t, n_sel)
    tq = _tile(t, 512)
    shift, use_shift = softmax_shift(q_gain, k_gain, rel_bias)
    y_a = attention(q, k, v, mask, bias_tiles(rel_bias, tq), proj3d, shift, use_shift, b, t, tq)
    qh, kh, vv, gcb, bb, grow = gdn_prep(proj3d, small3d, conv_w, a_log, dt_bias, b, t)
    y_b = gdn_scan(qh, kh, vv, gcb, bb, grow, proj3d, o_norm, b, t)
    wo = w_out.astype(BF16)
    return out_proj(x2d, y_a.reshape(b * t, -1), y_b.reshape(b * t, -1), wo[:A_WIDTH], wo[A_WIDTH:])


def _cd_layer(x2d, b, t, norm_w, w_in, dw_w, dw_b, ln_w, ln_b, d_conv_w, w_out):
    proj = norm_matmul(x2d, norm_w, w_in.astype(BF16))
    y_c, y_d = cd_mix(proj.reshape(b, t, -1), dw_w, dw_b, ln_w, ln_b, d_conv_w, b, t)
    wo = w_out.astype(BF16)
    return out_proj(x2d, y_c.reshape(b * t, -1), y_d.reshape(b * t, -1), wo[:C_WIDTH], wo[C_WIDTH:])


def kernel(x, norm_w, rel_bias, ab_w_in, a_q_norm, a_w_uq, a_w_iq, a_kv_norm, a_w_uk, a_w_uv, a_q_gain,
           a_k_gain, a_ik_norm_w, a_ik_norm_b, b_conv_w, b_a_log, b_dt_bias, b_o_norm, ab_w_out, cd_w_in,
           c_dw_w, c_dw_b, c_ln_w, c_ln_b, d_conv_w, cd_w_out):
    b, t, d = x.shape
    depth = norm_w.shape[0]
    x2d = x.reshape(b * t, d)
    for i in range(depth):
        j = i // 2
        if i % 2 == 0:
            x2d = _ab_layer(x2d, b, t, norm_w[i], rel_bias, ab_w_in[j], a_q_norm[j], a_w_uq[j], a_w_iq[j],
                            a_kv_norm[j], a_w_uk[j], a_w_uv[j], a_q_gain[j], a_k_gain[j], a_ik_norm_w[j],
                            a_ik_norm_b[j], b_conv_w[j], b_a_log[j], b_dt_bias[j], b_o_norm[j], ab_w_out[j])
        else:
            x2d = _cd_layer(x2d, b, t, norm_w[i], cd_w_in[j], c_dw_w[j], c_dw_b[j], c_ln_w[j], c_ln_b[j],
                            d_conv_w[j], cd_w_out[j])
    return x2d.reshape(b, t, d)
```

```python
import functools
import math

import jax
import jax.numpy as jnp
from jax import lax
from jax.experimental import pallas as pl
from jax.experimental.pallas import tpu as pltpu

A_HEADS = 8
A_HEAD_DIM = 128
A_WIDTH = A_HEADS * A_HEAD_DIM
A_Q_LORA = 512
A_KV_LORA = 256
IDX_HEADS = 16
IDX_DIM = 64
TOPK_MAX = 256
REL_BUCKETS = 32
REL_MAX_DIST = 128
B_HEADS = 8
B_HEAD_DIM = 128
B_WIDTH = B_HEADS * B_HEAD_DIM
B_CONV = 4
GDN_CHUNK = 64
C_WIDTH = 1024
C_CONV = 31
D_WIDTH = 1024
D_CONV = 3
EPS = 1e-6

AB_SPLITS = (A_Q_LORA, A_KV_LORA, IDX_DIM, IDX_HEADS, A_WIDTH, 3 * B_WIDTH, B_HEADS, B_HEADS, B_WIDTH)
AB_IN_PAD = 6144
SM_KIDX = 0
SM_WIDX = 64
SM_BETA = 80
SM_ALPHA = 88
COL_QKV_B = 0
COL_Z_A = 3072
COL_Z_B = 4096
COL_C_Q = 5120
COL_C_KV = 5632

LANES = 128
MASK_NEG = -1e30
INT_MIN = -(2 ** 31)
LOG2E = math.log2(math.e)
BF16_ROWS = 16
V_ROWS = A_HEAD_DIM + BF16_ROWS
VMEM_LIMIT = 56 * 1024 * 1024

F32 = jnp.float32
BF16 = jnp.bfloat16
HIGHEST = lax.Precision.HIGHEST


def _tile(n, pref):
    t = min(n, pref)
    assert n % t == 0, (n, t)
    return t


def _params(*sem):
    return pltpu.CompilerParams(dimension_semantics=sem, vmem_limit_bytes=VMEM_LIMIT)


def _dot(a, b):
    return jnp.dot(a, b, preferred_element_type=F32)


def _dot_nt(a, b):
    return lax.dot_general(a, b, (((1,), (1,)), ((), ())), preferred_element_type=F32)


def _silu(x):
    return x * (1.0 / (1.0 + jnp.exp(-x)))


def _sigmoid(x):
    return 1.0 / (1.0 + jnp.exp(-x))


def _norm_matmul_body(*refs, with_f32_cols):
    if with_f32_cols:
        x_ref, nw_ref, w_ref, ws_ref, o_ref, os_ref, h_ref = refs
    else:
        x_ref, nw_ref, w_ref, o_ref, h_ref = refs

    @pl.when(pl.program_id(1) == 0)
    def _():
        x = x_ref[...]
        ms = jnp.mean(x * x, axis=-1, keepdims=True)
        h_ref[...] = (x * lax.rsqrt(ms + EPS) * nw_ref[...]).astype(h_ref.dtype)
        if with_f32_cols:
            os_ref[...] = _dot(h_ref[...], ws_ref[...])

    o_ref[...] = _dot(h_ref[...], w_ref[...]).astype(o_ref.dtype)


def norm_matmul(x2d, norm_w, w_bf16, w_f32_cols=None):
    m, k = x2d.shape
    n = w_bf16.shape[1]
    tm = _tile(m, 1024)
    tn = max(d for d in range(256, min(n, 2048) + 1, 256) if n % d == 0)
    extra = w_f32_cols is not None
    in_specs = [
        pl.BlockSpec((tm, k), lambda i, j: (i, 0)),
        pl.BlockSpec((1, k), lambda i, j: (0, 0)),
        pl.BlockSpec((k, tn), lambda i, j: (0, j)),
    ]
    out_specs = [pl.BlockSpec((tm, tn), lambda i, j: (i, j))]
    out_shape = [jax.ShapeDtypeStruct((m, n), BF16)]
    args = [x2d, norm_w.reshape(1, k), w_bf16]
    if extra:
        in_specs.append(pl.BlockSpec((k, LANES), lambda i, j: (0, 0)))
        out_specs.append(pl.BlockSpec((tm, LANES), lambda i, j: (i, 0)))
        out_shape.append(jax.ShapeDtypeStruct((m, LANES), F32))
        args.append(w_f32_cols)
    res = pl.pallas_call(
        functools.partial(_norm_matmul_body, with_f32_cols=extra),
        grid=(m // tm, n // tn),
        in_specs=in_specs,
        out_specs=out_specs,
        out_shape=out_shape,
        scratch_shapes=[pltpu.VMEM((tm, k), BF16)],
        compiler_params=_params("parallel", "arbitrary"),
        name="norm_matmul",
    )(*args)
    return res if extra else res[0]


def _out_proj_body(x_ref, ya_ref, yb_ref, wa_ref, wb_ref, o_ref):
    o_ref[...] = x_ref[...] + _dot(ya_ref[...], wa_ref[...]) + _dot(yb_ref[...], wb_ref[...])


def out_proj(x2d, ya, yb, wa_bf16, wb_bf16):
    m, n = x2d.shape
    ka, kb = ya.shape[1], yb.shape[1]
    tm = _tile(m, 512)
    return pl.pallas_call(
        _out_proj_body,
        grid=(m // tm,),
        in_specs=[
            pl.BlockSpec((tm, n), lambda i: (i, 0)),
            pl.BlockSpec((tm, ka), lambda i: (i, 0)),
            pl.BlockSpec((tm, kb), lambda i: (i, 0)),
            pl.BlockSpec((ka, n), lambda i: (0, 0)),
            pl.BlockSpec((kb, n), lambda i: (0, 0)),
        ],
        out_specs=pl.BlockSpec((tm, n), lambda i: (i, 0)),
        out_shape=jax.ShapeDtypeStruct((m, n), F32),
        compiler_params=_params("parallel"),
        name="out_proj",
    )(x2d, ya, yb, wa_bf16, wb_bf16)


def _head_rmsnorm(y, gain_row, scale):
    outs = []
    for h in range(y.shape[1] // LANES):
        seg = y[:, h * LANES:(h + 1) * LANES]
        ms = jnp.mean(seg * seg, axis=-1, keepdims=True)
        outs.append(seg * lax.rsqrt(ms + EPS) * (gain_row * scale))
    return jnp.concatenate(outs, axis=-1)


def _dsa_prep_body(cq_ref, ckv_ref, sm_ref, qn_ref, kvn_ref, wuq_ref, wiq_ref, wuk_ref, wuv_ref,
                   qg_ref, kg_ref, ikw_ref, ikb_ref,
                   q_ref, k_ref, v_ref, qi_ref, ke_ref, ko_ref):
    cq = cq_ref[...].astype(F32)
    cq = cq * lax.rsqrt(jnp.mean(cq * cq, axis=-1, keepdims=True) + EPS) * qn_ref[...]
    cqb = cq.astype(BF16)
    q = _dot(cqb, wuq_ref[...])
    q_ref[...] = _head_rmsnorm(q, qg_ref[...], A_HEAD_DIM ** -0.5 * LOG2E).astype(q_ref.dtype)
    qi_ref[...] = _dot(cqb, wiq_ref[...]).astype(qi_ref.dtype)

    ckv = ckv_ref[...].astype(F32)
    ckv = ckv * lax.rsqrt(jnp.mean(ckv * ckv, axis=-1, keepdims=True) + EPS) * kvn_ref[...]
    ckvb = ckv.astype(BF16)
    k = _dot(ckvb, wuk_ref[...])
    k_ref[...] = _head_rmsnorm(k, kg_ref[...], 1.0).astype(k_ref.dtype)
    vt = _dot_nt(wuv_ref[...], ckvb).astype(v_ref.dtype)
    ones = jnp.ones((BF16_ROWS, vt.shape[1]), v_ref.dtype)
    for h in range(A_HEADS):
        v_ref[h * V_ROWS:h * V_ROWS + A_HEAD_DIM, :] = vt[h * A_HEAD_DIM:(h + 1) * A_HEAD_DIM, :]
        v_ref[h * V_ROWS + A_HEAD_DIM:(h + 1) * V_ROWS, :] = ones

    kx = sm_ref[...][:, SM_KIDX:SM_KIDX + IDX_DIM]
    mu = jnp.mean(kx, axis=-1, keepdims=True)
    kc = kx - mu
    kl = kc * lax.rsqrt(jnp.mean(kc * kc, axis=-1, keepdims=True) + EPS) * ikw_ref[...] + ikb_ref[...]
    z = jnp.zeros_like(kl)
    ke_ref[...] = jnp.concatenate([kl, z], axis=-1).astype(ke_ref.dtype)
    ko_ref[...] = jnp.concatenate([z, kl], axis=-1).astype(ko_ref.dtype)


def dsa_prep(proj, small, q_norm, kv_norm, w_uq, w_iq, w_uk, w_uv, q_gain, k_gain, ik_w, ik_b):
    m = proj.shape[0]
    tm = _tile(m, 512)
    row = lambda a: a.reshape(1, -1)
    full = lambda a: pl.BlockSpec(a.shape, lambda i: (0,) * a.ndim)
    consts = [row(q_norm), row(kv_norm), w_uq.astype(BF16), w_iq.astype(BF16), w_uk.astype(BF16),
              w_uv.T.astype(BF16), row(q_gain), row(k_gain), row(ik_w), row(ik_b)]
    outs = [jax.ShapeDtypeStruct((m, A_WIDTH), BF16)] * 2 + [
        jax.ShapeDtypeStruct((A_HEADS * V_ROWS, m), BF16),
        jax.ShapeDtypeStruct((m, IDX_HEADS * IDX_DIM), BF16),
        jax.ShapeDtypeStruct((m, LANES), BF16), jax.ShapeDtypeStruct((m, LANES), BF16)]
    out_specs = [pl.BlockSpec((tm, s.shape[1]), lambda i: (i, 0)) for s in outs]
    out_specs[2] = pl.BlockSpec((A_HEADS * V_ROWS, tm), lambda i: (0, i))
    return pl.pallas_call(
        _dsa_prep_body,
        grid=(m // tm,),
        in_specs=[
            pl.BlockSpec((tm, A_Q_LORA), lambda i: (i, COL_C_Q // A_Q_LORA)),
            pl.BlockSpec((tm, A_KV_LORA), lambda i: (i, COL_C_KV // A_KV_LORA)),
            pl.BlockSpec((tm, LANES), lambda i: (i, 0)),
        ] + [full(c) for c in consts],
        out_specs=out_specs,
        out_shape=outs,
        compiler_params=_params("parallel"),
        name="dsa_prep",
    )(proj, proj, small, *consts)


def _t5_bucket(dist):
    max_exact = REL_BUCKETS // 2
    large = max_exact + (jnp.log(jnp.maximum(dist, 1).astype(F32) / max_exact)
                         / math.log(REL_MAX_DIST / max_exact) * (REL_BUCKETS - max_exact)).astype(jnp.int32)
    large = jnp.minimum(large, REL_BUCKETS - 1)
    return jnp.where(dist < max_exact, dist, large)


def _bias_tiles_body(rb_ref, o_ref, *, t):
    h = pl.program_id(0)
    off = pl.program_id(1) * t
    dist = off + lax.broadcasted_iota(jnp.int32, (t, t), 1) - lax.broadcasted_iota(jnp.int32, (t, t), 0)
    bucket = _t5_bucket(jnp.maximum(dist, 0))
    acc = jnp.zeros((t, t), F32)
    for b in range(REL_BUCKETS):
        acc = jnp.where(bucket == b, rb_ref[b, h], acc)
    o_ref[0, 0] = ((acc - rb_ref[REL_BUCKETS - 1, h]) * LOG2E).astype(o_ref.dtype)


def bias_tiles(rel_bias, t):
    return pl.pallas_call(
        functools.partial(_bias_tiles_body, t=t),
        grid=(A_HEADS, 2),
        in_specs=[pl.BlockSpec(memory_space=pltpu.SMEM)],
        out_specs=pl.BlockSpec((1, 1, t, t), lambda h, c: (c, h, 0, 0)),
        out_shape=jax.ShapeDtypeStruct((2, A_HEADS, t, t), BF16),
        compiler_params=_params("parallel", "parallel"),
        name="bias_tiles",
    )(rel_bias)


SUBLANES = 8
MAX_SELECT_STEPS = 24
UPPER_STEPS_PER_CHECK = 4
LOWER_STEPS_PER_CHECK = 2
HALF_BITS = 16
HALF_SPAN = 2 ** HALF_BITS
I16_MIN = -(2 ** (HALF_BITS - 1))


def _key_of(x):
    bits = pltpu.bitcast(x, jnp.int32)
    key = jnp.where(bits < 0, bits ^ jnp.int32(0x7FFFFFFF), bits)
    return jnp.where(bits == jnp.int32(INT_MIN), 0, key)


def _indexer_body(qi_ref, sm_ref, ke_ref, ko_ref, mask_ref, key_ref, hi_ref, low_ref, w_ref, *, tq, ck, n_sel):
    i = pl.program_id(1)
    nk = ((i + 1) * tq + ck - 1) // ck
    groups = ck // SUBLANES
    scale = (IDX_HEADS ** -0.5) * (IDX_DIM ** -0.5)
    smt = sm_ref[0].T
    for h in range(IDX_HEADS):
        w_ref[h] = jnp.broadcast_to(smt[SM_WIDX + h:SM_WIDX + h + 1, :] * scale, (SUBLANES, tq))
    kloc = lax.broadcasted_iota(jnp.int32, (ck, tq), 0)
    qpos = i * tq + lax.broadcasted_iota(jnp.int32, (ck, tq), 1)
    int_max = jnp.int32(2 ** 31 - 1)

    def score_chunk(c, carry):
        kmin, kmax = carry
        ks = pl.multiple_of(c * ck, ck)
        ke = ke_ref[0, pl.ds(ks, ck), :]
        ko = ko_ref[0, pl.ds(ks, ck), :]
        acc = jnp.zeros((ck, tq), F32)
        for j in range(IDX_HEADS // 2):
            qp = qi_ref[0, :, j * LANES:(j + 1) * LANES]
            for par, kk in ((0, ke), (1, ko)):
                s = jnp.maximum(_dot_nt(kk, qp), 0.0)
                acc = acc + s * jnp.tile(w_ref[2 * j + par], (groups, 1))
        key = _key_of(acc)
        valid = c * ck + kloc <= qpos
        masked = jnp.where(valid, key, jnp.int32(INT_MIN))
        key_ref[pl.ds(ks, ck), :] = masked
        hi_ref[pl.ds(ks, ck), :] = jnp.right_shift(masked, HALF_BITS).astype(jnp.int16)
        kmin = jnp.minimum(kmin, jnp.min(jnp.where(valid, key, int_max).reshape(groups, SUBLANES, tq), axis=0))
        kmax = jnp.maximum(kmax, jnp.max(masked.reshape(groups, SUBLANES, tq), axis=0))
        return kmin, kmax

    kmin, kmax = lax.fori_loop(0, nk, score_chunk, (jnp.full((SUBLANES, tq), int_max, jnp.int32),
                                                    jnp.full((SUBLANES, tq), INT_MIN, jnp.int32)))

    rows16 = ck // BF16_ROWS
    i16_min = I16_MIN

    def scan16(a_ref, init, fn):
        def body(c, acc):
            ks = pl.multiple_of(c * ck, ck)
            return fn(acc, a_ref[pl.ds(ks, ck), :].reshape(rows16, BF16_ROWS, tq))

        return lax.fori_loop(0, nk, body, init)

    def count_ge16(a_ref, p):
        p16 = jnp.broadcast_to(p.astype(jnp.int16), (BF16_ROWS, tq))

        def fn(cnt, a):
            ind = jnp.where(a >= p16[None], jnp.int16(1), jnp.int16(0))
            for g in range(rows16):
                cnt = cnt + ind[g]
            return cnt

        cnt = scan16(a_ref, jnp.zeros((BF16_ROWS, tq), jnp.int16), fn)
        return jnp.sum(cnt.astype(jnp.int32), axis=0, keepdims=True)

    def max_le16(a_ref, h):
        h16 = jnp.broadcast_to(h.astype(jnp.int16), (BF16_ROWS, tq))

        def fn(acc, a):
            v = jnp.where(a <= h16[None], a, jnp.int16(i16_min))
            for g in range(rows16):
                acc = jnp.where(v[g] > acc, v[g], acc)
            return acc

        acc = scan16(a_ref, jnp.full((BF16_ROWS, tq), i16_min, jnp.int16), fn)
        return jnp.max(acc.astype(jnp.int32), axis=0, keepdims=True)

    def rank_select16(a_ref, r, skip, lo, hi, clo, chi, floor_count, steps_per_check):
        def closed(lo, hi, clo):
            return skip | (clo == r) | (hi == lo + 1)

        def open_rows(lo, hi, clo, chi):
            return jnp.logical_not(closed(lo, hi, clo) | (chi == r - 1))

        def cond(st):
            it, lo, hi, clo, chi = st
            n_open = jnp.max(jnp.where(open_rows(lo, hi, clo, chi), 1, 0))
            return jnp.logical_and(it < MAX_SELECT_STEPS, n_open > 0)

        def step(st):
            it, lo, hi, clo, chi = st
            for _ in range(steps_per_check):
                upd = open_rows(lo, hi, clo, chi)
                p = lo + jnp.maximum(jnp.right_shift(hi - lo, 1), 1)
                cnt = count_ge16(a_ref, p)
                ge = cnt >= r
                up, dn = upd & ge, upd & jnp.logical_not(ge)
                lo, hi = jnp.where(up, p, lo), jnp.where(dn, p, hi)
                clo, chi = jnp.where(up, cnt, clo), jnp.where(dn, cnt, chi)
            return it + steps_per_check, lo, hi, clo, chi

        _, lo, hi, clo, chi = lax.while_loop(cond, step, (jnp.int32(0), lo, hi, clo, chi))
        last = jnp.logical_not(closed(lo, hi, clo))
        t_last = max_le16(a_ref, hi - 1)
        c_last = jnp.where(t_last == i16_min, floor_count, count_ge16(a_ref, t_last))
        return jnp.where(last, t_last, lo), jnp.where(last, c_last, clo), chi

    n_valid = i * tq + lax.broadcasted_iota(jnp.int32, (1, tq), 1) + 1
    few = n_valid <= n_sel
    zeros = jnp.zeros((1, tq), jnp.int32)
    hi_lo0 = jnp.right_shift(jnp.min(kmin, axis=0, keepdims=True), HALF_BITS)
    hi_hi0 = jnp.right_shift(jnp.max(kmax, axis=0, keepdims=True), HALF_BITS) + 1
    t_hi, c_ge_hi, c_gt_hi = rank_select16(hi_ref, n_sel, few, hi_lo0, hi_hi0, n_valid, zeros, n_valid,
                                           UPPER_STEPS_PER_CHECK)
    exact_hi = c_ge_hi == n_sel

    t_hi16 = jnp.broadcast_to(t_hi.astype(jnp.int16), (BF16_ROWS, tq))

    def fill_low(c, carry):
        ks = pl.multiple_of(c * ck, ck)
        low = (key_ref[pl.ds(ks, ck), :] & (HALF_SPAN - 1)) + I16_MIN
        low = low.astype(jnp.int16).reshape(rows16, BF16_ROWS, tq)
        same = hi_ref[pl.ds(ks, ck), :].reshape(rows16, BF16_ROWS, tq) == t_hi16[None]
        low_ref[pl.ds(ks, ck), :] = jnp.where(same, low, jnp.int16(i16_min)).reshape(ck, tq)
        return carry

    lax.fori_loop(0, nk, fill_low, 0)
    in_bucket = c_ge_hi - c_gt_hi
    t_lo, c_ge_lo, c_gt_lo = rank_select16(low_ref, n_sel - c_gt_hi, few | exact_hi,
                                           jnp.full((1, tq), I16_MIN, jnp.int32), jnp.full((1, tq), -I16_MIN, jnp.int32),
                                           in_bucket, zeros, in_bucket, LOWER_STEPS_PER_CHECK)
    lo = jnp.where(exact_hi, t_hi * HALF_SPAN, t_hi * HALF_SPAN + (t_lo - I16_MIN))
    clo = jnp.where(exact_hi, n_sel, c_gt_hi + c_ge_lo)
    chi = c_gt_hi + c_gt_lo
    thr = jnp.where(few, jnp.int32(INT_MIN + 1), lo)
    thr8 = jnp.broadcast_to(thr, (SUBLANES, tq))
    tied = jnp.logical_not(few) & (clo > n_sel)
    need = n_sel - chi
    n_keys = mask_ref.shape[1]

    def tie_cut(_):
        def count_le(j):
            j8 = jnp.broadcast_to(j, (SUBLANES, tq))

            def body(c, cnt):
                ks = pl.multiple_of(c * ck, ck)
                kk = key_ref[pl.ds(ks, ck), :].reshape(groups, SUBLANES, tq)
                idx = (c * ck + kloc).reshape(groups, SUBLANES, tq)
                hit = jnp.where(kk == thr8[None], jnp.where(idx <= j8[None], 1, 0), 0)
                return cnt + jnp.sum(hit, axis=0)

            cnt = lax.fori_loop(0, nk, body, jnp.zeros((SUBLANES, tq), jnp.int32))
            return jnp.sum(cnt, axis=0, keepdims=True)

        def bisect(_, st):
            jlo, jhi = st
            mid = jlo + jnp.right_shift(jhi - jlo, 1)
            ok = count_le(mid) >= need
            return jnp.where(ok, jlo, mid), jnp.where(ok, mid, jhi)

        _, jhi = lax.fori_loop(0, n_keys.bit_length(), bisect,
                               (jnp.full((1, tq), -1, jnp.int32), jnp.full((1, tq), n_keys - 1, jnp.int32)))
        return jnp.where(tied, jhi, int_max)

    jcut = lax.cond(jnp.max(jnp.where(tied, 1, 0)) > 0, tie_cut, lambda _: jnp.full((1, tq), int_max, jnp.int32), 0)
    jcut8 = jnp.broadcast_to(jcut, (SUBLANES, tq))

    def write_chunk(c, carry):
        ks = pl.multiple_of(c * ck, ck)
        kk = key_ref[pl.ds(ks, ck), :].reshape(groups, SUBLANES, tq)
        idx = (c * ck + kloc).reshape(groups, SUBLANES, tq)
        at_thr = jnp.where(kk == thr8[None], jnp.where(idx <= jcut8[None], 1, 0), 0)
        sel = jnp.where(kk > thr8[None], 1, at_thr)
        mask_ref[0, pl.ds(ks, ck), :] = sel.reshape(ck, tq).astype(mask_ref.dtype)
        return carry

    lax.fori_loop(0, nk, write_chunk, 0)

    def zero_chunk(c, carry):
        ks = pl.multiple_of(c * ck, ck)
        mask_ref[0, pl.ds(ks, ck), :] = jnp.zeros((ck, tq), mask_ref.dtype)
        return carry

    lax.fori_loop(nk, mask_ref.shape[1] // ck, zero_chunk, 0)


def indexer_mask(qi, small, k_even, k_odd, b, t, n_sel):
    tq = _tile(t, 512)
    ck = _tile(t, 512)
    return pl.pallas_call(
        functools.partial(_indexer_body, tq=tq, ck=ck, n_sel=n_sel),
        grid=(b, t // tq),
        in_specs=[
            pl.BlockSpec((1, tq, IDX_HEADS * IDX_DIM), lambda bb, i: (bb, i, 0)),
            pl.BlockSpec((1, tq, LANES), lambda bb, i: (bb, i, 0)),
            pl.BlockSpec((1, t, LANES), lambda bb, i: (bb, 0, 0)),
            pl.BlockSpec((1, t, LANES), lambda bb, i: (bb, 0, 0)),
        ],
        out_specs=pl.BlockSpec((1, t, tq), lambda bb, i: (bb, 0, i)),
        out_shape=jax.ShapeDtypeStruct((b, t, t), jnp.int8),
        scratch_shapes=[pltpu.VMEM((t, tq), jnp.int32), pltpu.VMEM((t, tq), jnp.int16),
                        pltpu.VMEM((t, tq), jnp.int16), pltpu.VMEM((IDX_HEADS, SUBLANES, tq), F32)],
        compiler_params=_params("parallel", "parallel"),
        name="indexer",
    )(qi.reshape(b, t, -1), small, k_even.reshape(b, t, LANES), k_odd.reshape(b, t, LANES))


ATTN_HEAD_GROUP = 4


def _attn_body(qi_ref, ki_ref, fix_ref, q_ref, k_ref, vt_ref, mask_ref, bias_ref, z_ref, m0_ref, o_ref, m_ref, acc_ref,
               *, tq):
    qi, ki = qi_ref[pl.program_id(1)], ki_ref[pl.program_id(1)]
    fixed = fix_ref[0] == 1

    @pl.when(ki == 0)
    def _():
        m_ref[...] = jnp.full(m_ref.shape, MASK_NEG, F32)
        acc_ref[...] = jnp.zeros(acc_ref.shape, F32)

    def tile(near, fixed_shift):
        madd = (1.0 - mask_ref[0].astype(F32)) * MASK_NEG
        if fixed_shift:
            madd = madd - m0_ref[0, 0]
        off = jnp.where(ki == qi, 0, 1)
        for g in range(0, A_HEADS, ATTN_HEAD_GROUP):
            hs = range(g, g + ATTN_HEAD_GROUP)
            grp = slice(g, g + ATTN_HEAD_GROUP)
            head = lambda ref, h: ref[0, :, h * A_HEAD_DIM:(h + 1) * A_HEAD_DIM]
            k2 = jnp.stack([head(k_ref, h) for h in hs])
            q2 = jnp.stack([head(q_ref, h) for h in hs])
            s = lax.dot_general(k2, q2, (((2,), (2,)), ((0,), (0,))), preferred_element_type=F32) + madd[None]
            if near:
                s = s + bias_ref[off, grp].astype(F32)
            vt2 = jnp.stack([vt_ref[h * V_ROWS:(h + 1) * V_ROWS, :] for h in hs])
            pv = lambda p: lax.dot_general(vt2, p, (((2,), (1,)), ((0,), (0,))), preferred_element_type=F32)
            if fixed_shift:
                acc_ref[grp] = acc_ref[grp] + pv(jnp.exp2(s).astype(BF16))
            else:
                m_prev = m_ref[grp]
                m_new = jnp.maximum(m_prev, jnp.max(s, axis=1, keepdims=True))
                acc_ref[grp] = jnp.exp2(m_prev - m_new) * acc_ref[grp] + pv(jnp.exp2(s - m_new).astype(BF16))
                m_ref[grp] = m_new

    far = ki + 1 < qi
    for near in (False, True):
        for fixed_shift in (False, True):
            cond = jnp.logical_and(jnp.logical_not(far) if near else far,
                                   fixed if fixed_shift else jnp.logical_not(fixed))
            pl.when(cond)(functools.partial(tile, near, fixed_shift))

    @pl.when(ki == qi)
    def _():
        outs = []
        for h in range(A_HEADS):
            a = acc_ref[h]
            outs.append((a[:A_HEAD_DIM] / a[A_HEAD_DIM:A_HEAD_DIM + 1]).T)
        o_ref[0] = (jnp.concatenate(outs, axis=-1) * _silu(z_ref[0].astype(F32))).astype(o_ref.dtype)


MAX_LOGIT_SPAN = 100.0


def softmax_shift(q_gain, k_gain, rel_bias):
    slack = 1.02
    qk = (A_HEAD_DIM ** 0.5) * LOG2E * jnp.max(jnp.abs(q_gain)) * jnp.max(jnp.abs(k_gain)) * slack
    b2 = (rel_bias - rel_bias[REL_BUCKETS - 1]) * (LOG2E * slack)
    shift = qk + jnp.max(b2)
    span = 2.0 * qk + jnp.max(b2) - jnp.min(b2)
    return shift.astype(F32), (span <= MAX_LOGIT_SPAN).astype(jnp.int32)


def attention(q, k, vt, mask, bias, proj3d, shift, use_shift, b, t, tq):
    nq = t // tq
    pairs = [(i, j) for i in range(nq) for j in range(i + 1)]
    qi_tab = jnp.asarray([p[0] for p in pairs], jnp.int32)
    ki_tab = jnp.asarray([p[1] for p in pairs], jnp.int32)
    grid_spec = pltpu.PrefetchScalarGridSpec(
        num_scalar_prefetch=3,
        grid=(b, len(pairs)),
        in_specs=[
            pl.BlockSpec((1, tq, A_WIDTH), lambda bb, p, qi, ki, fx: (bb, qi[p], 0)),
            pl.BlockSpec((1, tq, A_WIDTH), lambda bb, p, qi, ki, fx: (bb, ki[p], 0)),
            pl.BlockSpec((A_HEADS * V_ROWS, tq), lambda bb, p, qi, ki, fx: (0, bb * nq + ki[p])),
            pl.BlockSpec((1, tq, tq), lambda bb, p, qi, ki, fx: (bb, ki[p], qi[p])),
            pl.BlockSpec(bias.shape, lambda bb, p, qi, ki, fx: (0, 0, 0, 0)),
            pl.BlockSpec((1, tq, A_WIDTH), lambda bb, p, qi, ki, fx: (bb, qi[p], COL_Z_A // A_WIDTH)),
            pl.BlockSpec(memory_space=pltpu.SMEM),
        ],
        out_specs=pl.BlockSpec((1, tq, A_WIDTH), lambda bb, p, qi, ki, fx: (bb, qi[p], 0)),
        scratch_shapes=[pltpu.VMEM((A_HEADS, 1, tq), F32), pltpu.VMEM((A_HEADS, V_ROWS, tq), F32)],
    )
    return pl.pallas_call(
        functools.partial(_attn_body, tq=tq),
        grid_spec=grid_spec,
        out_shape=jax.ShapeDtypeStruct((b, t, A_WIDTH), BF16),
        compiler_params=_params("parallel", "arbitrary"),
        name="dsa_attention",
    )(qi_tab, ki_tab, use_shift.reshape(1), q.reshape(b, t, -1), k.reshape(b, t, -1), vt, mask, bias, proj3d,
      shift.reshape(1, 1))


def _gdn_prep_body(x_ref, halo_ref, sm_ref, cw_ref, alog_ref, dtb_ref,
                   q_ref, k_ref, v_ref, gcb_ref, bb_ref, grow_ref, *, tm):
    i = pl.program_id(1)
    x = x_ref[0]
    hal = halo_ref[0]
    hal = jnp.where(i > 0, hal, jnp.zeros_like(hal))
    rt = lax.broadcasted_iota(jnp.int32, (tm, tm), 0)
    ct = lax.broadcasted_iota(jnp.int32, (tm, tm), 1)
    rh = lax.broadcasted_iota(jnp.int32, (SUBLANES, BF16_ROWS), 0)
    ch = lax.broadcasted_iota(jnp.int32, (SUBLANES, BF16_ROWS), 1)
    y = cw_ref[B_CONV - 1:B_CONV, :] * x.astype(F32)
    for j in range(B_CONV - 1):
        back = B_CONV - 1 - j
        main = _dot(jnp.where(ct == rt - back, 1.0, 0.0).astype(BF16), x)
        head = _dot(jnp.where(ch == rh + (BF16_ROWS - back), 1.0, 0.0).astype(BF16), hal)
        y = y + cw_ref[j:j + 1, :] * jnp.concatenate([main[:SUBLANES] + head, main[SUBLANES:]], axis=0)
    y = _silu(y)
    for h in range(B_HEADS):
        sl = slice(h * LANES, (h + 1) * LANES)
        qh = y[:, sl]
        q_ref[0, :, sl] = qh * lax.rsqrt(jnp.sum(qh * qh, axis=-1, keepdims=True) + EPS) * (B_HEAD_DIM ** -0.5)
        kh = y[:, B_WIDTH + h * LANES:B_WIDTH + (h + 1) * LANES]
        k_ref[0, :, sl] = kh * lax.rsqrt(jnp.sum(kh * kh, axis=-1, keepdims=True) + EPS)
    v_ref[0] = y[:, 2 * B_WIDTH:]

    sm = sm_ref[0]
    xg = sm + dtb_ref[...]
    softplus = jnp.maximum(xg, 0.0) + jnp.log(1.0 + jnp.exp(-jnp.abs(xg)))
    g = -jnp.exp(alog_ref[...]) * softplus
    r = lax.broadcasted_iota(jnp.int32, (tm, tm), 0)
    c = lax.broadcasted_iota(jnp.int32, (tm, tm), 1)
    sh = int(math.log2(GDN_CHUNK))
    same_chunk = jnp.right_shift(r, sh) == jnp.right_shift(c, sh)
    tri = jnp.where(jnp.logical_and(same_chunk, c <= r), 1.0, 0.0).astype(F32)
    gc = lax.dot_general(tri, g, (((1,), (0,)), ((), ())), precision=HIGHEST, preferred_element_type=F32)
    beta = _sigmoid(sm)
    for h in range(B_HEADS):
        sl = slice(h * LANES, (h + 1) * LANES)
        gcb_ref[0, :, sl] = jnp.broadcast_to(gc[:, SM_ALPHA + h:SM_ALPHA + h + 1], (tm, LANES))
        bb_ref[0, :, sl] = jnp.broadcast_to(beta[:, SM_BETA + h:SM_BETA + h + 1], (tm, LANES))
    gct = gc.T
    for cc in range(tm // GDN_CHUNK):
        grow_ref[0, cc] = gct[SM_ALPHA:SM_ALPHA + B_HEADS, cc * GDN_CHUNK:(cc + 1) * GDN_CHUNK]


def gdn_prep(proj3d, small3d, conv_w, a_log, dt_bias, b, t):
    tm = _tile(t, 256)
    pad_row = lambda v: jnp.zeros((1, LANES), F32).at[0, SM_ALPHA:SM_ALPHA + B_HEADS].set(v)
    nc = tm // GDN_CHUNK
    act = jax.ShapeDtypeStruct((b, t, B_WIDTH), F32)
    full = lambda a: pl.BlockSpec(a.shape, lambda bb, i: (0,) * a.ndim)
    consts = [conv_w, pad_row(a_log), pad_row(dt_bias)]
    return pl.pallas_call(
        functools.partial(_gdn_prep_body, tm=tm),
        grid=(b, t // tm),
        in_specs=[
            pl.BlockSpec((1, tm, 3 * B_WIDTH), lambda bb, i: (bb, i, COL_QKV_B // (3 * B_WIDTH))),
            pl.BlockSpec((1, BF16_ROWS, 3 * B_WIDTH),
                         lambda bb, i: (bb, jnp.maximum(i * (tm // BF16_ROWS) - 1, 0), 0)),
            pl.BlockSpec((1, tm, LANES), lambda bb, i: (bb, i, 0)),
        ] + [full(c) for c in consts],
        out_specs=[pl.BlockSpec((1, tm, B_WIDTH), lambda bb, i: (bb, i, 0))] * 5
        + [pl.BlockSpec((1, nc, B_HEADS, GDN_CHUNK), lambda bb, i: (bb, i, 0, 0))],
        out_shape=[act] * 5 + [jax.ShapeDtypeStruct((b, t // GDN_CHUNK, B_HEADS, GDN_CHUNK), F32)],
        compiler_params=_params("parallel", "parallel"),
        name="gdn_prep",
    )(proj3d, proj3d, small3d, *consts)


def _gdn_body(q_ref, k_ref, v_ref, gcb_ref, bb_ref, grow_ref, z_ref, on_ref, o_ref, s_ref, *, nc):
    @pl.when(pl.program_id(1) == 0)
    def _():
        s_ref[...] = jnp.zeros(s_ref.shape, F32)

    cs, nh = GDN_CHUNK, B_HEADS
    nb = nc * nh
    ri = lax.broadcasted_iota(jnp.int32, (nb, cs, cs), 1)
    ci = lax.broadcasted_iota(jnp.int32, (nb, cs, cs), 2)
    lower = ci <= ri
    strict = ci < ri
    eye = jnp.where(ci == ri, 1.0, 0.0).astype(F32)
    bf = lambda a: a.astype(BF16)
    bmm = lambda a, b: lax.dot_general(a, b, (((2,), (1,)), ((0,), (0,))), preferred_element_type=F32)
    bmm_nt = lambda a, b: lax.dot_general(a, b, (((2,), (2,)), ((0,), (0,))), preferred_element_type=F32)
    bmm_tn = lambda a, b: lax.dot_general(a, b, (((1,), (1,)), ((0,), (0,))), preferred_element_type=F32)

    def stack(ref):
        return jnp.stack([ref[0, c * cs:(c + 1) * cs, h * LANES:(h + 1) * LANES]
                          for c in range(nc) for h in range(nh)])

    q, k, v = stack(q_ref), stack(k_ref), stack(v_ref)
    gcb = stack(gcb_ref)
    beta = stack(bb_ref)
    grow = jnp.stack([grow_ref[0, c, h:h + 1, :] for c in range(nc) for h in range(nh)])
    diff = gcb[:, :, :cs] - grow
    decay = jnp.where(lower, jnp.exp(jnp.where(lower, diff, 0.0)), 0.0)
    eg = jnp.exp(gcb)
    glast = gcb[:, cs - 1:cs, :]
    kb = k * beta
    lmat = jnp.where(strict, bmm_nt(bf(kb), bf(k)) * decay, 0.0)
    n = -lmat
    tinv = eye + n
    for _ in range(int(math.log2(cs)) - 1):
        n = bmm(bf(n), bf(n))
        tinv = tinv + bmm(bf(tinv), bf(n))
    uw = bmm(bf(tinv), bf(jnp.concatenate([v * beta, kb * eg], axis=-1)))
    attn = bf(jnp.where(lower, bmm_nt(bf(q), bf(k)) * decay, 0.0))
    qg = bf(q * eg)
    kdec = bf(k * jnp.exp(glast - gcb))
    egl = jnp.exp(glast)

    s = s_ref[...]
    for c in range(nc):
        sl = slice(c * nh, (c + 1) * nh)
        sb = bf(s)
        v_new = uw[sl, :, :LANES] - bmm(bf(uw[sl, :, LANES:]), sb)
        o = bmm(qg[sl], sb) + bmm(attn[sl], bf(v_new))
        s = s * egl[sl] + bmm_tn(kdec[sl], bf(v_new))
        o = o * lax.rsqrt(jnp.mean(o * o, axis=-1, keepdims=True) + EPS) * on_ref[...]
        for h in range(nh):
            rows, cols = slice(c * cs, (c + 1) * cs), slice(h * LANES, (h + 1) * LANES)
            o_ref[0, rows, cols] = (o[h] * _silu(z_ref[0, rows, cols].astype(F32))).astype(o_ref.dtype)
    s_ref[...] = s


def gdn_scan(qh, kh, v, gcb, bb, grow, proj3d, o_norm, b, t):
    tt = _tile(t, 256)
    nc = tt // GDN_CHUNK
    blk = pl.BlockSpec((1, tt, B_WIDTH), lambda bb_, i: (bb_, i, 0))
    return pl.pallas_call(
        functools.partial(_gdn_body, nc=nc),
        grid=(b, t // tt),
        in_specs=[blk] * 5 + [
            pl.BlockSpec((1, nc, B_HEADS, GDN_CHUNK), lambda bb_, i: (bb_, i, 0, 0)),
            pl.BlockSpec((1, tt, B_WIDTH), lambda bb_, i: (bb_, i, COL_Z_B // B_WIDTH)),
            pl.BlockSpec((1, LANES), lambda bb_, i: (0, 0)),
        ],
        out_specs=blk,
        out_shape=jax.ShapeDtypeStruct((b, t, B_WIDTH), BF16),
        scratch_shapes=[pltpu.VMEM((B_HEADS, B_HEAD_DIM, B_HEAD_DIM), F32)],
        compiler_params=_params("parallel", "arbitrary"),
        name="gdn_scan",
    )(qh, kh, v, gcb, bb, grow, proj3d, o_norm.reshape(1, LANES))


C_HALO = 32
D_HALO = BF16_ROWS


def _cd_mix_body(a_ref, g_ref, ah_ref, gh_ref, zc_ref, bg_ref, cg_ref, ud_ref, cgh_ref, udh_ref, zd_ref,
                 dww_ref, dwb_ref, lnw_ref, lnb_ref, dcw_ref, yc_ref, yd_ref, us_ref, ds_ref, *, tm):
    i = pl.program_id(1)
    first = i == 0
    f32 = lambda r: r[0].astype(F32)
    uh = f32(ah_ref) * _sigmoid(f32(gh_ref))
    us_ref[0, 0:C_HALO, :] = jnp.where(first, jnp.zeros_like(uh), uh)
    us_ref[0, C_HALO:, :] = f32(a_ref) * _sigmoid(f32(g_ref))
    span = tm + C_HALO - SUBLANES
    for r in range(1, SUBLANES):
        us_ref[r, 0:span, :] = us_ref[0, pl.ds(r, span), :]
    u = jnp.zeros((tm, C_WIDTH), F32)
    for j in range(C_CONV):
        off = C_HALO - (C_CONV - 1) + j
        r, base = off % SUBLANES, off - off % SUBLANES
        u = u + dww_ref[j:j + 1, :] * us_ref[r, base:base + tm, :]
    u = u + dwb_ref[...]
    mu = jnp.mean(u, axis=-1, keepdims=True)
    uc = u - mu
    u = uc * lax.rsqrt(jnp.mean(uc * uc, axis=-1, keepdims=True) + EPS) * lnw_ref[...] + lnb_ref[...]
    yc_ref[0] = (_silu(u) * _silu(f32(zc_ref))).astype(yc_ref.dtype)

    dh = f32(cgh_ref) * f32(udh_ref)
    ds_ref[0:D_HALO, :] = jnp.where(first, jnp.zeros_like(dh), dh)
    ds_ref[D_HALO:, :] = f32(cg_ref) * f32(ud_ref)
    d = jnp.zeros((tm, D_WIDTH), F32)
    for j in range(D_CONV):
        d = d + dcw_ref[j:j + 1, :] * ds_ref[pl.ds(D_HALO - (D_CONV - 1) + j, tm), :]
    yd_ref[0] = (f32(bg_ref) * d * _silu(f32(zd_ref))).astype(yd_ref.dtype)


def cd_mix(proj3d, dw_w, dw_b, ln_w, ln_b, d_conv_w, b, t):
    tm = _tile(t, 256)
    w = C_WIDTH
    col = lambda n: pl.BlockSpec((1, tm, w), lambda bb, i, n=n: (bb, i, n))
    halo = lambda n, rows: pl.BlockSpec(
        (1, rows, w), lambda bb, i, n=n, rows=rows: (bb, jnp.maximum(i * (tm // rows) - 1, 0), n))
    row = lambda a: a.reshape(1, -1)
    full = lambda a: pl.BlockSpec(a.shape, lambda bb, i: (0,) * a.ndim)
    consts = [dw_w, row(dw_b), row(ln_w), row(ln_b), d_conv_w]
    out = jax.ShapeDtypeStruct((b, t, w), BF16)
    return pl.pallas_call(
        functools.partial(_cd_mix_body, tm=tm),
        grid=(b, t // tm),
        in_specs=[col(0), col(1), halo(0, C_HALO), halo(1, C_HALO), col(2), col(3), col(4), col(5),
                  halo(4, D_HALO), halo(5, D_HALO), col(6)] + [full(c) for c in consts],
        out_specs=[pl.BlockSpec((1, tm, w), lambda bb, i: (bb, i, 0))] * 2,
        out_shape=[out, out],
        scratch_shapes=[pltpu.VMEM((SUBLANES, tm + C_HALO, w), F32), pltpu.VMEM((tm + D_HALO, w), F32)],
        compiler_params=_params("parallel", "parallel"),
        name="cd_mix",
    )(*([proj3d] * 11), *consts)


def _reorder_ab_w_in(w):
    offs = [0]
    for s in AB_SPLITS:
        offs.append(offs[-1] + s)
    part = lambda n: w[:, offs[n]:offs[n + 1]]
    c_q, c_kv, k_idx, w_idx, z_a, qkv_b, beta_b, alpha_b, z_b = (part(n) for n in range(9))
    small = jnp.concatenate([k_idx, w_idx, beta_b, alpha_b], axis=1)
    small = jnp.pad(small, ((0, 0), (0, LANES - small.shape[1])))
    out = jnp.concatenate([qkv_b, z_a, z_b, c_q, c_kv], axis=1)
    return jnp.pad(out, ((0, 0), (0, AB_IN_PAD - out.shape[1]))).astype(BF16), small.astype(BF16)


def _ab_layer(x2d, b, t, norm_w, rel_bias, w_in, q_norm, w_uq, w_iq, kv_norm, w_uk, w_uv, q_gain, k_gain,
              ik_w, ik_b, conv_w, a_log, dt_bias, o_norm, w_out):
    proj, small = norm_matmul(x2d, norm_w, *_reorder_ab_w_in(w_in))
    proj3d, small3d = proj.reshape(b, t, AB_IN_PAD), small.reshape(b, t, LANES)
    q, k, v, qi, k_even, k_odd = dsa_prep(proj, small, q_norm, kv_norm, w_uq, w_iq, w_uk, w_uv, q_gain, k_gain,
                                          ik_w, ik_b)
    n_sel = min(TOPK_MAX, t // 4)
    mask = indexer_mask(qi, small3d, k_even, k_odd, b, t, n_sel)
    tq = _tile(t, 512)
    shift, use_shift = softmax_shift(q_gain, k_gain, rel_bias)
    y_a = attention(q, k, v, mask, bias_tiles(rel_bias, tq), proj3d, shift, use_shift, b, t, tq)
    qh, kh, vv, gcb, bb, grow = gdn_prep(proj3d, small3d, conv_w, a_log, dt_bias, b, t)
    y_b = gdn_scan(qh, kh, vv, gcb, bb, grow, proj3d, o_norm, b, t)
    wo = w_out.astype(BF16)
    return out_proj(x2d, y_a.reshape(b * t, -1), y_b.reshape(b * t, -1), wo[:A_WIDTH], wo[A_WIDTH:])


def _cd_layer(x2d, b, t, norm_w, w_in, dw_w, dw_b, ln_w, ln_b, d_conv_w, w_out):
    proj = norm_matmul(x2d, norm_w, w_in.astype(BF16))
    y_c, y_d = cd_mix(proj.reshape(b, t, -1), dw_w, dw_b, ln_w, ln_b, d_conv_w, b, t)
    wo = w_out.astype(BF16)
    return out_proj(x2d, y_c.reshape(b * t, -1), y_d.reshape(b * t, -1), wo[:C_WIDTH], wo[C_WIDTH:])


def kernel(x, norm_w, rel_bias, ab_w_in, a_q_norm, a_w_uq, a_w_iq, a_kv_norm, a_w_uk, a_w_uv, a_q_gain,
           a_k_gain, a_ik_norm_w, a_ik_norm_b, b_conv_w, b_a_log, b_dt_bias, b_o_norm, ab_w_out, cd_w_in,
           c_dw_w, c_dw_b, c_ln_w, c_ln_b, d_conv_w, cd_w_out):
    b, t, d = x.shape
    depth = norm_w.shape[0]
    x2d = x.reshape(b * t, d)
    for i in range(depth):
        j = i // 2
        if i % 2 == 0:
            x2d = _ab_layer(x2d, b, t, norm_w[i], rel_bias, ab_w_in[j], a_q_norm[j], a_w_uq[j], a_w_iq[j],
                            a_kv_norm[j], a_w_uk[j], a_w_uv[j], a_q_gain[j], a_k_gain[j], a_ik_norm_w[j],
                            a_ik_norm_b[j], b_conv_w[j], b_a_log[j], b_dt_bias[j], b_o_norm[j], ab_w_out[j])
        else:
            x2d = _cd_layer(x2d, b, t, norm_w[i], cd_w_in[j], c_dw_w[j], c_dw_b[j], c_ln_w[j], c_ln_b[j],
                            d_conv_w[j], cd_w_out[j])
    return x2d.reshape(b, t, d)
```

```python
import functools
import math

import jax
import jax.numpy as jnp
from jax import lax
from jax.experimental import pallas as pl
from jax.experimental.pallas import tpu as pltpu

A_HEADS = 8
A_HEAD_DIM = 128
A_WIDTH = A_HEADS * A_HEAD_DIM
A_Q_LORA = 512
A_KV_LORA = 256
IDX_HEADS = 16
IDX_DIM = 64
TOPK_MAX = 256
REL_BUCKETS = 32
REL_MAX_DIST = 128
B_HEADS = 8
B_HEAD_DIM = 128
B_WIDTH = B_HEADS * B_HEAD_DIM
B_CONV = 4
GDN_CHUNK = 64
C_WIDTH = 1024
C_CONV = 31
D_WIDTH = 1024
D_CONV = 3
EPS = 1e-6

AB_SPLITS = (A_Q_LORA, A_KV_LORA, IDX_DIM, IDX_HEADS, A_WIDTH, 3 * B_WIDTH, B_HEADS, B_HEADS, B_WIDTH)
AB_IN_PAD = 6144
SM_KIDX = 0
SM_WIDX = 64
SM_BETA = 80
SM_ALPHA = 88
COL_QKV_B = 0
COL_Z_A = 3072
COL_Z_B = 4096
COL_C_Q = 5120
COL_C_KV = 5632

LANES = 128
MASK_NEG = -1e30
INT_MIN = -(2 ** 31)
LOG2E = math.log2(math.e)
BF16_ROWS = 16
V_ROWS = A_HEAD_DIM + BF16_ROWS
VMEM_LIMIT = 56 * 1024 * 1024

F32 = jnp.float32
BF16 = jnp.bfloat16
HIGHEST = lax.Precision.HIGHEST


def _tile(n, pref):
    t = min(n, pref)
    assert n % t == 0, (n, t)
    return t


def _params(*sem):
    return pltpu.CompilerParams(dimension_semantics=sem, vmem_limit_bytes=VMEM_LIMIT)


def _dot(a, b):
    return jnp.dot(a, b, preferred_element_type=F32)


def _dot_nt(a, b):
    return lax.dot_general(a, b, (((1,), (1,)), ((), ())), preferred_element_type=F32)


def _silu(x):
    return x * (1.0 / (1.0 + jnp.exp(-x)))


def _sigmoid(x):
    return 1.0 / (1.0 + jnp.exp(-x))


def _norm_matmul_body(*refs, with_f32_cols):
    if with_f32_cols:
        x_ref, nw_ref, w_ref, ws_ref, o_ref, os_ref, h_ref = refs
    else:
        x_ref, nw_ref, w_ref, o_ref, h_ref = refs

    @pl.when(pl.program_id(1) == 0)
    def _():
        x = x_ref[...]
        ms = jnp.mean(x * x, axis=-1, keepdims=True)
        h_ref[...] = (x * lax.rsqrt(ms + EPS) * nw_ref[...]).astype(h_ref.dtype)
        if with_f32_cols:
            os_ref[...] = _dot(h_ref[...], ws_ref[...])

    o_ref[...] = _dot(h_ref[...], w_ref[...]).astype(o_ref.dtype)


def norm_matmul(x2d, norm_w, w_bf16, w_f32_cols=None):
    m, k = x2d.shape
    n = w_bf16.shape[1]
    tm = _tile(m, 1024)
    tn = max(d for d in range(256, min(n, 2048) + 1, 256) if n % d == 0)
    extra = w_f32_cols is not None
    in_specs = [
        pl.BlockSpec((tm, k), lambda i, j: (i, 0)),
        pl.BlockSpec((1, k), lambda i, j: (0, 0)),
        pl.BlockSpec((k, tn), lambda i, j: (0, j)),
    ]
    out_specs = [pl.BlockSpec((tm, tn), lambda i, j: (i, j))]
    out_shape = [jax.ShapeDtypeStruct((m, n), BF16)]
    args = [x2d, norm_w.reshape(1, k), w_bf16]
    if extra:
        in_specs.append(pl.BlockSpec((k, LANES), lambda i, j: (0, 0)))
        out_specs.append(pl.BlockSpec((tm, LANES), lambda i, j: (i, 0)))
        out_shape.append(jax.ShapeDtypeStruct((m, LANES), F32))
        args.append(w_f32_cols)
    res = pl.pallas_call(
        functools.partial(_norm_matmul_body, with_f32_cols=extra),
        grid=(m // tm, n // tn),
        in_specs=in_specs,
        out_specs=out_specs,
        out_shape=out_shape,
        scratch_shapes=[pltpu.VMEM((tm, k), BF16)],
        compiler_params=_params("parallel", "arbitrary"),
        name="norm_matmul",
    )(*args)
    return res if extra else res[0]


def _out_proj_body(x_ref, ya_ref, yb_ref, wa_ref, wb_ref, o_ref):
    o_ref[...] = x_ref[...] + _dot(ya_ref[...], wa_ref[...]) + _dot(yb_ref[...], wb_ref[...])


def out_proj(x2d, ya, yb, wa_bf16, wb_bf16):
    m, n = x2d.shape
    ka, kb = ya.shape[1], yb.shape[1]
    tm = _tile(m, 512)
    return pl.pallas_call(
        _out_proj_body,
        grid=(m // tm,),
        in_specs=[
            pl.BlockSpec((tm, n), lambda i: (i, 0)),
            pl.BlockSpec((tm, ka), lambda i: (i, 0)),
            pl.BlockSpec((tm, kb), lambda i: (i, 0)),
            pl.BlockSpec((ka, n), lambda i: (0, 0)),
            pl.BlockSpec((kb, n), lambda i: (0, 0)),
        ],
        out_specs=pl.BlockSpec((tm, n), lambda i: (i, 0)),
        out_shape=jax.ShapeDtypeStruct((m, n), F32),
        compiler_params=_params("parallel"),
        name="out_proj",
    )(x2d, ya, yb, wa_bf16, wb_bf16)


def _head_rmsnorm(y, gain_row, scale):
    outs = []
    for h in range(y.shape[1] // LANES):
        seg = y[:, h * LANES:(h + 1) * LANES]
        ms = jnp.mean(seg * seg, axis=-1, keepdims=True)
        outs.append(seg * lax.rsqrt(ms + EPS) * (gain_row * scale))
    return jnp.concatenate(outs, axis=-1)


def _dsa_prep_body(cq_ref, ckv_ref, sm_ref, qn_ref, kvn_ref, wuq_ref, wiq_ref, wuk_ref, wuv_ref,
                   qg_ref, kg_ref, ikw_ref, ikb_ref,
                   q_ref, k_ref, v_ref, qi_ref, ke_ref, ko_ref):
    cq = cq_ref[...].astype(F32)
    cq = cq * lax.rsqrt(jnp.mean(cq * cq, axis=-1, keepdims=True) + EPS) * qn_ref[...]
    cqb = cq.astype(BF16)
    q = _dot(cqb, wuq_ref[...])
    q_ref[...] = _head_rmsnorm(q, qg_ref[...], A_HEAD_DIM ** -0.5 * LOG2E).astype(q_ref.dtype)
    qi_ref[...] = _dot(cqb, wiq_ref[...]).astype(qi_ref.dtype)

    ckv = ckv_ref[...].astype(F32)
    ckv = ckv * lax.rsqrt(jnp.mean(ckv * ckv, axis=-1, keepdims=True) + EPS) * kvn_ref[...]
    ckvb = ckv.astype(BF16)
    k = _dot(ckvb, wuk_ref[...])
    k_ref[...] = _head_rmsnorm(k, kg_ref[...], 1.0).astype(k_ref.dtype)
    vt = _dot_nt(wuv_ref[...], ckvb).astype(v_ref.dtype)
    ones = jnp.ones((BF16_ROWS, vt.shape[1]), v_ref.dtype)
    for h in range(A_HEADS):
        v_ref[h * V_ROWS:h * V_ROWS + A_HEAD_DIM, :] = vt[h * A_HEAD_DIM:(h + 1) * A_HEAD_DIM, :]
        v_ref[h * V_ROWS + A_HEAD_DIM:(h + 1) * V_ROWS, :] = ones

    kx = sm_ref[...][:, SM_KIDX:SM_KIDX + IDX_DIM]
    mu = jnp.mean(kx, axis=-1, keepdims=True)
    kc = kx - mu
    kl = kc * lax.rsqrt(jnp.mean(kc * kc, axis=-1, keepdims=True) + EPS) * ikw_ref[...] + ikb_ref[...]
    z = jnp.zeros_like(kl)
    ke_ref[...] = jnp.concatenate([kl, z], axis=-1).astype(ke_ref.dtype)
    ko_ref[...] = jnp.concatenate([z, kl], axis=-1).astype(ko_ref.dtype)


def dsa_prep(proj, small, q_norm, kv_norm, w_uq, w_iq, w_uk, w_uv, q_gain, k_gain, ik_w, ik_b):
    m = proj.shape[0]
    tm = _tile(m, 512)
    row = lambda a: a.reshape(1, -1)
    full = lambda a: pl.BlockSpec(a.shape, lambda i: (0,) * a.ndim)
    consts = [row(q_norm), row(kv_norm), w_uq.astype(BF16), w_iq.astype(BF16), w_uk.astype(BF16),
              w_uv.T.astype(BF16), row(q_gain), row(k_gain), row(ik_w), row(ik_b)]
    outs = [jax.ShapeDtypeStruct((m, A_WIDTH), BF16)] * 2 + [
        jax.ShapeDtypeStruct((A_HEADS * V_ROWS, m), BF16),
        jax.ShapeDtypeStruct((m, IDX_HEADS * IDX_DIM), BF16),
        jax.ShapeDtypeStruct((m, LANES), BF16), jax.ShapeDtypeStruct((m, LANES), BF16)]
    out_specs = [pl.BlockSpec((tm, s.shape[1]), lambda i: (i, 0)) for s in outs]
    out_specs[2] = pl.BlockSpec((A_HEADS * V_ROWS, tm), lambda i: (0, i))
    return pl.pallas_call(
        _dsa_prep_body,
        grid=(m // tm,),
        in_specs=[
            pl.BlockSpec((tm, A_Q_LORA), lambda i: (i, COL_C_Q // A_Q_LORA)),
            pl.BlockSpec((tm, A_KV_LORA), lambda i: (i, COL_C_KV // A_KV_LORA)),
            pl.BlockSpec((tm, LANES), lambda i: (i, 0)),
        ] + [full(c) for c in consts],
        out_specs=out_specs,
        out_shape=outs,
        compiler_params=_params("parallel"),
        name="dsa_prep",
    )(proj, proj, small, *consts)


def _t5_bucket(dist):
    max_exact = REL_BUCKETS // 2
    large = max_exact + (jnp.log(jnp.maximum(dist, 1).astype(F32) / max_exact)
                         / math.log(REL_MAX_DIST / max_exact) * (REL_BUCKETS - max_exact)).astype(jnp.int32)
    large = jnp.minimum(large, REL_BUCKETS - 1)
    return jnp.where(dist < max_exact, dist, large)


def _bias_tiles_body(rb_ref, o_ref, *, t):
    h = pl.program_id(0)
    sb = min(t, LANES)
    row = lax.broadcasted_iota(jnp.int32, (sb, sb), 0)
    col = lax.broadcasted_iota(jnp.int32, (sb, sb), 1)
    for typ in range(2):
        for rk in range(t // sb):
            for cq in range(t // sb):
                base = typ * t + (cq - rk) * sb
                blk = (typ, 0, slice(rk * sb, (rk + 1) * sb), slice(cq * sb, (cq + 1) * sb))
                if base - (sb - 1) >= REL_MAX_DIST or base + (sb - 1) < 0:
                    o_ref[blk] = jnp.zeros((sb, sb), o_ref.dtype)
                    continue
                bucket = _t5_bucket(jnp.maximum(base + col - row, 0))
                acc = jnp.zeros((sb, sb), F32)
                for b in range(REL_BUCKETS):
                    acc = jnp.where(bucket == b, rb_ref[b, h], acc)
                o_ref[blk] = ((acc - rb_ref[REL_BUCKETS - 1, h]) * LOG2E).astype(o_ref.dtype)


def bias_tiles(rel_bias, t):
    return pl.pallas_call(
        functools.partial(_bias_tiles_body, t=t),
        grid=(A_HEADS,),
        in_specs=[pl.BlockSpec(memory_space=pltpu.SMEM)],
        out_specs=pl.BlockSpec((2, 1, t, t), lambda h: (0, h, 0, 0)),
        out_shape=jax.ShapeDtypeStruct((2, A_HEADS, t, t), BF16),
        compiler_params=_params("parallel"),
        name="bias_tiles",
    )(rel_bias)


SUBLANES = 8
MAX_SELECT_STEPS = 24
UPPER_STEPS_PER_CHECK = 4
LOWER_STEPS_PER_CHECK = 2
HALF_BITS = 16
HALF_SPAN = 2 ** HALF_BITS
I16_MIN = -(2 ** (HALF_BITS - 1))


def _key_of(x):
    bits = pltpu.bitcast(x, jnp.int32)
    key = jnp.where(bits < 0, bits ^ jnp.int32(0x7FFFFFFF), bits)
    return jnp.where(bits == jnp.int32(INT_MIN), 0, key)


def _indexer_body(qi_ref, sm_ref, ke_ref, ko_ref, mask_ref, key_ref, hi_ref, low_ref, w_ref, *, tq, ck, n_sel):
    i = pl.program_id(1)
    nk = ((i + 1) * tq + ck - 1) // ck
    groups = ck // SUBLANES
    scale = (IDX_HEADS ** -0.5) * (IDX_DIM ** -0.5)
    smt = sm_ref[0].T
    for h in range(IDX_HEADS):
        w_ref[h] = jnp.broadcast_to(smt[SM_WIDX + h:SM_WIDX + h + 1, :] * scale, (SUBLANES, tq))
    kloc = lax.broadcasted_iota(jnp.int32, (ck, tq), 0)
    qpos = i * tq + lax.broadcasted_iota(jnp.int32, (ck, tq), 1)
    int_max = jnp.int32(2 ** 31 - 1)

    def score_chunk(c, carry):
        kmin, kmax = carry
        ks = pl.multiple_of(c * ck, ck)
        ke = ke_ref[0, pl.ds(ks, ck), :]
        ko = ko_ref[0, pl.ds(ks, ck), :]
        acc = jnp.zeros((ck, tq), F32)
        for j in range(IDX_HEADS // 2):
            qp = qi_ref[0, :, j * LANES:(j + 1) * LANES]
            for par, kk in ((0, ke), (1, ko)):
                s = jnp.maximum(_dot_nt(kk, qp), 0.0)
                acc = acc + s * jnp.tile(w_ref[2 * j + par], (groups, 1))
        key = _key_of(acc)
        valid = c * ck + kloc <= qpos
        masked = jnp.where(valid, key, jnp.int32(INT_MIN))
        key_ref[pl.ds(ks, ck), :] = masked
        hi_ref[pl.ds(ks, ck), :] = jnp.right_shift(masked, HALF_BITS).astype(jnp.int16)
        kmin = jnp.minimum(kmin, jnp.min(jnp.where(valid, key, int_max).reshape(groups, SUBLANES, tq), axis=0))
        kmax = jnp.maximum(kmax, jnp.max(masked.reshape(groups, SUBLANES, tq), axis=0))
        return kmin, kmax

    kmin, kmax = lax.fori_loop(0, nk, score_chunk, (jnp.full((SUBLANES, tq), int_max, jnp.int32),
                                                    jnp.full((SUBLANES, tq), INT_MIN, jnp.int32)))

    rows16 = ck // BF16_ROWS
    i16_min = I16_MIN

    def scan16(a_ref, init, fn):
        def body(c, acc):
            ks = pl.multiple_of(c * ck, ck)
            return fn(acc, a_ref[pl.ds(ks, ck), :].reshape(rows16, BF16_ROWS, tq))

        return lax.fori_loop(0, nk, body, init)

    def count_ge16(a_ref, p):
        p16 = jnp.broadcast_to(p.astype(jnp.int16), (BF16_ROWS, tq))

        def fn(cnt, a):
            ind = jnp.where(a >= p16[None], jnp.int16(1), jnp.int16(0))
            for g in range(rows16):
                cnt = cnt + ind[g]
            return cnt

        cnt = scan16(a_ref, jnp.zeros((BF16_ROWS, tq), jnp.int16), fn)
        return jnp.sum(cnt.astype(jnp.int32), axis=0, keepdims=True)

    def max_le16(a_ref, h):
        h16 = jnp.broadcast_to(h.astype(jnp.int16), (BF16_ROWS, tq))

        def fn(acc, a):
            v = jnp.where(a <= h16[None], a, jnp.int16(i16_min))
            for g in range(rows16):
                acc = jnp.where(v[g] > acc, v[g], acc)
            return acc

        acc = scan16(a_ref, jnp.full((BF16_ROWS, tq), i16_min, jnp.int16), fn)
        return jnp.max(acc.astype(jnp.int32), axis=0, keepdims=True)

    def rank_select16(a_ref, r, skip, lo, hi, clo, chi, floor_count, steps_per_check):
        def closed(lo, hi, clo):
            return skip | (clo == r) | (hi == lo + 1)

        def open_rows(lo, hi, clo, chi):
            return jnp.logical_not(closed(lo, hi, clo) | (chi == r - 1))

        def cond(st):
            it, lo, hi, clo, chi = st
            n_open = jnp.max(jnp.where(open_rows(lo, hi, clo, chi), 1, 0))
            return jnp.logical_and(it < MAX_SELECT_STEPS, n_open > 0)

        def step(st):
            it, lo, hi, clo, chi = st
            for _ in range(steps_per_check):
                upd = open_rows(lo, hi, clo, chi)
                p = lo + jnp.maximum(jnp.right_shift(hi - lo, 1), 1)
                cnt = count_ge16(a_ref, p)
                ge = cnt >= r
                up, dn = upd & ge, upd & jnp.logical_not(ge)
                lo, hi = jnp.where(up, p, lo), jnp.where(dn, p, hi)
                clo, chi = jnp.where(up, cnt, clo), jnp.where(dn, cnt, chi)
            return it + steps_per_check, lo, hi, clo, chi

        _, lo, hi, clo, chi = lax.while_loop(cond, step, (jnp.int32(0), lo, hi, clo, chi))
        last = jnp.logical_not(closed(lo, hi, clo))
        t_last = max_le16(a_ref, hi - 1)
        c_last = jnp.where(t_last == i16_min, floor_count, count_ge16(a_ref, t_last))
        return jnp.where(last, t_last, lo), jnp.where(last, c_last, clo), chi

    n_valid = i * tq + lax.broadcasted_iota(jnp.int32, (1, tq), 1) + 1
    few = n_valid <= n_sel
    zeros = jnp.zeros((1, tq), jnp.int32)
    hi_lo0 = jnp.right_shift(jnp.min(kmin, axis=0, keepdims=True), HALF_BITS)
    hi_hi0 = jnp.right_shift(jnp.max(kmax, axis=0, keepdims=True), HALF_BITS) + 1
    t_hi, c_ge_hi, c_gt_hi = rank_select16(hi_ref, n_sel, few, hi_lo0, hi_hi0, n_valid, zeros, n_valid,
                                           UPPER_STEPS_PER_CHECK)
    exact_hi = c_ge_hi == n_sel

    t_hi16 = jnp.broadcast_to(t_hi.astype(jnp.int16), (BF16_ROWS, tq))

    def fill_low(c, carry):
        ks = pl.multiple_of(c * ck, ck)
        low = (key_ref[pl.ds(ks, ck), :] & (HALF_SPAN - 1)) + I16_MIN
        low = low.astype(jnp.int16).reshape(rows16, BF16_ROWS, tq)
        same = hi_ref[pl.ds(ks, ck), :].reshape(rows16, BF16_ROWS, tq) == t_hi16[None]
        low_ref[pl.ds(ks, ck), :] = jnp.where(same, low, jnp.int16(i16_min)).reshape(ck, tq)
        return carry

    lax.fori_loop(0, nk, fill_low, 0)
    in_bucket = c_ge_hi - c_gt_hi
    t_lo, c_ge_lo, c_gt_lo = rank_select16(low_ref, n_sel - c_gt_hi, few | exact_hi,
                                           jnp.full((1, tq), I16_MIN, jnp.int32), jnp.full((1, tq), -I16_MIN, jnp.int32),
                                           in_bucket, zeros, in_bucket, LOWER_STEPS_PER_CHECK)
    lo = jnp.where(exact_hi, t_hi * HALF_SPAN, t_hi * HALF_SPAN + (t_lo - I16_MIN))
    clo = jnp.where(exact_hi, n_sel, c_gt_hi + c_ge_lo)
    chi = c_gt_hi + c_gt_lo
    thr = jnp.where(few, jnp.int32(INT_MIN + 1), lo)
    thr8 = jnp.broadcast_to(thr, (SUBLANES, tq))
    tied = jnp.logical_not(few) & (clo > n_sel)
    need = n_sel - chi
    n_keys = mask_ref.shape[1]

    def tie_cut(_):
        def count_le(j):
            j8 = jnp.broadcast_to(j, (SUBLANES, tq))

            def body(c, cnt):
                ks = pl.multiple_of(c * ck, ck)
                kk = key_ref[pl.ds(ks, ck), :].reshape(groups, SUBLANES, tq)
                idx = (c * ck + kloc).reshape(groups, SUBLANES, tq)
                hit = jnp.where(kk == thr8[None], jnp.where(idx <= j8[None], 1, 0), 0)
                return cnt + jnp.sum(hit, axis=0)

            cnt = lax.fori_loop(0, nk, body, jnp.zeros((SUBLANES, tq), jnp.int32))
            return jnp.sum(cnt, axis=0, keepdims=True)

        def bisect(_, st):
            jlo, jhi = st
            mid = jlo + jnp.right_shift(jhi - jlo, 1)
            ok = count_le(mid) >= need
            return jnp.where(ok, jlo, mid), jnp.where(ok, mid, jhi)

        _, jhi = lax.fori_loop(0, n_keys.bit_length(), bisect,
                               (jnp.full((1, tq), -1, jnp.int32), jnp.full((1, tq), n_keys - 1, jnp.int32)))
        return jnp.where(tied, jhi, int_max)

    jcut = lax.cond(jnp.max(jnp.where(tied, 1, 0)) > 0, tie_cut, lambda _: jnp.full((1, tq), int_max, jnp.int32), 0)
    jcut8 = jnp.broadcast_to(jcut, (SUBLANES, tq))

    def write_chunk(c, carry):
        ks = pl.multiple_of(c * ck, ck)
        kk = key_ref[pl.ds(ks, ck), :].reshape(groups, SUBLANES, tq)
        idx = (c * ck + kloc).reshape(groups, SUBLANES, tq)
        at_thr = jnp.where(kk == thr8[None], jnp.where(idx <= jcut8[None], 1, 0), 0)
        sel = jnp.where(kk > thr8[None], 1, at_thr)
        mask_ref[0, pl.ds(ks, ck), :] = sel.reshape(ck, tq).astype(mask_ref.dtype)
        return carry

    lax.fori_loop(0, nk, write_chunk, 0)

    def zero_chunk(c, carry):
        ks = pl.multiple_of(c * ck, ck)
        mask_ref[0, pl.ds(ks, ck), :] = jnp.zeros((ck, tq), mask_ref.dtype)
        return carry

    lax.fori_loop(nk, mask_ref.shape[1] // ck, zero_chunk, 0)


def indexer_mask(qi, small, k_even, k_odd, b, t, n_sel):
    tq = _tile(t, 512)
    ck = _tile(t, 512)
    return pl.pallas_call(
        functools.partial(_indexer_body, tq=tq, ck=ck, n_sel=n_sel),
        grid=(b, t // tq),
        in_specs=[
            pl.BlockSpec((1, tq, IDX_HEADS * IDX_DIM), lambda bb, i: (bb, i, 0)),
            pl.BlockSpec((1, tq, LANES), lambda bb, i: (bb, i, 0)),
            pl.BlockSpec((1, t, LANES), lambda bb, i: (bb, 0, 0)),
            pl.BlockSpec((1, t, LANES), lambda bb, i: (bb, 0, 0)),
        ],
        out_specs=pl.BlockSpec((1, t, tq), lambda bb, i: (bb, 0, i)),
        out_shape=jax.ShapeDtypeStruct((b, t, t), jnp.int8),
        scratch_shapes=[pltpu.VMEM((t, tq), jnp.int32), pltpu.VMEM((t, tq), jnp.int16),
                        pltpu.VMEM((t, tq), jnp.int16), pltpu.VMEM((IDX_HEADS, SUBLANES, tq), F32)],
        compiler_params=_params("parallel", "parallel"),
        name="indexer",
    )(qi.reshape(b, t, -1), small, k_even.reshape(b, t, LANES), k_odd.reshape(b, t, LANES))


ATTN_HEAD_GROUP = 4
ATTN_HEAD_GROUP_FIXED = 2


def _attn_body(qi_ref, ki_ref, fix_ref, q_ref, k_ref, vt_ref, mask_ref, bias_ref, z_ref, m0_ref, o_ref, m_ref, acc_ref,
               *, tq):
    qi, ki = qi_ref[pl.program_id(1)], ki_ref[pl.program_id(1)]
    fixed = fix_ref[0] == 1

    @pl.when(ki == 0)
    def _():
        m_ref[...] = jnp.full(m_ref.shape, MASK_NEG, F32)
        acc_ref[...] = jnp.zeros(acc_ref.shape, F32)

    def tile(near, fixed_shift):
        madd = (1.0 - mask_ref[0].astype(F32)) * MASK_NEG
        if fixed_shift:
            madd = madd - m0_ref[0, 0]
        off = jnp.where(ki == qi, 0, 1)
        group = ATTN_HEAD_GROUP_FIXED if fixed_shift else ATTN_HEAD_GROUP
        for g in range(0, A_HEADS, group):
            hs = range(g, g + group)
            grp = slice(g, g + group)
            head = lambda ref, h: ref[0, :, h * A_HEAD_DIM:(h + 1) * A_HEAD_DIM]
            k2 = jnp.stack([head(k_ref, h) for h in hs])
            q2 = jnp.stack([head(q_ref, h) for h in hs])
            s = lax.dot_general(k2, q2, (((2,), (2,)), ((0,), (0,))), preferred_element_type=F32) + madd[None]
            if near:
                s = s + bias_ref[off, grp].astype(F32)
            vt2 = jnp.stack([vt_ref[h * V_ROWS:(h + 1) * V_ROWS, :] for h in hs])
            pv = lambda p: lax.dot_general(vt2, p, (((2,), (1,)), ((0,), (0,))), preferred_element_type=F32)
            if fixed_shift:
                acc_ref[grp] = acc_ref[grp] + pv(jnp.exp2(s).astype(BF16))
            else:
                m_prev = m_ref[grp]
                m_new = jnp.maximum(m_prev, jnp.max(s, axis=1, keepdims=True))
                acc_ref[grp] = jnp.exp2(m_prev - m_new) * acc_ref[grp] + pv(jnp.exp2(s - m_new).astype(BF16))
                m_ref[grp] = m_new

    far = ki + 1 < qi
    for near in (False, True):
        for fixed_shift in (False, True):
            cond = jnp.logical_and(jnp.logical_not(far) if near else far,
                                   fixed if fixed_shift else jnp.logical_not(fixed))
            pl.when(cond)(functools.partial(tile, near, fixed_shift))

    @pl.when(ki == qi)
    def _():
        outs = []
        for h in range(A_HEADS):
            a = acc_ref[h]
            outs.append((a[:A_HEAD_DIM] / a[A_HEAD_DIM:A_HEAD_DIM + 1]).T)
        o_ref[0] = (jnp.concatenate(outs, axis=-1) * _silu(z_ref[0].astype(F32))).astype(o_ref.dtype)


MAX_LOGIT_SPAN = 100.0


def softmax_shift(q_gain, k_gain, rel_bias):
    slack = 1.02
    qk = (A_HEAD_DIM ** 0.5) * LOG2E * jnp.max(jnp.abs(q_gain)) * jnp.max(jnp.abs(k_gain)) * slack
    b2 = (rel_bias - rel_bias[REL_BUCKETS - 1]) * (LOG2E * slack)
    shift = qk + jnp.max(b2)
    span = 2.0 * qk + jnp.max(b2) - jnp.min(b2)
    return shift.astype(F32), (span <= MAX_LOGIT_SPAN).astype(jnp.int32)


def attention(q, k, vt, mask, bias, proj3d, shift, use_shift, b, t, tq):
    nq = t // tq
    pairs = [(i, j) for i in range(nq) for j in range(i + 1)]
    qi_tab = jnp.asarray([p[0] for p in pairs], jnp.int32)
    ki_tab = jnp.asarray([p[1] for p in pairs], jnp.int32)
    grid_spec = pltpu.PrefetchScalarGridSpec(
        num_scalar_prefetch=3,
        grid=(b, len(pairs)),
        in_specs=[
            pl.BlockSpec((1, tq, A_WIDTH), lambda bb, p, qi, ki, fx: (bb, qi[p], 0)),
            pl.BlockSpec((1, tq, A_WIDTH), lambda bb, p, qi, ki, fx: (bb, ki[p], 0)),
            pl.BlockSpec((A_HEADS * V_ROWS, tq), lambda bb, p, qi, ki, fx: (0, bb * nq + ki[p])),
            pl.BlockSpec((1, tq, tq), lambda bb, p, qi, ki, fx: (bb, ki[p], qi[p])),
            pl.BlockSpec(bias.shape, lambda bb, p, qi, ki, fx: (0, 0, 0, 0)),
            pl.BlockSpec((1, tq, A_WIDTH), lambda bb, p, qi, ki, fx: (bb, qi[p], COL_Z_A // A_WIDTH)),
            pl.BlockSpec(memory_space=pltpu.SMEM),
        ],
        out_specs=pl.BlockSpec((1, tq, A_WIDTH), lambda bb, p, qi, ki, fx: (bb, qi[p], 0)),
        scratch_shapes=[pltpu.VMEM((A_HEADS, 1, tq), F32), pltpu.VMEM((A_HEADS, V_ROWS, tq), F32)],
    )
    return pl.pallas_call(
        functools.partial(_attn_body, tq=tq),
        grid_spec=grid_spec,
        out_shape=jax.ShapeDtypeStruct((b, t, A_WIDTH), BF16),
        compiler_params=_params("parallel", "arbitrary"),
        name="dsa_attention",
    )(qi_tab, ki_tab, use_shift.reshape(1), q.reshape(b, t, -1), k.reshape(b, t, -1), vt, mask, bias, proj3d,
      shift.reshape(1, 1))


def _gdn_prep_body(x_ref, halo_ref, sm_ref, cw_ref, alog_ref, dtb_ref,
                   q_ref, k_ref, v_ref, gcb_ref, bb_ref, grow_ref, *, tm):
    i = pl.program_id(1)
    x = x_ref[0]
    hal = halo_ref[0]
    hal = jnp.where(i > 0, hal, jnp.zeros_like(hal))
    rt = lax.broadcasted_iota(jnp.int32, (tm, tm), 0)
    ct = lax.broadcasted_iota(jnp.int32, (tm, tm), 1)
    rh = lax.broadcasted_iota(jnp.int32, (SUBLANES, BF16_ROWS), 0)
    ch = lax.broadcasted_iota(jnp.int32, (SUBLANES, BF16_ROWS), 1)
    y = cw_ref[B_CONV - 1:B_CONV, :] * x.astype(F32)
    for j in range(B_CONV - 1):
        back = B_CONV - 1 - j
        main = _dot(jnp.where(ct == rt - back, 1.0, 0.0).astype(BF16), x)
        head = _dot(jnp.where(ch == rh + (BF16_ROWS - back), 1.0, 0.0).astype(BF16), hal)
        y = y + cw_ref[j:j + 1, :] * jnp.concatenate([main[:SUBLANES] + head, main[SUBLANES:]], axis=0)
    y = _silu(y)
    for h in range(B_HEADS):
        sl = slice(h * LANES, (h + 1) * LANES)
        qh = y[:, sl]
        q_ref[0, :, sl] = qh * lax.rsqrt(jnp.sum(qh * qh, axis=-1, keepdims=True) + EPS) * (B_HEAD_DIM ** -0.5)
        kh = y[:, B_WIDTH + h * LANES:B_WIDTH + (h + 1) * LANES]
        k_ref[0, :, sl] = kh * lax.rsqrt(jnp.sum(kh * kh, axis=-1, keepdims=True) + EPS)
    v_ref[0] = y[:, 2 * B_WIDTH:]

    sm = sm_ref[0]
    xg = sm + dtb_ref[...]
    softplus = jnp.maximum(xg, 0.0) + jnp.log(1.0 + jnp.exp(-jnp.abs(xg)))
    g = -jnp.exp(alog_ref[...]) * softplus
    r = lax.broadcasted_iota(jnp.int32, (tm, tm), 0)
    c = lax.broadcasted_iota(jnp.int32, (tm, tm), 1)
    sh = int(math.log2(GDN_CHUNK))
    same_chunk = jnp.right_shift(r, sh) == jnp.right_shift(c, sh)
    tri = jnp.where(jnp.logical_and(same_chunk, c <= r), 1.0, 0.0).astype(F32)
    gc = lax.dot_general(tri, g, (((1,), (0,)), ((), ())), precision=HIGHEST, preferred_element_type=F32)
    beta = _sigmoid(sm)
    for h in range(B_HEADS):
        sl = slice(h * LANES, (h + 1) * LANES)
        gcb_ref[0, :, sl] = jnp.broadcast_to(gc[:, SM_ALPHA + h:SM_ALPHA + h + 1], (tm, LANES))
        bb_ref[0, :, sl] = jnp.broadcast_to(beta[:, SM_BETA + h:SM_BETA + h + 1], (tm, LANES))
    gct = gc.T
    for cc in range(tm // GDN_CHUNK):
        grow_ref[0, cc] = gct[SM_ALPHA:SM_ALPHA + B_HEADS, cc * GDN_CHUNK:(cc + 1) * GDN_CHUNK]


def gdn_prep(proj3d, small3d, conv_w, a_log, dt_bias, b, t):
    tm = _tile(t, 256)
    pad_row = lambda v: jnp.zeros((1, LANES), F32).at[0, SM_ALPHA:SM_ALPHA + B_HEADS].set(v)
    nc = tm // GDN_CHUNK
    act = jax.ShapeDtypeStruct((b, t, B_WIDTH), F32)
    full = lambda a: pl.BlockSpec(a.shape, lambda bb, i: (0,) * a.ndim)
    consts = [conv_w, pad_row(a_log), pad_row(dt_bias)]
    return pl.pallas_call(
        functools.partial(_gdn_prep_body, tm=tm),
        grid=(b, t // tm),
        in_specs=[
            pl.BlockSpec((1, tm, 3 * B_WIDTH), lambda bb, i: (bb, i, COL_QKV_B // (3 * B_WIDTH))),
            pl.BlockSpec((1, BF16_ROWS, 3 * B_WIDTH),
                         lambda bb, i: (bb, jnp.maximum(i * (tm // BF16_ROWS) - 1, 0), 0)),
            pl.BlockSpec((1, tm, LANES), lambda bb, i: (bb, i, 0)),
        ] + [full(c) for c in consts],
        out_specs=[pl.BlockSpec((1, tm, B_WIDTH), lambda bb, i: (bb, i, 0))] * 5
        + [pl.BlockSpec((1, nc, B_HEADS, GDN_CHUNK), lambda bb, i: (bb, i, 0, 0))],
        out_shape=[act] * 5 + [jax.ShapeDtypeStruct((b, t // GDN_CHUNK, B_HEADS, GDN_CHUNK), F32)],
        compiler_params=_params("parallel", "parallel"),
        name="gdn_prep",
    )(proj3d, proj3d, small3d, *consts)


def _gdn_body(q_ref, k_ref, v_ref, gcb_ref, bb_ref, grow_ref, z_ref, on_ref, o_ref, s_ref, *, nc):
    @pl.when(pl.program_id(1) == 0)
    def _():
        s_ref[...] = jnp.zeros(s_ref.shape, F32)

    cs, nh = GDN_CHUNK, B_HEADS
    nb = nc * nh
    ri = lax.broadcasted_iota(jnp.int32, (nb, cs, cs), 1)
    ci = lax.broadcasted_iota(jnp.int32, (nb, cs, cs), 2)
    lower = ci <= ri
    strict = ci < ri
    eye = jnp.where(ci == ri, 1.0, 0.0).astype(F32)
    bf = lambda a: a.astype(BF16)
    bmm = lambda a, b: lax.dot_general(a, b, (((2,), (1,)), ((0,), (0,))), preferred_element_type=F32)
    bmm_nt = lambda a, b: lax.dot_general(a, b, (((2,), (2,)), ((0,), (0,))), preferred_element_type=F32)
    bmm_tn = lambda a, b: lax.dot_general(a, b, (((1,), (1,)), ((0,), (0,))), preferred_element_type=F32)

    def stack(ref):
        return jnp.stack([ref[0, c * cs:(c + 1) * cs, h * LANES:(h + 1) * LANES]
                          for c in range(nc) for h in range(nh)])

    q, k, v = stack(q_ref), stack(k_ref), stack(v_ref)
    gcb = stack(gcb_ref)
    beta = stack(bb_ref)
    grow = jnp.stack([grow_ref[0, c, h:h + 1, :] for c in range(nc) for h in range(nh)])
    diff = gcb[:, :, :cs] - grow
    decay = jnp.where(lower, jnp.exp(jnp.where(lower, diff, 0.0)), 0.0)
    eg = jnp.exp(gcb)
    glast = gcb[:, cs - 1:cs, :]
    kb = k * beta
    lmat = jnp.where(strict, bmm_nt(bf(kb), bf(k)) * decay, 0.0)
    n = -lmat
    tinv = eye + n
    for _ in range(int(math.log2(cs)) - 1):
        n = bmm(bf(n), bf(n))
        tinv = tinv + bmm(bf(tinv), bf(n))
    uw = bmm(bf(tinv), bf(jnp.concatenate([v * beta, kb * eg], axis=-1)))
    attn = bf(jnp.where(lower, bmm_nt(bf(q), bf(k)) * decay, 0.0))
    qg = bf(q * eg)
    kdec = bf(k * jnp.exp(glast - gcb))
    egl = jnp.exp(glast)

    s = s_ref[...]
    for c in range(nc):
        sl = slice(c * nh, (c + 1) * nh)
        sb = bf(s)
        v_new = uw[sl, :, :LANES] - bmm(bf(uw[sl, :, LANES:]), sb)
        o = bmm(qg[sl], sb) + bmm(attn[sl], bf(v_new))
        s = s * egl[sl] + bmm_tn(kdec[sl], bf(v_new))
        o = o * lax.rsqrt(jnp.mean(o * o, axis=-1, keepdims=True) + EPS) * on_ref[...]
        for h in range(nh):
            rows, cols = slice(c * cs, (c + 1) * cs), slice(h * LANES, (h + 1) * LANES)
            o_ref[0, rows, cols] = (o[h] * _silu(z_ref[0, rows, cols].astype(F32))).astype(o_ref.dtype)
    s_ref[...] = s


def gdn_scan(qh, kh, v, gcb, bb, grow, proj3d, o_norm, b, t):
    tt = _tile(t, 256)
    nc = tt // GDN_CHUNK
    blk = pl.BlockSpec((1, tt, B_WIDTH), lambda bb_, i: (bb_, i, 0))
    return pl.pallas_call(
        functools.partial(_gdn_body, nc=nc),
        grid=(b, t // tt),
        in_specs=[blk] * 5 + [
            pl.BlockSpec((1, nc, B_HEADS, GDN_CHUNK), lambda bb_, i: (bb_, i, 0, 0)),
            pl.BlockSpec((1, tt, B_WIDTH), lambda bb_, i: (bb_, i, COL_Z_B // B_WIDTH)),
            pl.BlockSpec((1, LANES), lambda bb_, i: (0, 0)),
        ],
        out_specs=blk,
        out_shape=jax.ShapeDtypeStruct((b, t, B_WIDTH), BF16),
        scratch_shapes=[pltpu.VMEM((B_HEADS, B_HEAD_DIM, B_HEAD_DIM), F32)],
        compiler_params=_params("parallel", "arbitrary"),
        name="gdn_scan",
    )(qh, kh, v, gcb, bb, grow, proj3d, o_norm.reshape(1, LANES))


C_HALO = 32
D_HALO = BF16_ROWS


def _cd_mix_body(a_ref, g_ref, ah_ref, gh_ref, zc_ref, bg_ref, cg_ref, ud_ref, cgh_ref, udh_ref, zd_ref,
                 dww_ref, dwb_ref, lnw_ref, lnb_ref, dcw_ref, yc_ref, yd_ref, us_ref, ds_ref, *, tm):
    i = pl.program_id(1)
    first = i == 0
    f32 = lambda r: r[0].astype(F32)
    uh = f32(ah_ref) * _sigmoid(f32(gh_ref))
    us_ref[0, 0:C_HALO, :] = jnp.where(first, jnp.zeros_like(uh), uh)
    us_ref[0, C_HALO:, :] = f32(a_ref) * _sigmoid(f32(g_ref))
    span = tm + C_HALO - SUBLANES
    for r in range(1, SUBLANES):
        us_ref[r, 0:span, :] = us_ref[0, pl.ds(r, span), :]
    u = jnp.zeros((tm, C_WIDTH), F32)
    for j in range(C_CONV):
        off = C_HALO - (C_CONV - 1) + j
        r, base = off % SUBLANES, off - off % SUBLANES
        u = u + dww_ref[j:j + 1, :] * us_ref[r, base:base + tm, :]
    u = u + dwb_ref[...]
    mu = jnp.mean(u, axis=-1, keepdims=True)
    uc = u - mu
    u = uc * lax.rsqrt(jnp.mean(uc * uc, axis=-1, keepdims=True) + EPS) * lnw_ref[...] + lnb_ref[...]
    yc_ref[0] = (_silu(u) * _silu(f32(zc_ref))).astype(yc_ref.dtype)

    dh = f32(cgh_ref) * f32(udh_ref)
    ds_ref[0:D_HALO, :] = jnp.where(first, jnp.zeros_like(dh), dh)
    ds_ref[D_HALO:, :] = f32(cg_ref) * f32(ud_ref)
    d = jnp.zeros((tm, D_WIDTH), F32)
    for j in range(D_CONV):
        d = d + dcw_ref[j:j + 1, :] * ds_ref[pl.ds(D_HALO - (D_CONV - 1) + j, tm), :]
    yd_ref[0] = (f32(bg_ref) * d * _silu(f32(zd_ref))).astype(yd_ref.dtype)


def cd_mix(proj3d, dw_w, dw_b, ln_w, ln_b, d_conv_w, b, t):
    tm = _tile(t, 256)
    w = C_WIDTH
    col = lambda n: pl.BlockSpec((1, tm, w), lambda bb, i, n=n: (bb, i, n))
    halo = lambda n, rows: pl.BlockSpec(
        (1, rows, w), lambda bb, i, n=n, rows=rows: (bb, jnp.maximum(i * (tm // rows) - 1, 0), n))
    row = lambda a: a.reshape(1, -1)
    full = lambda a: pl.BlockSpec(a.shape, lambda bb, i: (0,) * a.ndim)
    consts = [dw_w, row(dw_b), row(ln_w), row(ln_b), d_conv_w]
    out = jax.ShapeDtypeStruct((b, t, w), BF16)
    return pl.pallas_call(
        functools.partial(_cd_mix_body, tm=tm),
        grid=(b, t // tm),
        in_specs=[col(0), col(1), halo(0, C_HALO), halo(1, C_HALO), col(2), col(3), col(4), col(5),
                  halo(4, D_HALO), halo(5, D_HALO), col(6)] + [full(c) for c in consts],
        out_specs=[pl.BlockSpec((1, tm, w), lambda bb, i: (bb, i, 0))] * 2,
        out_shape=[out, out],
        scratch_shapes=[pltpu.VMEM((SUBLANES, tm + C_HALO, w), F32), pltpu.VMEM((tm + D_HALO, w), F32)],
        compiler_params=_params("parallel", "parallel"),
        name="cd_mix",
    )(*([proj3d] * 11), *consts)


def _reorder_ab_w_in(w):
    offs = [0]
    for s in AB_SPLITS:
        offs.append(offs[-1] + s)
    part = lambda n: w[:, offs[n]:offs[n + 1]]
    c_q, c_kv, k_idx, w_idx, z_a, qkv_b, beta_b, alpha_b, z_b = (part(n) for n in range(9))
    small = jnp.concatenate([k_idx, w_idx, beta_b, alpha_b], axis=1)
    small = jnp.pad(small, ((0, 0), (0, LANES - small.shape[1])))
    out = jnp.concatenate([qkv_b, z_a, z_b, c_q, c_kv], axis=1)
    return jnp.pad(out, ((0, 0), (0, AB_IN_PAD - out.shape[1]))).astype(BF16), small.astype(BF16)


def _ab_layer(x2d, b, t, norm_w, rel_bias, w_in, q_norm, w_uq, w_iq, kv_norm, w_uk, w_uv, q_gain, k_gain,
              ik_w, ik_b, conv_w, a_log, dt_bias, o_norm, w_out):
    proj, small = norm_matmul(x2d, norm_w, *_reorder_ab_w_in(w_in))
    proj3d, small3d = proj.reshape(b, t, AB_IN_PAD), small.reshape(b, t, LANES)
    q, k, v, qi, k_even, k_odd = dsa_prep(proj, small, q_norm, kv_norm, w_uq, w_iq, w_uk, w_uv, q_gain, k_gain,
                                          ik_w, ik_b)
    n_sel = min(TOPK_MAX, t // 4)
    mask = indexer_mask(qi, small3d, k_even, k_odd, b, t, n_sel)
    tq = _tile(t, 512)
    shift, use_shift = softmax_shift(q_gain, k_gain, rel_bias)
    y_a = attention(q, k, v, mask, bias_tiles(rel_bias, tq), proj3d, shift, use_shift, b, t, tq)
    qh, kh, vv, gcb, bb, grow = gdn_prep(proj3d, small3d, conv_w, a_log, dt_bias, b, t)
    y_b = gdn_scan(qh, kh, vv, gcb, bb, grow, proj3d, o_norm, b, t)
    wo = w_out.astype(BF16)
    return out_proj(x2d, y_a.reshape(b * t, -1), y_b.reshape(b * t, -1), wo[:A_WIDTH], wo[A_WIDTH:])


def _cd_layer(x2d, b, t, norm_w, w_in, dw_w, dw_b, ln_w, ln_b, d_conv_w, w_out):
    proj = norm_matmul(x2d, norm_w, w_in.astype(BF16))
    y_c, y_d = cd_mix(proj.reshape(b, t, -1), dw_w, dw_b, ln_w, ln_b, d_conv_w, b, t)
    wo = w_out.astype(BF16)
    return out_proj(x2d, y_c.reshape(b * t, -1), y_d.reshape(b * t, -1), wo[:C_WIDTH], wo[C_WIDTH:])


def kernel(x, norm_w, rel_bias, ab_w_in, a_q_norm, a_w_uq, a_w_iq, a_kv_norm, a_w_uk, a_w_uv, a_q_gain,
           a_k_gain, a_ik_norm_w, a_ik_norm_b, b_conv_w, b_a_log, b_dt_bias, b_o_norm, ab_w_out, cd_w_in,
           c_dw_w, c_dw_b, c_ln_w, c_ln_b, d_conv_w, cd_w_out):
    b, t, d = x.shape
    depth = norm_w.shape[0]
    x2d = x.reshape(b * t, d)
    for i in range(depth):
        j = i // 2
        if i % 2 == 0:
            x2d = _ab_layer(x2d, b, t, norm_w[i], rel_bias, ab_w_in[j], a_q_norm[j], a_w_uq[j], a_w_iq[j],
                            a_kv_norm[j], a_w_uk[j], a_w_uv[j], a_q_gain[j], a_k_gain[j], a_ik_norm_w[j],
                            a_ik_norm_b[j], b_conv_w[j], b_a_log[j], b_dt_bias[j], b_o_norm[j], ab_w_out[j])
        else:
            x2d = _cd_layer(x2d, b, t, norm_w[i], cd_w_in[j], c_dw_w[j], c_dw_b[j], c_ln_w[j], c_ln_b[j],
                            d_conv_w[j], cd_w_out[j])
    return x2d.reshape(b, t, d)
```

```python
import functools
import math

import jax
import jax.numpy as jnp
from jax import lax
from jax.experimental import pallas as pl
from jax.experimental.pallas import tpu as pltpu

A_HEADS = 8
A_HEAD_DIM = 128
A_WIDTH = A_HEADS * A_HEAD_DIM
A_Q_LORA = 512
A_KV_LORA = 256
IDX_HEADS = 16
IDX_DIM = 64
TOPK_MAX = 256
REL_BUCKETS = 32
REL_MAX_DIST = 128
B_HEADS = 8
B_HEAD_DIM = 128
B_WIDTH = B_HEADS * B_HEAD_DIM
B_CONV = 4
GDN_CHUNK = 64
C_WIDTH = 1024
C_CONV = 31
D_WIDTH = 1024
D_CONV = 3
EPS = 1e-6

AB_SPLITS = (A_Q_LORA, A_KV_LORA, IDX_DIM, IDX_HEADS, A_WIDTH, 3 * B_WIDTH, B_HEADS, B_HEADS, B_WIDTH)
AB_IN_PAD = 6144
SM_KIDX = 0
SM_WIDX = 64
SM_BETA = 80
SM_ALPHA = 88
COL_QKV_B = 0
COL_Z_A = 3072
COL_Z_B = 4096
COL_C_Q = 5120
COL_C_KV = 5632

LANES = 128
MASK_NEG = -1e30
INT_MIN = -(2 ** 31)
LOG2E = math.log2(math.e)
BF16_ROWS = 16
V_ROWS = A_HEAD_DIM + BF16_ROWS
VMEM_LIMIT = 56 * 1024 * 1024

F32 = jnp.float32
BF16 = jnp.bfloat16
HIGHEST = lax.Precision.HIGHEST


def _tile(n, pref):
    t = min(n, pref)
    assert n % t == 0, (n, t)
    return t


def _params(*sem):
    return pltpu.CompilerParams(dimension_semantics=sem, vmem_limit_bytes=VMEM_LIMIT)


def _dot(a, b):
    return jnp.dot(a, b, preferred_element_type=F32)


def _dot_nt(a, b):
    return lax.dot_general(a, b, (((1,), (1,)), ((), ())), preferred_element_type=F32)


def _silu(x):
    return x * (1.0 / (1.0 + jnp.exp(-x)))


def _sigmoid(x):
    return 1.0 / (1.0 + jnp.exp(-x))


def _norm_matmul_body(*refs, with_f32_cols):
    if with_f32_cols:
        x_ref, nw_ref, w_ref, ws_ref, o_ref, os_ref, h_ref = refs
    else:
        x_ref, nw_ref, w_ref, o_ref, h_ref = refs

    @pl.when(pl.program_id(1) == 0)
    def _():
        x = x_ref[...]
        ms = jnp.mean(x * x, axis=-1, keepdims=True)
        h_ref[...] = (x * lax.rsqrt(ms + EPS) * nw_ref[...]).astype(h_ref.dtype)
        if with_f32_cols:
            os_ref[...] = _dot(h_ref[...], ws_ref[...])

    o_ref[...] = _dot(h_ref[...], w_ref[...]).astype(o_ref.dtype)


def norm_matmul(x2d, norm_w, w_bf16, w_f32_cols=None):
    m, k = x2d.shape
    n = w_bf16.shape[1]
    tm = _tile(m, 1024)
    tn = max(d for d in range(256, min(n, 2048) + 1, 256) if n % d == 0)
    extra = w_f32_cols is not None
    in_specs = [
        pl.BlockSpec((tm, k), lambda i, j: (i, 0)),
        pl.BlockSpec((1, k), lambda i, j: (0, 0)),
        pl.BlockSpec((k, tn), lambda i, j: (0, j)),
    ]
    out_specs = [pl.BlockSpec((tm, tn), lambda i, j: (i, j))]
    out_shape = [jax.ShapeDtypeStruct((m, n), BF16)]
    args = [x2d, norm_w.reshape(1, k), w_bf16]
    if extra:
        in_specs.append(pl.BlockSpec((k, LANES), lambda i, j: (0, 0)))
        out_specs.append(pl.BlockSpec((tm, LANES), lambda i, j: (i, 0)))
        out_shape.append(jax.ShapeDtypeStruct((m, LANES), F32))
        args.append(w_f32_cols)
    res = pl.pallas_call(
        functools.partial(_norm_matmul_body, with_f32_cols=extra),
        grid=(m // tm, n // tn),
        in_specs=in_specs,
        out_specs=out_specs,
        out_shape=out_shape,
        scratch_shapes=[pltpu.VMEM((tm, k), BF16)],
        compiler_params=_params("parallel", "arbitrary"),
        name="norm_matmul",
    )(*args)
    return res if extra else res[0]


def _out_proj_body(x_ref, ya_ref, yb_ref, wa_ref, wb_ref, o_ref):
    o_ref[...] = x_ref[...] + _dot(ya_ref[...], wa_ref[...]) + _dot(yb_ref[...], wb_ref[...])


def out_proj(x2d, ya, yb, wa_bf16, wb_bf16):
    m, n = x2d.shape
    ka, kb = ya.shape[1], yb.shape[1]
    tm = _tile(m, 512)
    return pl.pallas_call(
        _out_proj_body,
        grid=(m // tm,),
        in_specs=[
            pl.BlockSpec((tm, n), lambda i: (i, 0)),
            pl.BlockSpec((tm, ka), lambda i: (i, 0)),
            pl.BlockSpec((tm, kb), lambda i: (i, 0)),
            pl.BlockSpec((ka, n), lambda i: (0, 0)),
            pl.BlockSpec((kb, n), lambda i: (0, 0)),
        ],
        out_specs=pl.BlockSpec((tm, n), lambda i: (i, 0)),
        out_shape=jax.ShapeDtypeStruct((m, n), F32),
        compiler_params=_params("parallel"),
        name="out_proj",
    )(x2d, ya, yb, wa_bf16, wb_bf16)


def _head_rmsnorm(y, gain_row, scale):
    outs = []
    for h in range(y.shape[1] // LANES):
        seg = y[:, h * LANES:(h + 1) * LANES]
        ms = jnp.mean(seg * seg, axis=-1, keepdims=True)
        outs.append(seg * lax.rsqrt(ms + EPS) * (gain_row * scale))
    return jnp.concatenate(outs, axis=-1)


def _dsa_prep_body(cq_ref, ckv_ref, sm_ref, qn_ref, kvn_ref, wuq_ref, wiq_ref, wuk_ref, wuv_ref,
                   qg_ref, kg_ref, ikw_ref, ikb_ref,
                   q_ref, k_ref, v_ref, qi_ref, ke_ref, ko_ref):
    cq = cq_ref[...].astype(F32)
    cq = cq * lax.rsqrt(jnp.mean(cq * cq, axis=-1, keepdims=True) + EPS) * qn_ref[...]
    cqb = cq.astype(BF16)
    q = _dot(cqb, wuq_ref[...])
    q_ref[...] = _head_rmsnorm(q, qg_ref[...], A_HEAD_DIM ** -0.5 * LOG2E).astype(q_ref.dtype)
    qi_ref[...] = _dot(cqb, wiq_ref[...]).astype(qi_ref.dtype)

    ckv = ckv_ref[...].astype(F32)
    ckv = ckv * lax.rsqrt(jnp.mean(ckv * ckv, axis=-1, keepdims=True) + EPS) * kvn_ref[...]
    ckvb = ckv.astype(BF16)
    k = _dot(ckvb, wuk_ref[...])
    k_ref[...] = _head_rmsnorm(k, kg_ref[...], 1.0).astype(k_ref.dtype)
    vt = _dot_nt(wuv_ref[...], ckvb).astype(v_ref.dtype)
    ones = jnp.ones((BF16_ROWS, vt.shape[1]), v_ref.dtype)
    for h in range(A_HEADS):
        v_ref[h * V_ROWS:h * V_ROWS + A_HEAD_DIM, :] = vt[h * A_HEAD_DIM:(h + 1) * A_HEAD_DIM, :]
        v_ref[h * V_ROWS + A_HEAD_DIM:(h + 1) * V_ROWS, :] = ones

    kx = sm_ref[...][:, SM_KIDX:SM_KIDX + IDX_DIM]
    mu = jnp.mean(kx, axis=-1, keepdims=True)
    kc = kx - mu
    kl = kc * lax.rsqrt(jnp.mean(kc * kc, axis=-1, keepdims=True) + EPS) * ikw_ref[...] + ikb_ref[...]
    z = jnp.zeros_like(kl)
    ke_ref[...] = jnp.concatenate([kl, z], axis=-1).astype(ke_ref.dtype)
    ko_ref[...] = jnp.concatenate([z, kl], axis=-1).astype(ko_ref.dtype)


def dsa_prep(proj, small, q_norm, kv_norm, w_uq, w_iq, w_uk, w_uv, q_gain, k_gain, ik_w, ik_b):
    m = proj.shape[0]
    tm = _tile(m, 512)
    row = lambda a: a.reshape(1, -1)
    full = lambda a: pl.BlockSpec(a.shape, lambda i: (0,) * a.ndim)
    consts = [row(q_norm), row(kv_norm), w_uq.astype(BF16), w_iq.astype(BF16), w_uk.astype(BF16),
              w_uv.T.astype(BF16), row(q_gain), row(k_gain), row(ik_w), row(ik_b)]
    outs = [jax.ShapeDtypeStruct((m, A_WIDTH), BF16)] * 2 + [
        jax.ShapeDtypeStruct((A_HEADS * V_ROWS, m), BF16),
        jax.ShapeDtypeStruct((m, IDX_HEADS * IDX_DIM), BF16),
        jax.ShapeDtypeStruct((m, LANES), BF16), jax.ShapeDtypeStruct((m, LANES), BF16)]
    out_specs = [pl.BlockSpec((tm, s.shape[1]), lambda i: (i, 0)) for s in outs]
    out_specs[2] = pl.BlockSpec((A_HEADS * V_ROWS, tm), lambda i: (0, i))
    return pl.pallas_call(
        _dsa_prep_body,
        grid=(m // tm,),
        in_specs=[
            pl.BlockSpec((tm, A_Q_LORA), lambda i: (i, COL_C_Q // A_Q_LORA)),
            pl.BlockSpec((tm, A_KV_LORA), lambda i: (i, COL_C_KV // A_KV_LORA)),
            pl.BlockSpec((tm, LANES), lambda i: (i, 0)),
        ] + [full(c) for c in consts],
        out_specs=out_specs,
        out_shape=outs,
        compiler_params=_params("parallel"),
        name="dsa_prep",
    )(proj, proj, small, *consts)


def _t5_bucket(dist):
    max_exact = REL_BUCKETS // 2
    large = max_exact + (jnp.log(jnp.maximum(dist, 1).astype(F32) / max_exact)
                         / math.log(REL_MAX_DIST / max_exact) * (REL_BUCKETS - max_exact)).astype(jnp.int32)
    large = jnp.minimum(large, REL_BUCKETS - 1)
    return jnp.where(dist < max_exact, dist, large)


def _bias_tiles_body(rb_ref, o_ref, *, t):
    h = pl.program_id(0)
    sb = min(t, LANES)
    row = lax.broadcasted_iota(jnp.int32, (sb, sb), 0)
    col = lax.broadcasted_iota(jnp.int32, (sb, sb), 1)
    for typ in range(2):
        for rk in range(t // sb):
            for cq in range(t // sb):
                base = typ * t + (cq - rk) * sb
                blk = (typ, 0, slice(rk * sb, (rk + 1) * sb), slice(cq * sb, (cq + 1) * sb))
                if base - (sb - 1) >= REL_MAX_DIST or base + (sb - 1) < 0:
                    o_ref[blk] = jnp.zeros((sb, sb), o_ref.dtype)
                    continue
                bucket = _t5_bucket(jnp.maximum(base + col - row, 0))
                acc = jnp.zeros((sb, sb), F32)
                for b in range(REL_BUCKETS):
                    acc = jnp.where(bucket == b, rb_ref[b, h], acc)
                o_ref[blk] = ((acc - rb_ref[REL_BUCKETS - 1, h]) * LOG2E).astype(o_ref.dtype)


def bias_tiles(rel_bias, t):
    return pl.pallas_call(
        functools.partial(_bias_tiles_body, t=t),
        grid=(A_HEADS,),
        in_specs=[pl.BlockSpec(memory_space=pltpu.SMEM)],
        out_specs=pl.BlockSpec((2, 1, t, t), lambda h: (0, h, 0, 0)),
        out_shape=jax.ShapeDtypeStruct((2, A_HEADS, t, t), BF16),
        compiler_params=_params("parallel"),
        name="bias_tiles",
    )(rel_bias)


SUBLANES = 8
MAX_SELECT_STEPS = 24
UPPER_STEPS_PER_CHECK = 4
LOWER_STEPS_PER_CHECK = 2
HALF_BITS = 16
HALF_SPAN = 2 ** HALF_BITS
I16_MIN = -(2 ** (HALF_BITS - 1))


def _key_of(x):
    bits = pltpu.bitcast(x, jnp.int32)
    key = jnp.where(bits < 0, bits ^ jnp.int32(0x7FFFFFFF), bits)
    return jnp.where(bits == jnp.int32(INT_MIN), 0, key)


def _indexer_body(qi_ref, sm_ref, ke_ref, ko_ref, mask_ref, key_ref, hi_ref, low_ref, w_ref, *, tq, ck, n_sel):
    i = pl.program_id(1)
    nk = ((i + 1) * tq + ck - 1) // ck
    groups = ck // SUBLANES
    scale = (IDX_HEADS ** -0.5) * (IDX_DIM ** -0.5)
    smt = sm_ref[0].T
    for h in range(IDX_HEADS):
        w_ref[h] = jnp.broadcast_to(smt[SM_WIDX + h:SM_WIDX + h + 1, :] * scale, (SUBLANES, tq))
    kloc = lax.broadcasted_iota(jnp.int32, (ck, tq), 0)
    qpos = i * tq + lax.broadcasted_iota(jnp.int32, (ck, tq), 1)
    int_max = jnp.int32(2 ** 31 - 1)

    def score_chunk(c, carry):
        kmin, kmax = carry
        ks = pl.multiple_of(c * ck, ck)
        ke = ke_ref[0, pl.ds(ks, ck), :]
        ko = ko_ref[0, pl.ds(ks, ck), :]
        acc = jnp.zeros((ck, tq), F32)
        for j in range(IDX_HEADS // 2):
            qp = qi_ref[0, :, j * LANES:(j + 1) * LANES]
            for par, kk in ((0, ke), (1, ko)):
                s = jnp.maximum(_dot_nt(kk, qp), 0.0)
                acc = acc + s * jnp.tile(w_ref[2 * j + par], (groups, 1))
        key = _key_of(acc)
        valid = c * ck + kloc <= qpos
        masked = jnp.where(valid, key, jnp.int32(INT_MIN))
        key_ref[pl.ds(ks, ck), :] = masked
        hi_ref[pl.ds(ks, ck), :] = jnp.right_shift(masked, HALF_BITS).astype(jnp.int16)
        kmin = jnp.minimum(kmin, jnp.min(jnp.where(valid, key, int_max).reshape(groups, SUBLANES, tq), axis=0))
        kmax = jnp.maximum(kmax, jnp.max(masked.reshape(groups, SUBLANES, tq), axis=0))
        return kmin, kmax

    kmin, kmax = lax.fori_loop(0, nk, score_chunk, (jnp.full((SUBLANES, tq), int_max, jnp.int32),
                                                    jnp.full((SUBLANES, tq), INT_MIN, jnp.int32)))

    rows16 = ck // BF16_ROWS
    i16_min = I16_MIN

    def scan16(a_ref, init, fn):
        def body(c, acc):
            ks = pl.multiple_of(c * ck, ck)
            return fn(acc, a_ref[pl.ds(ks, ck), :].reshape(rows16, BF16_ROWS, tq))

        return lax.fori_loop(0, nk, body, init)

    def count_ge16(a_ref, p):
        p16 = jnp.broadcast_to(p.astype(jnp.int16), (BF16_ROWS, tq))

        def fn(cnt, a):
            ind = jnp.where(a >= p16[None], jnp.int16(1), jnp.int16(0))
            for g in range(rows16):
                cnt = cnt + ind[g]
            return cnt

        cnt = scan16(a_ref, jnp.zeros((BF16_ROWS, tq), jnp.int16), fn)
        return jnp.sum(cnt.astype(jnp.int32), axis=0, keepdims=True)

    def max_le16(a_ref, h):
        h16 = jnp.broadcast_to(h.astype(jnp.int16), (BF16_ROWS, tq))

        def fn(acc, a):
            v = jnp.where(a <= h16[None], a, jnp.int16(i16_min))
            for g in range(rows16):
                acc = jnp.where(v[g] > acc, v[g], acc)
            return acc

        acc = scan16(a_ref, jnp.full((BF16_ROWS, tq), i16_min, jnp.int16), fn)
        return jnp.max(acc.astype(jnp.int32), axis=0, keepdims=True)

    def rank_select16(a_ref, r, skip, lo, hi, clo, chi, floor_count, steps_per_check):
        def closed(lo, hi, clo):
            return skip | (clo == r) | (hi == lo + 1)

        def open_rows(lo, hi, clo, chi):
            return jnp.logical_not(closed(lo, hi, clo) | (chi == r - 1))

        def cond(st):
            it, lo, hi, clo, chi = st
            n_open = jnp.max(jnp.where(open_rows(lo, hi, clo, chi), 1, 0))
            return jnp.logical_and(it < MAX_SELECT_STEPS, n_open > 0)

        def step(st):
            it, lo, hi, clo, chi = st
            for _ in range(steps_per_check):
                upd = open_rows(lo, hi, clo, chi)
                p = lo + jnp.maximum(jnp.right_shift(hi - lo, 1), 1)
                cnt = count_ge16(a_ref, p)
                ge = cnt >= r
                up, dn = upd & ge, upd & jnp.logical_not(ge)
                lo, hi = jnp.where(up, p, lo), jnp.where(dn, p, hi)
                clo, chi = jnp.where(up, cnt, clo), jnp.where(dn, cnt, chi)
            return it + steps_per_check, lo, hi, clo, chi

        _, lo, hi, clo, chi = lax.while_loop(cond, step, (jnp.int32(0), lo, hi, clo, chi))
        last = jnp.logical_not(closed(lo, hi, clo))
        t_last = max_le16(a_ref, hi - 1)
        c_last = jnp.where(t_last == i16_min, floor_count, count_ge16(a_ref, t_last))
        return jnp.where(last, t_last, lo), jnp.where(last, c_last, clo), chi

    n_valid = i * tq + lax.broadcasted_iota(jnp.int32, (1, tq), 1) + 1
    few = n_valid <= n_sel
    zeros = jnp.zeros((1, tq), jnp.int32)
    hi_lo0 = jnp.right_shift(jnp.min(kmin, axis=0, keepdims=True), HALF_BITS)
    hi_hi0 = jnp.right_shift(jnp.max(kmax, axis=0, keepdims=True), HALF_BITS) + 1
    t_hi, c_ge_hi, c_gt_hi = rank_select16(hi_ref, n_sel, few, hi_lo0, hi_hi0, n_valid, zeros, n_valid,
                                           UPPER_STEPS_PER_CHECK)
    exact_hi = c_ge_hi == n_sel

    t_hi16 = jnp.broadcast_to(t_hi.astype(jnp.int16), (BF16_ROWS, tq))

    def fill_low(c, carry):
        ks = pl.multiple_of(c * ck, ck)
        low = (key_ref[pl.ds(ks, ck), :] & (HALF_SPAN - 1)) + I16_MIN
        low = low.astype(jnp.int16).reshape(rows16, BF16_ROWS, tq)
        same = hi_ref[pl.ds(ks, ck), :].reshape(rows16, BF16_ROWS, tq) == t_hi16[None]
        low_ref[pl.ds(ks, ck), :] = jnp.where(same, low, jnp.int16(i16_min)).reshape(ck, tq)
        return carry

    lax.fori_loop(0, nk, fill_low, 0)
    in_bucket = c_ge_hi - c_gt_hi
    t_lo, c_ge_lo, c_gt_lo = rank_select16(low_ref, n_sel - c_gt_hi, few | exact_hi,
                                           jnp.full((1, tq), I16_MIN, jnp.int32), jnp.full((1, tq), -I16_MIN, jnp.int32),
                                           in_bucket, zeros, in_bucket, LOWER_STEPS_PER_CHECK)
    lo = jnp.where(exact_hi, t_hi * HALF_SPAN, t_hi * HALF_SPAN + (t_lo - I16_MIN))
    clo = jnp.where(exact_hi, n_sel, c_gt_hi + c_ge_lo)
    chi = c_gt_hi + c_gt_lo
    thr = jnp.where(few, jnp.int32(INT_MIN + 1), lo)
    thr8 = jnp.broadcast_to(thr, (SUBLANES, tq))
    tied = jnp.logical_not(few) & (clo > n_sel)
    need = n_sel - chi
    n_keys = mask_ref.shape[1]

    def tie_cut(_):
        def count_le(j):
            j8 = jnp.broadcast_to(j, (SUBLANES, tq))

            def body(c, cnt):
                ks = pl.multiple_of(c * ck, ck)
                kk = key_ref[pl.ds(ks, ck), :].reshape(groups, SUBLANES, tq)
                idx = (c * ck + kloc).reshape(groups, SUBLANES, tq)
                hit = jnp.where(kk == thr8[None], jnp.where(idx <= j8[None], 1, 0), 0)
                return cnt + jnp.sum(hit, axis=0)

            cnt = lax.fori_loop(0, nk, body, jnp.zeros((SUBLANES, tq), jnp.int32))
            return jnp.sum(cnt, axis=0, keepdims=True)

        def bisect(_, st):
            jlo, jhi = st
            mid = jlo + jnp.right_shift(jhi - jlo, 1)
            ok = count_le(mid) >= need
            return jnp.where(ok, jlo, mid), jnp.where(ok, mid, jhi)

        _, jhi = lax.fori_loop(0, n_keys.bit_length(), bisect,
                               (jnp.full((1, tq), -1, jnp.int32), jnp.full((1, tq), n_keys - 1, jnp.int32)))
        return jnp.where(tied, jhi, int_max)

    jcut = lax.cond(jnp.max(jnp.where(tied, 1, 0)) > 0, tie_cut, lambda _: jnp.full((1, tq), int_max, jnp.int32), 0)
    jcut8 = jnp.broadcast_to(jcut, (SUBLANES, tq))

    def write_chunk(c, carry):
        ks = pl.multiple_of(c * ck, ck)
        kk = key_ref[pl.ds(ks, ck), :].reshape(groups, SUBLANES, tq)
        idx = (c * ck + kloc).reshape(groups, SUBLANES, tq)
        at_thr = jnp.where(kk == thr8[None], jnp.where(idx <= jcut8[None], 1, 0), 0)
        sel = jnp.where(kk > thr8[None], 1, at_thr)
        mask_ref[0, pl.ds(ks, ck), :] = sel.reshape(ck, tq).astype(mask_ref.dtype)
        return carry

    lax.fori_loop(0, nk, write_chunk, 0)

    def zero_chunk(c, carry):
        ks = pl.multiple_of(c * ck, ck)
        mask_ref[0, pl.ds(ks, ck), :] = jnp.zeros((ck, tq), mask_ref.dtype)
        return carry

    lax.fori_loop(nk, mask_ref.shape[1] // ck, zero_chunk, 0)


def indexer_mask(qi, small, k_even, k_odd, b, t, n_sel):
    tq = _tile(t, 512)
    ck = _tile(t, 512)
    return pl.pallas_call(
        functools.partial(_indexer_body, tq=tq, ck=ck, n_sel=n_sel),
        grid=(b, t // tq),
        in_specs=[
            pl.BlockSpec((1, tq, IDX_HEADS * IDX_DIM), lambda bb, i: (bb, i, 0)),
            pl.BlockSpec((1, tq, LANES), lambda bb, i: (bb, i, 0)),
            pl.BlockSpec((1, t, LANES), lambda bb, i: (bb, 0, 0)),
            pl.BlockSpec((1, t, LANES), lambda bb, i: (bb, 0, 0)),
        ],
        out_specs=pl.BlockSpec((1, t, tq), lambda bb, i: (bb, 0, i)),
        out_shape=jax.ShapeDtypeStruct((b, t, t), jnp.int8),
        scratch_shapes=[pltpu.VMEM((t, tq), jnp.int32), pltpu.VMEM((t, tq), jnp.int16),
                        pltpu.VMEM((t, tq), jnp.int16), pltpu.VMEM((IDX_HEADS, SUBLANES, tq), F32)],
        compiler_params=_params("parallel", "parallel"),
        name="indexer",
    )(qi.reshape(b, t, -1), small, k_even.reshape(b, t, LANES), k_odd.reshape(b, t, LANES))


ATTN_HEAD_GROUP = 4
ATTN_HEAD_GROUP_FIXED = 2


def _attn_body(qi_ref, ki_ref, fix_ref, q_ref, k_ref, vt_ref, mask_ref, bias_ref, z_ref, m0_ref, o_ref, m_ref, acc_ref,
               *, tq):
    qi, ki = qi_ref[pl.program_id(1)], ki_ref[pl.program_id(1)]
    fixed = fix_ref[0] == 1

    @pl.when(ki == 0)
    def _():
        m_ref[...] = jnp.full(m_ref.shape, MASK_NEG, F32)
        acc_ref[...] = jnp.zeros(acc_ref.shape, F32)

    def tile(near, fixed_shift):
        madd = (1.0 - mask_ref[0].astype(F32)) * MASK_NEG
        if fixed_shift:
            madd = madd - m0_ref[0, 0]
        off = jnp.where(ki == qi, 0, 1)
        group = ATTN_HEAD_GROUP_FIXED if fixed_shift else ATTN_HEAD_GROUP
        for g in range(0, A_HEADS, group):
            hs = range(g, g + group)
            grp = slice(g, g + group)
            head = lambda ref, h: ref[0, :, h * A_HEAD_DIM:(h + 1) * A_HEAD_DIM]
            k2 = jnp.stack([head(k_ref, h) for h in hs])
            q2 = jnp.stack([head(q_ref, h) for h in hs])
            s = lax.dot_general(k2, q2, (((2,), (2,)), ((0,), (0,))), preferred_element_type=F32) + madd[None]
            if near:
                s = s + bias_ref[off, grp].astype(F32)
            vt2 = jnp.stack([vt_ref[h * V_ROWS:(h + 1) * V_ROWS, :] for h in hs])
            pv = lambda p: lax.dot_general(vt2, p, (((2,), (1,)), ((0,), (0,))), preferred_element_type=F32)
            if fixed_shift:
                acc_ref[grp] = acc_ref[grp] + pv(jnp.exp2(s).astype(BF16))
            else:
                m_prev = m_ref[grp]
                m_new = jnp.maximum(m_prev, jnp.max(s, axis=1, keepdims=True))
                acc_ref[grp] = jnp.exp2(m_prev - m_new) * acc_ref[grp] + pv(jnp.exp2(s - m_new).astype(BF16))
                m_ref[grp] = m_new

    far = ki + 1 < qi
    for near in (False, True):
        for fixed_shift in (False, True):
            cond = jnp.logical_and(jnp.logical_not(far) if near else far,
                                   fixed if fixed_shift else jnp.logical_not(fixed))
            pl.when(cond)(functools.partial(tile, near, fixed_shift))

    @pl.when(ki == qi)
    def _():
        outs = []
        for h in range(A_HEADS):
            a = acc_ref[h]
            outs.append((a[:A_HEAD_DIM] / a[A_HEAD_DIM:A_HEAD_DIM + 1]).T)
        o_ref[0] = (jnp.concatenate(outs, axis=-1) * _silu(z_ref[0].astype(F32))).astype(o_ref.dtype)


MAX_LOGIT_SPAN = 100.0


def softmax_shift(q_gain, k_gain, rel_bias):
    slack = 1.02
    qk = (A_HEAD_DIM ** 0.5) * LOG2E * jnp.max(jnp.abs(q_gain)) * jnp.max(jnp.abs(k_gain)) * slack
    b2 = (rel_bias - rel_bias[REL_BUCKETS - 1]) * (LOG2E * slack)
    shift = qk + jnp.max(b2)
    span = 2.0 * qk + jnp.max(b2) - jnp.min(b2)
    return shift.astype(F32), (span <= MAX_LOGIT_SPAN).astype(jnp.int32)


def attention(q, k, vt, mask, bias, proj3d, shift, use_shift, b, t, tq):
    nq = t // tq
    pairs = [(i, j) for i in range(nq) for j in range(i + 1)]
    qi_tab = jnp.asarray([p[0] for p in pairs], jnp.int32)
    ki_tab = jnp.asarray([p[1] for p in pairs], jnp.int32)
    grid_spec = pltpu.PrefetchScalarGridSpec(
        num_scalar_prefetch=3,
        grid=(b, len(pairs)),
        in_specs=[
            pl.BlockSpec((1, tq, A_WIDTH), lambda bb, p, qi, ki, fx: (bb, qi[p], 0)),
            pl.BlockSpec((1, tq, A_WIDTH), lambda bb, p, qi, ki, fx: (bb, ki[p], 0)),
            pl.BlockSpec((A_HEADS * V_ROWS, tq), lambda bb, p, qi, ki, fx: (0, bb * nq + ki[p])),
            pl.BlockSpec((1, tq, tq), lambda bb, p, qi, ki, fx: (bb, ki[p], qi[p])),
            pl.BlockSpec(bias.shape, lambda bb, p, qi, ki, fx: (0, 0, 0, 0)),
            pl.BlockSpec((1, tq, A_WIDTH), lambda bb, p, qi, ki, fx: (bb, qi[p], COL_Z_A // A_WIDTH)),
            pl.BlockSpec(memory_space=pltpu.SMEM),
        ],
        out_specs=pl.BlockSpec((1, tq, A_WIDTH), lambda bb, p, qi, ki, fx: (bb, qi[p], 0)),
        scratch_shapes=[pltpu.VMEM((A_HEADS, 1, tq), F32), pltpu.VMEM((A_HEADS, V_ROWS, tq), F32)],
    )
    return pl.pallas_call(
        functools.partial(_attn_body, tq=tq),
        grid_spec=grid_spec,
        out_shape=jax.ShapeDtypeStruct((b, t, A_WIDTH), BF16),
        compiler_params=_params("parallel", "arbitrary"),
        name="dsa_attention",
    )(qi_tab, ki_tab, use_shift.reshape(1), q.reshape(b, t, -1), k.reshape(b, t, -1), vt, mask, bias, proj3d,
      shift.reshape(1, 1))


def _gdn_prep_body(x_ref, halo_ref, sm_ref, cw_ref, alog_ref, dtb_ref,
                   q_ref, k_ref, v_ref, gcb_ref, bb_ref, grow_ref, *, tm):
    i = pl.program_id(1)
    x = x_ref[0]
    hal = halo_ref[0]
    hal = jnp.where(i > 0, hal, jnp.zeros_like(hal))
    rt = lax.broadcasted_iota(jnp.int32, (tm, tm), 0)
    ct = lax.broadcasted_iota(jnp.int32, (tm, tm), 1)
    rh = lax.broadcasted_iota(jnp.int32, (SUBLANES, BF16_ROWS), 0)
    ch = lax.broadcasted_iota(jnp.int32, (SUBLANES, BF16_ROWS), 1)
    y = cw_ref[B_CONV - 1:B_CONV, :] * x.astype(F32)
    for j in range(B_CONV - 1):
        back = B_CONV - 1 - j
        main = _dot(jnp.where(ct == rt - back, 1.0, 0.0).astype(BF16), x)
        head = _dot(jnp.where(ch == rh + (BF16_ROWS - back), 1.0, 0.0).astype(BF16), hal)
        y = y + cw_ref[j:j + 1, :] * jnp.concatenate([main[:SUBLANES] + head, main[SUBLANES:]], axis=0)
    y = _silu(y)
    for h in range(B_HEADS):
        sl = slice(h * LANES, (h + 1) * LANES)
        qh = y[:, sl]
        q_ref[0, :, sl] = qh * lax.rsqrt(jnp.sum(qh * qh, axis=-1, keepdims=True) + EPS) * (B_HEAD_DIM ** -0.5)
        kh = y[:, B_WIDTH + h * LANES:B_WIDTH + (h + 1) * LANES]
        k_ref[0, :, sl] = kh * lax.rsqrt(jnp.sum(kh * kh, axis=-1, keepdims=True) + EPS)
    v_ref[0] = y[:, 2 * B_WIDTH:]

    sm = sm_ref[0]
    xg = sm + dtb_ref[...]
    softplus = jnp.maximum(xg, 0.0) + jnp.log(1.0 + jnp.exp(-jnp.abs(xg)))
    g = -jnp.exp(alog_ref[...]) * softplus
    r = lax.broadcasted_iota(jnp.int32, (tm, tm), 0)
    c = lax.broadcasted_iota(jnp.int32, (tm, tm), 1)
    sh = int(math.log2(GDN_CHUNK))
    same_chunk = jnp.right_shift(r, sh) == jnp.right_shift(c, sh)
    tri = jnp.where(jnp.logical_and(same_chunk, c <= r), 1.0, 0.0).astype(F32)
    gc = lax.dot_general(tri, g, (((1,), (0,)), ((), ())), precision=HIGHEST, preferred_element_type=F32)
    beta = _sigmoid(sm)
    for h in range(B_HEADS):
        sl = slice(h * LANES, (h + 1) * LANES)
        gcb_ref[0, :, sl] = jnp.broadcast_to(gc[:, SM_ALPHA + h:SM_ALPHA + h + 1], (tm, LANES))
        bb_ref[0, :, sl] = jnp.broadcast_to(beta[:, SM_BETA + h:SM_BETA + h + 1], (tm, LANES))
    gct = gc.T
    for cc in range(tm // GDN_CHUNK):
        grow_ref[0, cc] = gct[SM_ALPHA:SM_ALPHA + B_HEADS, cc * GDN_CHUNK:(cc + 1) * GDN_CHUNK]


def gdn_prep(proj3d, small3d, conv_w, a_log, dt_bias, b, t):
    tm = _tile(t, 256)
    pad_row = lambda v: jnp.zeros((1, LANES), F32).at[0, SM_ALPHA:SM_ALPHA + B_HEADS].set(v)
    nc = tm // GDN_CHUNK
    act = jax.ShapeDtypeStruct((b, t, B_WIDTH), F32)
    full = lambda a: pl.BlockSpec(a.shape, lambda bb, i: (0,) * a.ndim)
    consts = [conv_w, pad_row(a_log), pad_row(dt_bias)]
    return pl.pallas_call(
        functools.partial(_gdn_prep_body, tm=tm),
        grid=(b, t // tm),
        in_specs=[
            pl.BlockSpec((1, tm, 3 * B_WIDTH), lambda bb, i: (bb, i, COL_QKV_B // (3 * B_WIDTH))),
            pl.BlockSpec((1, BF16_ROWS, 3 * B_WIDTH),
                         lambda bb, i: (bb, jnp.maximum(i * (tm // BF16_ROWS) - 1, 0), 0)),
            pl.BlockSpec((1, tm, LANES), lambda bb, i: (bb, i, 0)),
        ] + [full(c) for c in consts],
        out_specs=[pl.BlockSpec((1, tm, B_WIDTH), lambda bb, i: (bb, i, 0))] * 5
        + [pl.BlockSpec((1, nc, B_HEADS, GDN_CHUNK), lambda bb, i: (bb, i, 0, 0))],
        out_shape=[act] * 5 + [jax.ShapeDtypeStruct((b, t // GDN_CHUNK, B_HEADS, GDN_CHUNK), F32)],
        compiler_params=_params("parallel", "parallel"),
        name="gdn_prep",
    )(proj3d, proj3d, small3d, *consts)


def _gdn_body(q_ref, k_ref, v_ref, gcb_ref, bb_ref, grow_ref, z_ref, on_ref, o_ref, s_ref, *, nc):
    @pl.when(pl.program_id(1) == 0)
    def _():
        s_ref[...] = jnp.zeros(s_ref.shape, F32)

    cs, nh = GDN_CHUNK, B_HEADS
    nb = nc * nh
    ri = lax.broadcasted_iota(jnp.int32, (nb, cs, cs), 1)
    ci = lax.broadcasted_iota(jnp.int32, (nb, cs, cs), 2)
    lower = ci <= ri
    strict = ci < ri
    eye = jnp.where(ci == ri, 1.0, 0.0).astype(F32)
    bf = lambda a: a.astype(BF16)
    bmm = lambda a, b: lax.dot_general(a, b, (((2,), (1,)), ((0,), (0,))), preferred_element_type=F32)
    bmm_nt = lambda a, b: lax.dot_general(a, b, (((2,), (2,)), ((0,), (0,))), preferred_element_type=F32)
    bmm_tn = lambda a, b: lax.dot_general(a, b, (((1,), (1,)), ((0,), (0,))), preferred_element_type=F32)

    def stack(ref):
        return jnp.stack([ref[0, c * cs:(c + 1) * cs, h * LANES:(h + 1) * LANES]
                          for c in range(nc) for h in range(nh)])

    q, k, v = stack(q_ref), stack(k_ref), stack(v_ref)
    gcb = stack(gcb_ref)
    beta = stack(bb_ref)
    grow = jnp.stack([grow_ref[0, c, h:h + 1, :] for c in range(nc) for h in range(nh)])
    diff = gcb[:, :, :cs] - grow
    decay = jnp.where(lower, jnp.exp(jnp.where(lower, diff, 0.0)), 0.0)
    eg = jnp.exp(gcb)
    glast = gcb[:, cs - 1:cs, :]
    kb = k * beta
    lmat = jnp.where(strict, bmm_nt(bf(kb), bf(k)) * decay, 0.0)
    n = -lmat
    tinv = eye + n
    for _ in range(int(math.log2(cs)) - 1):
        n = bmm(bf(n), bf(n))
        tinv = tinv + bmm(bf(tinv), bf(n))
    uw = bmm(bf(tinv), bf(jnp.concatenate([v * beta, kb * eg], axis=-1)))
    attn = bf(jnp.where(lower, bmm_nt(bf(q), bf(k)) * decay, 0.0))
    qg = bf(q * eg)
    kdec = bf(k * jnp.exp(glast - gcb))
    egl = jnp.exp(glast)

    s = s_ref[...]
    for c in range(nc):
        sl = slice(c * nh, (c + 1) * nh)
        sb = bf(s)
        v_new = uw[sl, :, :LANES] - bmm(bf(uw[sl, :, LANES:]), sb)
        o = bmm(qg[sl], sb) + bmm(attn[sl], bf(v_new))
        s = s * egl[sl] + bmm_tn(kdec[sl], bf(v_new))
        o = o * lax.rsqrt(jnp.mean(o * o, axis=-1, keepdims=True) + EPS) * on_ref[...]
        for h in range(nh):
            rows, cols = slice(c * cs, (c + 1) * cs), slice(h * LANES, (h + 1) * LANES)
            o_ref[0, rows, cols] = (o[h] * _silu(z_ref[0, rows, cols].astype(F32))).astype(o_ref.dtype)
    s_ref[...] = s


def gdn_scan(qh, kh, v, gcb, bb, grow, proj3d, o_norm, b, t):
    tt = _tile(t, 256)
    nc = tt // GDN_CHUNK
    blk = pl.BlockSpec((1, tt, B_WIDTH), lambda bb_, i: (bb_, i, 0))
    return pl.pallas_call(
        functools.partial(_gdn_body, nc=nc),
        grid=(b, t // tt),
        in_specs=[blk] * 5 + [
            pl.BlockSpec((1, nc, B_HEADS, GDN_CHUNK), lambda bb_, i: (bb_, i, 0, 0)),
            pl.BlockSpec((1, tt, B_WIDTH), lambda bb_, i: (bb_, i, COL_Z_B // B_WIDTH)),
            pl.BlockSpec((1, LANES), lambda bb_, i: (0, 0)),
        ],
        out_specs=blk,
        out_shape=jax.ShapeDtypeStruct((b, t, B_WIDTH), BF16),
        scratch_shapes=[pltpu.VMEM((B_HEADS, B_HEAD_DIM, B_HEAD_DIM), F32)],
        compiler_params=_params("parallel", "arbitrary"),
        name="gdn_scan",
    )(qh, kh, v, gcb, bb, grow, proj3d, o_norm.reshape(1, LANES))


C_HALO = 32
D_HALO = BF16_ROWS


def _cd_layer_body(a_ref, g_ref, ah_ref, gh_ref, zc_ref, bg_ref, cg_ref, ud_ref, cgh_ref, udh_ref, zd_ref,
                   x_ref, wo_ref, dww_ref, dwb_ref, lnw_ref, lnb_ref, dcw_ref, o_ref, us_ref, ds_ref, y_ref,
                   *, tm, tiles_per_seq):
    s = pl.program_id(0)
    first = lax.rem(s, tiles_per_seq) == 0
    slot = lax.rem(s, 2)

    @pl.when(s == 0)
    def _():
        y_ref[...] = jnp.zeros(y_ref.shape, y_ref.dtype)

    o_ref[...] = x_ref[...] + _dot(y_ref[1 - slot], wo_ref[...])

    f32 = lambda r: r[...].astype(F32)
    uh = f32(ah_ref) * _sigmoid(f32(gh_ref))
    us_ref[0, 0:C_HALO, :] = jnp.where(first, jnp.zeros_like(uh), uh)
    us_ref[0, C_HALO:, :] = f32(a_ref) * _sigmoid(f32(g_ref))
    span = tm + C_HALO - SUBLANES
    for r in range(1, SUBLANES):
        us_ref[r, 0:span, :] = us_ref[0, pl.ds(r, span), :]
    u = jnp.zeros((tm, C_WIDTH), F32)
    for j in range(C_CONV):
        off = C_HALO - (C_CONV - 1) + j
        r, base = off % SUBLANES, off - off % SUBLANES
        u = u + dww_ref[j:j + 1, :] * us_ref[r, base:base + tm, :]
    u = u + dwb_ref[...]
    mu = jnp.mean(u, axis=-1, keepdims=True)
    uc = u - mu
    u = uc * lax.rsqrt(jnp.mean(uc * uc, axis=-1, keepdims=True) + EPS) * lnw_ref[...] + lnb_ref[...]
    y_ref[slot, :, 0:C_WIDTH] = (_silu(u) * _silu(f32(zc_ref))).astype(y_ref.dtype)

    dh = f32(cgh_ref) * f32(udh_ref)
    ds_ref[0:D_HALO, :] = jnp.where(first, jnp.zeros_like(dh), dh)
    ds_ref[D_HALO:, :] = f32(cg_ref) * f32(ud_ref)
    d = jnp.zeros((tm, D_WIDTH), F32)
    for j in range(D_CONV):
        d = d + dcw_ref[j:j + 1, :] * ds_ref[pl.ds(D_HALO - (D_CONV - 1) + j, tm), :]
    y_ref[slot, :, C_WIDTH:] = (f32(bg_ref) * d * _silu(f32(zd_ref))).astype(y_ref.dtype)


def cd_layer(proj, x2d, w_out_bf16, dw_w, dw_b, ln_w, ln_b, d_conv_w, t):
    m, n = x2d.shape
    tm = _tile(t, 256)
    w = C_WIDTH
    last = m // tm - 1
    cur = lambda s: jnp.minimum(s, last)
    col = lambda c: pl.BlockSpec((tm, w), lambda s, c=c: (cur(s), c))
    halo = lambda c, rows: pl.BlockSpec(
        (rows, w), lambda s, c=c, rows=rows: (jnp.maximum(cur(s) * (tm // rows) - 1, 0), c))
    prev = pl.BlockSpec((tm, n), lambda s: (jnp.maximum(s - 1, 0), 0))
    row = lambda a: a.reshape(1, -1)
    full = lambda a: pl.BlockSpec(a.shape, lambda s: (0,) * a.ndim)
    consts = [w_out_bf16, dw_w, row(dw_b), row(ln_w), row(ln_b), d_conv_w]
    return pl.pallas_call(
        functools.partial(_cd_layer_body, tm=tm, tiles_per_seq=t // tm),
        grid=(m // tm + 1,),
        in_specs=[col(0), col(1), halo(0, C_HALO), halo(1, C_HALO), col(2), col(3), col(4), col(5),
                  halo(4, D_HALO), halo(5, D_HALO), col(6), prev] + [full(c) for c in consts],
        out_specs=prev,
        out_shape=jax.ShapeDtypeStruct((m, n), F32),
        scratch_shapes=[pltpu.VMEM((SUBLANES, tm + C_HALO, w), F32), pltpu.VMEM((tm + D_HALO, w), F32),
                        pltpu.VMEM((2, tm, C_WIDTH + D_WIDTH), BF16)],
        compiler_params=_params("arbitrary"),
        name="cd_layer",
    )(*([proj] * 11), x2d, *consts)


def _reorder_ab_w_in(w):
    offs = [0]
    for s in AB_SPLITS:
        offs.append(offs[-1] + s)
    part = lambda n: w[:, offs[n]:offs[n + 1]]
    c_q, c_kv, k_idx, w_idx, z_a, qkv_b, beta_b, alpha_b, z_b = (part(n) for n in range(9))
    small = jnp.concatenate([k_idx, w_idx, beta_b, alpha_b], axis=1)
    small = jnp.pad(small, ((0, 0), (0, LANES - small.shape[1])))
    out = jnp.concatenate([qkv_b, z_a, z_b, c_q, c_kv], axis=1)
    return jnp.pad(out, ((0, 0), (0, AB_IN_PAD - out.shape[1]))).astype(BF16), small.astype(BF16)


def _ab_layer(x2d, b, t, norm_w, rel_bias, w_in, q_norm, w_uq, w_iq, kv_norm, w_uk, w_uv, q_gain, k_gain,
              ik_w, ik_b, conv_w, a_log, dt_bias, o_norm, w_out):
    proj, small = norm_matmul(x2d, norm_w, *_reorder_ab_w_in(w_in))
    proj3d, small3d = proj.reshape(b, t, AB_IN_PAD), small.reshape(b, t, LANES)
    q, k, v, qi, k_even, k_odd = dsa_prep(proj, small, q_norm, kv_norm, w_uq, w_iq, w_uk, w_uv, q_gain, k_gain,
                                          ik_w, ik_b)
    n_sel = min(TOPK_MAX, t // 4)
    mask = indexer_mask(qi, small3d, k_even, k_odd, b, t, n_sel)
    tq = _tile(t, 512)
    shift, use_shift = softmax_shift(q_gain, k_gain, rel_bias)
    y_a = attention(q, k, v, mask, bias_tiles(rel_bias, tq), proj3d, shift, use_shift, b, t, tq)
    qh, kh, vv, gcb, bb, grow = gdn_prep(proj3d, small3d, conv_w, a_log, dt_bias, b, t)
    y_b = gdn_scan(qh, kh, vv, gcb, bb, grow, proj3d, o_norm, b, t)
    wo = w_out.astype(BF16)
    return out_proj(x2d, y_a.reshape(b * t, -1), y_b.reshape(b * t, -1), wo[:A_WIDTH], wo[A_WIDTH:])


def _cd_layer(x2d, b, t, norm_w, w_in, dw_w, dw_b, ln_w, ln_b, d_conv_w, w_out):
    proj = norm_matmul(x2d, norm_w, w_in.astype(BF16))
    return cd_layer(proj, x2d, w_out.astype(BF16), dw_w, dw_b, ln_w, ln_b, d_conv_w, t)


def kernel(x, norm_w, rel_bias, ab_w_in, a_q_norm, a_w_uq, a_w_iq, a_kv_norm, a_w_uk, a_w_uv, a_q_gain,
           a_k_gain, a_ik_norm_w, a_ik_norm_b, b_conv_w, b_a_log, b_dt_bias, b_o_norm, ab_w_out, cd_w_in,
           c_dw_w, c_dw_b, c_ln_w, c_ln_b, d_conv_w, cd_w_out):
    b, t, d = x.shape
    depth = norm_w.shape[0]
    x2d = x.reshape(b * t, d)
    for i in range(depth):
        j = i // 2
        if i % 2 == 0:
            x2d = _ab_layer(x2d, b, t, norm_w[i], rel_bias, ab_w_in[j], a_q_norm[j], a_w_uq[j], a_w_iq[j],
                            a_kv_norm[j], a_w_uk[j], a_w_uv[j], a_q_gain[j], a_k_gain[j], a_ik_norm_w[j],
                            a_ik_norm_b[j], b_conv_w[j], b_a_log[j], b_dt_bias[j], b_o_norm[j], ab_w_out[j])
        else:
            x2d = _cd_layer(x2d, b, t, norm_w[i], cd_w_in[j], c_dw_w[j], c_dw_b[j], c_ln_w[j], c_ln_b[j],
                            d_conv_w[j], cd_w_out[j])
    return x2d.reshape(b, t, d)
```

```python
import functools
import math

import jax
import jax.numpy as jnp
from jax import lax
from jax.experimental import pallas as pl
from jax.experimental.pallas import tpu as pltpu

A_HEADS = 8
A_HEAD_DIM = 128
A_WIDTH = A_HEADS * A_HEAD_DIM
A_Q_LORA = 512
A_KV_LORA = 256
IDX_HEADS = 16
IDX_DIM = 64
TOPK_MAX = 256
REL_BUCKETS = 32
REL_MAX_DIST = 128
B_HEADS = 8
B_HEAD_DIM = 128
B_WIDTH = B_HEADS * B_HEAD_DIM
B_CONV = 4
GDN_CHUNK = 64
C_WIDTH = 1024
C_CONV = 31
D_WIDTH = 1024
D_CONV = 3
EPS = 1e-6

AB_SPLITS = (A_Q_LORA, A_KV_LORA, IDX_DIM, IDX_HEADS, A_WIDTH, 3 * B_WIDTH, B_HEADS, B_HEADS, B_WIDTH)
AB_IN_PAD = 6144
SM_KIDX = 0
SM_WIDX = 64
SM_BETA = 80
SM_ALPHA = 88
COL_QKV_B = 0
COL_Z_A = 3072
COL_Z_B = 4096
COL_C_Q = 5120
COL_C_KV = 5632

LANES = 128
MASK_NEG = -1e30
INT_MIN = -(2 ** 31)
LOG2E = math.log2(math.e)
BF16_ROWS = 16
V_ROWS = A_HEAD_DIM + BF16_ROWS
VMEM_LIMIT = 56 * 1024 * 1024

F32 = jnp.float32
BF16 = jnp.bfloat16
HIGHEST = lax.Precision.HIGHEST


def _tile(n, pref):
    t = min(n, pref)
    assert n % t == 0, (n, t)
    return t


def _params(*sem):
    return pltpu.CompilerParams(dimension_semantics=sem, vmem_limit_bytes=VMEM_LIMIT)


def _dot(a, b):
    return jnp.dot(a, b, preferred_element_type=F32)


def _dot_nt(a, b):
    return lax.dot_general(a, b, (((1,), (1,)), ((), ())), preferred_element_type=F32)


def _silu(x):
    return x * (1.0 / (1.0 + jnp.exp(-x)))


def _sigmoid(x):
    return 1.0 / (1.0 + jnp.exp(-x))


def _norm_matmul_body(*refs, with_f32_cols):
    if with_f32_cols:
        x_ref, nw_ref, w_ref, ws_ref, o_ref, os_ref, h_ref = refs
    else:
        x_ref, nw_ref, w_ref, o_ref, h_ref = refs

    @pl.when(pl.program_id(1) == 0)
    def _():
        x = x_ref[...]
        ms = jnp.mean(x * x, axis=-1, keepdims=True)
        h_ref[...] = (x * lax.rsqrt(ms + EPS) * nw_ref[...]).astype(h_ref.dtype)
        if with_f32_cols:
            os_ref[...] = _dot(h_ref[...], ws_ref[...])

    o_ref[...] = _dot(h_ref[...], w_ref[...]).astype(o_ref.dtype)


def norm_matmul(x2d, norm_w, w_bf16, w_f32_cols=None):
    m, k = x2d.shape
    n = w_bf16.shape[1]
    tm = _tile(m, 1024)
    tn = max(d for d in range(256, min(n, 2048) + 1, 256) if n % d == 0)
    extra = w_f32_cols is not None
    in_specs = [
        pl.BlockSpec((tm, k), lambda i, j: (i, 0)),
        pl.BlockSpec((1, k), lambda i, j: (0, 0)),
        pl.BlockSpec((k, tn), lambda i, j: (0, j)),
    ]
    out_specs = [pl.BlockSpec((tm, tn), lambda i, j: (i, j))]
    out_shape = [jax.ShapeDtypeStruct((m, n), BF16)]
    args = [x2d, norm_w.reshape(1, k), w_bf16]
    if extra:
        in_specs.append(pl.BlockSpec((k, LANES), lambda i, j: (0, 0)))
        out_specs.append(pl.BlockSpec((tm, LANES), lambda i, j: (i, 0)))
        out_shape.append(jax.ShapeDtypeStruct((m, LANES), F32))
        args.append(w_f32_cols)
    res = pl.pallas_call(
        functools.partial(_norm_matmul_body, with_f32_cols=extra),
        grid=(m // tm, n // tn),
        in_specs=in_specs,
        out_specs=out_specs,
        out_shape=out_shape,
        scratch_shapes=[pltpu.VMEM((tm, k), BF16)],
        compiler_params=_params("parallel", "arbitrary"),
        name="norm_matmul",
    )(*args)
    return res if extra else res[0]


def _out_proj_body(x_ref, ya_ref, yb_ref, wa_ref, wb_ref, o_ref):
    o_ref[...] = x_ref[...] + _dot(ya_ref[...], wa_ref[...]) + _dot(yb_ref[...], wb_ref[...])


def out_proj(x2d, ya, yb, wa_bf16, wb_bf16):
    m, n = x2d.shape
    ka, kb = ya.shape[1], yb.shape[1]
    tm = _tile(m, 512)
    return pl.pallas_call(
        _out_proj_body,
        grid=(m // tm,),
        in_specs=[
            pl.BlockSpec((tm, n), lambda i: (i, 0)),
            pl.BlockSpec((tm, ka), lambda i: (i, 0)),
            pl.BlockSpec((tm, kb), lambda i: (i, 0)),
            pl.BlockSpec((ka, n), lambda i: (0, 0)),
            pl.BlockSpec((kb, n), lambda i: (0, 0)),
        ],
        out_specs=pl.BlockSpec((tm, n), lambda i: (i, 0)),
        out_shape=jax.ShapeDtypeStruct((m, n), F32),
        compiler_params=_params("parallel"),
        name="out_proj",
    )(x2d, ya, yb, wa_bf16, wb_bf16)


def _head_rmsnorm(y, gain_row, scale):
    outs = []
    for h in range(y.shape[1] // LANES):
        seg = y[:, h * LANES:(h + 1) * LANES]
        ms = jnp.mean(seg * seg, axis=-1, keepdims=True)
        outs.append(seg * lax.rsqrt(ms + EPS) * (gain_row * scale))
    return jnp.concatenate(outs, axis=-1)


def _dsa_prep_body(cq_ref, ckv_ref, sm_ref, qn_ref, kvn_ref, wuq_ref, wiq_ref, wuk_ref, wuv_ref,
                   qg_ref, kg_ref, ikw_ref, ikb_ref,
                   q_ref, k_ref, v_ref, qi_ref, ke_ref, ko_ref):
    cq = cq_ref[...].astype(F32)
    cq = cq * lax.rsqrt(jnp.mean(cq * cq, axis=-1, keepdims=True) + EPS) * qn_ref[...]
    cqb = cq.astype(BF16)
    q = _dot(cqb, wuq_ref[...])
    q_ref[...] = _head_rmsnorm(q, qg_ref[...], A_HEAD_DIM ** -0.5 * LOG2E).astype(q_ref.dtype)
    qi_ref[...] = _dot(cqb, wiq_ref[...]).astype(qi_ref.dtype)

    ckv = ckv_ref[...].astype(F32)
    ckv = ckv * lax.rsqrt(jnp.mean(ckv * ckv, axis=-1, keepdims=True) + EPS) * kvn_ref[...]
    ckvb = ckv.astype(BF16)
    k = _dot(ckvb, wuk_ref[...])
    k_ref[...] = _head_rmsnorm(k, kg_ref[...], 1.0).astype(k_ref.dtype)
    vt = _dot_nt(wuv_ref[...], ckvb).astype(v_ref.dtype)
    ones = jnp.ones((BF16_ROWS, vt.shape[1]), v_ref.dtype)
    for h in range(A_HEADS):
        v_ref[0, h * V_ROWS:h * V_ROWS + A_HEAD_DIM, :] = vt[h * A_HEAD_DIM:(h + 1) * A_HEAD_DIM, :]
        v_ref[0, h * V_ROWS + A_HEAD_DIM:(h + 1) * V_ROWS, :] = ones

    kx = sm_ref[...][:, SM_KIDX:SM_KIDX + IDX_DIM]
    mu = jnp.mean(kx, axis=-1, keepdims=True)
    kc = kx - mu
    kl = kc * lax.rsqrt(jnp.mean(kc * kc, axis=-1, keepdims=True) + EPS) * ikw_ref[...] + ikb_ref[...]
    z = jnp.zeros_like(kl)
    ke_ref[...] = jnp.concatenate([kl, z], axis=-1).astype(ke_ref.dtype)
    ko_ref[...] = jnp.concatenate([z, kl], axis=-1).astype(ko_ref.dtype)


def dsa_prep(proj, small, q_norm, kv_norm, w_uq, w_iq, w_uk, w_uv, q_gain, k_gain, ik_w, ik_b, tm=None):
    m = proj.shape[0]
    tm = _tile(m, 512) if tm is None else tm
    row = lambda a: a.reshape(1, -1)
    full = lambda a: pl.BlockSpec(a.shape, lambda i: (0,) * a.ndim)
    consts = [row(q_norm), row(kv_norm), w_uq.astype(BF16), w_iq.astype(BF16), w_uk.astype(BF16),
              w_uv.T.astype(BF16), row(q_gain), row(k_gain), row(ik_w), row(ik_b)]
    outs = [jax.ShapeDtypeStruct((m, A_WIDTH), BF16)] * 2 + [
        jax.ShapeDtypeStruct((m // tm, A_HEADS * V_ROWS, tm), BF16),
        jax.ShapeDtypeStruct((m, IDX_HEADS * IDX_DIM), BF16),
        jax.ShapeDtypeStruct((m, LANES), BF16), jax.ShapeDtypeStruct((m, LANES), BF16)]
    out_specs = [pl.BlockSpec((tm, s.shape[-1]), lambda i: (i, 0)) for s in outs]
    out_specs[2] = pl.BlockSpec((1, A_HEADS * V_ROWS, tm), lambda i: (i, 0, 0))
    return pl.pallas_call(
        _dsa_prep_body,
        grid=(m // tm,),
        in_specs=[
            pl.BlockSpec((tm, A_Q_LORA), lambda i: (i, COL_C_Q // A_Q_LORA)),
            pl.BlockSpec((tm, A_KV_LORA), lambda i: (i, COL_C_KV // A_KV_LORA)),
            pl.BlockSpec((tm, LANES), lambda i: (i, 0)),
        ] + [full(c) for c in consts],
        out_specs=out_specs,
        out_shape=outs,
        compiler_params=_params("parallel"),
        name="dsa_prep",
    )(proj, proj, small, *consts)


def _t5_bucket(dist):
    max_exact = REL_BUCKETS // 2
    large = max_exact + (jnp.log(jnp.maximum(dist, 1).astype(F32) / max_exact)
                         / math.log(REL_MAX_DIST / max_exact) * (REL_BUCKETS - max_exact)).astype(jnp.int32)
    large = jnp.minimum(large, REL_BUCKETS - 1)
    return jnp.where(dist < max_exact, dist, large)


def _bias_tiles_body(rb_ref, o_ref, *, t):
    h = pl.program_id(0)
    sb = min(t, LANES)
    row = lax.broadcasted_iota(jnp.int32, (sb, sb), 0)
    col = lax.broadcasted_iota(jnp.int32, (sb, sb), 1)
    for typ in range(2):
        for rk in range(t // sb):
            for cq in range(t // sb):
                base = typ * t + (cq - rk) * sb
                blk = (typ, 0, slice(rk * sb, (rk + 1) * sb), slice(cq * sb, (cq + 1) * sb))
                if base - (sb - 1) >= REL_MAX_DIST or base + (sb - 1) < 0:
                    o_ref[blk] = jnp.zeros((sb, sb), o_ref.dtype)
                    continue
                bucket = _t5_bucket(jnp.maximum(base + col - row, 0))
                acc = jnp.zeros((sb, sb), F32)
                for b in range(REL_BUCKETS):
                    acc = jnp.where(bucket == b, rb_ref[b, h], acc)
                o_ref[blk] = ((acc - rb_ref[REL_BUCKETS - 1, h]) * LOG2E).astype(o_ref.dtype)


def bias_tiles(rel_bias, t):
    return pl.pallas_call(
        functools.partial(_bias_tiles_body, t=t),
        grid=(A_HEADS,),
        in_specs=[pl.BlockSpec(memory_space=pltpu.SMEM)],
        out_specs=pl.BlockSpec((2, 1, t, t), lambda h: (0, h, 0, 0)),
        out_shape=jax.ShapeDtypeStruct((2, A_HEADS, t, t), BF16),
        compiler_params=_params("parallel"),
        name="bias_tiles",
    )(rel_bias)


SUBLANES = 8
MAX_SELECT_STEPS = 24
UPPER_STEPS_PER_CHECK = 4
LOWER_STEPS_PER_CHECK = 2
HALF_BITS = 16
HALF_SPAN = 2 ** HALF_BITS
I16_MIN = -(2 ** (HALF_BITS - 1))


def _key_of(x):
    bits = pltpu.bitcast(x, jnp.int32)
    key = jnp.where(bits < 0, bits ^ jnp.int32(0x7FFFFFFF), bits)
    return jnp.where(bits == jnp.int32(INT_MIN), 0, key)


def _indexer_body(qi_ref, sm_ref, ke_ref, ko_ref, mask_ref, key_ref, hi_ref, low_ref, w_ref, *, tq, ck, n_sel):
    i = pl.program_id(1)
    nk = ((i + 1) * tq + ck - 1) // ck
    groups = ck // SUBLANES
    scale = (IDX_HEADS ** -0.5) * (IDX_DIM ** -0.5)
    smt = sm_ref[0].T
    for h in range(IDX_HEADS):
        w_ref[h] = jnp.broadcast_to(smt[SM_WIDX + h:SM_WIDX + h + 1, :] * scale, (SUBLANES, tq))
    kloc = lax.broadcasted_iota(jnp.int32, (ck, tq), 0)
    qpos = i * tq + lax.broadcasted_iota(jnp.int32, (ck, tq), 1)
    int_max = jnp.int32(2 ** 31 - 1)

    def score_chunk(c, carry):
        kmin, kmax = carry
        ks = pl.multiple_of(c * ck, ck)
        ke = ke_ref[0, pl.ds(ks, ck), :]
        ko = ko_ref[0, pl.ds(ks, ck), :]
        acc = jnp.zeros((ck, tq), F32)
        for j in range(IDX_HEADS // 2):
            qp = qi_ref[0, :, j * LANES:(j + 1) * LANES]
            for par, kk in ((0, ke), (1, ko)):
                s = jnp.maximum(_dot_nt(kk, qp), 0.0)
                acc = acc + s * jnp.tile(w_ref[2 * j + par], (groups, 1))
        key = _key_of(acc)
        valid = c * ck + kloc <= qpos
        masked = jnp.where(valid, key, jnp.int32(INT_MIN))
        key_ref[pl.ds(ks, ck), :] = masked
        hi_ref[pl.ds(ks, ck), :] = jnp.right_shift(masked, HALF_BITS).astype(jnp.int16)
        kmin = jnp.minimum(kmin, jnp.min(jnp.where(valid, key, int_max).reshape(groups, SUBLANES, tq), axis=0))
        kmax = jnp.maximum(kmax, jnp.max(masked.reshape(groups, SUBLANES, tq), axis=0))
        return kmin, kmax

    kmin, kmax = lax.fori_loop(0, nk, score_chunk, (jnp.full((SUBLANES, tq), int_max, jnp.int32),
                                                    jnp.full((SUBLANES, tq), INT_MIN, jnp.int32)))

    rows16 = ck // BF16_ROWS
    i16_min = I16_MIN

    def scan16(a_ref, init, fn):
        def body(c, acc):
            ks = pl.multiple_of(c * ck, ck)
            return fn(acc, a_ref[pl.ds(ks, ck), :].reshape(rows16, BF16_ROWS, tq))

        return lax.fori_loop(0, nk, body, init)

    def count_ge16(a_ref, p):
        p16 = jnp.broadcast_to(p.astype(jnp.int16), (BF16_ROWS, tq))

        def fn(cnt, a):
            ind = jnp.where(a >= p16[None], jnp.int16(1), jnp.int16(0))
            for g in range(rows16):
                cnt = cnt + ind[g]
            return cnt

        cnt = scan16(a_ref, jnp.zeros((BF16_ROWS, tq), jnp.int16), fn)
        return jnp.sum(cnt.astype(jnp.int32), axis=0, keepdims=True)

    def max_le16(a_ref, h):
        h16 = jnp.broadcast_to(h.astype(jnp.int16), (BF16_ROWS, tq))

        def fn(acc, a):
            v = jnp.where(a <= h16[None], a, jnp.int16(i16_min))
            for g in range(rows16):
                acc = jnp.where(v[g] > acc, v[g], acc)
            return acc

        acc = scan16(a_ref, jnp.full((BF16_ROWS, tq), i16_min, jnp.int16), fn)
        return jnp.max(acc.astype(jnp.int32), axis=0, keepdims=True)

    def rank_select16(a_ref, r, skip, lo, hi, clo, chi, floor_count, steps_per_check):
        def closed(lo, hi, clo):
            return skip | (clo == r) | (hi == lo + 1)

        def open_rows(lo, hi, clo, chi):
            return jnp.logical_not(closed(lo, hi, clo) | (chi == r - 1))

        def cond(st):
            it, lo, hi, clo, chi = st
            n_open = jnp.max(jnp.where(open_rows(lo, hi, clo, chi), 1, 0))
            return jnp.logical_and(it < MAX_SELECT_STEPS, n_open > 0)

        def step(st):
            it, lo, hi, clo, chi = st
            for _ in range(steps_per_check):
                upd = open_rows(lo, hi, clo, chi)
                p = lo + jnp.maximum(jnp.right_shift(hi - lo, 1), 1)
                cnt = count_ge16(a_ref, p)
                ge = cnt >= r
                up, dn = upd & ge, upd & jnp.logical_not(ge)
                lo, hi = jnp.where(up, p, lo), jnp.where(dn, p, hi)
                clo, chi = jnp.where(up, cnt, clo), jnp.where(dn, cnt, chi)
            return it + steps_per_check, lo, hi, clo, chi

        _, lo, hi, clo, chi = lax.while_loop(cond, step, (jnp.int32(0), lo, hi, clo, chi))
        last = jnp.logical_not(closed(lo, hi, clo))
        t_last = max_le16(a_ref, hi - 1)
        c_last = jnp.where(t_last == i16_min, floor_count, count_ge16(a_ref, t_last))
        return jnp.where(last, t_last, lo), jnp.where(last, c_last, clo), chi

    n_valid = i * tq + lax.broadcasted_iota(jnp.int32, (1, tq), 1) + 1
    few = n_valid <= n_sel
    zeros = jnp.zeros((1, tq), jnp.int32)
    hi_lo0 = jnp.right_shift(jnp.min(kmin, axis=0, keepdims=True), HALF_BITS)
    hi_hi0 = jnp.right_shift(jnp.max(kmax, axis=0, keepdims=True), HALF_BITS) + 1
    t_hi, c_ge_hi, c_gt_hi = rank_select16(hi_ref, n_sel, few, hi_lo0, hi_hi0, n_valid, zeros, n_valid,
                                           UPPER_STEPS_PER_CHECK)
    exact_hi = c_ge_hi == n_sel

    t_hi16 = jnp.broadcast_to(t_hi.astype(jnp.int16), (BF16_ROWS, tq))

    def fill_low(c, carry):
        ks = pl.multiple_of(c * ck, ck)
        low = (key_ref[pl.ds(ks, ck), :] & (HALF_SPAN - 1)) + I16_MIN
        low = low.astype(jnp.int16).reshape(rows16, BF16_ROWS, tq)
        same = hi_ref[pl.ds(ks, ck), :].reshape(rows16, BF16_ROWS, tq) == t_hi16[None]
        low_ref[pl.ds(ks, ck), :] = jnp.where(same, low, jnp.int16(i16_min)).reshape(ck, tq)
        return carry

    lax.fori_loop(0, nk, fill_low, 0)
    in_bucket = c_ge_hi - c_gt_hi
    t_lo, c_ge_lo, c_gt_lo = rank_select16(low_ref, n_sel - c_gt_hi, few | exact_hi,
                                           jnp.full((1, tq), I16_MIN, jnp.int32), jnp.full((1, tq), -I16_MIN, jnp.int32),
                                           in_bucket, zeros, in_bucket, LOWER_STEPS_PER_CHECK)
    lo = jnp.where(exact_hi, t_hi * HALF_SPAN, t_hi * HALF_SPAN + (t_lo - I16_MIN))
    clo = jnp.where(exact_hi, n_sel, c_gt_hi + c_ge_lo)
    chi = c_gt_hi + c_gt_lo
    thr = jnp.where(few, jnp.int32(INT_MIN + 1), lo)
    thr8 = jnp.broadcast_to(thr, (SUBLANES, tq))
    tied = jnp.logical_not(few) & (clo > n_sel)
    need = n_sel - chi
    n_keys = mask_ref.shape[2]

    def tie_cut(_):
        def count_le(j):
            j8 = jnp.broadcast_to(j, (SUBLANES, tq))

            def body(c, cnt):
                ks = pl.multiple_of(c * ck, ck)
                kk = key_ref[pl.ds(ks, ck), :].reshape(groups, SUBLANES, tq)
                idx = (c * ck + kloc).reshape(groups, SUBLANES, tq)
                hit = jnp.where(kk == thr8[None], jnp.where(idx <= j8[None], 1, 0), 0)
                return cnt + jnp.sum(hit, axis=0)

            cnt = lax.fori_loop(0, nk, body, jnp.zeros((SUBLANES, tq), jnp.int32))
            return jnp.sum(cnt, axis=0, keepdims=True)

        def bisect(_, st):
            jlo, jhi = st
            mid = jlo + jnp.right_shift(jhi - jlo, 1)
            ok = count_le(mid) >= need
            return jnp.where(ok, jlo, mid), jnp.where(ok, mid, jhi)

        _, jhi = lax.fori_loop(0, n_keys.bit_length(), bisect,
                               (jnp.full((1, tq), -1, jnp.int32), jnp.full((1, tq), n_keys - 1, jnp.int32)))
        return jnp.where(tied, jhi, int_max)

    jcut = lax.cond(jnp.max(jnp.where(tied, 1, 0)) > 0, tie_cut, lambda _: jnp.full((1, tq), int_max, jnp.int32), 0)
    jcut8 = jnp.broadcast_to(jcut, (SUBLANES, tq))

    def write_chunk(c, carry):
        ks = pl.multiple_of(c * ck, ck)
        kk = key_ref[pl.ds(ks, ck), :].reshape(groups, SUBLANES, tq)
        idx = (c * ck + kloc).reshape(groups, SUBLANES, tq)
        at_thr = jnp.where(kk == thr8[None], jnp.where(idx <= jcut8[None], 1, 0), 0)
        sel = jnp.where(kk > thr8[None], 1, at_thr)
        mask_ref[0, 0, pl.ds(ks, ck), :] = sel.reshape(ck, tq).astype(mask_ref.dtype)
        return carry

    lax.fori_loop(0, nk, write_chunk, 0)

    def zero_chunk(c, carry):
        ks = pl.multiple_of(c * ck, ck)
        mask_ref[0, 0, pl.ds(ks, ck), :] = jnp.zeros((ck, tq), mask_ref.dtype)
        return carry

    lax.fori_loop(nk, mask_ref.shape[2] // ck, zero_chunk, 0)


def indexer_mask(qi, small, k_even, k_odd, b, t, n_sel):
    tq = _tile(t, 512)
    ck = _tile(t, 512)
    return pl.pallas_call(
        functools.partial(_indexer_body, tq=tq, ck=ck, n_sel=n_sel),
        grid=(b, t // tq),
        in_specs=[
            pl.BlockSpec((1, tq, IDX_HEADS * IDX_DIM), lambda bb, i: (bb, i, 0)),
            pl.BlockSpec((1, tq, LANES), lambda bb, i: (bb, i, 0)),
            pl.BlockSpec((1, t, LANES), lambda bb, i: (bb, 0, 0)),
            pl.BlockSpec((1, t, LANES), lambda bb, i: (bb, 0, 0)),
        ],
        out_specs=pl.BlockSpec((1, 1, t, tq), lambda bb, i: (bb, i, 0, 0)),
        out_shape=jax.ShapeDtypeStruct((b, t // tq, t, tq), jnp.int8),
        scratch_shapes=[pltpu.VMEM((t, tq), jnp.int32), pltpu.VMEM((t, tq), jnp.int16),
                        pltpu.VMEM((t, tq), jnp.int16), pltpu.VMEM((IDX_HEADS, SUBLANES, tq), F32)],
        compiler_params=_params("parallel", "parallel"),
        name="indexer",
    )(qi.reshape(b, t, -1), small, k_even.reshape(b, t, LANES), k_odd.reshape(b, t, LANES))


ATTN_HEAD_GROUP = 4
ATTN_HEAD_GROUP_FIXED = 2


def _attn_body(qi_ref, ki_ref, fix_ref, q_ref, k_ref, vt_ref, mask_ref, bias_ref, z_ref, m0_ref, o_ref, m_ref, acc_ref,
               *, tq):
    qi, ki = qi_ref[pl.program_id(1)], ki_ref[pl.program_id(1)]
    fixed = fix_ref[0] == 1

    @pl.when(ki == 0)
    def _():
        m_ref[...] = jnp.full(m_ref.shape, MASK_NEG, F32)
        acc_ref[...] = jnp.zeros(acc_ref.shape, F32)

    def tile(near, fixed_shift):
        madd = (1.0 - mask_ref[0, 0].astype(F32)) * MASK_NEG
        if fixed_shift:
            madd = madd - m0_ref[0, 0]
        off = jnp.where(ki == qi, 0, 1)
        group = ATTN_HEAD_GROUP_FIXED if fixed_shift else ATTN_HEAD_GROUP
        for g in range(0, A_HEADS, group):
            hs = range(g, g + group)
            grp = slice(g, g + group)
            head = lambda ref, h: ref[0, :, h * A_HEAD_DIM:(h + 1) * A_HEAD_DIM]
            k2 = jnp.stack([head(k_ref, h) for h in hs])
            q2 = jnp.stack([head(q_ref, h) for h in hs])
            s = lax.dot_general(k2, q2, (((2,), (2,)), ((0,), (0,))), preferred_element_type=F32) + madd[None]
            if near:
                s = s + bias_ref[off, grp].astype(F32)
            vt2 = jnp.stack([vt_ref[0, h * V_ROWS:(h + 1) * V_ROWS, :] for h in hs])
            pv = lambda p: lax.dot_general(vt2, p, (((2,), (1,)), ((0,), (0,))), preferred_element_type=F32)
            if fixed_shift:
                acc_ref[grp] = acc_ref[grp] + pv(jnp.exp2(s).astype(BF16))
            else:
                m_prev = m_ref[grp]
                m_new = jnp.maximum(m_prev, jnp.max(s, axis=1, keepdims=True))
                acc_ref[grp] = jnp.exp2(m_prev - m_new) * acc_ref[grp] + pv(jnp.exp2(s - m_new).astype(BF16))
                m_ref[grp] = m_new

    far = ki + 1 < qi
    for near in (False, True):
        for fixed_shift in (False, True):
            cond = jnp.logical_and(jnp.logical_not(far) if near else far,
                                   fixed if fixed_shift else jnp.logical_not(fixed))
            pl.when(cond)(functools.partial(tile, near, fixed_shift))

    @pl.when(ki == qi)
    def _():
        outs = []
        for h in range(A_HEADS):
            a = acc_ref[h]
            outs.append((a[:A_HEAD_DIM] / a[A_HEAD_DIM:A_HEAD_DIM + 1]).T)
        o_ref[0] = (jnp.concatenate(outs, axis=-1) * _silu(z_ref[0].astype(F32))).astype(o_ref.dtype)


MAX_LOGIT_SPAN = 100.0


def softmax_shift(q_gain, k_gain, rel_bias):
    slack = 1.02
    qk = (A_HEAD_DIM ** 0.5) * LOG2E * jnp.max(jnp.abs(q_gain)) * jnp.max(jnp.abs(k_gain)) * slack
    b2 = (rel_bias - rel_bias[REL_BUCKETS - 1]) * (LOG2E * slack)
    shift = qk + jnp.max(b2)
    span = 2.0 * qk + jnp.max(b2) - jnp.min(b2)
    return shift.astype(F32), (span <= MAX_LOGIT_SPAN).astype(jnp.int32)


def attention(q, k, vt, mask, bias, proj3d, shift, use_shift, b, t, tq):
    nq = t // tq
    pairs = [(i, j) for i in range(nq) for j in range(i + 1)]
    qi_tab = jnp.asarray([p[0] for p in pairs], jnp.int32)
    ki_tab = jnp.asarray([p[1] for p in pairs], jnp.int32)
    grid_spec = pltpu.PrefetchScalarGridSpec(
        num_scalar_prefetch=3,
        grid=(b, len(pairs)),
        in_specs=[
            pl.BlockSpec((1, tq, A_WIDTH), lambda bb, p, qi, ki, fx: (bb, qi[p], 0)),
            pl.BlockSpec((1, tq, A_WIDTH), lambda bb, p, qi, ki, fx: (bb, ki[p], 0)),
            pl.BlockSpec((1, A_HEADS * V_ROWS, tq), lambda bb, p, qi, ki, fx: (bb * nq + ki[p], 0, 0)),
            pl.BlockSpec((1, 1, tq, tq), lambda bb, p, qi, ki, fx: (bb, qi[p], ki[p], 0)),
            pl.BlockSpec(bias.shape, lambda bb, p, qi, ki, fx: (0, 0, 0, 0)),
            pl.BlockSpec((1, tq, A_WIDTH), lambda bb, p, qi, ki, fx: (bb, qi[p], COL_Z_A // A_WIDTH)),
            pl.BlockSpec(memory_space=pltpu.SMEM),
        ],
        out_specs=pl.BlockSpec((1, tq, A_WIDTH), lambda bb, p, qi, ki, fx: (bb, qi[p], 0)),
        scratch_shapes=[pltpu.VMEM((A_HEADS, 1, tq), F32), pltpu.VMEM((A_HEADS, V_ROWS, tq), F32)],
    )
    return pl.pallas_call(
        functools.partial(_attn_body, tq=tq),
        grid_spec=grid_spec,
        out_shape=jax.ShapeDtypeStruct((b, t, A_WIDTH), BF16),
        compiler_params=_params("parallel", "arbitrary"),
        name="dsa_attention",
    )(qi_tab, ki_tab, use_shift.reshape(1), q.reshape(b, t, -1), k.reshape(b, t, -1), vt, mask, bias, proj3d,
      shift.reshape(1, 1))


def _gdn_prep_body(x_ref, halo_ref, sm_ref, cw_ref, alog_ref, dtb_ref,
                   q_ref, k_ref, v_ref, gcb_ref, bb_ref, grow_ref, *, tm):
    i = pl.program_id(1)
    x = x_ref[0]
    hal = halo_ref[0]
    hal = jnp.where(i > 0, hal, jnp.zeros_like(hal))
    rt = lax.broadcasted_iota(jnp.int32, (tm, tm), 0)
    ct = lax.broadcasted_iota(jnp.int32, (tm, tm), 1)
    rh = lax.broadcasted_iota(jnp.int32, (SUBLANES, BF16_ROWS), 0)
    ch = lax.broadcasted_iota(jnp.int32, (SUBLANES, BF16_ROWS), 1)
    y = cw_ref[B_CONV - 1:B_CONV, :] * x.astype(F32)
    for j in range(B_CONV - 1):
        back = B_CONV - 1 - j
        main = _dot(jnp.where(ct == rt - back, 1.0, 0.0).astype(BF16), x)
        head = _dot(jnp.where(ch == rh + (BF16_ROWS - back), 1.0, 0.0).astype(BF16), hal)
        y = y + cw_ref[j:j + 1, :] * jnp.concatenate([main[:SUBLANES] + head, main[SUBLANES:]], axis=0)
    y = _silu(y)
    for h in range(B_HEADS):
        sl = slice(h * LANES, (h + 1) * LANES)
        qh = y[:, sl]
        q_ref[0, :, sl] = qh * lax.rsqrt(jnp.sum(qh * qh, axis=-1, keepdims=True) + EPS) * (B_HEAD_DIM ** -0.5)
        kh = y[:, B_WIDTH + h * LANES:B_WIDTH + (h + 1) * LANES]
        k_ref[0, :, sl] = kh * lax.rsqrt(jnp.sum(kh * kh, axis=-1, keepdims=True) + EPS)
    v_ref[0] = y[:, 2 * B_WIDTH:]

    sm = sm_ref[0]
    xg = sm + dtb_ref[...]
    softplus = jnp.maximum(xg, 0.0) + jnp.log(1.0 + jnp.exp(-jnp.abs(xg)))
    g = -jnp.exp(alog_ref[...]) * softplus
    r = lax.broadcasted_iota(jnp.int32, (tm, tm), 0)
    c = lax.broadcasted_iota(jnp.int32, (tm, tm), 1)
    sh = int(math.log2(GDN_CHUNK))
    same_chunk = jnp.right_shift(r, sh) == jnp.right_shift(c, sh)
    tri = jnp.where(jnp.logical_and(same_chunk, c <= r), 1.0, 0.0).astype(F32)
    gc = lax.dot_general(tri, g, (((1,), (0,)), ((), ())), precision=HIGHEST, preferred_element_type=F32)
    beta = _sigmoid(sm)
    for h in range(B_HEADS):
        sl = slice(h * LANES, (h + 1) * LANES)
        gcb_ref[0, :, sl] = jnp.broadcast_to(gc[:, SM_ALPHA + h:SM_ALPHA + h + 1], (tm, LANES))
        bb_ref[0, :, sl] = jnp.broadcast_to(beta[:, SM_BETA + h:SM_BETA + h + 1], (tm, LANES))
    gct = gc.T
    for cc in range(tm // GDN_CHUNK):
        grow_ref[0, cc] = gct[SM_ALPHA:SM_ALPHA + B_HEADS, cc * GDN_CHUNK:(cc + 1) * GDN_CHUNK]


def gdn_prep(proj3d, small3d, conv_w, a_log, dt_bias, b, t):
    tm = _tile(t, 256)
    pad_row = lambda v: jnp.zeros((1, LANES), F32).at[0, SM_ALPHA:SM_ALPHA + B_HEADS].set(v)
    nc = tm // GDN_CHUNK
    act = jax.ShapeDtypeStruct((b, t, B_WIDTH), F32)
    full = lambda a: pl.BlockSpec(a.shape, lambda bb, i: (0,) * a.ndim)
    consts = [conv_w, pad_row(a_log), pad_row(dt_bias)]
    return pl.pallas_call(
        functools.partial(_gdn_prep_body, tm=tm),
        grid=(b, t // tm),
        in_specs=[
            pl.BlockSpec((1, tm, 3 * B_WIDTH), lambda bb, i: (bb, i, COL_QKV_B // (3 * B_WIDTH))),
            pl.BlockSpec((1, BF16_ROWS, 3 * B_WIDTH),
                         lambda bb, i: (bb, jnp.maximum(i * (tm // BF16_ROWS) - 1, 0), 0)),
            pl.BlockSpec((1, tm, LANES), lambda bb, i: (bb, i, 0)),
        ] + [full(c) for c in consts],
        out_specs=[pl.BlockSpec((1, tm, B_WIDTH), lambda bb, i: (bb, i, 0))] * 5
        + [pl.BlockSpec((1, nc, B_HEADS, GDN_CHUNK), lambda bb, i: (bb, i, 0, 0))],
        out_shape=[act] * 5 + [jax.ShapeDtypeStruct((b, t // GDN_CHUNK, B_HEADS, GDN_CHUNK), F32)],
        compiler_params=_params("parallel", "parallel"),
        name="gdn_prep",
    )(proj3d, proj3d, small3d, *consts)


def _gdn_body(q_ref, k_ref, v_ref, gcb_ref, bb_ref, grow_ref, z_ref, on_ref, o_ref, s_ref, *, nc):
    @pl.when(pl.program_id(1) == 0)
    def _():
        s_ref[...] = jnp.zeros(s_ref.shape, F32)

    cs, nh = GDN_CHUNK, B_HEADS
    nb = nc * nh
    ri = lax.broadcasted_iota(jnp.int32, (nb, cs, cs), 1)
    ci = lax.broadcasted_iota(jnp.int32, (nb, cs, cs), 2)
    lower = ci <= ri
    strict = ci < ri
    eye = jnp.where(ci == ri, 1.0, 0.0).astype(F32)
    bf = lambda a: a.astype(BF16)
    bmm = lambda a, b: lax.dot_general(a, b, (((2,), (1,)), ((0,), (0,))), preferred_element_type=F32)
    bmm_nt = lambda a, b: lax.dot_general(a, b, (((2,), (2,)), ((0,), (0,))), preferred_element_type=F32)
    bmm_tn = lambda a, b: lax.dot_general(a, b, (((1,), (1,)), ((0,), (0,))), preferred_element_type=F32)

    def stack(ref):
        return jnp.stack([ref[0, c * cs:(c + 1) * cs, h * LANES:(h + 1) * LANES]
                          for c in range(nc) for h in range(nh)])

    q, k, v = stack(q_ref), stack(k_ref), stack(v_ref)
    gcb = stack(gcb_ref)
    beta = stack(bb_ref)
    grow = jnp.stack([grow_ref[0, c, h:h + 1, :] for c in range(nc) for h in range(nh)])
    diff = gcb[:, :, :cs] - grow
    decay = jnp.where(lower, jnp.exp(jnp.where(lower, diff, 0.0)), 0.0)
    eg = jnp.exp(gcb)
    glast = gcb[:, cs - 1:cs, :]
    kb = k * beta
    lmat = jnp.where(strict, bmm_nt(bf(kb), bf(k)) * decay, 0.0)
    n = -lmat
    tinv = eye + n
    for _ in range(int(math.log2(cs)) - 1):
        n = bmm(bf(n), bf(n))
        tinv = tinv + bmm(bf(tinv), bf(n))
    uw = bmm(bf(tinv), bf(jnp.concatenate([v * beta, kb * eg], axis=-1)))
    attn = bf(jnp.where(lower, bmm_nt(bf(q), bf(k)) * decay, 0.0))
    qg = bf(q * eg)
    kdec = bf(k * jnp.exp(glast - gcb))
    egl = jnp.exp(glast)

    s = s_ref[...]
    for c in range(nc):
        sl = slice(c * nh, (c + 1) * nh)
        sb = bf(s)
        v_new = uw[sl, :, :LANES] - bmm(bf(uw[sl, :, LANES:]), sb)
        o = bmm(qg[sl], sb) + bmm(attn[sl], bf(v_new))
        s = s * egl[sl] + bmm_tn(kdec[sl], bf(v_new))
        o = o * lax.rsqrt(jnp.mean(o * o, axis=-1, keepdims=True) + EPS) * on_ref[...]
        for h in range(nh):
            rows, cols = slice(c * cs, (c + 1) * cs), slice(h * LANES, (h + 1) * LANES)
            o_ref[0, rows, cols] = (o[h] * _silu(z_ref[0, rows, cols].astype(F32))).astype(o_ref.dtype)
    s_ref[...] = s


def gdn_scan(qh, kh, v, gcb, bb, grow, proj3d, o_norm, b, t):
    tt = _tile(t, 256)
    nc = tt // GDN_CHUNK
    blk = pl.BlockSpec((1, tt, B_WIDTH), lambda bb_, i: (bb_, i, 0))
    return pl.pallas_call(
        functools.partial(_gdn_body, nc=nc),
        grid=(b, t // tt),
        in_specs=[blk] * 5 + [
            pl.BlockSpec((1, nc, B_HEADS, GDN_CHUNK), lambda bb_, i: (bb_, i, 0, 0)),
            pl.BlockSpec((1, tt, B_WIDTH), lambda bb_, i: (bb_, i, COL_Z_B // B_WIDTH)),
            pl.BlockSpec((1, LANES), lambda bb_, i: (0, 0)),
        ],
        out_specs=blk,
        out_shape=jax.ShapeDtypeStruct((b, t, B_WIDTH), BF16),
        scratch_shapes=[pltpu.VMEM((B_HEADS, B_HEAD_DIM, B_HEAD_DIM), F32)],
        compiler_params=_params("parallel", "arbitrary"),
        name="gdn_scan",
    )(qh, kh, v, gcb, bb, grow, proj3d, o_norm.reshape(1, LANES))


C_HALO = 32
D_HALO = BF16_ROWS


def _cd_layer_body(a_ref, g_ref, ah_ref, gh_ref, zc_ref, bg_ref, cg_ref, ud_ref, cgh_ref, udh_ref, zd_ref,
                   x_ref, wo_ref, dww_ref, dwb_ref, lnw_ref, lnb_ref, dcw_ref, o_ref, us_ref, ds_ref, y_ref,
                   *, tm, tiles_per_seq):
    s = pl.program_id(0)
    first = lax.rem(s, tiles_per_seq) == 0
    slot = lax.rem(s, 2)

    @pl.when(s == 0)
    def _():
        y_ref[...] = jnp.zeros(y_ref.shape, y_ref.dtype)

    o_ref[...] = x_ref[...] + _dot(y_ref[1 - slot], wo_ref[...])

    f32 = lambda r: r[...].astype(F32)
    uh = f32(ah_ref) * _sigmoid(f32(gh_ref))
    us_ref[0, 0:C_HALO, :] = jnp.where(first, jnp.zeros_like(uh), uh)
    us_ref[0, C_HALO:, :] = f32(a_ref) * _sigmoid(f32(g_ref))
    span = tm + C_HALO - SUBLANES
    for r in range(1, SUBLANES):
        us_ref[r, 0:span, :] = us_ref[0, pl.ds(r, span), :]
    u = jnp.zeros((tm, C_WIDTH), F32)
    for j in range(C_CONV):
        off = C_HALO - (C_CONV - 1) + j
        r, base = off % SUBLANES, off - off % SUBLANES
        u = u + dww_ref[j:j + 1, :] * us_ref[r, base:base + tm, :]
    u = u + dwb_ref[...]
    mu = jnp.mean(u, axis=-1, keepdims=True)
    uc = u - mu
    u = uc * lax.rsqrt(jnp.mean(uc * uc, axis=-1, keepdims=True) + EPS) * lnw_ref[...] + lnb_ref[...]
    y_ref[slot, :, 0:C_WIDTH] = (_silu(u) * _silu(f32(zc_ref))).astype(y_ref.dtype)

    dh = f32(cgh_ref) * f32(udh_ref)
    ds_ref[0:D_HALO, :] = jnp.where(first, jnp.zeros_like(dh), dh)
    ds_ref[D_HALO:, :] = f32(cg_ref) * f32(ud_ref)
    d = jnp.zeros((tm, D_WIDTH), F32)
    for j in range(D_CONV):
        d = d + dcw_ref[j:j + 1, :] * ds_ref[pl.ds(D_HALO - (D_CONV - 1) + j, tm), :]
    y_ref[slot, :, C_WIDTH:] = (f32(bg_ref) * d * _silu(f32(zd_ref))).astype(y_ref.dtype)


def cd_layer(proj, x2d, w_out_bf16, dw_w, dw_b, ln_w, ln_b, d_conv_w, t):
    m, n = x2d.shape
    tm = _tile(t, 256)
    w = C_WIDTH
    last = m // tm - 1
    cur = lambda s: jnp.minimum(s, last)
    col = lambda c: pl.BlockSpec((tm, w), lambda s, c=c: (cur(s), c))
    halo = lambda c, rows: pl.BlockSpec(
        (rows, w), lambda s, c=c, rows=rows: (jnp.maximum(cur(s) * (tm // rows) - 1, 0), c))
    prev = pl.BlockSpec((tm, n), lambda s: (jnp.maximum(s - 1, 0), 0))
    row = lambda a: a.reshape(1, -1)
    full = lambda a: pl.BlockSpec(a.shape, lambda s: (0,) * a.ndim)
    consts = [w_out_bf16, dw_w, row(dw_b), row(ln_w), row(ln_b), d_conv_w]
    return pl.pallas_call(
        functools.partial(_cd_layer_body, tm=tm, tiles_per_seq=t // tm),
        grid=(m // tm + 1,),
        in_specs=[col(0), col(1), halo(0, C_HALO), halo(1, C_HALO), col(2), col(3), col(4), col(5),
                  halo(4, D_HALO), halo(5, D_HALO), col(6), prev] + [full(c) for c in consts],
        out_specs=prev,
        out_shape=jax.ShapeDtypeStruct((m, n), F32),
        scratch_shapes=[pltpu.VMEM((SUBLANES, tm + C_HALO, w), F32), pltpu.VMEM((tm + D_HALO, w), F32),
                        pltpu.VMEM((2, tm, C_WIDTH + D_WIDTH), BF16)],
        compiler_params=_params("arbitrary"),
        name="cd_layer",
    )(*([proj] * 11), x2d, *consts)


def _reorder_ab_w_in(w):
    offs = [0]
    for s in AB_SPLITS:
        offs.append(offs[-1] + s)
    part = lambda n: w[:, offs[n]:offs[n + 1]]
    c_q, c_kv, k_idx, w_idx, z_a, qkv_b, beta_b, alpha_b, z_b = (part(n) for n in range(9))
    small = jnp.concatenate([k_idx, w_idx, beta_b, alpha_b], axis=1)
    small = jnp.pad(small, ((0, 0), (0, LANES - small.shape[1])))
    out = jnp.concatenate([qkv_b, z_a, z_b, c_q, c_kv], axis=1)
    return jnp.pad(out, ((0, 0), (0, AB_IN_PAD - out.shape[1]))).astype(BF16), small.astype(BF16)


def _ab_layer(x2d, b, t, norm_w, rel_bias, w_in, q_norm, w_uq, w_iq, kv_norm, w_uk, w_uv, q_gain, k_gain,
              ik_w, ik_b, conv_w, a_log, dt_bias, o_norm, w_out):
    proj, small = norm_matmul(x2d, norm_w, *_reorder_ab_w_in(w_in))
    proj3d, small3d = proj.reshape(b, t, AB_IN_PAD), small.reshape(b, t, LANES)
    tq = _tile(t, 512)
    q, k, v, qi, k_even, k_odd = dsa_prep(proj, small, q_norm, kv_norm, w_uq, w_iq, w_uk, w_uv, q_gain, k_gain,
                                          ik_w, ik_b, tm=tq)
    n_sel = min(TOPK_MAX, t // 4)
    mask = indexer_mask(qi, small3d, k_even, k_odd, b, t, n_sel)
    shift, use_shift = softmax_shift(q_gain, k_gain, rel_bias)
    y_a = attention(q, k, v, mask, bias_tiles(rel_bias, tq), proj3d, shift, use_shift, b, t, tq)
    qh, kh, vv, gcb, bb, grow = gdn_prep(proj3d, small3d, conv_w, a_log, dt_bias, b, t)
    y_b = gdn_scan(qh, kh, vv, gcb, bb, grow, proj3d, o_norm, b, t)
    wo = w_out.astype(BF16)
    return out_proj(x2d, y_a.reshape(b * t, -1), y_b.reshape(b * t, -1), wo[:A_WIDTH], wo[A_WIDTH:])


def _cd_layer(x2d, b, t, norm_w, w_in, dw_w, dw_b, ln_w, ln_b, d_conv_w, w_out):
    proj = norm_matmul(x2d, norm_w, w_in.astype(BF16))
    return cd_layer(proj, x2d, w_out.astype(BF16), dw_w, dw_b, ln_w, ln_b, d_conv_w, t)


def kernel(x, norm_w, rel_bias, ab_w_in, a_q_norm, a_w_uq, a_w_iq, a_kv_norm, a_w_uk, a_w_uv, a_q_gain,
           a_k_gain, a_ik_norm_w, a_ik_norm_b, b_conv_w, b_a_log, b_dt_bias, b_o_norm, ab_w_out, cd_w_in,
           c_dw_w, c_dw_b, c_ln_w, c_ln_b, d_conv_w, cd_w_out):
    b, t, d = x.shape
    depth = norm_w.shape[0]
    x2d = x.reshape(b * t, d)
    for i in range(depth):
        j = i // 2
        if i % 2 == 0:
            x2d = _ab_layer(x2d, b, t, norm_w[i], rel_bias, ab_w_in[j], a_q_norm[j], a_w_uq[j], a_w_iq[j],
                            a_kv_norm[j], a_w_uk[j], a_w_uv[j], a_q_gain[j], a_k_gain[j], a_ik_norm_w[j],
                            a_ik_norm_b[j], b_conv_w[j], b_a_log[j], b_dt_bias[j], b_o_norm[j], ab_w_out[j])
        else:
            x2d = _cd_layer(x2d, b, t, norm_w[i], cd_w_in[j], c_dw_w[j], c_dw_b[j], c_ln_w[j], c_ln_b[j],
                            d_conv_w[j], cd_w_out[j])
    return x2d.reshape(b, t, d)
```

```python
import functools
import math

import jax
import jax.numpy as jnp
from jax import lax
from jax.experimental import pallas as pl
from jax.experimental.pallas import tpu as pltpu

A_HEADS = 8
A_HEAD_DIM = 128
A_WIDTH = A_HEADS * A_HEAD_DIM
A_Q_LORA = 512
A_KV_LORA = 256
IDX_HEADS = 16
IDX_DIM = 64
TOPK_MAX = 256
REL_BUCKETS = 32
REL_MAX_DIST = 128
B_HEADS = 8
B_HEAD_DIM = 128
B_WIDTH = B_HEADS * B_HEAD_DIM
B_CONV = 4
GDN_CHUNK = 64
C_WIDTH = 1024
C_CONV = 31
D_WIDTH = 1024
D_CONV = 3
EPS = 1e-6

AB_SPLITS = (A_Q_LORA, A_KV_LORA, IDX_DIM, IDX_HEADS, A_WIDTH, 3 * B_WIDTH, B_HEADS, B_HEADS, B_WIDTH)
AB_IN_PAD = 6144
SM_KIDX = 0
SM_WIDX = 64
SM_BETA = 80
SM_ALPHA = 88
COL_QKV_B = 0
COL_Z_A = 3072
COL_Z_B = 4096
COL_C_Q = 5120
COL_C_KV = 5632

LANES = 128
MASK_NEG = -1e30
INT_MIN = -(2 ** 31)
LOG2E = math.log2(math.e)
BF16_ROWS = 16
V_ROWS = A_HEAD_DIM + BF16_ROWS
VMEM_LIMIT = 56 * 1024 * 1024

F32 = jnp.float32
BF16 = jnp.bfloat16
HIGHEST = lax.Precision.HIGHEST


def _tile(n, pref):
    t = min(n, pref)
    assert n % t == 0, (n, t)
    return t


def _params(*sem):
    return pltpu.CompilerParams(dimension_semantics=sem, vmem_limit_bytes=VMEM_LIMIT)


def _dot(a, b):
    return jnp.dot(a, b, preferred_element_type=F32)


def _dot_nt(a, b):
    return lax.dot_general(a, b, (((1,), (1,)), ((), ())), preferred_element_type=F32)


def _silu(x):
    return x * (1.0 / (1.0 + jnp.exp(-x)))


def _sigmoid(x):
    return 1.0 / (1.0 + jnp.exp(-x))


def _norm_matmul_body(*refs, with_f32_cols):
    if with_f32_cols:
        x_ref, nw_ref, w_ref, ws_ref, o_ref, os_ref, h_ref = refs
    else:
        x_ref, nw_ref, w_ref, o_ref, h_ref = refs

    @pl.when(pl.program_id(1) == 0)
    def _():
        x = x_ref[...]
        ms = jnp.mean(x * x, axis=-1, keepdims=True)
        h_ref[...] = (x * lax.rsqrt(ms + EPS) * nw_ref[...]).astype(h_ref.dtype)
        if with_f32_cols:
            os_ref[...] = _dot(h_ref[...], ws_ref[...])

    o_ref[...] = _dot(h_ref[...], w_ref[...]).astype(o_ref.dtype)


def norm_matmul(x2d, norm_w, w_bf16, w_f32_cols=None):
    m, k = x2d.shape
    n = w_bf16.shape[1]
    tm = _tile(m, 1024)
    tn = max(d for d in range(256, min(n, 2048) + 1, 256) if n % d == 0)
    extra = w_f32_cols is not None
    in_specs = [
        pl.BlockSpec((tm, k), lambda i, j: (i, 0)),
        pl.BlockSpec((1, k), lambda i, j: (0, 0)),
        pl.BlockSpec((k, tn), lambda i, j: (0, j)),
    ]
    out_specs = [pl.BlockSpec((tm, tn), lambda i, j: (i, j))]
    out_shape = [jax.ShapeDtypeStruct((m, n), BF16)]
    args = [x2d, norm_w.reshape(1, k), w_bf16]
    if extra:
        in_specs.append(pl.BlockSpec((k, LANES), lambda i, j: (0, 0)))
        out_specs.append(pl.BlockSpec((tm, LANES), lambda i, j: (i, 0)))
        out_shape.append(jax.ShapeDtypeStruct((m, LANES), F32))
        args.append(w_f32_cols)
    res = pl.pallas_call(
        functools.partial(_norm_matmul_body, with_f32_cols=extra),
        grid=(m // tm, n // tn),
        in_specs=in_specs,
        out_specs=out_specs,
        out_shape=out_shape,
        scratch_shapes=[pltpu.VMEM((tm, k), BF16)],
        compiler_params=_params("parallel", "arbitrary"),
        name="norm_matmul",
    )(*args)
    return res if extra else res[0]


def _out_proj_body(x_ref, ya_ref, yb_ref, wa_ref, wb_ref, o_ref):
    o_ref[...] = x_ref[...] + _dot(ya_ref[...], wa_ref[...]) + _dot(yb_ref[...], wb_ref[...])


def out_proj(x2d, ya, yb, wa_bf16, wb_bf16):
    m, n = x2d.shape
    ka, kb = ya.shape[1], yb.shape[1]
    tm = _tile(m, 512)
    return pl.pallas_call(
        _out_proj_body,
        grid=(m // tm,),
        in_specs=[
            pl.BlockSpec((tm, n), lambda i: (i, 0)),
            pl.BlockSpec((tm, ka), lambda i: (i, 0)),
            pl.BlockSpec((tm, kb), lambda i: (i, 0)),
            pl.BlockSpec((ka, n), lambda i: (0, 0)),
            pl.BlockSpec((kb, n), lambda i: (0, 0)),
        ],
        out_specs=pl.BlockSpec((tm, n), lambda i: (i, 0)),
        out_shape=jax.ShapeDtypeStruct((m, n), F32),
        compiler_params=_params("parallel"),
        name="out_proj",
    )(x2d, ya, yb, wa_bf16, wb_bf16)


def _head_rmsnorm(y, gain_row, scale):
    outs = []
    for h in range(y.shape[1] // LANES):
        seg = y[:, h * LANES:(h + 1) * LANES]
        ms = jnp.mean(seg * seg, axis=-1, keepdims=True)
        outs.append(seg * lax.rsqrt(ms + EPS) * (gain_row * scale))
    return jnp.concatenate(outs, axis=-1)


def _dsa_prep_body(cq_ref, ckv_ref, sm_ref, qn_ref, kvn_ref, wuq_ref, wiq_ref, wuk_ref, wuv_ref,
                   qg_ref, kg_ref, ikw_ref, ikb_ref,
                   q_ref, k_ref, v_ref, qi_ref, ke_ref, ko_ref):
    cq = cq_ref[...].astype(F32)
    cq = cq * lax.rsqrt(jnp.mean(cq * cq, axis=-1, keepdims=True) + EPS) * qn_ref[...]
    cqb = cq.astype(BF16)
    q = _dot(cqb, wuq_ref[...])
    q_ref[...] = _head_rmsnorm(q, qg_ref[...], A_HEAD_DIM ** -0.5 * LOG2E).astype(q_ref.dtype)
    qi_ref[...] = _dot(cqb, wiq_ref[...]).astype(qi_ref.dtype)

    ckv = ckv_ref[...].astype(F32)
    ckv = ckv * lax.rsqrt(jnp.mean(ckv * ckv, axis=-1, keepdims=True) + EPS) * kvn_ref[...]
    ckvb = ckv.astype(BF16)
    k = _dot(ckvb, wuk_ref[...])
    k_ref[...] = _head_rmsnorm(k, kg_ref[...], 1.0).astype(k_ref.dtype)
    vt = _dot_nt(wuv_ref[...], ckvb).astype(v_ref.dtype)
    ones = jnp.ones((BF16_ROWS, vt.shape[1]), v_ref.dtype)
    for h in range(A_HEADS):
        v_ref[0, h * V_ROWS:h * V_ROWS + A_HEAD_DIM, :] = vt[h * A_HEAD_DIM:(h + 1) * A_HEAD_DIM, :]
        v_ref[0, h * V_ROWS + A_HEAD_DIM:(h + 1) * V_ROWS, :] = ones

    kx = sm_ref[...][:, SM_KIDX:SM_KIDX + IDX_DIM]
    mu = jnp.mean(kx, axis=-1, keepdims=True)
    kc = kx - mu
    kl = kc * lax.rsqrt(jnp.mean(kc * kc, axis=-1, keepdims=True) + EPS) * ikw_ref[...] + ikb_ref[...]
    z = jnp.zeros_like(kl)
    ke_ref[...] = jnp.concatenate([kl, z], axis=-1).astype(ke_ref.dtype)
    ko_ref[...] = jnp.concatenate([z, kl], axis=-1).astype(ko_ref.dtype)


def dsa_prep(proj, small, q_norm, kv_norm, w_uq, w_iq, w_uk, w_uv, q_gain, k_gain, ik_w, ik_b, tm=None):
    m = proj.shape[0]
    tm = _tile(m, 512) if tm is None else tm
    row = lambda a: a.reshape(1, -1)
    full = lambda a: pl.BlockSpec(a.shape, lambda i: (0,) * a.ndim)
    consts = [row(q_norm), row(kv_norm), w_uq.astype(BF16), w_iq.astype(BF16), w_uk.astype(BF16),
              w_uv.T.astype(BF16), row(q_gain), row(k_gain), row(ik_w), row(ik_b)]
    outs = [jax.ShapeDtypeStruct((m, A_WIDTH), BF16)] * 2 + [
        jax.ShapeDtypeStruct((m // tm, A_HEADS * V_ROWS, tm), BF16),
        jax.ShapeDtypeStruct((m, IDX_HEADS * IDX_DIM), BF16),
        jax.ShapeDtypeStruct((m, LANES), BF16), jax.ShapeDtypeStruct((m, LANES), BF16)]
    out_specs = [pl.BlockSpec((tm, s.shape[-1]), lambda i: (i, 0)) for s in outs]
    out_specs[2] = pl.BlockSpec((1, A_HEADS * V_ROWS, tm), lambda i: (i, 0, 0))
    return pl.pallas_call(
        _dsa_prep_body,
        grid=(m // tm,),
        in_specs=[
            pl.BlockSpec((tm, A_Q_LORA), lambda i: (i, COL_C_Q // A_Q_LORA)),
            pl.BlockSpec((tm, A_KV_LORA), lambda i: (i, COL_C_KV // A_KV_LORA)),
            pl.BlockSpec((tm, LANES), lambda i: (i, 0)),
        ] + [full(c) for c in consts],
        out_specs=out_specs,
        out_shape=outs,
        compiler_params=_params("parallel"),
        name="dsa_prep",
    )(proj, proj, small, *consts)


def _t5_bucket(dist):
    max_exact = REL_BUCKETS // 2
    large = max_exact + (jnp.log(jnp.maximum(dist, 1).astype(F32) / max_exact)
                         / math.log(REL_MAX_DIST / max_exact) * (REL_BUCKETS - max_exact)).astype(jnp.int32)
    large = jnp.minimum(large, REL_BUCKETS - 1)
    return jnp.where(dist < max_exact, dist, large)


def _bias_tiles_body(rb_ref, o_ref, *, t):
    h = pl.program_id(0)
    sb = min(t, LANES)
    row = lax.broadcasted_iota(jnp.int32, (sb, sb), 0)
    col = lax.broadcasted_iota(jnp.int32, (sb, sb), 1)
    for typ in range(2):
        for rk in range(t // sb):
            for cq in range(t // sb):
                base = typ * t + (cq - rk) * sb
                blk = (typ, 0, slice(rk * sb, (rk + 1) * sb), slice(cq * sb, (cq + 1) * sb))
                if base - (sb - 1) >= REL_MAX_DIST or base + (sb - 1) < 0:
                    o_ref[blk] = jnp.zeros((sb, sb), o_ref.dtype)
                    continue
                bucket = _t5_bucket(jnp.maximum(base + col - row, 0))
                acc = jnp.zeros((sb, sb), F32)
                for b in range(REL_BUCKETS):
                    acc = jnp.where(bucket == b, rb_ref[b, h], acc)
                o_ref[blk] = ((acc - rb_ref[REL_BUCKETS - 1, h]) * LOG2E).astype(o_ref.dtype)


def bias_tiles(rel_bias, t):
    return pl.pallas_call(
        functools.partial(_bias_tiles_body, t=t),
        grid=(A_HEADS,),
        in_specs=[pl.BlockSpec(memory_space=pltpu.SMEM)],
        out_specs=pl.BlockSpec((2, 1, t, t), lambda h: (0, h, 0, 0)),
        out_shape=jax.ShapeDtypeStruct((2, A_HEADS, t, t), BF16),
        compiler_params=_params("parallel"),
        name="bias_tiles",
    )(rel_bias)


SUBLANES = 8
MAX_SELECT_STEPS = 24
UPPER_STEPS_PER_CHECK = 4
LOWER_STEPS_PER_CHECK = 2
HALF_BITS = 16
HALF_SPAN = 2 ** HALF_BITS
I16_MIN = -(2 ** (HALF_BITS - 1))


def _key_of(x):
    bits = pltpu.bitcast(x, jnp.int32)
    key = jnp.where(bits < 0, bits ^ jnp.int32(0x7FFFFFFF), bits)
    return jnp.where(bits == jnp.int32(INT_MIN), 0, key)


def _indexer_body(qi_ref, sm_ref, ke_ref, ko_ref, mask_ref, key_ref, hi_ref, low_ref, w_ref, *, tq, ck, n_sel):
    i = pl.program_id(1)
    nk = ((i + 1) * tq + ck - 1) // ck
    groups = ck // SUBLANES
    scale = (IDX_HEADS ** -0.5) * (IDX_DIM ** -0.5)
    smt = sm_ref[0].T
    for h in range(IDX_HEADS):
        w_ref[h] = jnp.broadcast_to(smt[SM_WIDX + h:SM_WIDX + h + 1, :] * scale, (SUBLANES, tq))
    kloc = lax.broadcasted_iota(jnp.int32, (ck, tq), 0)
    qpos = i * tq + lax.broadcasted_iota(jnp.int32, (ck, tq), 1)
    int_max = jnp.int32(2 ** 31 - 1)

    def score_chunk(c, carry):
        kmin, kmax = carry
        ks = pl.multiple_of(c * ck, ck)
        ke = ke_ref[0, pl.ds(ks, ck), :]
        ko = ko_ref[0, pl.ds(ks, ck), :]
        acc = jnp.zeros((ck, tq), F32)
        for j in range(IDX_HEADS // 2):
            qp = qi_ref[0, :, j * LANES:(j + 1) * LANES]
            for par, kk in ((0, ke), (1, ko)):
                s = jnp.maximum(_dot_nt(kk, qp), 0.0)
                acc = acc + s * jnp.tile(w_ref[2 * j + par], (groups, 1))
        key = _key_of(acc)
        valid = c * ck + kloc <= qpos
        masked = jnp.where(valid, key, jnp.int32(INT_MIN))
        key_ref[pl.ds(ks, ck), :] = masked
        hi_ref[pl.ds(ks, ck), :] = jnp.right_shift(masked, HALF_BITS).astype(jnp.int16)
        kmin = jnp.minimum(kmin, jnp.min(jnp.where(valid, key, int_max).reshape(groups, SUBLANES, tq), axis=0))
        kmax = jnp.maximum(kmax, jnp.max(masked.reshape(groups, SUBLANES, tq), axis=0))
        return kmin, kmax

    kmin, kmax = lax.fori_loop(0, nk, score_chunk, (jnp.full((SUBLANES, tq), int_max, jnp.int32),
                                                    jnp.full((SUBLANES, tq), INT_MIN, jnp.int32)))

    rows16 = ck // BF16_ROWS
    i16_min = I16_MIN

    def scan16(a_ref, init, fn):
        def body(c, acc):
            ks = pl.multiple_of(c * ck, ck)
            return fn(acc, a_ref[pl.ds(ks, ck), :].reshape(rows16, BF16_ROWS, tq))

        return lax.fori_loop(0, nk, body, init)

    def count_ge16(a_ref, p):
        p16 = jnp.broadcast_to(p.astype(jnp.int16), (BF16_ROWS, tq))

        def fn(cnt, a):
            ind = jnp.where(a >= p16[None], jnp.int16(1), jnp.int16(0))
            for g in range(rows16):
                cnt = cnt + ind[g]
            return cnt

        cnt = scan16(a_ref, jnp.zeros((BF16_ROWS, tq), jnp.int16), fn)
        return jnp.sum(cnt.astype(jnp.int32), axis=0, keepdims=True)

    def max_le16(a_ref, h):
        h16 = jnp.broadcast_to(h.astype(jnp.int16), (BF16_ROWS, tq))

        def fn(acc, a):
            v = jnp.where(a <= h16[None], a, jnp.int16(i16_min))
            for g in range(rows16):
                acc = jnp.where(v[g] > acc, v[g], acc)
            return acc

        acc = scan16(a_ref, jnp.full((BF16_ROWS, tq), i16_min, jnp.int16), fn)
        return jnp.max(acc.astype(jnp.int32), axis=0, keepdims=True)

    def rank_select16(a_ref, r, skip, lo, hi, clo, chi, floor_count, steps_per_check):
        def closed(lo, hi, clo):
            return skip | (clo == r) | (hi == lo + 1)

        def open_rows(lo, hi, clo, chi):
            return jnp.logical_not(closed(lo, hi, clo) | (chi == r - 1))

        def cond(st):
            it, lo, hi, clo, chi = st
            n_open = jnp.max(jnp.where(open_rows(lo, hi, clo, chi), 1, 0))
            return jnp.logical_and(it < MAX_SELECT_STEPS, n_open > 0)

        def step(st):
            it, lo, hi, clo, chi = st
            for _ in range(steps_per_check):
                upd = open_rows(lo, hi, clo, chi)
                p = lo + jnp.maximum(jnp.right_shift(hi - lo, 1), 1)
                cnt = count_ge16(a_ref, p)
                ge = cnt >= r
                up, dn = upd & ge, upd & jnp.logical_not(ge)
                lo, hi = jnp.where(up, p, lo), jnp.where(dn, p, hi)
                clo, chi = jnp.where(up, cnt, clo), jnp.where(dn, cnt, chi)
            return it + steps_per_check, lo, hi, clo, chi

        _, lo, hi, clo, chi = lax.while_loop(cond, step, (jnp.int32(0), lo, hi, clo, chi))
        last = jnp.logical_not(closed(lo, hi, clo))
        t_last = max_le16(a_ref, hi - 1)
        c_last = jnp.where(t_last == i16_min, floor_count, count_ge16(a_ref, t_last))
        return jnp.where(last, t_last, lo), jnp.where(last, c_last, clo), chi

    n_valid = i * tq + lax.broadcasted_iota(jnp.int32, (1, tq), 1) + 1
    few = n_valid <= n_sel
    zeros = jnp.zeros((1, tq), jnp.int32)
    hi_lo0 = jnp.right_shift(jnp.min(kmin, axis=0, keepdims=True), HALF_BITS)
    hi_hi0 = jnp.right_shift(jnp.max(kmax, axis=0, keepdims=True), HALF_BITS) + 1
    t_hi, c_ge_hi, c_gt_hi = rank_select16(hi_ref, n_sel, few, hi_lo0, hi_hi0, n_valid, zeros, n_valid,
                                           UPPER_STEPS_PER_CHECK)
    exact_hi = c_ge_hi == n_sel

    t_hi16 = jnp.broadcast_to(t_hi.astype(jnp.int16), (BF16_ROWS, tq))

    def fill_low(c, carry):
        ks = pl.multiple_of(c * ck, ck)
        low = (key_ref[pl.ds(ks, ck), :] & (HALF_SPAN - 1)) + I16_MIN
        low = low.astype(jnp.int16).reshape(rows16, BF16_ROWS, tq)
        same = hi_ref[pl.ds(ks, ck), :].reshape(rows16, BF16_ROWS, tq) == t_hi16[None]
        low_ref[pl.ds(ks, ck), :] = jnp.where(same, low, jnp.int16(i16_min)).reshape(ck, tq)
        return carry

    lax.fori_loop(0, nk, fill_low, 0)
    in_bucket = c_ge_hi - c_gt_hi
    t_lo, c_ge_lo, c_gt_lo = rank_select16(low_ref, n_sel - c_gt_hi, few | exact_hi,
                                           jnp.full((1, tq), I16_MIN, jnp.int32), jnp.full((1, tq), -I16_MIN, jnp.int32),
                                           in_bucket, zeros, in_bucket, LOWER_STEPS_PER_CHECK)
    lo = jnp.where(exact_hi, t_hi * HALF_SPAN, t_hi * HALF_SPAN + (t_lo - I16_MIN))
    clo = jnp.where(exact_hi, n_sel, c_gt_hi + c_ge_lo)
    chi = c_gt_hi + c_gt_lo
    thr = jnp.where(few, jnp.int32(INT_MIN + 1), lo)
    thr8 = jnp.broadcast_to(thr, (SUBLANES, tq))
    tied = jnp.logical_not(few) & (clo > n_sel)
    need = n_sel - chi
    n_keys = mask_ref.shape[2]

    def tie_cut(_):
        def count_le(j):
            j8 = jnp.broadcast_to(j, (SUBLANES, tq))

            def body(c, cnt):
                ks = pl.multiple_of(c * ck, ck)
                kk = key_ref[pl.ds(ks, ck), :].reshape(groups, SUBLANES, tq)
                idx = (c * ck + kloc).reshape(groups, SUBLANES, tq)
                hit = jnp.where(kk == thr8[None], jnp.where(idx <= j8[None], 1, 0), 0)
                return cnt + jnp.sum(hit, axis=0)

            cnt = lax.fori_loop(0, nk, body, jnp.zeros((SUBLANES, tq), jnp.int32))
            return jnp.sum(cnt, axis=0, keepdims=True)

        def bisect(_, st):
            jlo, jhi = st
            mid = jlo + jnp.right_shift(jhi - jlo, 1)
            ok = count_le(mid) >= need
            return jnp.where(ok, jlo, mid), jnp.where(ok, mid, jhi)

        _, jhi = lax.fori_loop(0, n_keys.bit_length(), bisect,
                               (jnp.full((1, tq), -1, jnp.int32), jnp.full((1, tq), n_keys - 1, jnp.int32)))
        return jnp.where(tied, jhi, int_max)

    jcut = lax.cond(jnp.max(jnp.where(tied, 1, 0)) > 0, tie_cut, lambda _: jnp.full((1, tq), int_max, jnp.int32), 0)
    jcut8 = jnp.broadcast_to(jcut, (SUBLANES, tq))

    def write_chunk(c, carry):
        ks = pl.multiple_of(c * ck, ck)
        kk = key_ref[pl.ds(ks, ck), :].reshape(groups, SUBLANES, tq)
        idx = (c * ck + kloc).reshape(groups, SUBLANES, tq)
        at_thr = jnp.where(kk == thr8[None], jnp.where(idx <= jcut8[None], 1, 0), 0)
        sel = jnp.where(kk > thr8[None], 1, at_thr)
        mask_ref[0, 0, pl.ds(ks, ck), :] = sel.reshape(ck, tq).astype(mask_ref.dtype)
        return carry

    lax.fori_loop(0, nk, write_chunk, 0)

    def zero_chunk(c, carry):
        ks = pl.multiple_of(c * ck, ck)
        mask_ref[0, 0, pl.ds(ks, ck), :] = jnp.zeros((ck, tq), mask_ref.dtype)
        return carry

    lax.fori_loop(nk, mask_ref.shape[2] // ck, zero_chunk, 0)


def indexer_mask(qi, small, k_even, k_odd, b, t, n_sel):
    tq = _tile(t, 512)
    ck = _tile(t, 512)
    return pl.pallas_call(
        functools.partial(_indexer_body, tq=tq, ck=ck, n_sel=n_sel),
        grid=(b, t // tq),
        in_specs=[
            pl.BlockSpec((1, tq, IDX_HEADS * IDX_DIM), lambda bb, i: (bb, i, 0)),
            pl.BlockSpec((1, tq, LANES), lambda bb, i: (bb, i, 0)),
            pl.BlockSpec((1, t, LANES), lambda bb, i: (bb, 0, 0)),
            pl.BlockSpec((1, t, LANES), lambda bb, i: (bb, 0, 0)),
        ],
        out_specs=pl.BlockSpec((1, 1, t, tq), lambda bb, i: (bb, i, 0, 0)),
        out_shape=jax.ShapeDtypeStruct((b, t // tq, t, tq), jnp.int8),
        scratch_shapes=[pltpu.VMEM((t, tq), jnp.int32), pltpu.VMEM((t, tq), jnp.int16),
                        pltpu.VMEM((t, tq), jnp.int16), pltpu.VMEM((IDX_HEADS, SUBLANES, tq), F32)],
        compiler_params=_params("parallel", "parallel"),
        name="indexer",
    )(qi.reshape(b, t, -1), small, k_even.reshape(b, t, LANES), k_odd.reshape(b, t, LANES))


ATTN_HEAD_GROUP = 4
ATTN_HEAD_GROUP_FIXED = 2


def _attn_body(qi_ref, ki_ref, fix_ref, q_ref, k_ref, vt_ref, mask_ref, bias_ref, z_ref, m0_ref, o_ref, m_ref, acc_ref,
               *, tq):
    qi, ki = qi_ref[pl.program_id(1)], ki_ref[pl.program_id(1)]
    fixed = fix_ref[0] == 1

    @pl.when(ki == 0)
    def _():
        m_ref[...] = jnp.full(m_ref.shape, MASK_NEG, F32)
        acc_ref[...] = jnp.zeros(acc_ref.shape, F32)

    def tile(near, fixed_shift):
        madd = (1.0 - mask_ref[0, 0].astype(F32)) * MASK_NEG
        if fixed_shift:
            madd = madd - m0_ref[0, 0]
        off = jnp.where(ki == qi, 0, 1)
        krow = pl.multiple_of(ki * tq, tq)
        group = ATTN_HEAD_GROUP_FIXED if fixed_shift else ATTN_HEAD_GROUP
        for g in range(0, A_HEADS, group):
            hs = range(g, g + group)
            grp = slice(g, g + group)
            head = lambda ref, h: ref[0, :, h * A_HEAD_DIM:(h + 1) * A_HEAD_DIM]
            k2 = jnp.stack([k_ref[0, pl.ds(krow, tq), h * A_HEAD_DIM:(h + 1) * A_HEAD_DIM] for h in hs])
            q2 = jnp.stack([head(q_ref, h) for h in hs])
            s = lax.dot_general(k2, q2, (((2,), (2,)), ((0,), (0,))), preferred_element_type=F32) + madd[None]
            if near:
                s = s + bias_ref[off, grp].astype(F32)
            vt2 = jnp.stack([vt_ref[0, h * V_ROWS:(h + 1) * V_ROWS, :] for h in hs])
            pv = lambda p: lax.dot_general(vt2, p, (((2,), (1,)), ((0,), (0,))), preferred_element_type=F32)
            if fixed_shift:
                acc_ref[grp] = acc_ref[grp] + pv(jnp.exp2(s).astype(BF16))
            else:
                m_prev = m_ref[grp]
                m_new = jnp.maximum(m_prev, jnp.max(s, axis=1, keepdims=True))
                acc_ref[grp] = jnp.exp2(m_prev - m_new) * acc_ref[grp] + pv(jnp.exp2(s - m_new).astype(BF16))
                m_ref[grp] = m_new

    far = ki + 1 < qi
    for near in (False, True):
        for fixed_shift in (False, True):
            cond = jnp.logical_and(jnp.logical_not(far) if near else far,
                                   fixed if fixed_shift else jnp.logical_not(fixed))
            pl.when(cond)(functools.partial(tile, near, fixed_shift))

    @pl.when(ki == qi)
    def _():
        outs = []
        for h in range(A_HEADS):
            a = acc_ref[h]
            outs.append((a[:A_HEAD_DIM] / a[A_HEAD_DIM:A_HEAD_DIM + 1]).T)
        o_ref[0] = (jnp.concatenate(outs, axis=-1) * _silu(z_ref[0].astype(F32))).astype(o_ref.dtype)


MAX_LOGIT_SPAN = 100.0


def softmax_shift(q_gain, k_gain, rel_bias):
    slack = 1.02
    qk = (A_HEAD_DIM ** 0.5) * LOG2E * jnp.max(jnp.abs(q_gain)) * jnp.max(jnp.abs(k_gain)) * slack
    b2 = (rel_bias - rel_bias[REL_BUCKETS - 1]) * (LOG2E * slack)
    shift = qk + jnp.max(b2)
    span = 2.0 * qk + jnp.max(b2) - jnp.min(b2)
    return shift.astype(F32), (span <= MAX_LOGIT_SPAN).astype(jnp.int32)


def attention(q, k, vt, mask, bias, proj3d, shift, use_shift, b, t, tq):
    nq = t // tq
    pairs = [(i, j) for i in range(nq) for j in range(i + 1)]
    qi_tab = jnp.asarray([p[0] for p in pairs], jnp.int32)
    ki_tab = jnp.asarray([p[1] for p in pairs], jnp.int32)
    grid_spec = pltpu.PrefetchScalarGridSpec(
        num_scalar_prefetch=3,
        grid=(b, len(pairs)),
        in_specs=[
            pl.BlockSpec((1, tq, A_WIDTH), lambda bb, p, qi, ki, fx: (bb, qi[p], 0)),
            pl.BlockSpec((1, t, A_WIDTH), lambda bb, p, qi, ki, fx: (bb, 0, 0)),
            pl.BlockSpec((1, A_HEADS * V_ROWS, tq), lambda bb, p, qi, ki, fx: (bb * nq + ki[p], 0, 0)),
            pl.BlockSpec((1, 1, tq, tq), lambda bb, p, qi, ki, fx: (bb, qi[p], ki[p], 0)),
            pl.BlockSpec(bias.shape, lambda bb, p, qi, ki, fx: (0, 0, 0, 0)),
            pl.BlockSpec((1, tq, A_WIDTH), lambda bb, p, qi, ki, fx: (bb, qi[p], COL_Z_A // A_WIDTH)),
            pl.BlockSpec(memory_space=pltpu.SMEM),
        ],
        out_specs=pl.BlockSpec((1, tq, A_WIDTH), lambda bb, p, qi, ki, fx: (bb, qi[p], 0)),
        scratch_shapes=[pltpu.VMEM((A_HEADS, 1, tq), F32), pltpu.VMEM((A_HEADS, V_ROWS, tq), F32)],
    )
    return pl.pallas_call(
        functools.partial(_attn_body, tq=tq),
        grid_spec=grid_spec,
        out_shape=jax.ShapeDtypeStruct((b, t, A_WIDTH), BF16),
        compiler_params=_params("parallel", "arbitrary"),
        name="dsa_attention",
    )(qi_tab, ki_tab, use_shift.reshape(1), q.reshape(b, t, -1), k.reshape(b, t, -1), vt, mask, bias, proj3d,
      shift.reshape(1, 1))


def _gdn_prep_body(x_ref, halo_ref, sm_ref, cw_ref, alog_ref, dtb_ref,
                   q_ref, k_ref, v_ref, gcb_ref, bb_ref, grow_ref, *, tm):
    i = pl.program_id(1)
    x = x_ref[0]
    hal = halo_ref[0]
    hal = jnp.where(i > 0, hal, jnp.zeros_like(hal))
    rt = lax.broadcasted_iota(jnp.int32, (tm, tm), 0)
    ct = lax.broadcasted_iota(jnp.int32, (tm, tm), 1)
    rh = lax.broadcasted_iota(jnp.int32, (SUBLANES, BF16_ROWS), 0)
    ch = lax.broadcasted_iota(jnp.int32, (SUBLANES, BF16_ROWS), 1)
    y = cw_ref[B_CONV - 1:B_CONV, :] * x.astype(F32)
    for j in range(B_CONV - 1):
        back = B_CONV - 1 - j
        main = _dot(jnp.where(ct == rt - back, 1.0, 0.0).astype(BF16), x)
        head = _dot(jnp.where(ch == rh + (BF16_ROWS - back), 1.0, 0.0).astype(BF16), hal)
        y = y + cw_ref[j:j + 1, :] * jnp.concatenate([main[:SUBLANES] + head, main[SUBLANES:]], axis=0)
    y = _silu(y)
    for h in range(B_HEADS):
        sl = slice(h * LANES, (h + 1) * LANES)
        qh = y[:, sl]
        q_ref[0, :, sl] = qh * lax.rsqrt(jnp.sum(qh * qh, axis=-1, keepdims=True) + EPS) * (B_HEAD_DIM ** -0.5)
        kh = y[:, B_WIDTH + h * LANES:B_WIDTH + (h + 1) * LANES]
        k_ref[0, :, sl] = kh * lax.rsqrt(jnp.sum(kh * kh, axis=-1, keepdims=True) + EPS)
    v_ref[0] = y[:, 2 * B_WIDTH:]

    sm = sm_ref[0]
    xg = sm + dtb_ref[...]
    softplus = jnp.maximum(xg, 0.0) + jnp.log(1.0 + jnp.exp(-jnp.abs(xg)))
    g = -jnp.exp(alog_ref[...]) * softplus
    r = lax.broadcasted_iota(jnp.int32, (tm, tm), 0)
    c = lax.broadcasted_iota(jnp.int32, (tm, tm), 1)
    sh = int(math.log2(GDN_CHUNK))
    same_chunk = jnp.right_shift(r, sh) == jnp.right_shift(c, sh)
    tri = jnp.where(jnp.logical_and(same_chunk, c <= r), 1.0, 0.0).astype(F32)
    gc = lax.dot_general(tri, g, (((1,), (0,)), ((), ())), precision=HIGHEST, preferred_element_type=F32)
    beta = _sigmoid(sm)
    for h in range(B_HEADS):
        sl = slice(h * LANES, (h + 1) * LANES)
        gcb_ref[0, :, sl] = jnp.broadcast_to(gc[:, SM_ALPHA + h:SM_ALPHA + h + 1], (tm, LANES))
        bb_ref[0, :, sl] = jnp.broadcast_to(beta[:, SM_BETA + h:SM_BETA + h + 1], (tm, LANES))
    gct = gc.T
    for cc in range(tm // GDN_CHUNK):
        grow_ref[0, cc] = gct[SM_ALPHA:SM_ALPHA + B_HEADS, cc * GDN_CHUNK:(cc + 1) * GDN_CHUNK]


def gdn_prep(proj3d, small3d, conv_w, a_log, dt_bias, b, t):
    tm = _tile(t, 256)
    pad_row = lambda v: jnp.zeros((1, LANES), F32).at[0, SM_ALPHA:SM_ALPHA + B_HEADS].set(v)
    nc = tm // GDN_CHUNK
    act = jax.ShapeDtypeStruct((b, t, B_WIDTH), F32)
    full = lambda a: pl.BlockSpec(a.shape, lambda bb, i: (0,) * a.ndim)
    consts = [conv_w, pad_row(a_log), pad_row(dt_bias)]
    return pl.pallas_call(
        functools.partial(_gdn_prep_body, tm=tm),
        grid=(b, t // tm),
        in_specs=[
            pl.BlockSpec((1, tm, 3 * B_WIDTH), lambda bb, i: (bb, i, COL_QKV_B // (3 * B_WIDTH))),
            pl.BlockSpec((1, BF16_ROWS, 3 * B_WIDTH),
                         lambda bb, i: (bb, jnp.maximum(i * (tm // BF16_ROWS) - 1, 0), 0)),
            pl.BlockSpec((1, tm, LANES), lambda bb, i: (bb, i, 0)),
        ] + [full(c) for c in consts],
        out_specs=[pl.BlockSpec((1, tm, B_WIDTH), lambda bb, i: (bb, i, 0))] * 5
        + [pl.BlockSpec((1, nc, B_HEADS, GDN_CHUNK), lambda bb, i: (bb, i, 0, 0))],
        out_shape=[act] * 5 + [jax.ShapeDtypeStruct((b, t // GDN_CHUNK, B_HEADS, GDN_CHUNK), F32)],
        compiler_params=_params("parallel", "parallel"),
        name="gdn_prep",
    )(proj3d, proj3d, small3d, *consts)


def _gdn_body(q_ref, k_ref, v_ref, gcb_ref, bb_ref, grow_ref, z_ref, on_ref, o_ref, s_ref, *, nc):
    @pl.when(pl.program_id(1) == 0)
    def _():
        s_ref[...] = jnp.zeros(s_ref.shape, F32)

    cs, nh = GDN_CHUNK, B_HEADS
    nb = nc * nh
    ri = lax.broadcasted_iota(jnp.int32, (nb, cs, cs), 1)
    ci = lax.broadcasted_iota(jnp.int32, (nb, cs, cs), 2)
    lower = ci <= ri
    strict = ci < ri
    eye = jnp.where(ci == ri, 1.0, 0.0).astype(F32)
    bf = lambda a: a.astype(BF16)
    bmm = lambda a, b: lax.dot_general(a, b, (((2,), (1,)), ((0,), (0,))), preferred_element_type=F32)
    bmm_nt = lambda a, b: lax.dot_general(a, b, (((2,), (2,)), ((0,), (0,))), preferred_element_type=F32)
    bmm_tn = lambda a, b: lax.dot_general(a, b, (((1,), (1,)), ((0,), (0,))), preferred_element_type=F32)

    def stack(ref):
        return jnp.stack([ref[0, c * cs:(c + 1) * cs, h * LANES:(h + 1) * LANES]
                          for c in range(nc) for h in range(nh)])

    q, k, v = stack(q_ref), stack(k_ref), stack(v_ref)
    gcb = stack(gcb_ref)
    beta = stack(bb_ref)
    grow = jnp.stack([grow_ref[0, c, h:h + 1, :] for c in range(nc) for h in range(nh)])
    diff = gcb[:, :, :cs] - grow
    decay = jnp.where(lower, jnp.exp(jnp.where(lower, diff, 0.0)), 0.0)
    eg = jnp.exp(gcb)
    glast = gcb[:, cs - 1:cs, :]
    kb = k * beta
    lmat = jnp.where(strict, bmm_nt(bf(kb), bf(k)) * decay, 0.0)
    n = -lmat
    tinv = eye + n
    for _ in range(int(math.log2(cs)) - 1):
        n = bmm(bf(n), bf(n))
        tinv = tinv + bmm(bf(tinv), bf(n))
    uw = bmm(bf(tinv), bf(jnp.concatenate([v * beta, kb * eg], axis=-1)))
    attn = bf(jnp.where(lower, bmm_nt(bf(q), bf(k)) * decay, 0.0))
    qg = bf(q * eg)
    kdec = bf(k * jnp.exp(glast - gcb))
    egl = jnp.exp(glast)

    s = s_ref[...]
    for c in range(nc):
        sl = slice(c * nh, (c + 1) * nh)
        sb = bf(s)
        v_new = uw[sl, :, :LANES] - bmm(bf(uw[sl, :, LANES:]), sb)
        o = bmm(qg[sl], sb) + bmm(attn[sl], bf(v_new))
        s = s * egl[sl] + bmm_tn(kdec[sl], bf(v_new))
        o = o * lax.rsqrt(jnp.mean(o * o, axis=-1, keepdims=True) + EPS) * on_ref[...]
        for h in range(nh):
            rows, cols = slice(c * cs, (c + 1) * cs), slice(h * LANES, (h + 1) * LANES)
            o_ref[0, rows, cols] = (o[h] * _silu(z_ref[0, rows, cols].astype(F32))).astype(o_ref.dtype)
    s_ref[...] = s


def gdn_scan(qh, kh, v, gcb, bb, grow, proj3d, o_norm, b, t):
    tt = _tile(t, 256)
    nc = tt // GDN_CHUNK
    blk = pl.BlockSpec((1, tt, B_WIDTH), lambda bb_, i: (bb_, i, 0))
    return pl.pallas_call(
        functools.partial(_gdn_body, nc=nc),
        grid=(b, t // tt),
        in_specs=[blk] * 5 + [
            pl.BlockSpec((1, nc, B_HEADS, GDN_CHUNK), lambda bb_, i: (bb_, i, 0, 0)),
            pl.BlockSpec((1, tt, B_WIDTH), lambda bb_, i: (bb_, i, COL_Z_B // B_WIDTH)),
            pl.BlockSpec((1, LANES), lambda bb_, i: (0, 0)),
        ],
        out_specs=blk,
        out_shape=jax.ShapeDtypeStruct((b, t, B_WIDTH), BF16),
        scratch_shapes=[pltpu.VMEM((B_HEADS, B_HEAD_DIM, B_HEAD_DIM), F32)],
        compiler_params=_params("parallel", "arbitrary"),
        name="gdn_scan",
    )(qh, kh, v, gcb, bb, grow, proj3d, o_norm.reshape(1, LANES))


C_HALO = 32
D_HALO = BF16_ROWS


def _cd_layer_body(a_ref, g_ref, ah_ref, gh_ref, zc_ref, bg_ref, cg_ref, ud_ref, cgh_ref, udh_ref, zd_ref,
                   x_ref, wo_ref, dww_ref, dwb_ref, lnw_ref, lnb_ref, dcw_ref, o_ref, us_ref, ds_ref, y_ref,
                   *, tm, tiles_per_seq):
    s = pl.program_id(0)
    first = lax.rem(s, tiles_per_seq) == 0
    slot = lax.rem(s, 2)

    @pl.when(s == 0)
    def _():
        y_ref[...] = jnp.zeros(y_ref.shape, y_ref.dtype)

    o_ref[...] = x_ref[...] + _dot(y_ref[1 - slot], wo_ref[...])

    f32 = lambda r: r[...].astype(F32)
    uh = f32(ah_ref) * _sigmoid(f32(gh_ref))
    us_ref[0, 0:C_HALO, :] = jnp.where(first, jnp.zeros_like(uh), uh)
    us_ref[0, C_HALO:, :] = f32(a_ref) * _sigmoid(f32(g_ref))
    span = tm + C_HALO - SUBLANES
    for r in range(1, SUBLANES):
        us_ref[r, 0:span, :] = us_ref[0, pl.ds(r, span), :]
    u = jnp.zeros((tm, C_WIDTH), F32)
    for j in range(C_CONV):
        off = C_HALO - (C_CONV - 1) + j
        r, base = off % SUBLANES, off - off % SUBLANES
        u = u + dww_ref[j:j + 1, :] * us_ref[r, base:base + tm, :]
    u = u + dwb_ref[...]
    mu = jnp.mean(u, axis=-1, keepdims=True)
    uc = u - mu
    u = uc * lax.rsqrt(jnp.mean(uc * uc, axis=-1, keepdims=True) + EPS) * lnw_ref[...] + lnb_ref[...]
    y_ref[slot, :, 0:C_WIDTH] = (_silu(u) * _silu(f32(zc_ref))).astype(y_ref.dtype)

    dh = f32(cgh_ref) * f32(udh_ref)
    ds_ref[0:D_HALO, :] = jnp.where(first, jnp.zeros_like(dh), dh)
    ds_ref[D_HALO:, :] = f32(cg_ref) * f32(ud_ref)
    d = jnp.zeros((tm, D_WIDTH), F32)
    for j in range(D_CONV):
        d = d + dcw_ref[j:j + 1, :] * ds_ref[pl.ds(D_HALO - (D_CONV - 1) + j, tm), :]
    y_ref[slot, :, C_WIDTH:] = (f32(bg_ref) * d * _silu(f32(zd_ref))).astype(y_ref.dtype)


def cd_layer(proj, x2d, w_out_bf16, dw_w, dw_b, ln_w, ln_b, d_conv_w, t):
    m, n = x2d.shape
    tm = _tile(t, 256)
    w = C_WIDTH
    last = m // tm - 1
    cur = lambda s: jnp.minimum(s, last)
    col = lambda c: pl.BlockSpec((tm, w), lambda s, c=c: (cur(s), c))
    halo = lambda c, rows: pl.BlockSpec(
        (rows, w), lambda s, c=c, rows=rows: (jnp.maximum(cur(s) * (tm // rows) - 1, 0), c))
    prev = pl.BlockSpec((tm, n), lambda s: (jnp.maximum(s - 1, 0), 0))
    row = lambda a: a.reshape(1, -1)
    full = lambda a: pl.BlockSpec(a.shape, lambda s: (0,) * a.ndim)
    consts = [w_out_bf16, dw_w, row(dw_b), row(ln_w), row(ln_b), d_conv_w]
    return pl.pallas_call(
        functools.partial(_cd_layer_body, tm=tm, tiles_per_seq=t // tm),
        grid=(m // tm + 1,),
        in_specs=[col(0), col(1), halo(0, C_HALO), halo(1, C_HALO), col(2), col(3), col(4), col(5),
                  halo(4, D_HALO), halo(5, D_HALO), col(6), prev] + [full(c) for c in consts],
        out_specs=prev,
        out_shape=jax.ShapeDtypeStruct((m, n), F32),
        scratch_shapes=[pltpu.VMEM((SUBLANES, tm + C_HALO, w), F32), pltpu.VMEM((tm + D_HALO, w), F32),
                        pltpu.VMEM((2, tm, C_WIDTH + D_WIDTH), BF16)],
        compiler_params=_params("arbitrary"),
        name="cd_layer",
    )(*([proj] * 11), x2d, *consts)


def _reorder_ab_w_in(w):
    offs = [0]
    for s in AB_SPLITS:
        offs.append(offs[-1] + s)
    part = lambda n: w[:, offs[n]:offs[n + 1]]
    c_q, c_kv, k_idx, w_idx, z_a, qkv_b, beta_b, alpha_b, z_b = (part(n) for n in range(9))
    small = jnp.concatenate([k_idx, w_idx, beta_b, alpha_b], axis=1)
    small = jnp.pad(small, ((0, 0), (0, LANES - small.shape[1])))
    out = jnp.concatenate([qkv_b, z_a, z_b, c_q, c_kv], axis=1)
    return jnp.pad(out, ((0, 0), (0, AB_IN_PAD - out.shape[1]))).astype(BF16), small.astype(BF16)


def _ab_layer(x2d, b, t, norm_w, rel_bias, w_in, q_norm, w_uq, w_iq, kv_norm, w_uk, w_uv, q_gain, k_gain,
              ik_w, ik_b, conv_w, a_log, dt_bias, o_norm, w_out):
    proj, small = norm_matmul(x2d, norm_w, *_reorder_ab_w_in(w_in))
    proj3d, small3d = proj.reshape(b, t, AB_IN_PAD), small.reshape(b, t, LANES)
    tq = _tile(t, 512)
    q, k, v, qi, k_even, k_odd = dsa_prep(proj, small, q_norm, kv_norm, w_uq, w_iq, w_uk, w_uv, q_gain, k_gain,
                                          ik_w, ik_b, tm=tq)
    n_sel = min(TOPK_MAX, t // 4)
    mask = indexer_mask(qi, small3d, k_even, k_odd, b, t, n_sel)
    shift, use_shift = softmax_shift(q_gain, k_gain, rel_bias)
    y_a = attention(q, k, v, mask, bias_tiles(rel_bias, tq), proj3d, shift, use_shift, b, t, tq)
    qh, kh, vv, gcb, bb, grow = gdn_prep(proj3d, small3d, conv_w, a_log, dt_bias, b, t)
    y_b = gdn_scan(qh, kh, vv, gcb, bb, grow, proj3d, o_norm, b, t)
    wo = w_out.astype(BF16)
    return out_proj(x2d, y_a.reshape(b * t, -1), y_b.reshape(b * t, -1), wo[:A_WIDTH], wo[A_WIDTH:])


def _cd_layer(x2d, b, t, norm_w, w_in, dw_w, dw_b, ln_w, ln_b, d_conv_w, w_out):
    proj = norm_matmul(x2d, norm_w, w_in.astype(BF16))
    return cd_layer(proj, x2d, w_out.astype(BF16), dw_w, dw_b, ln_w, ln_b, d_conv_w, t)


def kernel(x, norm_w, rel_bias, ab_w_in, a_q_norm, a_w_uq, a_w_iq, a_kv_norm, a_w_uk, a_w_uv, a_q_gain,
           a_k_gain, a_ik_norm_w, a_ik_norm_b, b_conv_w, b_a_log, b_dt_bias, b_o_norm, ab_w_out, cd_w_in,
           c_dw_w, c_dw_b, c_ln_w, c_ln_b, d_conv_w, cd_w_out):
    b, t, d = x.shape
    depth = norm_w.shape[0]
    x2d = x.reshape(b * t, d)
    for i in range(depth):
        j = i // 2
        if i % 2 == 0:
            x2d = _ab_layer(x2d, b, t, norm_w[i], rel_bias, ab_w_in[j], a_q_norm[j], a_w_uq[j], a_w_iq[j],
                            a_kv_norm[j], a_w_uk[j], a_w_uv[j], a_q_gain[j], a_k_gain[j], a_ik_norm_w[j],
                            a_ik_norm_b[j], b_conv_w[j], b_a_log[j], b_dt_bias[j], b_o_norm[j], ab_w_out[j])
        else:
            x2d = _cd_layer(x2d, b, t, norm_w[i], cd_w_in[j], c_dw_w[j], c_dw_b[j], c_ln_w[j], c_ln_b[j],
                            d_conv_w[j], cd_w_out[j])
    return x2d.reshape(b, t, d)
```

```python
import functools
import math

import jax
import jax.numpy as jnp
from jax import lax
from jax.experimental import pallas as pl
from jax.experimental.pallas import tpu as pltpu

A_HEADS = 8
A_HEAD_DIM = 128
A_WIDTH = A_HEADS * A_HEAD_DIM
A_Q_LORA = 512
A_KV_LORA = 256
IDX_HEADS = 16
IDX_DIM = 64
TOPK_MAX = 256
REL_BUCKETS = 32
REL_MAX_DIST = 128
B_HEADS = 8
B_HEAD_DIM = 128
B_WIDTH = B_HEADS * B_HEAD_DIM
B_CONV = 4
GDN_CHUNK = 64
C_WIDTH = 1024
C_CONV = 31
D_WIDTH = 1024
D_CONV = 3
EPS = 1e-6

AB_SPLITS = (A_Q_LORA, A_KV_LORA, IDX_DIM, IDX_HEADS, A_WIDTH, 3 * B_WIDTH, B_HEADS, B_HEADS, B_WIDTH)
AB_IN_PAD = 6144
SM_KIDX = 0
SM_WIDX = 64
SM_BETA = 80
SM_ALPHA = 88
COL_QKV_B = 0
COL_Z_A = 3072
COL_Z_B = 4096
COL_C_Q = 5120
COL_C_KV = 5632

LANES = 128
MASK_NEG = -1e30
INT_MIN = -(2 ** 31)
LOG2E = math.log2(math.e)
BF16_ROWS = 16
V_ROWS = A_HEAD_DIM + BF16_ROWS
VMEM_LIMIT = 56 * 1024 * 1024

F32 = jnp.float32
BF16 = jnp.bfloat16
HIGHEST = lax.Precision.HIGHEST


def _tile(n, pref):
    t = min(n, pref)
    assert n % t == 0, (n, t)
    return t


def _params(*sem):
    return pltpu.CompilerParams(dimension_semantics=sem, vmem_limit_bytes=VMEM_LIMIT)


def _dot(a, b):
    return jnp.dot(a, b, preferred_element_type=F32)


def _dot_nt(a, b):
    return lax.dot_general(a, b, (((1,), (1,)), ((), ())), preferred_element_type=F32)


def _silu(x):
    return x * (1.0 / (1.0 + jnp.exp(-x)))


def _sigmoid(x):
    return 1.0 / (1.0 + jnp.exp(-x))


def _norm_matmul_body(*refs, with_f32_cols):
    if with_f32_cols:
        x_ref, nw_ref, w_ref, ws_ref, o_ref, os_ref, h_ref = refs
    else:
        x_ref, nw_ref, w_ref, o_ref, h_ref = refs

    @pl.when(pl.program_id(1) == 0)
    def _():
        x = x_ref[...]
        ms = jnp.mean(x * x, axis=-1, keepdims=True)
        h_ref[...] = (x * lax.rsqrt(ms + EPS) * nw_ref[...]).astype(h_ref.dtype)
        if with_f32_cols:
            os_ref[...] = _dot(h_ref[...], ws_ref[...])

    o_ref[...] = _dot(h_ref[...], w_ref[...]).astype(o_ref.dtype)


def norm_matmul(x2d, norm_w, w_bf16, w_f32_cols=None):
    m, k = x2d.shape
    n = w_bf16.shape[1]
    tm = _tile(m, 1024)
    tn = max(d for d in range(256, min(n, 2048) + 1, 256) if n % d == 0)
    extra = w_f32_cols is not None
    in_specs = [
        pl.BlockSpec((tm, k), lambda i, j: (i, 0)),
        pl.BlockSpec((1, k), lambda i, j: (0, 0)),
        pl.BlockSpec((k, tn), lambda i, j: (0, j)),
    ]
    out_specs = [pl.BlockSpec((tm, tn), lambda i, j: (i, j))]
    out_shape = [jax.ShapeDtypeStruct((m, n), BF16)]
    args = [x2d, norm_w.reshape(1, k), w_bf16]
    if extra:
        in_specs.append(pl.BlockSpec((k, LANES), lambda i, j: (0, 0)))
        out_specs.append(pl.BlockSpec((tm, LANES), lambda i, j: (i, 0)))
        out_shape.append(jax.ShapeDtypeStruct((m, LANES), F32))
        args.append(w_f32_cols)
    res = pl.pallas_call(
        functools.partial(_norm_matmul_body, with_f32_cols=extra),
        grid=(m // tm, n // tn),
        in_specs=in_specs,
        out_specs=out_specs,
        out_shape=out_shape,
        scratch_shapes=[pltpu.VMEM((tm, k), BF16)],
        compiler_params=_params("parallel", "arbitrary"),
        name="norm_matmul",
    )(*args)
    return res if extra else res[0]


def _out_proj_body(x_ref, ya_ref, yb_ref, wa_ref, wb_ref, o_ref):
    o_ref[...] = x_ref[...] + _dot(ya_ref[...], wa_ref[...]) + _dot(yb_ref[...], wb_ref[...])


def out_proj(x2d, ya, yb, wa_bf16, wb_bf16):
    m, n = x2d.shape
    ka, kb = ya.shape[1], yb.shape[1]
    tm = _tile(m, 512)
    return pl.pallas_call(
        _out_proj_body,
        grid=(m // tm,),
        in_specs=[
            pl.BlockSpec((tm, n), lambda i: (i, 0)),
            pl.BlockSpec((tm, ka), lambda i: (i, 0)),
            pl.BlockSpec((tm, kb), lambda i: (i, 0)),
            pl.BlockSpec((ka, n), lambda i: (0, 0)),
            pl.BlockSpec((kb, n), lambda i: (0, 0)),
        ],
        out_specs=pl.BlockSpec((tm, n), lambda i: (i, 0)),
        out_shape=jax.ShapeDtypeStruct((m, n), F32),
        compiler_params=_params("parallel"),
        name="out_proj",
    )(x2d, ya, yb, wa_bf16, wb_bf16)


def _head_rmsnorm(y, gain_row, scale):
    outs = []
    for h in range(y.shape[1] // LANES):
        seg = y[:, h * LANES:(h + 1) * LANES]
        ms = jnp.mean(seg * seg, axis=-1, keepdims=True)
        outs.append(seg * lax.rsqrt(ms + EPS) * (gain_row * scale))
    return jnp.concatenate(outs, axis=-1)


def _dsa_prep_body(cq_ref, ckv_ref, sm_ref, qn_ref, kvn_ref, wuq_ref, wiq_ref, wuk_ref, wuv_ref,
                   qg_ref, kg_ref, ikw_ref, ikb_ref,
                   q_ref, k_ref, v_ref, qi_ref, ke_ref, ko_ref):
    cq = cq_ref[...].astype(F32)
    cq = cq * lax.rsqrt(jnp.mean(cq * cq, axis=-1, keepdims=True) + EPS) * qn_ref[...]
    cqb = cq.astype(BF16)
    q = _dot(cqb, wuq_ref[...])
    q_ref[...] = _head_rmsnorm(q, qg_ref[...], A_HEAD_DIM ** -0.5 * LOG2E).astype(q_ref.dtype)
    qi_ref[...] = _dot(cqb, wiq_ref[...]).astype(qi_ref.dtype)

    ckv = ckv_ref[...].astype(F32)
    ckv = ckv * lax.rsqrt(jnp.mean(ckv * ckv, axis=-1, keepdims=True) + EPS) * kvn_ref[...]
    ckvb = ckv.astype(BF16)
    k = _dot(ckvb, wuk_ref[...])
    k_ref[...] = _head_rmsnorm(k, kg_ref[...], 1.0).astype(k_ref.dtype)
    vt = _dot_nt(wuv_ref[...], ckvb).astype(v_ref.dtype)
    ones = jnp.ones((BF16_ROWS, vt.shape[1]), v_ref.dtype)
    for h in range(A_HEADS):
        v_ref[0, h * V_ROWS:h * V_ROWS + A_HEAD_DIM, :] = vt[h * A_HEAD_DIM:(h + 1) * A_HEAD_DIM, :]
        v_ref[0, h * V_ROWS + A_HEAD_DIM:(h + 1) * V_ROWS, :] = ones

    kx = sm_ref[...][:, SM_KIDX:SM_KIDX + IDX_DIM]
    mu = jnp.mean(kx, axis=-1, keepdims=True)
    kc = kx - mu
    kl = kc * lax.rsqrt(jnp.mean(kc * kc, axis=-1, keepdims=True) + EPS) * ikw_ref[...] + ikb_ref[...]
    z = jnp.zeros_like(kl)
    ke_ref[...] = jnp.concatenate([kl, z], axis=-1).astype(ke_ref.dtype)
    ko_ref[...] = jnp.concatenate([z, kl], axis=-1).astype(ko_ref.dtype)


def dsa_prep(proj, small, q_norm, kv_norm, w_uq, w_iq, w_uk, w_uv, q_gain, k_gain, ik_w, ik_b, tm=None):
    m = proj.shape[0]
    tm = _tile(m, 512) if tm is None else tm
    row = lambda a: a.reshape(1, -1)
    full = lambda a: pl.BlockSpec(a.shape, lambda i: (0,) * a.ndim)
    consts = [row(q_norm), row(kv_norm), w_uq.astype(BF16), w_iq.astype(BF16), w_uk.astype(BF16),
              w_uv.T.astype(BF16), row(q_gain), row(k_gain), row(ik_w), row(ik_b)]
    outs = [jax.ShapeDtypeStruct((m, A_WIDTH), BF16)] * 2 + [
        jax.ShapeDtypeStruct((m // tm, A_HEADS * V_ROWS, tm), BF16),
        jax.ShapeDtypeStruct((m, IDX_HEADS * IDX_DIM), BF16),
        jax.ShapeDtypeStruct((m, LANES), BF16), jax.ShapeDtypeStruct((m, LANES), BF16)]
    out_specs = [pl.BlockSpec((tm, s.shape[-1]), lambda i: (i, 0)) for s in outs]
    out_specs[2] = pl.BlockSpec((1, A_HEADS * V_ROWS, tm), lambda i: (i, 0, 0))
    return pl.pallas_call(
        _dsa_prep_body,
        grid=(m // tm,),
        in_specs=[
            pl.BlockSpec((tm, A_Q_LORA), lambda i: (i, COL_C_Q // A_Q_LORA)),
            pl.BlockSpec((tm, A_KV_LORA), lambda i: (i, COL_C_KV // A_KV_LORA)),
            pl.BlockSpec((tm, LANES), lambda i: (i, 0)),
        ] + [full(c) for c in consts],
        out_specs=out_specs,
        out_shape=outs,
        compiler_params=_params("parallel"),
        name="dsa_prep",
    )(proj, proj, small, *consts)


def _t5_bucket(dist):
    max_exact = REL_BUCKETS // 2
    large = max_exact + (jnp.log(jnp.maximum(dist, 1).astype(F32) / max_exact)
                         / math.log(REL_MAX_DIST / max_exact) * (REL_BUCKETS - max_exact)).astype(jnp.int32)
    large = jnp.minimum(large, REL_BUCKETS - 1)
    return jnp.where(dist < max_exact, dist, large)


def _bias_tiles_body(rb_ref, o_ref, *, t):
    h = pl.program_id(0)
    sb = min(t, LANES)
    row = lax.broadcasted_iota(jnp.int32, (sb, sb), 0)
    col = lax.broadcasted_iota(jnp.int32, (sb, sb), 1)
    for typ in range(2):
        for rk in range(t // sb):
            for cq in range(t // sb):
                base = typ * t + (cq - rk) * sb
                blk = (typ, 0, slice(rk * sb, (rk + 1) * sb), slice(cq * sb, (cq + 1) * sb))
                if base - (sb - 1) >= REL_MAX_DIST or base + (sb - 1) < 0:
                    o_ref[blk] = jnp.zeros((sb, sb), o_ref.dtype)
                    continue
                bucket = _t5_bucket(jnp.maximum(base + col - row, 0))
                acc = jnp.zeros((sb, sb), F32)
                for b in range(REL_BUCKETS):
                    acc = jnp.where(bucket == b, rb_ref[b, h], acc)
                o_ref[blk] = ((acc - rb_ref[REL_BUCKETS - 1, h]) * LOG2E).astype(o_ref.dtype)


def bias_tiles(rel_bias, t):
    return pl.pallas_call(
        functools.partial(_bias_tiles_body, t=t),
        grid=(A_HEADS,),
        in_specs=[pl.BlockSpec(memory_space=pltpu.SMEM)],
        out_specs=pl.BlockSpec((2, 1, t, t), lambda h: (0, h, 0, 0)),
        out_shape=jax.ShapeDtypeStruct((2, A_HEADS, t, t), BF16),
        compiler_params=_params("parallel"),
        name="bias_tiles",
    )(rel_bias)


SUBLANES = 8
MAX_SELECT_STEPS = 24
UPPER_STEPS_PER_CHECK = 4
LOWER_STEPS_PER_CHECK = 2
HALF_BITS = 16
HALF_SPAN = 2 ** HALF_BITS
I16_MIN = -(2 ** (HALF_BITS - 1))


def _key_of(x):
    bits = pltpu.bitcast(x, jnp.int32)
    key = jnp.where(bits < 0, bits ^ jnp.int32(0x7FFFFFFF), bits)
    return jnp.where(bits == jnp.int32(INT_MIN), 0, key)


def _indexer_body(qi_ref, sm_ref, ke_ref, ko_ref, mask_ref, key_ref, hi_ref, low_ref, w_ref, *, tq, ck, n_sel):
    i = pl.program_id(1)
    nk = ((i + 1) * tq + ck - 1) // ck
    groups = ck // SUBLANES
    scale = (IDX_HEADS ** -0.5) * (IDX_DIM ** -0.5)
    smt = sm_ref[0].T
    for h in range(IDX_HEADS):
        w_ref[h] = jnp.broadcast_to(smt[SM_WIDX + h:SM_WIDX + h + 1, :] * scale, (SUBLANES, tq))
    kloc = lax.broadcasted_iota(jnp.int32, (ck, tq), 0)
    qpos = i * tq + lax.broadcasted_iota(jnp.int32, (ck, tq), 1)
    int_max = jnp.int32(2 ** 31 - 1)

    def score_chunk(c, carry):
        kmin, kmax = carry
        ks = pl.multiple_of(c * ck, ck)
        ke = ke_ref[0, pl.ds(ks, ck), :]
        ko = ko_ref[0, pl.ds(ks, ck), :]
        acc = jnp.zeros((ck, tq), F32)
        for j in range(IDX_HEADS // 2):
            qp = qi_ref[0, :, j * LANES:(j + 1) * LANES]
            for par, kk in ((0, ke), (1, ko)):
                s = jnp.maximum(_dot_nt(kk, qp), 0.0)
                acc = acc + s * jnp.tile(w_ref[2 * j + par], (groups, 1))
        key = _key_of(acc)
        valid = c * ck + kloc <= qpos
        masked = jnp.where(valid, key, jnp.int32(INT_MIN))
        key_ref[pl.ds(ks, ck), :] = masked
        hi_ref[pl.ds(ks, ck), :] = jnp.right_shift(masked, HALF_BITS).astype(jnp.int16)
        kmin = jnp.minimum(kmin, jnp.min(jnp.where(valid, key, int_max).reshape(groups, SUBLANES, tq), axis=0))
        kmax = jnp.maximum(kmax, jnp.max(masked.reshape(groups, SUBLANES, tq), axis=0))
        return kmin, kmax

    kmin, kmax = lax.fori_loop(0, nk, score_chunk, (jnp.full((SUBLANES, tq), int_max, jnp.int32),
                                                    jnp.full((SUBLANES, tq), INT_MIN, jnp.int32)))

    rows16 = ck // BF16_ROWS
    i16_min = I16_MIN

    def scan16(a_ref, init, fn):
        def body(c, acc):
            ks = pl.multiple_of(c * ck, ck)
            return fn(acc, a_ref[pl.ds(ks, ck), :].reshape(rows16, BF16_ROWS, tq))

        return lax.fori_loop(0, nk, body, init)

    def count_ge16(a_ref, p):
        p16 = jnp.broadcast_to(p.astype(jnp.int16), (BF16_ROWS, tq))

        def fn(cnt, a):
            ind = jnp.where(a >= p16[None], jnp.int16(1), jnp.int16(0))
            for g in range(rows16):
                cnt = cnt + ind[g]
            return cnt

        cnt = scan16(a_ref, jnp.zeros((BF16_ROWS, tq), jnp.int16), fn)
        return jnp.sum(cnt.astype(jnp.int32), axis=0, keepdims=True)

    def max_le16(a_ref, h):
        h16 = jnp.broadcast_to(h.astype(jnp.int16), (BF16_ROWS, tq))

        def fn(acc, a):
            v = jnp.where(a <= h16[None], a, jnp.int16(i16_min))
            for g in range(rows16):
                acc = jnp.where(v[g] > acc, v[g], acc)
            return acc

        acc = scan16(a_ref, jnp.full((BF16_ROWS, tq), i16_min, jnp.int16), fn)
        return jnp.max(acc.astype(jnp.int32), axis=0, keepdims=True)

    def rank_select16(a_ref, r, skip, lo, hi, clo, chi, floor_count, steps_per_check):
        def closed(lo, hi, clo):
            return skip | (clo == r) | (hi == lo + 1)

        def open_rows(lo, hi, clo, chi):
            return jnp.logical_not(closed(lo, hi, clo) | (chi == r - 1))

        def cond(st):
            it, lo, hi, clo, chi = st
            n_open = jnp.max(jnp.where(open_rows(lo, hi, clo, chi), 1, 0))
            return jnp.logical_and(it < MAX_SELECT_STEPS, n_open > 0)

        def step(st):
            it, lo, hi, clo, chi = st
            for _ in range(steps_per_check):
                upd = open_rows(lo, hi, clo, chi)
                p = lo + jnp.maximum(jnp.right_shift(hi - lo, 1), 1)
                cnt = count_ge16(a_ref, p)
                ge = cnt >= r
                up, dn = upd & ge, upd & jnp.logical_not(ge)
                lo, hi = jnp.where(up, p, lo), jnp.where(dn, p, hi)
                clo, chi = jnp.where(up, cnt, clo), jnp.where(dn, cnt, chi)
            return it + steps_per_check, lo, hi, clo, chi

        _, lo, hi, clo, chi = lax.while_loop(cond, step, (jnp.int32(0), lo, hi, clo, chi))
        last = jnp.logical_not(closed(lo, hi, clo))
        t_last = max_le16(a_ref, hi - 1)
        c_last = jnp.where(t_last == i16_min, floor_count, count_ge16(a_ref, t_last))
        return jnp.where(last, t_last, lo), jnp.where(last, c_last, clo), chi

    n_valid = i * tq + lax.broadcasted_iota(jnp.int32, (1, tq), 1) + 1
    few = n_valid <= n_sel
    zeros = jnp.zeros((1, tq), jnp.int32)
    hi_lo0 = jnp.right_shift(jnp.min(kmin, axis=0, keepdims=True), HALF_BITS)
    hi_hi0 = jnp.right_shift(jnp.max(kmax, axis=0, keepdims=True), HALF_BITS) + 1
    t_hi, c_ge_hi, c_gt_hi = rank_select16(hi_ref, n_sel, few, hi_lo0, hi_hi0, n_valid, zeros, n_valid,
                                           UPPER_STEPS_PER_CHECK)
    exact_hi = c_ge_hi == n_sel

    t_hi16 = jnp.broadcast_to(t_hi.astype(jnp.int16), (BF16_ROWS, tq))

    def fill_low(c, carry):
        ks = pl.multiple_of(c * ck, ck)
        low = (key_ref[pl.ds(ks, ck), :] & (HALF_SPAN - 1)) + I16_MIN
        low = low.astype(jnp.int16).reshape(rows16, BF16_ROWS, tq)
        same = hi_ref[pl.ds(ks, ck), :].reshape(rows16, BF16_ROWS, tq) == t_hi16[None]
        low_ref[pl.ds(ks, ck), :] = jnp.where(same, low, jnp.int16(i16_min)).reshape(ck, tq)
        return carry

    lax.fori_loop(0, nk, fill_low, 0)
    in_bucket = c_ge_hi - c_gt_hi
    t_lo, c_ge_lo, c_gt_lo = rank_select16(low_ref, n_sel - c_gt_hi, few | exact_hi,
                                           jnp.full((1, tq), I16_MIN, jnp.int32), jnp.full((1, tq), -I16_MIN, jnp.int32),
                                           in_bucket, zeros, in_bucket, LOWER_STEPS_PER_CHECK)
    lo = jnp.where(exact_hi, t_hi * HALF_SPAN, t_hi * HALF_SPAN + (t_lo - I16_MIN))
    clo = jnp.where(exact_hi, n_sel, c_gt_hi + c_ge_lo)
    chi = c_gt_hi + c_gt_lo
    thr = jnp.where(few, jnp.int32(INT_MIN + 1), lo)
    thr8 = jnp.broadcast_to(thr, (SUBLANES, tq))
    tied = jnp.logical_not(few) & (clo > n_sel)
    need = n_sel - chi
    n_keys = mask_ref.shape[2]

    def tie_cut(_):
        def count_le(j):
            j8 = jnp.broadcast_to(j, (SUBLANES, tq))

            def body(c, cnt):
                ks = pl.multiple_of(c * ck, ck)
                kk = key_ref[pl.ds(ks, ck), :].reshape(groups, SUBLANES, tq)
                idx = (c * ck + kloc).reshape(groups, SUBLANES, tq)
                hit = jnp.where(kk == thr8[None], jnp.where(idx <= j8[None], 1, 0), 0)
                return cnt + jnp.sum(hit, axis=0)

            cnt = lax.fori_loop(0, nk, body, jnp.zeros((SUBLANES, tq), jnp.int32))
            return jnp.sum(cnt, axis=0, keepdims=True)

        def bisect(_, st):
            jlo, jhi = st
            mid = jlo + jnp.right_shift(jhi - jlo, 1)
            ok = count_le(mid) >= need
            return jnp.where(ok, jlo, mid), jnp.where(ok, mid, jhi)

        _, jhi = lax.fori_loop(0, n_keys.bit_length(), bisect,
                               (jnp.full((1, tq), -1, jnp.int32), jnp.full((1, tq), n_keys - 1, jnp.int32)))
        return jnp.where(tied, jhi, int_max)

    jcut = lax.cond(jnp.max(jnp.where(tied, 1, 0)) > 0, tie_cut, lambda _: jnp.full((1, tq), int_max, jnp.int32), 0)
    jcut8 = jnp.broadcast_to(jcut, (SUBLANES, tq))

    def write_chunk(c, carry):
        ks = pl.multiple_of(c * ck, ck)
        kk = key_ref[pl.ds(ks, ck), :].reshape(groups, SUBLANES, tq)
        idx = (c * ck + kloc).reshape(groups, SUBLANES, tq)
        at_thr = jnp.where(kk == thr8[None], jnp.where(idx <= jcut8[None], 1, 0), 0)
        sel = jnp.where(kk > thr8[None], 1, at_thr)
        mask_ref[0, 0, pl.ds(ks, ck), :] = sel.reshape(ck, tq).astype(mask_ref.dtype)
        return carry

    lax.fori_loop(0, nk, write_chunk, 0)

    def zero_chunk(c, carry):
        ks = pl.multiple_of(c * ck, ck)
        mask_ref[0, 0, pl.ds(ks, ck), :] = jnp.zeros((ck, tq), mask_ref.dtype)
        return carry

    lax.fori_loop(nk, mask_ref.shape[2] // ck, zero_chunk, 0)


def indexer_mask(qi, small, k_even, k_odd, b, t, n_sel):
    tq = _tile(t, 512)
    ck = _tile(t, 512)
    return pl.pallas_call(
        functools.partial(_indexer_body, tq=tq, ck=ck, n_sel=n_sel),
        grid=(b, t // tq),
        in_specs=[
            pl.BlockSpec((1, tq, IDX_HEADS * IDX_DIM), lambda bb, i: (bb, i, 0)),
            pl.BlockSpec((1, tq, LANES), lambda bb, i: (bb, i, 0)),
            pl.BlockSpec((1, t, LANES), lambda bb, i: (bb, 0, 0)),
            pl.BlockSpec((1, t, LANES), lambda bb, i: (bb, 0, 0)),
        ],
        out_specs=pl.BlockSpec((1, 1, t, tq), lambda bb, i: (bb, i, 0, 0)),
        out_shape=jax.ShapeDtypeStruct((b, t // tq, t, tq), jnp.int8),
        scratch_shapes=[pltpu.VMEM((t, tq), jnp.int32), pltpu.VMEM((t, tq), jnp.int16),
                        pltpu.VMEM((t, tq), jnp.int16), pltpu.VMEM((IDX_HEADS, SUBLANES, tq), F32)],
        compiler_params=_params("parallel", "parallel"),
        name="indexer",
    )(qi.reshape(b, t, -1), small, k_even.reshape(b, t, LANES), k_odd.reshape(b, t, LANES))


ATTN_HEAD_GROUP = 4
ATTN_HEAD_GROUP_FIXED = 2


def _attn_body(qi_ref, ki_ref, fix_ref, q_ref, k_ref, vt_ref, mask_ref, bias_ref, z_ref, m0_ref, o_ref, m_ref, acc_ref,
               *, tq, kt):
    qi, kp = qi_ref[pl.program_id(1)], ki_ref[pl.program_id(1)]
    fixed = fix_ref[0] == 1

    @pl.when(kp == 0)
    def _():
        m_ref[...] = jnp.full(m_ref.shape, MASK_NEG, F32)
        acc_ref[...] = jnp.zeros(acc_ref.shape, F32)

    def tile(hf, near, fixed_shift):
        ki = kp * kt + hf
        rows = slice(hf * tq, (hf + 1) * tq)
        madd = (1.0 - mask_ref[0, 0, rows, :].astype(F32)) * MASK_NEG
        if fixed_shift:
            madd = madd - m0_ref[0, 0]
        off = jnp.where(ki == qi, 0, 1)
        group = ATTN_HEAD_GROUP_FIXED if fixed_shift else ATTN_HEAD_GROUP
        for g in range(0, A_HEADS, group):
            hs = range(g, g + group)
            grp = slice(g, g + group)
            k2 = jnp.stack([k_ref[0, rows, h * A_HEAD_DIM:(h + 1) * A_HEAD_DIM] for h in hs])
            q2 = jnp.stack([q_ref[0, :, h * A_HEAD_DIM:(h + 1) * A_HEAD_DIM] for h in hs])
            s = lax.dot_general(k2, q2, (((2,), (2,)), ((0,), (0,))), preferred_element_type=F32) + madd[None]
            if near:
                s = s + bias_ref[off, grp].astype(F32)
            vt2 = jnp.stack([vt_ref[hf, h * V_ROWS:(h + 1) * V_ROWS, :] for h in hs])
            pv = lambda p: lax.dot_general(vt2, p, (((2,), (1,)), ((0,), (0,))), preferred_element_type=F32)
            if fixed_shift:
                acc_ref[grp] = acc_ref[grp] + pv(jnp.exp2(s).astype(BF16))
            else:
                m_prev = m_ref[grp]
                m_new = jnp.maximum(m_prev, jnp.max(s, axis=1, keepdims=True))
                acc_ref[grp] = jnp.exp2(m_prev - m_new) * acc_ref[grp] + pv(jnp.exp2(s - m_new).astype(BF16))
                m_ref[grp] = m_new

    for hf in range(kt):
        ki = kp * kt + hf
        far = ki + 1 < qi
        close = jnp.logical_and(jnp.logical_not(far), ki <= qi)
        for near in (False, True):
            for fixed_shift in (False, True):
                cond = jnp.logical_and(close if near else far, fixed if fixed_shift else jnp.logical_not(fixed))
                pl.when(cond)(functools.partial(tile, hf, near, fixed_shift))

    @pl.when(kp == lax.div(qi, jnp.int32(kt)))
    def _():
        outs = []
        for h in range(A_HEADS):
            a = acc_ref[h]
            outs.append((a[:A_HEAD_DIM] / a[A_HEAD_DIM:A_HEAD_DIM + 1]).T)
        o_ref[0] = (jnp.concatenate(outs, axis=-1) * _silu(z_ref[0].astype(F32))).astype(o_ref.dtype)


MAX_LOGIT_SPAN = 100.0


def softmax_shift(q_gain, k_gain, rel_bias):
    slack = 1.02
    qk = (A_HEAD_DIM ** 0.5) * LOG2E * jnp.max(jnp.abs(q_gain)) * jnp.max(jnp.abs(k_gain)) * slack
    b2 = (rel_bias - rel_bias[REL_BUCKETS - 1]) * (LOG2E * slack)
    shift = qk + jnp.max(b2)
    span = 2.0 * qk + jnp.max(b2) - jnp.min(b2)
    return shift.astype(F32), (span <= MAX_LOGIT_SPAN).astype(jnp.int32)


def attention(q, k, vt, mask, bias, proj3d, shift, use_shift, b, t, tq):
    nq = t // tq
    kt = 2 if nq % 2 == 0 else 1
    pairs = [(i, j) for i in range(nq) for j in range(i // kt + 1)]
    qi_tab = jnp.asarray([p[0] for p in pairs], jnp.int32)
    ki_tab = jnp.asarray([p[1] for p in pairs], jnp.int32)
    grid_spec = pltpu.PrefetchScalarGridSpec(
        num_scalar_prefetch=3,
        grid=(b, len(pairs)),
        in_specs=[
            pl.BlockSpec((1, tq, A_WIDTH), lambda bb, p, qi, ki, fx: (bb, qi[p], 0)),
            pl.BlockSpec((1, kt * tq, A_WIDTH), lambda bb, p, qi, ki, fx: (bb, ki[p], 0)),
            pl.BlockSpec((kt, A_HEADS * V_ROWS, tq), lambda bb, p, qi, ki, fx: (bb * (nq // kt) + ki[p], 0, 0)),
            pl.BlockSpec((1, 1, kt * tq, tq), lambda bb, p, qi, ki, fx: (bb, qi[p], ki[p], 0)),
            pl.BlockSpec(bias.shape, lambda bb, p, qi, ki, fx: (0, 0, 0, 0)),
            pl.BlockSpec((1, tq, A_WIDTH), lambda bb, p, qi, ki, fx: (bb, qi[p], COL_Z_A // A_WIDTH)),
            pl.BlockSpec(memory_space=pltpu.SMEM),
        ],
        out_specs=pl.BlockSpec((1, tq, A_WIDTH), lambda bb, p, qi, ki, fx: (bb, qi[p], 0)),
        scratch_shapes=[pltpu.VMEM((A_HEADS, 1, tq), F32), pltpu.VMEM((A_HEADS, V_ROWS, tq), F32)],
    )
    return pl.pallas_call(
        functools.partial(_attn_body, tq=tq, kt=kt),
        grid_spec=grid_spec,
        out_shape=jax.ShapeDtypeStruct((b, t, A_WIDTH), BF16),
        compiler_params=_params("parallel", "arbitrary"),
        name="dsa_attention",
    )(qi_tab, ki_tab, use_shift.reshape(1), q.reshape(b, t, -1), k.reshape(b, t, -1), vt, mask, bias, proj3d,
      shift.reshape(1, 1))


def _gdn_prep_body(x_ref, halo_ref, sm_ref, cw_ref, alog_ref, dtb_ref,
                   q_ref, k_ref, v_ref, gcb_ref, bb_ref, grow_ref, *, tm):
    i = pl.program_id(1)
    x = x_ref[0]
    hal = halo_ref[0]
    hal = jnp.where(i > 0, hal, jnp.zeros_like(hal))
    rt = lax.broadcasted_iota(jnp.int32, (tm, tm), 0)
    ct = lax.broadcasted_iota(jnp.int32, (tm, tm), 1)
    rh = lax.broadcasted_iota(jnp.int32, (SUBLANES, BF16_ROWS), 0)
    ch = lax.broadcasted_iota(jnp.int32, (SUBLANES, BF16_ROWS), 1)
    y = cw_ref[B_CONV - 1:B_CONV, :] * x.astype(F32)
    for j in range(B_CONV - 1):
        back = B_CONV - 1 - j
        main = _dot(jnp.where(ct == rt - back, 1.0, 0.0).astype(BF16), x)
        head = _dot(jnp.where(ch == rh + (BF16_ROWS - back), 1.0, 0.0).astype(BF16), hal)
        y = y + cw_ref[j:j + 1, :] * jnp.concatenate([main[:SUBLANES] + head, main[SUBLANES:]], axis=0)
    y = _silu(y)
    for h in range(B_HEADS):
        sl = slice(h * LANES, (h + 1) * LANES)
        qh = y[:, sl]
        q_ref[0, :, sl] = qh * lax.rsqrt(jnp.sum(qh * qh, axis=-1, keepdims=True) + EPS) * (B_HEAD_DIM ** -0.5)
        kh = y[:, B_WIDTH + h * LANES:B_WIDTH + (h + 1) * LANES]
        k_ref[0, :, sl] = kh * lax.rsqrt(jnp.sum(kh * kh, axis=-1, keepdims=True) + EPS)
    v_ref[0] = y[:, 2 * B_WIDTH:]

    sm = sm_ref[0]
    xg = sm + dtb_ref[...]
    softplus = jnp.maximum(xg, 0.0) + jnp.log(1.0 + jnp.exp(-jnp.abs(xg)))
    g = -jnp.exp(alog_ref[...]) * softplus
    r = lax.broadcasted_iota(jnp.int32, (tm, tm), 0)
    c = lax.broadcasted_iota(jnp.int32, (tm, tm), 1)
    sh = int(math.log2(GDN_CHUNK))
    same_chunk = jnp.right_shift(r, sh) == jnp.right_shift(c, sh)
    tri = jnp.where(jnp.logical_and(same_chunk, c <= r), 1.0, 0.0).astype(F32)
    gc = lax.dot_general(tri, g, (((1,), (0,)), ((), ())), precision=HIGHEST, preferred_element_type=F32)
    beta = _sigmoid(sm)
    for h in range(B_HEADS):
        sl = slice(h * LANES, (h + 1) * LANES)
        gcb_ref[0, :, sl] = jnp.broadcast_to(gc[:, SM_ALPHA + h:SM_ALPHA + h + 1], (tm, LANES))
        bb_ref[0, :, sl] = jnp.broadcast_to(beta[:, SM_BETA + h:SM_BETA + h + 1], (tm, LANES))
    gct = gc.T
    for cc in range(tm // GDN_CHUNK):
        grow_ref[0, cc] = gct[SM_ALPHA:SM_ALPHA + B_HEADS, cc * GDN_CHUNK:(cc + 1) * GDN_CHUNK]


def gdn_prep(proj3d, small3d, conv_w, a_log, dt_bias, b, t):
    tm = _tile(t, 256)
    pad_row = lambda v: jnp.zeros((1, LANES), F32).at[0, SM_ALPHA:SM_ALPHA + B_HEADS].set(v)
    nc = tm // GDN_CHUNK
    act = jax.ShapeDtypeStruct((b, t, B_WIDTH), F32)
    full = lambda a: pl.BlockSpec(a.shape, lambda bb, i: (0,) * a.ndim)
    consts = [conv_w, pad_row(a_log), pad_row(dt_bias)]
    return pl.pallas_call(
        functools.partial(_gdn_prep_body, tm=tm),
        grid=(b, t // tm),
        in_specs=[
            pl.BlockSpec((1, tm, 3 * B_WIDTH), lambda bb, i: (bb, i, COL_QKV_B // (3 * B_WIDTH))),
            pl.BlockSpec((1, BF16_ROWS, 3 * B_WIDTH),
                         lambda bb, i: (bb, jnp.maximum(i * (tm // BF16_ROWS) - 1, 0), 0)),
            pl.BlockSpec((1, tm, LANES), lambda bb, i: (bb, i, 0)),
        ] + [full(c) for c in consts],
        out_specs=[pl.BlockSpec((1, tm, B_WIDTH), lambda bb, i: (bb, i, 0))] * 5
        + [pl.BlockSpec((1, nc, B_HEADS, GDN_CHUNK), lambda bb, i: (bb, i, 0, 0))],
        out_shape=[act] * 5 + [jax.ShapeDtypeStruct((b, t // GDN_CHUNK, B_HEADS, GDN_CHUNK), F32)],
        compiler_params=_params("parallel", "parallel"),
        name="gdn_prep",
    )(proj3d, proj3d, small3d, *consts)


def _gdn_body(q_ref, k_ref, v_ref, gcb_ref, bb_ref, grow_ref, z_ref, on_ref, o_ref, s_ref, *, nc):
    @pl.when(pl.program_id(1) == 0)
    def _():
        s_ref[...] = jnp.zeros(s_ref.shape, F32)

    cs, nh = GDN_CHUNK, B_HEADS
    nb = nc * nh
    ri = lax.broadcasted_iota(jnp.int32, (nb, cs, cs), 1)
    ci = lax.broadcasted_iota(jnp.int32, (nb, cs, cs), 2)
    lower = ci <= ri
    strict = ci < ri
    eye = jnp.where(ci == ri, 1.0, 0.0).astype(F32)
    bf = lambda a: a.astype(BF16)
    bmm = lambda a, b: lax.dot_general(a, b, (((2,), (1,)), ((0,), (0,))), preferred_element_type=F32)
    bmm_nt = lambda a, b: lax.dot_general(a, b, (((2,), (2,)), ((0,), (0,))), preferred_element_type=F32)
    bmm_tn = lambda a, b: lax.dot_general(a, b, (((1,), (1,)), ((0,), (0,))), preferred_element_type=F32)

    def stack(ref):
        return jnp.stack([ref[0, c * cs:(c + 1) * cs, h * LANES:(h + 1) * LANES]
                          for c in range(nc) for h in range(nh)])

    q, k, v = stack(q_ref), stack(k_ref), stack(v_ref)
    gcb = stack(gcb_ref)
    beta = stack(bb_ref)
    grow = jnp.stack([grow_ref[0, c, h:h + 1, :] for c in range(nc) for h in range(nh)])
    diff = gcb[:, :, :cs] - grow
    decay = jnp.where(lower, jnp.exp(jnp.where(lower, diff, 0.0)), 0.0)
    eg = jnp.exp(gcb)
    glast = gcb[:, cs - 1:cs, :]
    kb = k * beta
    lmat = jnp.where(strict, bmm_nt(bf(kb), bf(k)) * decay, 0.0)
    n = -lmat
    tinv = eye + n
    for _ in range(int(math.log2(cs)) - 1):
        n = bmm(bf(n), bf(n))
        tinv = tinv + bmm(bf(tinv), bf(n))
    uw = bmm(bf(tinv), bf(jnp.concatenate([v * beta, kb * eg], axis=-1)))
    attn = bf(jnp.where(lower, bmm_nt(bf(q), bf(k)) * decay, 0.0))
    qg = bf(q * eg)
    kdec = bf(k * jnp.exp(glast - gcb))
    egl = jnp.exp(glast)

    s = s_ref[...]
    for c in range(nc):
        sl = slice(c * nh, (c + 1) * nh)
        sb = bf(s)
        v_new = uw[sl, :, :LANES] - bmm(bf(uw[sl, :, LANES:]), sb)
        o = bmm(qg[sl], sb) + bmm(attn[sl], bf(v_new))
        s = s * egl[sl] + bmm_tn(kdec[sl], bf(v_new))
        o = o * lax.rsqrt(jnp.mean(o * o, axis=-1, keepdims=True) + EPS) * on_ref[...]
        for h in range(nh):
            rows, cols = slice(c * cs, (c + 1) * cs), slice(h * LANES, (h + 1) * LANES)
            o_ref[0, rows, cols] = (o[h] * _silu(z_ref[0, rows, cols].astype(F32))).astype(o_ref.dtype)
    s_ref[...] = s


def gdn_scan(qh, kh, v, gcb, bb, grow, proj3d, o_norm, b, t):
    tt = _tile(t, 256)
    nc = tt // GDN_CHUNK
    blk = pl.BlockSpec((1, tt, B_WIDTH), lambda bb_, i: (bb_, i, 0))
    return pl.pallas_call(
        functools.partial(_gdn_body, nc=nc),
        grid=(b, t // tt),
        in_specs=[blk] * 5 + [
            pl.BlockSpec((1, nc, B_HEADS, GDN_CHUNK), lambda bb_, i: (bb_, i, 0, 0)),
            pl.BlockSpec((1, tt, B_WIDTH), lambda bb_, i: (bb_, i, COL_Z_B // B_WIDTH)),
            pl.BlockSpec((1, LANES), lambda bb_, i: (0, 0)),
        ],
        out_specs=blk,
        out_shape=jax.ShapeDtypeStruct((b, t, B_WIDTH), BF16),
        scratch_shapes=[pltpu.VMEM((B_HEADS, B_HEAD_DIM, B_HEAD_DIM), F32)],
        compiler_params=_params("parallel", "arbitrary"),
        name="gdn_scan",
    )(qh, kh, v, gcb, bb, grow, proj3d, o_norm.reshape(1, LANES))


C_HALO = 32
D_HALO = BF16_ROWS


def _cd_layer_body(a_ref, g_ref, ah_ref, gh_ref, zc_ref, bg_ref, cg_ref, ud_ref, cgh_ref, udh_ref, zd_ref,
                   x_ref, wo_ref, dww_ref, dwb_ref, lnw_ref, lnb_ref, dcw_ref, o_ref, us_ref, ds_ref, y_ref,
                   *, tm, tiles_per_seq):
    s = pl.program_id(0)
    first = lax.rem(s, tiles_per_seq) == 0
    slot = lax.rem(s, 2)

    @pl.when(s == 0)
    def _():
        y_ref[...] = jnp.zeros(y_ref.shape, y_ref.dtype)

    o_ref[...] = x_ref[...] + _dot(y_ref[1 - slot], wo_ref[...])

    f32 = lambda r: r[...].astype(F32)
    uh = f32(ah_ref) * _sigmoid(f32(gh_ref))
    us_ref[0, 0:C_HALO, :] = jnp.where(first, jnp.zeros_like(uh), uh)
    us_ref[0, C_HALO:, :] = f32(a_ref) * _sigmoid(f32(g_ref))
    span = tm + C_HALO - SUBLANES
    for r in range(1, SUBLANES):
        us_ref[r, 0:span, :] = us_ref[0, pl.ds(r, span), :]
    u = jnp.zeros((tm, C_WIDTH), F32)
    for j in range(C_CONV):
        off = C_HALO - (C_CONV - 1) + j
        r, base = off % SUBLANES, off - off % SUBLANES
        u = u + dww_ref[j:j + 1, :] * us_ref[r, base:base + tm, :]
    u = u + dwb_ref[...]
    mu = jnp.mean(u, axis=-1, keepdims=True)
    uc = u - mu
    u = uc * lax.rsqrt(jnp.mean(uc * uc, axis=-1, keepdims=True) + EPS) * lnw_ref[...] + lnb_ref[...]
    y_ref[slot, :, 0:C_WIDTH] = (_silu(u) * _silu(f32(zc_ref))).astype(y_ref.dtype)

    dh = f32(cgh_ref) * f32(udh_ref)
    ds_ref[0:D_HALO, :] = jnp.where(first, jnp.zeros_like(dh), dh)
    ds_ref[D_HALO:, :] = f32(cg_ref) * f32(ud_ref)
    d = jnp.zeros((tm, D_WIDTH), F32)
    for j in range(D_CONV):
        d = d + dcw_ref[j:j + 1, :] * ds_ref[pl.ds(D_HALO - (D_CONV - 1) + j, tm), :]
    y_ref[slot, :, C_WIDTH:] = (f32(bg_ref) * d * _silu(f32(zd_ref))).astype(y_ref.dtype)


def cd_layer(proj, x2d, w_out_bf16, dw_w, dw_b, ln_w, ln_b, d_conv_w, t):
    m, n = x2d.shape
    tm = _tile(t, 256)
    w = C_WIDTH
    last = m // tm - 1
    cur = lambda s: jnp.minimum(s, last)
    col = lambda c: pl.BlockSpec((tm, w), lambda s, c=c: (cur(s), c))
    halo = lambda c, rows: pl.BlockSpec(
        (rows, w), lambda s, c=c, rows=rows: (jnp.maximum(cur(s) * (tm // rows) - 1, 0), c))
    prev = pl.BlockSpec((tm, n), lambda s: (jnp.maximum(s - 1, 0), 0))
    row = lambda a: a.reshape(1, -1)
    full = lambda a: pl.BlockSpec(a.shape, lambda s: (0,) * a.ndim)
    consts = [w_out_bf16, dw_w, row(dw_b), row(ln_w), row(ln_b), d_conv_w]
    return pl.pallas_call(
        functools.partial(_cd_layer_body, tm=tm, tiles_per_seq=t // tm),
        grid=(m // tm + 1,),
        in_specs=[col(0), col(1), halo(0, C_HALO), halo(1, C_HALO), col(2), col(3), col(4), col(5),
                  halo(4, D_HALO), halo(5, D_HALO), col(6), prev] + [full(c) for c in consts],
        out_specs=prev,
        out_shape=jax.ShapeDtypeStruct((m, n), F32),
        scratch_shapes=[pltpu.VMEM((SUBLANES, tm + C_HALO, w), F32), pltpu.VMEM((tm + D_HALO, w), F32),
                        pltpu.VMEM((2, tm, C_WIDTH + D_WIDTH), BF16)],
        compiler_params=_params("arbitrary"),
        name="cd_layer",
    )(*([proj] * 11), x2d, *consts)


def _reorder_ab_w_in(w):
    offs = [0]
    for s in AB_SPLITS:
        offs.append(offs[-1] + s)
    part = lambda n: w[:, offs[n]:offs[n + 1]]
    c_q, c_kv, k_idx, w_idx, z_a, qkv_b, beta_b, alpha_b, z_b = (part(n) for n in range(9))
    small = jnp.concatenate([k_idx, w_idx, beta_b, alpha_b], axis=1)
    small = jnp.pad(small, ((0, 0), (0, LANES - small.shape[1])))
    out = jnp.concatenate([qkv_b, z_a, z_b, c_q, c_kv], axis=1)
    return jnp.pad(out, ((0, 0), (0, AB_IN_PAD - out.shape[1]))).astype(BF16), small.astype(BF16)


def _ab_layer(x2d, b, t, norm_w, rel_bias, w_in, q_norm, w_uq, w_iq, kv_norm, w_uk, w_uv, q_gain, k_gain,
              ik_w, ik_b, conv_w, a_log, dt_bias, o_norm, w_out):
    proj, small = norm_matmul(x2d, norm_w, *_reorder_ab_w_in(w_in))
    proj3d, small3d = proj.reshape(b, t, AB_IN_PAD), small.reshape(b, t, LANES)
    tq = _tile(t, 512)
    q, k, v, qi, k_even, k_odd = dsa_prep(proj, small, q_norm, kv_norm, w_uq, w_iq, w_uk, w_uv, q_gain, k_gain,
                                          ik_w, ik_b, tm=tq)
    n_sel = min(TOPK_MAX, t // 4)
    mask = indexer_mask(qi, small3d, k_even, k_odd, b, t, n_sel)
    shift, use_shift = softmax_shift(q_gain, k_gain, rel_bias)
    y_a = attention(q, k, v, mask, bias_tiles(rel_bias, tq), proj3d, shift, use_shift, b, t, tq)
    qh, kh, vv, gcb, bb, grow = gdn_prep(proj3d, small3d, conv_w, a_log, dt_bias, b, t)
    y_b = gdn_scan(qh, kh, vv, gcb, bb, grow, proj3d, o_norm, b, t)
    wo = w_out.astype(BF16)
    return out_proj(x2d, y_a.reshape(b * t, -1), y_b.reshape(b * t, -1), wo[:A_WIDTH], wo[A_WIDTH:])


def _cd_layer(x2d, b, t, norm_w, w_in, dw_w, dw_b, ln_w, ln_b, d_conv_w, w_out):
    proj = norm_matmul(x2d, norm_w, w_in.astype(BF16))
    return cd_layer(proj, x2d, w_out.astype(BF16), dw_w, dw_b, ln_w, ln_b, d_conv_w, t)


def kernel(x, norm_w, rel_bias, ab_w_in, a_q_norm, a_w_uq, a_w_iq, a_kv_norm, a_w_uk, a_w_uv, a_q_gain,
           a_k_gain, a_ik_norm_w, a_ik_norm_b, b_conv_w, b_a_log, b_dt_bias, b_o_norm, ab_w_out, cd_w_in,
           c_dw_w, c_dw_b, c_ln_w, c_ln_b, d_conv_w, cd_w_out):
    b, t, d = x.shape
    depth = norm_w.shape[0]
    x2d = x.reshape(b * t, d)
    for i in range(depth):
        j = i // 2
        if i % 2 == 0:
            x2d = _ab_layer(x2d, b, t, norm_w[i], rel_bias, ab_w_in[j], a_q_norm[j], a_w_uq[j], a_w_iq[j],
                            a_kv_norm[j], a_w_uk[j], a_w_uv[j], a_q_gain[j], a_k_gain[j], a_ik_norm_w[j],
                            a_ik_norm_b[j], b_conv_w[j], b_a_log[j], b_dt_bias[j], b_o_norm[j], ab_w_out[j])
        else:
            x2d = _cd_layer(x2d, b, t, norm_w[i], cd_w_in[j], c_dw_w[j], c_dw_b[j], c_ln_w[j], c_ln_b[j],
                            d_conv_w[j], cd_w_out[j])
    return x2d.reshape(b, t, d)
```
